```python
import jax, jax.numpy as jnp
from jax import lax
import numpy as np

D_MODEL = 1024
BATCH = 8
SEQ = 8192
DEPTH = 2

N_MIXERS = 2
HEAD_DIM = 64
N_SLOTS = 8
DILATED_GROUPS = ((128, 1), (512, 4), (2048, 16))
N_GROUPS = 3
GROUP_WIDTH = N_SLOTS * HEAD_DIM
ROT_DIM = HEAD_DIM // 4
ROPE_THETA = 500000.0
CONV_KERNEL = 31
CONV_INNER = D_MODEL
FFN_DIM = 2816
FFN_CONV = 3
EPS = 1e-6

kernel_name = "hybrid_dilated_attn_conformer_convffn"


def _rmsnorm(x, g):
    xf = x.astype(jnp.float32)
    y = xf * lax.rsqrt(jnp.mean(xf * xf, axis=-1, keepdims=True) + EPS)
    return (y * g.astype(jnp.float32)).astype(x.dtype)


def _layernorm(x, g, b):
    xf = x.astype(jnp.float32)
    mu = jnp.mean(xf, axis=-1, keepdims=True)
    var = jnp.mean(jnp.square(xf - mu), axis=-1, keepdims=True)
    y = (xf - mu) * lax.rsqrt(var + EPS)
    return (y * g.astype(jnp.float32) + b.astype(jnp.float32)).astype(x.dtype)


def _causal_depthwise_conv(x, w):
    k, c = w.shape
    return lax.conv_general_dilated(
        x, w[:, None, :].astype(x.dtype), window_strides=(1,),
        padding=[(k - 1, 0)], dimension_numbers=("NWC", "WIO", "NWC"),
        feature_group_count=c)


def _partial_rope(t, positions):
    half = ROT_DIM // 2
    inv_freq = ROPE_THETA ** (-jnp.arange(half, dtype=jnp.float32) / half)
    ang = positions.astype(jnp.float32)[:, :, None] * inv_freq
    cos = jnp.cos(ang)[:, :, None, None, :]
    sin = jnp.sin(ang)[:, :, None, None, :]
    tr = t[..., :ROT_DIM].astype(jnp.float32)
    t1, t2 = tr[..., :half], tr[..., half:]
    rot = jnp.concatenate([t1 * cos - t2 * sin, t2 * cos + t1 * sin], axis=-1)
    return jnp.concatenate([rot.astype(t.dtype), t[..., ROT_DIM:]], axis=-1)


def _banded_causal_attention(q, k, v, span):
    n, h, l, dh = q.shape
    nb = l // span
    qb = q.reshape(n, h, nb, span, dh)
    kb = k.reshape(n, h, nb, span, dh)
    vb = v.reshape(n, h, nb, span, dh)

    def with_prev(t):
        prev = jnp.pad(t, ((0, 0), (0, 0), (1, 0), (0, 0), (0, 0)))[:, :, :-1]
        return jnp.concatenate([prev, t], axis=3)

    kc, vc = with_prev(kb), with_prev(vb)
    s = jnp.einsum("nhbqd,nhbkd->nhbqk", qb, kc,
                   preferred_element_type=jnp.float32) * (HEAD_DIM ** -0.5)
    qi = jnp.arange(span)[:, None] + span
    ki = jnp.arange(2 * span)[None, :]
    dist = qi - ki
    band = (dist >= 0) & (dist <= span)
    has_prev = (jnp.arange(nb)[:, None, None] > 0) | (ki[None] >= span)
    mask = band[None] & has_prev
    s = jnp.where(mask, s, -jnp.inf)
    m = jnp.max(s, axis=-1, keepdims=True)
    p = jnp.exp(s - m)
    denom = jnp.sum(p, axis=-1)
    o = jnp.einsum("nhbqk,nhbkd->nhbqd", p, vc.astype(jnp.float32)) / denom[..., None]
    lse = m[..., 0] + jnp.log(denom)
    return o.reshape(n, h, l, dh), lse.reshape(n, h, l)


def _dilated_group_attention(q, k, v, window, dilation):
    b, s, h, dh = q.shape
    span = window // dilation
    l = s // dilation
    lp = -(-l // span) * span

    def to_phases(t):
        t = t.reshape(b, l, dilation, h, dh).transpose(0, 2, 3, 1, 4)
        t = t.reshape(b * dilation, h, l, dh)
        return jnp.pad(t, ((0, 0), (0, 0), (0, lp - l), (0, 0)))

    o, lse = _banded_causal_attention(to_phases(q), to_phases(k), to_phases(v), span)
    o = o[:, :, :l].reshape(b, dilation, h, l, dh).transpose(0, 3, 1, 2, 4)
    lse = lse[:, :, :l].reshape(b, dilation, h, l).transpose(0, 3, 1, 2)
    return o.reshape(b, s, h, dh), lse.reshape(b, s, h)


def _dilated_attention(h, positions, w_qkv, w_o):
    b, s, _ = h.shape
    qkv = (h @ w_qkv).reshape(b, s, 3, N_GROUPS, N_SLOTS, HEAD_DIM)
    q = _partial_rope(qkv[:, :, 0], positions)
    k = _partial_rope(qkv[:, :, 1], positions)
    v = qkv[:, :, 2]
    outs, lses = [], []
    for g, (window, dilation) in enumerate(DILATED_GROUPS):
        o_g, lse_g = _dilated_group_attention(q[:, :, g], k[:, :, g], v[:, :, g],
                                              window, dilation)
        outs.append(o_g)
        lses.append(lse_g)
    o = jnp.stack(outs, axis=0)
    wgt = jax.nn.softmax(jnp.stack(lses, axis=0), axis=0)
    mixed = jnp.sum(wgt[..., None] * o, axis=0).astype(h.dtype)
    return mixed.reshape(b, s, GROUP_WIDTH) @ w_o


def _conformer_conv(h, w_pw1, b_pw1, w_dw, b_dw, ln_g, ln_b, w_pw2, b_pw2):
    a, gate = jnp.split(h @ w_pw1 + b_pw1, 2, axis=-1)
    u = a * jax.nn.sigmoid(gate)
    u = _causal_depthwise_conv(u, w_dw) + b_dw
    u = jax.nn.silu(_layernorm(u, ln_g, ln_b))
    return u @ w_pw2 + b_pw2


def _conv_ffn(h, w_up, w_dw, b_dw, w_down):
    ug = _causal_depthwise_conv(h @ w_up, w_dw) + b_dw
    up, gate = jnp.split(ug, 2, axis=-1)
    return (jax.nn.silu(gate) * up) @ w_down


def _fwd_setup_inputs(seed: int = 0) -> dict:
    key = jax.random.key(seed)
    ks = jax.random.split(key, 20)
    n_attn = (DEPTH + N_MIXERS - 1) // N_MIXERS
    n_conv = DEPTH // N_MIXERS
    f32 = jnp.float32
    nrm = lambda k, shape, fan: jax.random.normal(k, shape, f32) * (fan ** -0.5)
    small = lambda k, shape: 0.02 * jax.random.normal(k, shape, f32)
    x = jax.random.normal(ks[0], (BATCH, SEQ, D_MODEL), f32)
    offset = jax.random.randint(ks[1], (BATCH, 1), 0, 4096, dtype=jnp.int32)
    positions = (jnp.arange(SEQ, dtype=jnp.int32)[None, :] + offset).astype(jnp.int32)
    return {
        "x": x,
        "positions": positions,
        "norm_g": 1.0 + small(ks[2], (DEPTH, 4, D_MODEL)),
        "attn_w_qkv": nrm(ks[3], (n_attn, D_MODEL, 3 * N_GROUPS * GROUP_WIDTH), D_MODEL),
        "attn_w_o": nrm(ks[4], (n_attn, GROUP_WIDTH, D_MODEL), GROUP_WIDTH),
        "conv_w_pw1": nrm(ks[5], (n_conv, D_MODEL, 2 * CONV_INNER), D_MODEL),
        "conv_b_pw1": small(ks[6], (n_conv, 2 * CONV_INNER)),
        "conv_w_dw": nrm(ks[7], (n_conv, CONV_KERNEL, CONV_INNER), CONV_KERNEL),
        "conv_b_dw": small(ks[8], (n_conv, CONV_INNER)),
        "conv_ln_g": 1.0 + small(ks[9], (n_conv, CONV_INNER)),
        "conv_ln_b": small(ks[10], (n_conv, CONV_INNER)),
        "conv_w_pw2": nrm(ks[11], (n_conv, CONV_INNER, D_MODEL), CONV_INNER),
        "conv_b_pw2": small(ks[12], (n_conv, D_MODEL)),
        "ffn_w_up": nrm(ks[13], (DEPTH, D_MODEL, 2 * FFN_DIM), D_MODEL),
        "ffn_w_dw": nrm(ks[14], (DEPTH, FFN_CONV, 2 * FFN_DIM), FFN_CONV),
        "ffn_b_dw": small(ks[15], (DEPTH, 2 * FFN_DIM)),
        "ffn_w_down": nrm(ks[16], (DEPTH, FFN_DIM, D_MODEL), FFN_DIM),
    }


def _fwd_reference(x, positions, norm_g, attn_w_qkv, attn_w_o, conv_w_pw1, conv_b_pw1,
              conv_w_dw, conv_b_dw, conv_ln_g, conv_ln_b, conv_w_pw2, conv_b_pw2,
              ffn_w_up, ffn_w_dw, ffn_b_dw, ffn_w_down):
    for i in range(DEPTH):
        g = norm_g[i]
        j = i // N_MIXERS
        hn = _rmsnorm(x, g[0])
        if i % N_MIXERS == 0:
            y = _dilated_attention(hn, positions, attn_w_qkv[j], attn_w_o[j])
        else:
            y = _conformer_conv(hn, conv_w_pw1[j], conv_b_pw1[j], conv_w_dw[j],
                                conv_b_dw[j], conv_ln_g[j], conv_ln_b[j],
                                conv_w_pw2[j], conv_b_pw2[j])
        x = x + _rmsnorm(y, g[1])
        hn = _rmsnorm(x, g[2])
        y = _conv_ffn(hn, ffn_w_up[i], ffn_w_dw[i], ffn_b_dw[i], ffn_w_down[i])
        x = x + _rmsnorm(y, g[3])
    return x


import jax as _jax
import jax.numpy as _jnp

TWIN_FORMAT = 'train_step'
FWD_PARAMS = ['x', 'positions', 'norm_g', 'attn_w_qkv', 'attn_w_o', 'conv_w_pw1', 'conv_b_pw1', 'conv_w_dw', 'conv_b_dw', 'conv_ln_g', 'conv_ln_b', 'conv_w_pw2', 'conv_b_pw2', 'ffn_w_up', 'ffn_w_dw', 'ffn_b_dw', 'ffn_w_down']
TWIN_WEIGHTS = ['norm_g', 'attn_w_qkv', 'attn_w_o', 'conv_w_pw1', 'conv_b_pw1', 'conv_w_dw', 'conv_b_dw', 'conv_ln_g', 'conv_ln_b', 'conv_w_pw2', 'conv_b_pw2', 'ffn_w_up', 'ffn_w_dw', 'ffn_b_dw', 'ffn_w_down']
TWIN_DIFF_INPUT = 'x'
TWIN_INPUTS = ['x', 'positions', 'norm_g', 'attn_w_qkv', 'attn_w_o', 'conv_w_pw1', 'conv_b_pw1', 'conv_w_dw', 'conv_b_dw', 'conv_ln_g', 'conv_ln_b', 'conv_w_pw2', 'conv_b_pw2', 'ffn_w_up', 'ffn_w_dw', 'ffn_b_dw', 'ffn_w_down', 'loss_target', 'm_norm_g', 'm_attn_w_qkv', 'm_attn_w_o', 'm_conv_w_pw1', 'm_conv_b_pw1', 'm_conv_w_dw', 'm_conv_b_dw', 'm_conv_ln_g', 'm_conv_ln_b', 'm_conv_w_pw2', 'm_conv_b_pw2', 'm_ffn_w_up', 'm_ffn_w_dw', 'm_ffn_b_dw', 'm_ffn_w_down', 'v_norm_g', 'v_attn_w_qkv', 'v_attn_w_o', 'v_conv_w_pw1', 'v_conv_b_pw1', 'v_conv_w_dw', 'v_conv_b_dw', 'v_conv_ln_g', 'v_conv_ln_b', 'v_conv_w_pw2', 'v_conv_b_pw2', 'v_ffn_w_up', 'v_ffn_w_dw', 'v_ffn_b_dw', 'v_ffn_w_down']
TWIN_OUTPUTS = ['loss', 'grad_x', 'grad_norm_g', 'grad_attn_w_qkv', 'grad_attn_w_o', 'grad_conv_w_pw1', 'grad_conv_b_pw1', 'grad_conv_w_dw', 'grad_conv_b_dw', 'grad_conv_ln_g', 'grad_conv_ln_b', 'grad_conv_w_pw2', 'grad_conv_b_pw2', 'grad_ffn_w_up', 'grad_ffn_w_dw', 'grad_ffn_b_dw', 'grad_ffn_w_down', 'delta_norm_g', 'delta_attn_w_qkv', 'delta_attn_w_o', 'delta_conv_w_pw1', 'delta_conv_b_pw1', 'delta_conv_w_dw', 'delta_conv_b_dw', 'delta_conv_ln_g', 'delta_conv_ln_b', 'delta_conv_w_pw2', 'delta_conv_b_pw2', 'delta_ffn_w_up', 'delta_ffn_w_dw', 'delta_ffn_b_dw', 'delta_ffn_w_down', 'new_m_norm_g', 'new_m_attn_w_qkv', 'new_m_attn_w_o', 'new_m_conv_w_pw1', 'new_m_conv_b_pw1', 'new_m_conv_w_dw', 'new_m_conv_b_dw', 'new_m_conv_ln_g', 'new_m_conv_ln_b', 'new_m_conv_w_pw2', 'new_m_conv_b_pw2', 'new_m_ffn_w_up', 'new_m_ffn_w_dw', 'new_m_ffn_b_dw', 'new_m_ffn_w_down', 'new_v_norm_g', 'new_v_attn_w_qkv', 'new_v_attn_w_o', 'new_v_conv_w_pw1', 'new_v_conv_b_pw1', 'new_v_conv_w_dw', 'new_v_conv_b_dw', 'new_v_conv_ln_g', 'new_v_conv_ln_b', 'new_v_conv_w_pw2', 'new_v_conv_b_pw2', 'new_v_ffn_w_up', 'new_v_ffn_w_dw', 'new_v_ffn_b_dw', 'new_v_ffn_w_down']
TWIN_LEAF_KINDS = {'loss': 'loss', 'grad_x': 'grad_x', 'grad_norm_g': 'grad_w', 'grad_attn_w_qkv': 'grad_w', 'grad_attn_w_o': 'grad_w', 'grad_conv_w_pw1': 'grad_w', 'grad_conv_b_pw1': 'grad_w', 'grad_conv_w_dw': 'grad_w', 'grad_conv_b_dw': 'grad_w', 'grad_conv_ln_g': 'grad_w', 'grad_conv_ln_b': 'grad_w', 'grad_conv_w_pw2': 'grad_w', 'grad_conv_b_pw2': 'grad_w', 'grad_ffn_w_up': 'grad_w', 'grad_ffn_w_dw': 'grad_w', 'grad_ffn_b_dw': 'grad_w', 'grad_ffn_w_down': 'grad_w', 'delta_norm_g': 'delta_w', 'delta_attn_w_qkv': 'delta_w', 'delta_attn_w_o': 'delta_w', 'delta_conv_w_pw1': 'delta_w', 'delta_conv_b_pw1': 'delta_w', 'delta_conv_w_dw': 'delta_w', 'delta_conv_b_dw': 'delta_w', 'delta_conv_ln_g': 'delta_w', 'delta_conv_ln_b': 'delta_w', 'delta_conv_w_pw2': 'delta_w', 'delta_conv_b_pw2': 'delta_w', 'delta_ffn_w_up': 'delta_w', 'delta_ffn_w_dw': 'delta_w', 'delta_ffn_b_dw': 'delta_w', 'delta_ffn_w_down': 'delta_w', 'new_m_norm_g': 'new_m', 'new_m_attn_w_qkv': 'new_m', 'new_m_attn_w_o': 'new_m', 'new_m_conv_w_pw1': 'new_m', 'new_m_conv_b_pw1': 'new_m', 'new_m_conv_w_dw': 'new_m', 'new_m_conv_b_dw': 'new_m', 'new_m_conv_ln_g': 'new_m', 'new_m_conv_ln_b': 'new_m', 'new_m_conv_w_pw2': 'new_m', 'new_m_conv_b_pw2': 'new_m', 'new_m_ffn_w_up': 'new_m', 'new_m_ffn_w_dw': 'new_m', 'new_m_ffn_b_dw': 'new_m', 'new_m_ffn_w_down': 'new_m', 'new_v_norm_g': 'new_v', 'new_v_attn_w_qkv': 'new_v', 'new_v_attn_w_o': 'new_v', 'new_v_conv_w_pw1': 'new_v', 'new_v_conv_b_pw1': 'new_v', 'new_v_conv_w_dw': 'new_v', 'new_v_conv_b_dw': 'new_v', 'new_v_conv_ln_g': 'new_v', 'new_v_conv_ln_b': 'new_v', 'new_v_conv_w_pw2': 'new_v', 'new_v_conv_b_pw2': 'new_v', 'new_v_ffn_w_up': 'new_v', 'new_v_ffn_w_dw': 'new_v', 'new_v_ffn_b_dw': 'new_v', 'new_v_ffn_w_down': 'new_v'}


def _forward(args):
    return _fwd_reference(*[args[k] for k in FWD_PARAMS])


def _output_shape():
    def fwd():
        inp = _fwd_setup_inputs(0)
        return _fwd_reference(*[inp[k] for k in FWD_PARAMS])
    out = _jax.eval_shape(fwd)
    return out.shape, out.dtype

N_MICROBATCH = 1
ADAM_LR = 0.001
ADAM_B1 = 0.9
ADAM_B2 = 0.999
ADAM_EPS = 1e-08
ADAM_WD = 0.01
ADAM_STEP = 10
PER_EXAMPLE_BATCH_AXIS = {'x': 0, 'positions': 0, 'loss_target': 0}
SHARED_INPUTS = []
_WEIGHT_DTYPES = {'norm_g': _jnp.float32, 'attn_w_qkv': _jnp.float32, 'attn_w_o': _jnp.float32, 'conv_w_pw1': _jnp.float32, 'conv_b_pw1': _jnp.float32, 'conv_w_dw': _jnp.float32, 'conv_b_dw': _jnp.float32, 'conv_ln_g': _jnp.float32, 'conv_ln_b': _jnp.float32, 'conv_w_pw2': _jnp.float32, 'conv_b_pw2': _jnp.float32, 'ffn_w_up': _jnp.float32, 'ffn_w_dw': _jnp.float32, 'ffn_b_dw': _jnp.float32, 'ffn_w_down': _jnp.float32}
MOMENT_SCALE = {'norm_g': 4.498812e+01, 'attn_w_qkv': 1.081384e+00, 'attn_w_o': 1.584148e+00, 'conv_w_pw1': 6.509334e-01, 'conv_b_pw1': 3.100585e+00, 'conv_w_dw': 9.496529e-01, 'conv_b_dw': 7.976102e+00, 'conv_ln_g': 3.211299e+00, 'conv_ln_b': 4.602006e+00, 'conv_w_pw2': 1.964754e+00, 'conv_b_pw2': 9.474748e+00, 'ffn_w_up': 6.177589e-01, 'ffn_w_dw': 7.138833e-01, 'ffn_b_dw': 3.810556e+00, 'ffn_w_down': 1.234842e+00}


def _to_microbatches(a, axis):
    t = _jnp.moveaxis(a, axis, 0)
    t = t.reshape((N_MICROBATCH, t.shape[0] // N_MICROBATCH) + t.shape[1:])
    return _jnp.moveaxis(t, 1, axis + 1)


def setup_inputs(seed: int = 0) -> dict:
    inp = _fwd_setup_inputs(seed)
    key = _jax.random.fold_in(_jax.random.key(seed), 7919)
    shape, _ = _output_shape()
    out = dict(inp)
    out["loss_target"] = _jax.random.normal(_jax.random.fold_in(key, 0), shape, _jnp.float32)
    for i, name in enumerate(TWIN_WEIGHTS):
        w = inp[name].astype(_jnp.float32)
        if MOMENT_SCALE is None:
            s = _jnp.sqrt(_jnp.mean(_jnp.square(w)) + 1e-30)
        else:
            s = MOMENT_SCALE[name]
        km, kv = _jax.random.split(_jax.random.fold_in(key, i + 1))
        out[name] = w
        out["m_" + name] = s * _jax.random.normal(km, w.shape, _jnp.float32)
        out["v_" + name] = (s * s) * _jax.random.uniform(kv, w.shape, _jnp.float32, 0.5, 1.5)
    if N_MICROBATCH > 1:
        for name, axis in PER_EXAMPLE_BATCH_AXIS.items():
            out[name] = _to_microbatches(out[name], axis)
    return {'x': out['x'], 'positions': out['positions'], 'norm_g': out['norm_g'], 'attn_w_qkv': out['attn_w_qkv'], 'attn_w_o': out['attn_w_o'], 'conv_w_pw1': out['conv_w_pw1'], 'conv_b_pw1': out['conv_b_pw1'], 'conv_w_dw': out['conv_w_dw'], 'conv_b_dw': out['conv_b_dw'], 'conv_ln_g': out['conv_ln_g'], 'conv_ln_b': out['conv_ln_b'], 'conv_w_pw2': out['conv_w_pw2'], 'conv_b_pw2': out['conv_b_pw2'], 'ffn_w_up': out['ffn_w_up'], 'ffn_w_dw': out['ffn_w_dw'], 'ffn_b_dw': out['ffn_b_dw'], 'ffn_w_down': out['ffn_w_down'], 'loss_target': out['loss_target'], 'm_norm_g': out['m_norm_g'], 'm_attn_w_qkv': out['m_attn_w_qkv'], 'm_attn_w_o': out['m_attn_w_o'], 'm_conv_w_pw1': out['m_conv_w_pw1'], 'm_conv_b_pw1': out['m_conv_b_pw1'], 'm_conv_w_dw': out['m_conv_w_dw'], 'm_conv_b_dw': out['m_conv_b_dw'], 'm_conv_ln_g': out['m_conv_ln_g'], 'm_conv_ln_b': out['m_conv_ln_b'], 'm_conv_w_pw2': out['m_conv_w_pw2'], 'm_conv_b_pw2': out['m_conv_b_pw2'], 'm_ffn_w_up': out['m_ffn_w_up'], 'm_ffn_w_dw': out['m_ffn_w_dw'], 'm_ffn_b_dw': out['m_ffn_b_dw'], 'm_ffn_w_down': out['m_ffn_w_down'], 'v_norm_g': out['v_norm_g'], 'v_attn_w_qkv': out['v_attn_w_qkv'], 'v_attn_w_o': out['v_attn_w_o'], 'v_conv_w_pw1': out['v_conv_w_pw1'], 'v_conv_b_pw1': out['v_conv_b_pw1'], 'v_conv_w_dw': out['v_conv_w_dw'], 'v_conv_b_dw': out['v_conv_b_dw'], 'v_conv_ln_g': out['v_conv_ln_g'], 'v_conv_ln_b': out['v_conv_ln_b'], 'v_conv_w_pw2': out['v_conv_w_pw2'], 'v_conv_b_pw2': out['v_conv_b_pw2'], 'v_ffn_w_up': out['v_ffn_w_up'], 'v_ffn_w_dw': out['v_ffn_w_dw'], 'v_ffn_b_dw': out['v_ffn_b_dw'], 'v_ffn_w_down': out['v_ffn_w_down']}


def _loss(weights, diff, rest, loss_target):
    with _jax.named_scope("forward"):
        args = {**rest, TWIN_DIFF_INPUT: diff, **{k: w.astype(_WEIGHT_DTYPES[k]) for k, w in weights.items()}}
        y = _forward(args)
    with _jax.named_scope("loss_head"):
        err = _jnp.square(y.astype(_jnp.float32) - loss_target)
        return 0.5 * _jnp.sum(_jnp.mean(err, axis=-1)) if err.ndim else 0.5 * err


def _adamw(w, g, m, v):
    m = ADAM_B1 * m + (1.0 - ADAM_B1) * g
    v = ADAM_B2 * v + (1.0 - ADAM_B2) * _jnp.square(g)
    m_hat = m / (1.0 - ADAM_B1 ** ADAM_STEP)
    v_hat = v / (1.0 - ADAM_B2 ** ADAM_STEP)
    delta = -ADAM_LR * (m_hat / (_jnp.sqrt(v_hat) + ADAM_EPS) + ADAM_WD * w)
    return delta, m, v


def reference(x, positions, norm_g, attn_w_qkv, attn_w_o, conv_w_pw1, conv_b_pw1, conv_w_dw, conv_b_dw, conv_ln_g, conv_ln_b, conv_w_pw2, conv_b_pw2, ffn_w_up, ffn_w_dw, ffn_b_dw, ffn_w_down, loss_target, m_norm_g, m_attn_w_qkv, m_attn_w_o, m_conv_w_pw1, m_conv_b_pw1, m_conv_w_dw, m_conv_b_dw, m_conv_ln_g, m_conv_ln_b, m_conv_w_pw2, m_conv_b_pw2, m_ffn_w_up, m_ffn_w_dw, m_ffn_b_dw, m_ffn_w_down, v_norm_g, v_attn_w_qkv, v_attn_w_o, v_conv_w_pw1, v_conv_b_pw1, v_conv_w_dw, v_conv_b_dw, v_conv_ln_g, v_conv_ln_b, v_conv_w_pw2, v_conv_b_pw2, v_ffn_w_up, v_ffn_w_dw, v_ffn_b_dw, v_ffn_w_down):
    given = dict(x=x, positions=positions, norm_g=norm_g, attn_w_qkv=attn_w_qkv, attn_w_o=attn_w_o, conv_w_pw1=conv_w_pw1, conv_b_pw1=conv_b_pw1, conv_w_dw=conv_w_dw, conv_b_dw=conv_b_dw, conv_ln_g=conv_ln_g, conv_ln_b=conv_ln_b, conv_w_pw2=conv_w_pw2, conv_b_pw2=conv_b_pw2, ffn_w_up=ffn_w_up, ffn_w_dw=ffn_w_dw, ffn_b_dw=ffn_b_dw, ffn_w_down=ffn_w_down, loss_target=loss_target, m_norm_g=m_norm_g, m_attn_w_qkv=m_attn_w_qkv, m_attn_w_o=m_attn_w_o, m_conv_w_pw1=m_conv_w_pw1, m_conv_b_pw1=m_conv_b_pw1, m_conv_w_dw=m_conv_w_dw, m_conv_b_dw=m_conv_b_dw, m_conv_ln_g=m_conv_ln_g, m_conv_ln_b=m_conv_ln_b, m_conv_w_pw2=m_conv_w_pw2, m_conv_b_pw2=m_conv_b_pw2, m_ffn_w_up=m_ffn_w_up, m_ffn_w_dw=m_ffn_w_dw, m_ffn_b_dw=m_ffn_b_dw, m_ffn_w_down=m_ffn_w_down, v_norm_g=v_norm_g, v_attn_w_qkv=v_attn_w_qkv, v_attn_w_o=v_attn_w_o, v_conv_w_pw1=v_conv_w_pw1, v_conv_b_pw1=v_conv_b_pw1, v_conv_w_dw=v_conv_w_dw, v_conv_b_dw=v_conv_b_dw, v_conv_ln_g=v_conv_ln_g, v_conv_ln_b=v_conv_ln_b, v_conv_w_pw2=v_conv_w_pw2, v_conv_b_pw2=v_conv_b_pw2, v_ffn_w_up=v_ffn_w_up, v_ffn_w_dw=v_ffn_w_dw, v_ffn_b_dw=v_ffn_b_dw, v_ffn_w_down=v_ffn_w_down)
    weights = {n: given[n] for n in TWIN_WEIGHTS}
    shared = {n: given[n] for n in SHARED_INPUTS}
    per_example = {n: given[n] for n in ['x', 'positions']}
    grad_fn = _jax.value_and_grad(_loss, argnums=(0, 1))

    def one_microbatch(ex, loss_target):
        ex = dict(ex)
        diff = ex.pop(TWIN_DIFF_INPUT)
        return grad_fn(weights, diff, {**shared, **ex}, loss_target)

    if N_MICROBATCH == 1:
        loss, (grad_w, grad_x) = one_microbatch(per_example, given["loss_target"])
    else:
        def body(carry, xs):
            loss_sum, grad_sum = carry
            l_k, (gw_k, gx_k) = one_microbatch(xs[0], xs[1])
            with _jax.named_scope("update"):
                return (loss_sum + l_k, _jax.tree.map(_jnp.add, grad_sum, gw_k)), gx_k

        init = (_jnp.zeros((), _jnp.float32), _jax.tree.map(_jnp.zeros_like, weights))
        (loss, grad_w), grad_x = _jax.lax.scan(body, init, (per_example, given["loss_target"]))
    with _jax.named_scope("update"):
        delta_w, new_m, new_v = {}, {}, {}
        for n in TWIN_WEIGHTS:
            delta_w[n], new_m[n], new_v[n] = _adamw(weights[n], grad_w[n], given["m_" + n], given["v_" + n])
    return (loss, grad_x, *[grad_w[n] for n in TWIN_WEIGHTS], *[delta_w[n] for n in TWIN_WEIGHTS],
            *[new_m[n] for n in TWIN_WEIGHTS], *[new_v[n] for n in TWIN_WEIGHTS])
```

```python
import functools

import jax
import jax.numpy as jnp
from jax import lax
from jax.experimental import pallas as pl
from jax.experimental.pallas import tpu as pltpu

f32 = jnp.float32
bf16 = jnp.bfloat16
SDS = jax.ShapeDtypeStruct

EPS = 1e-6
HEAD_DIM = 64
N_SLOTS = 8
GROUP_WIDTH = N_SLOTS * HEAD_DIM
DILATED_GROUPS = ((128, 1), (512, 4), (2048, 16))
N_GROUPS = 3
SPAN = 128
ROT_DIM = HEAD_DIM // 4
ROPE_THETA = 500000.0
CONV_KERNEL = 31
CONV_HALO = 32
FFN_CONV = 3
FFN_HALO = 8
ADAM_LR, ADAM_B1, ADAM_B2, ADAM_EPS, ADAM_WD, ADAM_STEP = 0.001, 0.9, 0.999, 1e-08, 0.01, 10
LANES = 128
N_CHIPS = 4
VMEM_LIMIT_BYTES = 56 * 1024 * 1024
NEG_BIG = -1e30
MESH = pl.DeviceIdType.MESH
ANY = pl.BlockSpec(memory_space=pl.ANY)


def _params(*sem):
    return pltpu.CompilerParams(dimension_semantics=sem, vmem_limit_bytes=VMEM_LIMIT_BYTES)


def _sigmoid(v):
    return 1.0 / (1.0 + jnp.exp(-v))


def _dot_nt(a, b):
    return lax.dot_general(a, b, (((1,), (1,)), ((), ())), preferred_element_type=f32)


def _dot_tn(a, b):
    return lax.dot_general(a, b, (((0,), (0,)), ((), ())), preferred_element_type=f32)


def mm_nn(x, w, li, bias, out_blocks, out_dtype, tm, tn, name):
    nq, m, kq = x.shape
    p, _, k, n = w.shape
    assert k == nq * kq and n % tn == 0 and m % tm == 0
    on = p * n // out_blocks
    assert on % tn == 0
    nj, onj = n // tn, on // tn

    def body(*refs):
        if bias is None:
            x_ref, w_ref, o_ref, *scr = refs
            b_ref = None
        else:
            x_ref, w_ref, b_ref, o_ref, *scr = refs
        part = jnp.dot(x_ref[...], w_ref[...], preferred_element_type=f32)

        def finish(acc):
            if b_ref is not None:
                acc = acc + b_ref[...]
            o_ref[...] = acc.astype(o_ref.dtype)

        if nq == 1:
            finish(part)
        else:
            acc_ref, = scr
            q = pl.program_id(2)

            @pl.when(q == 0)
            def _():
                acc_ref[...] = part

            @pl.when(q > 0)
            def _():
                acc_ref[...] += part

            @pl.when(q == nq - 1)
            def _():
                finish(acc_ref[...])

    in_specs = [pl.BlockSpec((None, tm, kq), lambda j, i, q: (q, i, 0)),
                pl.BlockSpec((None, None, kq, tn), lambda j, i, q: (j // nj, li, q, j % nj))]
    args = [x, w]
    if bias is not None:
        in_specs.append(pl.BlockSpec((1, tn), lambda j, i, q: (0, j)))
        args.append(bias)
    return pl.pallas_call(
        body, grid=(p * nj, m // tm, nq), in_specs=in_specs,
        out_specs=pl.BlockSpec((None, tm, tn), lambda j, i, q: (j // onj, i, j % onj)),
        out_shape=SDS((out_blocks, m, on), out_dtype),
        scratch_shapes=[] if nq == 1 else [pltpu.VMEM((tm, tn), f32)],
        compiler_params=_params("parallel", "parallel", "arbitrary"), name=name)(*args)


def mm_nt(dy, w, li, out_blocks, out_dtype, tm, tk, tn, name):
    ob, m, on = dy.shape
    p, _, k, n = w.shape
    assert ob * on == p * n and n % tn == 0 and on % tn == 0 and k % tk == 0 and m % tm == 0
    kq = k // out_blocks
    assert kq % tk == 0
    nj, onj, kqj = n // tn, on // tn, kq // tk
    nr = p * nj

    def body(dy_ref, w_ref, o_ref, *scr):
        part = _dot_nt(dy_ref[...], w_ref[...])
        if nr == 1:
            o_ref[...] = part.astype(o_ref.dtype)
        else:
            acc_ref, = scr
            r = pl.program_id(2)

            @pl.when(r == 0)
            def _():
                acc_ref[...] = part

            @pl.when(r > 0)
            def _():
                acc_ref[...] += part

            @pl.when(r == nr - 1)
            def _():
                o_ref[...] = acc_ref[...].astype(o_ref.dtype)

    return pl.pallas_call(
        body, grid=(k // tk, m // tm, nr),
        in_specs=[pl.BlockSpec((None, tm, tn), lambda kt, i, r: (r // onj, i, r % onj)),
                  pl.BlockSpec((None, None, tk, tn), lambda kt, i, r: (r // nj, li, kt, r % nj))],
        out_specs=pl.BlockSpec((None, tm, tk), lambda kt, i, r: (kt // kqj, i, kt % kqj)),
        out_shape=SDS((out_blocks, m, kq), out_dtype),
        scratch_shapes=[] if nr == 1 else [pltpu.VMEM((tm, tk), f32)],
        compiler_params=_params("parallel", "parallel", "arbitrary"), name=name)(dy, w)


def mm_tn(x, dy, p, tm, tk, tn, name):
    nq, m, kq = x.shape
    ob, _, on = dy.shape
    k = nq * kq
    n = ob * on // p
    assert n % tn == 0 and on % tn == 0 and kq % tk == 0 and m % tm == 0
    nj, onj, kqj = n // tn, on // tn, kq // tk

    def body(x_ref, dy_ref, o_ref):
        part = _dot_tn(x_ref[...], dy_ref[...])
        i = pl.program_id(2)

        @pl.when(i == 0)
        def _():
            o_ref[...] = part

        @pl.when(i > 0)
        def _():
            o_ref[...] += part

    return pl.pallas_call(
        body, grid=(k // tk, p * nj, m // tm),
        in_specs=[pl.BlockSpec((None, tm, tk), lambda kt, j, i: (kt // kqj, i, kt % kqj)),
                  pl.BlockSpec((None, tm, tn), lambda kt, j, i: (j // onj, i, j % onj))],
        out_specs=pl.BlockSpec((None, tk, tn), lambda kt, j, i: (j // nj, kt, j % nj)),
        out_shape=SDS((p, k, n), f32),
        compiler_params=_params("parallel", "parallel", "arbitrary"), name=name)(x, dy)


def _row_tile(s):
    return min(s, 512)


def _rows(tr, d):
    return pl.BlockSpec((tr, d), lambda i: (i, 0))


def _fixed(r, d):
    return pl.BlockSpec((r, d), lambda i: (0, 0))


def _rms(xv):
    return lax.rsqrt(jnp.mean(xv * xv, axis=-1, keepdims=True) + EPS)


def prenorm(x, g, name):
    s, d = x.shape
    tr = _row_tile(s)

    def body(x_ref, g_ref, o_ref):
        xv = x_ref[...]
        o_ref[...] = (xv * _rms(xv) * g_ref[...]).astype(o_ref.dtype)

    return pl.pallas_call(body, grid=(s // tr,), in_specs=[_rows(tr, d), _fixed(1, d)], out_specs=_rows(tr, d),
                          out_shape=SDS((s, d), bf16), compiler_params=_params("parallel"), name=name)(x, g)


def postnorm_residual(x, y, g, name):
    s, d = x.shape
    tr = _row_tile(s)

    def body(x_ref, y_ref, g_ref, o_ref):
        yv = y_ref[...]
        o_ref[...] = x_ref[...] + yv * _rms(yv) * g_ref[...]

    return pl.pallas_call(body, grid=(s // tr,), in_specs=[_rows(tr, d), _rows(tr, d), _fixed(1, d)],
                          out_specs=_rows(tr, d), out_shape=SDS((s, d), f32), compiler_params=_params("parallel"),
                          name=name)(x, y, g)


def norm_bwd(xin, g, dout, res, out_dtype, name):
    s, d = xin.shape
    tr = _row_tile(s)

    def body(*refs):
        if res is None:
            x_ref, g_ref, do_ref, dx_ref, dg_ref, cs_ref = refs
            r_ref = None
        else:
            x_ref, g_ref, do_ref, r_ref, dx_ref, dg_ref, cs_ref = refs
        xv = x_ref[...]
        r = _rms(xv)
        xh = xv * r
        dov = do_ref[...].astype(f32)
        gy = dov * g_ref[...]
        dx = r * (gy - xh * jnp.mean(gy * xh, axis=-1, keepdims=True))
        if r_ref is not None:
            dx = dx + r_ref[...]
        dx_ref[...] = dx.astype(dx_ref.dtype)
        dg = jnp.sum(dov * xh, axis=0, keepdims=True)
        cs = jnp.sum(dx, axis=0, keepdims=True)
        i = pl.program_id(0)

        @pl.when(i == 0)
        def _():
            dg_ref[...] = dg
            cs_ref[...] = cs

        @pl.when(i > 0)
        def _():
            dg_ref[...] += dg
            cs_ref[...] += cs

    in_specs = [_rows(tr, d), _fixed(1, d), _rows(tr, d)]
    args = [xin, g, dout]
    if res is not None:
        in_specs.append(_rows(tr, d))
        args.append(res)
    return pl.pallas_call(body, grid=(s // tr,), in_specs=in_specs,
                          out_specs=[_rows(tr, d), _fixed(1, d), _fixed(1, d)],
                          out_shape=[SDS((s, d), out_dtype), SDS((1, d), f32), SDS((1, d), f32)],
                          compiler_params=_params("arbitrary"), name=name)(*args)


def final_loss(x, y, g, target, name):
    s, d = x.shape
    tr = _row_tile(s)
    nt = s // tr

    def body(x_ref, y_ref, g_ref, t_ref, dx_ref, loss_ref, acc_ref):
        yv = y_ref[...]
        diff = x_ref[...] + yv * _rms(yv) * g_ref[...] - t_ref[...]
        dx_ref[...] = diff * (1.0 / d)
        sq = jnp.sum(diff * diff, axis=0, keepdims=True)
        i = pl.program_id(0)

        @pl.when(i == 0)
        def _():
            acc_ref[...] = sq

        @pl.when(i > 0)
        def _():
            acc_ref[...] += sq

        @pl.when(i == nt - 1)
        def _():
            total = jnp.sum(acc_ref[...], axis=1, keepdims=True) * (0.5 / d)
            loss_ref[...] = jnp.broadcast_to(total, (1, LANES))

    return pl.pallas_call(body, grid=(nt,), in_specs=[_rows(tr, d), _rows(tr, d), _fixed(1, d), _rows(tr, d)],
                          out_specs=[_rows(tr, d), _fixed(1, LANES)],
                          out_shape=[SDS((s, d), f32), SDS((1, LANES), f32)],
                          scratch_shapes=[pltpu.VMEM((1, d), f32)],
                          compiler_params=_params("arbitrary"), name=name)(x, y, g, target)


def _rotate_chunk(v, tc, ta, tb):
    return v * tc + pltpu.roll(v, LANES - ROT_DIM // 2, axis=1) * ta + pltpu.roll(v, ROT_DIM // 2, axis=1) * tb


def rope_qkv(qkv, tabs, name):
    s, w = qkv.shape
    tr = min(s, 256)
    n_rot = 2 * N_GROUPS * GROUP_WIDTH // LANES

    def body(x_ref, tc_ref, ta_ref, tb_ref, o_ref):
        tc, ta, tb = tc_ref[...], ta_ref[...], tb_ref[...]
        for ch in range(w // LANES):
            cols = slice(ch * LANES, (ch + 1) * LANES)
            v = x_ref[:, cols]
            if ch < n_rot:
                v = _rotate_chunk(v, tc, ta, tb)
            o_ref[:, cols] = v.astype(o_ref.dtype)

    tab = _rows(tr, LANES)
    return pl.pallas_call(body, grid=(s // tr,), in_specs=[_rows(tr, w), tab, tab, tab], out_specs=_rows(tr, w),
                          out_shape=SDS((s, w), bf16), compiler_params=_params("parallel"), name=name)(qkv, *tabs)


def _attn_masks(j):
    row = lax.broadcasted_iota(jnp.int32, (SPAN, SPAN), 0)
    col = lax.broadcasted_iota(jnp.int32, (SPAN, SPAN), 1)
    return col <= row, jnp.logical_and(col >= row, j > 0)


def _attn_in_specs(g, n_col_blocks):
    def at(kind, prev):
        def index(r, j):
            return (jnp.maximum(j - 1, 0) if prev else j, r * n_col_blocks + kind * N_GROUPS + g)
        return pl.BlockSpec((SPAN, GROUP_WIDTH), index)
    return [at(0, False), at(1, False), at(1, True), at(2, False), at(2, True)]


def attn_fwd(qkvr, g, dil, name):
    s, w = qkvr.shape
    l = s // dil
    nb = l // SPAN
    a = qkvr.reshape(l, dil * w)

    def body(q_ref, ko_ref, kp_ref, vo_ref, vp_ref, o_ref, lse_ref):
        own, prev = _attn_masks(pl.program_id(1))
        for h in range(N_SLOTS):
            hs = slice(h * HEAD_DIM, (h + 1) * HEAD_DIM)
            q = q_ref[:, hs]
            so = jnp.where(own, _dot_nt(q, ko_ref[:, hs]) * (HEAD_DIM ** -0.5), NEG_BIG)
            sp = jnp.where(prev, _dot_nt(q, kp_ref[:, hs]) * (HEAD_DIM ** -0.5), NEG_BIG)
            mx = jnp.maximum(jnp.max(so, axis=-1, keepdims=True), jnp.max(sp, axis=-1, keepdims=True))
            po = jnp.exp(so - mx)
            pp = jnp.exp(sp - mx)
            den = jnp.sum(po, axis=-1, keepdims=True) + jnp.sum(pp, axis=-1, keepdims=True)
            acc = jnp.dot(po.astype(bf16), vo_ref[:, hs], preferred_element_type=f32)
            acc = acc + jnp.dot(pp.astype(bf16), vp_ref[:, hs], preferred_element_type=f32)
            o_ref[:, hs] = acc / den
            lse_ref[:, hs] = jnp.broadcast_to(mx + jnp.log(den), (SPAN, HEAD_DIM))

    out = pl.BlockSpec((SPAN, GROUP_WIDTH), lambda r, j: (j, r))
    o, lse = pl.pallas_call(
        body, grid=(dil, nb), in_specs=_attn_in_specs(g, w // GROUP_WIDTH), out_specs=[out, out],
        out_shape=[SDS((l, dil * GROUP_WIDTH), f32)] * 2,
        compiler_params=_params("parallel", "parallel"), name=name)(a, a, a, a, a)
    return o.reshape(s, GROUP_WIDTH), lse.reshape(s, GROUP_WIDTH)


def _group_weights(lses):
    mx = jnp.maximum(jnp.maximum(lses[0], lses[1]), lses[2])
    es = [jnp.exp(v - mx) for v in lses]
    inv = 1.0 / (es[0] + es[1] + es[2])
    return [e * inv for e in es]


def mix_fwd(os_, lses, name):
    s, w = os_[0].shape
    tr = _row_tile(s)

    def body(o0, o1, o2, l0, l1, l2, out_ref):
        wg = _group_weights([l0[...], l1[...], l2[...]])
        out_ref[...] = (wg[0] * o0[...] + wg[1] * o1[...] + wg[2] * o2[...]).astype(out_ref.dtype)

    return pl.pallas_call(body, grid=(s // tr,), in_specs=[_rows(tr, w)] * 6, out_specs=_rows(tr, w),
                          out_shape=SDS((s, w), bf16), compiler_params=_params("parallel"), name=name)(*os_, *lses)


def mix_bwd(dmixed, os_, lses, head_ones, name):
    s, w = dmixed.shape
    tr = _row_tile(s)

    def head_sum(t, ones):
        hi = t.astype(bf16)
        lo = (t - hi.astype(f32)).astype(bf16)
        return jnp.dot(hi, ones, preferred_element_type=f32) + jnp.dot(lo, ones, preferred_element_type=f32)

    def body(dm_ref, o0, o1, o2, l0, l1, l2, ones_ref, d0, d1, d2, p0, p1, p2):
        dm = dm_ref[...]
        ones = ones_ref[...]
        wg = _group_weights([l0[...], l1[...], l2[...]])
        mean = sum(wg[k] * head_sum(dm * o[...], ones) for k, o in enumerate((o0, o1, o2)))
        for k, (d_ref, p_ref) in enumerate(((d0, p0), (d1, p1), (d2, p2))):
            d_ref[...] = (wg[k] * dm).astype(d_ref.dtype)
            p_ref[...] = wg[k] * mean

    return pl.pallas_call(body, grid=(s // tr,), in_specs=[_rows(tr, w)] * 7 + [_fixed(w, w)],
                          out_specs=[_rows(tr, w)] * 6,
                          out_shape=[SDS((s, w), bf16)] * 3 + [SDS((s, w), f32)] * 3,
                          compiler_params=_params("parallel"), name=name)(dmixed, *os_, *lses, head_ones)


def attn_bwd(qkvr, do, lse, dterm, g, dil, name):
    s, w = qkvr.shape
    l = s // dil
    nb = l // SPAN
    a = qkvr.reshape(l, dil * w)
    phased = lambda t: t.reshape(l, dil * GROUP_WIDTH)

    def body(q_ref, ko_ref, kp_ref, vo_ref, vp_ref, do_ref, lse_ref, dt_ref, dq_ref, dko_ref, dkp_ref, dvo_ref, dvp_ref):
        own, prev = _attn_masks(pl.program_id(1))
        scale = HEAD_DIM ** -0.5
        for h in range(N_SLOTS):
            hs = slice(h * HEAD_DIM, (h + 1) * HEAD_DIM)
            one = slice(h * HEAD_DIM, h * HEAD_DIM + 1)
            q, ko, kp, vo, vp, dov = q_ref[:, hs], ko_ref[:, hs], kp_ref[:, hs], vo_ref[:, hs], vp_ref[:, hs], do_ref[:, hs]
            lse_h, dt_h = lse_ref[:, one], dt_ref[:, one]
            po = jnp.exp(jnp.where(own, _dot_nt(q, ko) * scale - lse_h, NEG_BIG))
            pp = jnp.exp(jnp.where(prev, _dot_nt(q, kp) * scale - lse_h, NEG_BIG))
            dso = (po * (_dot_nt(dov, vo) - dt_h) * scale).astype(bf16)
            dsp = (pp * (_dot_nt(dov, vp) - dt_h) * scale).astype(bf16)
            dq_ref[:, hs] = jnp.dot(dso, ko, preferred_element_type=f32) + jnp.dot(dsp, kp, preferred_element_type=f32)
            dko_ref[:, hs] = _dot_tn(dso, q)
            dkp_ref[:, hs] = _dot_tn(dsp, q)
            dvo_ref[:, hs] = _dot_tn(po.astype(bf16), dov)
            dvp_ref[:, hs] = _dot_tn(pp.astype(bf16), dov)

    blk = pl.BlockSpec((SPAN, GROUP_WIDTH), lambda r, j: (j, r))
    outs = pl.pallas_call(
        body, grid=(dil, nb), in_specs=_attn_in_specs(g, w // GROUP_WIDTH) + [blk, blk, blk], out_specs=[blk] * 5,
        out_shape=[SDS((l, dil * GROUP_WIDTH), f32)] * 5,
        compiler_params=_params("parallel", "parallel"), name=name)(a, a, a, a, a, phased(do), phased(lse), phased(dterm))
    return [t.reshape(s, GROUP_WIDTH) for t in outs]


def dqkv_assemble(parts, tabs, name):
    s = parts[0][0].shape[0]
    nblk = s // SPAN
    width = 3 * N_GROUPS * GROUP_WIDTH

    def body(*refs):
        ins, (tc_ref, ta_ref, tb_ref, o_ref) = refs[:5 * N_GROUPS], refs[5 * N_GROUPS:]
        tc, ta, tb = tc_ref[...], ta_ref[...], tb_ref[...]
        i = pl.program_id(0)
        for g, (_, dil) in enumerate(DILATED_GROUPS):
            dq, dko, dkp, dvo, dvp = ins[5 * g:5 * g + 5]
            has_next = i + dil < nblk
            for ch in range(GROUP_WIDTH // LANES):
                cols = slice(ch * LANES, (ch + 1) * LANES)
                base = g * GROUP_WIDTH + ch * LANES
                dk = dko[:, cols] + jnp.where(has_next, dkp[:, cols], 0.0)
                dv = dvo[:, cols] + jnp.where(has_next, dvp[:, cols], 0.0)
                o_ref[:, base:base + LANES] = _rotate_chunk(dq[:, cols], tc, ta, tb).astype(o_ref.dtype)
                kb = N_GROUPS * GROUP_WIDTH + base
                o_ref[:, kb:kb + LANES] = _rotate_chunk(dk, tc, ta, tb).astype(o_ref.dtype)
                vb = 2 * N_GROUPS * GROUP_WIDTH + base
                o_ref[:, vb:vb + LANES] = dv.astype(o_ref.dtype)

    here = _rows(SPAN, GROUP_WIDTH)
    in_specs, args = [], []
    for g, (_, dil) in enumerate(DILATED_GROUPS):
        ahead = pl.BlockSpec((SPAN, GROUP_WIDTH), functools.partial(lambda i, dil: (jnp.minimum(i + dil, nblk - 1), 0), dil=dil))
        in_specs += [here, here, ahead, here, ahead]
        args += list(parts[g])
    tab = _rows(SPAN, LANES)
    return pl.pallas_call(body, grid=(nblk,), in_specs=in_specs + [tab] * 3, out_specs=_rows(SPAN, width),
                          out_shape=SDS((s, width), bf16), compiler_params=_params("parallel"), name=name)(*args, *tabs)


def _ffn_conv(cur, halo, wb, half, scr, first, tr):
    scr[0:FFN_HALO, :] = jnp.where(first, 0.0, halo)
    scr[FFN_HALO:FFN_HALO + tr, :] = cur
    acc = wb[half, FFN_CONV:FFN_CONV + 1, :]
    for k in range(FFN_CONV):
        acc = acc + wb[half, k:k + 1, :] * scr[pl.ds(FFN_HALO - (FFN_CONV - 1) + k, tr), :]
    return acc


def ffn_act_fwd(u, wb, name):
    _, nbk, s, c = u.shape
    tr = min(s, 256)

    def body(u_ref, h_ref, wb_ref, a_ref, su, sg):
        first = pl.program_id(1) == 0
        up = _ffn_conv(u_ref[0], h_ref[0], wb_ref, 0, su, first, tr)
        gate = _ffn_conv(u_ref[1], h_ref[1], wb_ref, 1, sg, first, tr)
        a_ref[...] = (gate * _sigmoid(gate) * up).astype(a_ref.dtype)

    return pl.pallas_call(
        body, grid=(nbk, s // tr),
        in_specs=[pl.BlockSpec((2, None, tr, c), lambda p, i: (0, p, i, 0)),
                  pl.BlockSpec((2, None, FFN_HALO, c), lambda p, i: (0, p, jnp.maximum(i * (tr // FFN_HALO) - 1, 0), 0)),
                  pl.BlockSpec((None, 2, 8, c), lambda p, i: (p, 0, 0, 0))],
        out_specs=pl.BlockSpec((None, tr, c), lambda p, i: (p, i, 0)),
        out_shape=SDS((nbk, s, c), bf16),
        scratch_shapes=[pltpu.VMEM((tr + FFN_HALO, c), f32)] * 2,
        compiler_params=_params("parallel", "arbitrary"), name=name)(u, u, wb)


def ffn_act_bwd(da, u, wb, name):
    _, nbk, s, c = u.shape
    tr = min(s, 256)
    nt = s // tr

    def body(da_ref, u_ref, h_ref, wb_ref, du_ref, dwb_ref, su, sg, eu, eg):
        step = pl.program_id(1)
        first = step == nt - 1
        up = _ffn_conv(u_ref[0], h_ref[0], wb_ref, 0, su, first, tr)
        gate = _ffn_conv(u_ref[1], h_ref[1], wb_ref, 1, sg, first, tr)
        sig = _sigmoid(gate)
        dav = da_ref[...].astype(f32)
        d_up = dav * (gate * sig)
        d_gate = dav * up * (sig * (1.0 + gate * (1.0 - sig)))

        @pl.when(step == 0)
        def _():
            eu[tr:tr + FFN_HALO, :] = jnp.zeros((FFN_HALO, c), f32)
            eg[tr:tr + FFN_HALO, :] = jnp.zeros((FFN_HALO, c), f32)
            dwb_ref[...] = jnp.zeros(dwb_ref.shape, f32)

        for half, (dv, ext, xs) in enumerate(((d_up, eu, su), (d_gate, eg, sg))):
            ext[0:tr, :] = dv
            acc = jnp.zeros((tr, c), f32)
            for k in range(FFN_CONV):
                acc = acc + wb_ref[half, k:k + 1, :] * ext[pl.ds(FFN_CONV - 1 - k, tr), :]
                dwb_ref[half, k:k + 1, :] += jnp.sum(dv * xs[pl.ds(FFN_HALO - (FFN_CONV - 1) + k, tr), :], axis=0, keepdims=True)
            dwb_ref[half, FFN_CONV:FFN_CONV + 1, :] += jnp.sum(dv, axis=0, keepdims=True)
            du_ref[half] = acc.astype(du_ref.dtype)
            ext[tr:tr + FFN_HALO, :] = ext[0:FFN_HALO, :]

    rev = lambda i: nt - 1 - i
    return pl.pallas_call(
        body, grid=(nbk, nt),
        in_specs=[pl.BlockSpec((None, tr, c), lambda p, i: (p, rev(i), 0)),
                  pl.BlockSpec((2, None, tr, c), lambda p, i: (0, p, rev(i), 0)),
                  pl.BlockSpec((2, None, FFN_HALO, c), lambda p, i: (0, p, jnp.maximum(rev(i) * (tr // FFN_HALO) - 1, 0), 0)),
                  pl.BlockSpec((None, 2, 8, c), lambda p, i: (p, 0, 0, 0))],
        out_specs=[pl.BlockSpec((2, None, tr, c), lambda p, i: (0, p, rev(i), 0)),
                   pl.BlockSpec((None, 2, 8, c), lambda p, i: (p, 0, 0, 0))],
        out_shape=[SDS((2, nbk, s, c), bf16), SDS((nbk, 2, 8, c), f32)],
        scratch_shapes=[pltpu.VMEM((tr + FFN_HALO, c), f32)] * 4,
        compiler_params=_params("parallel", "arbitrary"), name=name)(da, u, u, wb)


def _glu(zv, c):
    return zv[:, :c] * _sigmoid(zv[:, c:])


def _conv_fill(z_ref, h_ref, scr, first, tr, c):
    scr[0:CONV_HALO, :] = jnp.where(first, 0.0, _glu(h_ref[...], c))
    scr[CONV_HALO:CONV_HALO + tr, :] = _glu(z_ref[...], c)


def _layernorm_parts(cv):
    mu = jnp.mean(cv, axis=-1, keepdims=True)
    cen = cv - mu
    rstd = lax.rsqrt(jnp.mean(cen * cen, axis=-1, keepdims=True) + EPS)
    return cen * rstd, rstd


def conv_module_fwd(z, wdw, vecs, name):
    s, c2 = z.shape
    c = c2 // 2
    tr = min(s, 256)

    def body(z_ref, h_ref, w_ref, v_ref, c_ref, s_ref, scr):
        _conv_fill(z_ref, h_ref, scr, pl.program_id(0) == 0, tr, c)
        acc = jnp.broadcast_to(v_ref[0:1, :], (tr, c))
        for j in range(CONV_KERNEL):
            acc = acc + w_ref[j:j + 1, :] * scr[pl.ds(CONV_HALO - (CONV_KERNEL - 1) + j, tr), :]
        c_ref[...] = acc
        chat, _ = _layernorm_parts(acc)
        ln = chat * v_ref[1:2, :] + v_ref[2:3, :]
        s_ref[...] = (ln * _sigmoid(ln)).astype(s_ref.dtype)

    return pl.pallas_call(
        body, grid=(s // tr,),
        in_specs=[_rows(tr, c2), pl.BlockSpec((CONV_HALO, c2), lambda i: (jnp.maximum(i * (tr // CONV_HALO) - 1, 0), 0)),
                  _fixed(CONV_HALO, c), _fixed(8, c)],
        out_specs=[_rows(tr, c), _rows(tr, c)], out_shape=[SDS((s, c), f32), SDS((s, c), bf16)],
        scratch_shapes=[pltpu.VMEM((tr + CONV_HALO, c), f32)],
        compiler_params=_params("arbitrary"), name=name)(z, z, wdw, vecs)


def conv_module_bwd(ds, cpre, z, wdw, vecs, name):
    s, c2 = z.shape
    c = c2 // 2
    tr = min(s, 256)
    nt = s // tr

    def body(ds_ref, c_ref, z_ref, h_ref, w_ref, v_ref, dz_ref, dw_ref, dv_ref, db_ref, scr, ext):
        step = pl.program_id(0)
        _conv_fill(z_ref, h_ref, scr, step == nt - 1, tr, c)
        chat, rstd = _layernorm_parts(c_ref[...])
        gain = v_ref[1:2, :]
        ln = chat * gain + v_ref[2:3, :]
        sig = _sigmoid(ln)
        dln = ds_ref[...].astype(f32) * (sig * (1.0 + ln * (1.0 - sig)))
        gy = dln * gain
        dc = rstd * (gy - jnp.mean(gy, axis=-1, keepdims=True) - chat * jnp.mean(gy * chat, axis=-1, keepdims=True))

        @pl.when(step == 0)
        def _():
            ext[tr:tr + CONV_HALO, :] = jnp.zeros((CONV_HALO, c), f32)
            dw_ref[...] = jnp.zeros(dw_ref.shape, f32)
            dv_ref[...] = jnp.zeros(dv_ref.shape, f32)
            db_ref[...] = jnp.zeros(db_ref.shape, f32)

        dv_ref[0:1, :] += jnp.sum(dc, axis=0, keepdims=True)
        dv_ref[1:2, :] += jnp.sum(dln * chat, axis=0, keepdims=True)
        dv_ref[2:3, :] += jnp.sum(dln, axis=0, keepdims=True)
        ext[0:tr, :] = dc
        du = jnp.zeros((tr, c), f32)
        for j in range(CONV_KERNEL):
            du = du + w_ref[j:j + 1, :] * ext[pl.ds(CONV_KERNEL - 1 - j, tr), :]
            dw_ref[j:j + 1, :] += jnp.sum(dc * scr[pl.ds(CONV_HALO - (CONV_KERNEL - 1) + j, tr), :], axis=0, keepdims=True)
        ext[tr:tr + CONV_HALO, :] = ext[0:CONV_HALO, :]
        zv = z_ref[...]
        a, sg = zv[:, :c], _sigmoid(zv[:, c:])
        da = du * sg
        dg = du * a * (sg * (1.0 - sg))
        dz_ref[:, :c] = da.astype(dz_ref.dtype)
        dz_ref[:, c:] = dg.astype(dz_ref.dtype)
        db_ref[:, :c] += jnp.sum(da, axis=0, keepdims=True)
        db_ref[:, c:] += jnp.sum(dg, axis=0, keepdims=True)

    rev = lambda i: nt - 1 - i
    back = lambda d: pl.BlockSpec((tr, d), lambda i: (rev(i), 0))
    return pl.pallas_call(
        body, grid=(nt,),
        in_specs=[back(c), back(c), back(c2),
                  pl.BlockSpec((CONV_HALO, c2), lambda i: (jnp.maximum(rev(i) * (tr // CONV_HALO) - 1, 0), 0)),
                  _fixed(CONV_HALO, c), _fixed(8, c)],
        out_specs=[back(c2), _fixed(CONV_HALO, c), _fixed(8, c), _fixed(1, c2)],
        out_shape=[SDS((s, c2), bf16), SDS((CONV_HALO, c), f32), SDS((8, c), f32), SDS((1, c2), f32)],
        scratch_shapes=[pltpu.VMEM((tr + CONV_HALO, c), f32)] * 2,
        compiler_params=_params("arbitrary"), name=name)(ds, cpre, z, z, wdw, vecs)


def _tile2d(r, n):
    tn = n if n <= 2048 else 1024
    tr = r
    while tr * tn * 4 > (1 << 21) and tr % 16 == 0:
        tr //= 2
    assert r % tr == 0 and n % tn == 0
    return tr, tn


def adamw(w, g, m, v, name):
    r, n = w.shape
    tr, tn = _tile2d(r, n)

    def body(w_ref, g_ref, m_ref, v_ref, d_ref, nm_ref, nv_ref):
        gv = g_ref[...]
        nm = ADAM_B1 * m_ref[...] + (1.0 - ADAM_B1) * gv
        nv = ADAM_B2 * v_ref[...] + (1.0 - ADAM_B2) * (gv * gv)
        m_hat = nm / (1.0 - ADAM_B1 ** ADAM_STEP)
        v_hat = nv / (1.0 - ADAM_B2 ** ADAM_STEP)
        d_ref[...] = -ADAM_LR * (m_hat / (jnp.sqrt(v_hat) + ADAM_EPS) + ADAM_WD * w_ref[...])
        nm_ref[...] = nm
        nv_ref[...] = nv

    blk = pl.BlockSpec((tr, tn), lambda i, j: (i, j))
    return pl.pallas_call(body, grid=(r // tr, n // tn), in_specs=[blk] * 4, out_specs=[blk] * 3,
                          out_shape=[SDS((r, n), f32)] * 3, compiler_params=_params("parallel", "parallel"),
                          name=name)(w, g, m, v)


def add_core_halves(grad, got, core, name):
    _, _, rh, n = grad.shape
    tr, tn = _tile2d(rh, n)

    def body(c_ref, a_ref, b_ref, o_ref):
        o_ref[...] = a_ref[...] + b_ref[...]

    return pl.pallas_call(
        body,
        grid_spec=pltpu.PrefetchScalarGridSpec(
            num_scalar_prefetch=1, grid=(N_CHIPS, rh // tr, n // tn),
            in_specs=[pl.BlockSpec((None, None, tr, tn), lambda p, i, j, c_ref: (p, c_ref[0], i, j)),
                      pl.BlockSpec((None, tr, tn), lambda p, i, j, c_ref: (p, i, j))],
            out_specs=pl.BlockSpec((None, tr, tn), lambda p, i, j, c_ref: (p, i, j))),
        out_shape=SDS((N_CHIPS, rh, n), f32), compiler_params=_params("parallel", "parallel", "parallel"),
        name=name)(core, grad, got)


def add_chip_parts(part, got, chip, name):
    _, rh, n = part.shape
    tr, tn = _tile2d(rh, n)

    def body(c_ref, a_ref, b_ref, o_ref):
        o_ref[...] = ((a_ref[...] + b_ref[0]) + b_ref[1]) + b_ref[2]

    return pl.pallas_call(
        body,
        grid_spec=pltpu.PrefetchScalarGridSpec(
            num_scalar_prefetch=1, grid=(rh // tr, n // tn),
            in_specs=[pl.BlockSpec((None, tr, tn), lambda i, j, c_ref: (c_ref[0], i, j)),
                      pl.BlockSpec((N_CHIPS - 1, tr, tn), lambda i, j, c_ref: (0, i, j))],
            out_specs=pl.BlockSpec((tr, tn), lambda i, j, c_ref: (i, j))),
        out_shape=SDS((rh, n), f32), compiler_params=_params("parallel", "parallel"), name=name)(chip, part, got)


def sum_devices(parts, name):
    nd, r, n = parts.shape

    def body(p_ref, o_ref):
        acc = p_ref[0]
        for k in range(1, nd):
            acc = acc + p_ref[k]
        o_ref[...] = acc

    return pl.pallas_call(body, out_shape=SDS((r, n), f32), name=name)(parts)


def _position():
    return lax.axis_index("x"), lax.axis_index("y"), lax.axis_index("c")


def _other_chips(x, y):
    return [(1 - x, y), (x, 1 - y), (1 - x, 1 - y)]


def _remote(src, dst, send, recv, to):
    return pltpu.make_async_remote_copy(src_ref=src, dst_ref=dst, send_sem=send, recv_sem=recv, device_id=to,
                                        device_id_type=MESH)


def gather_chips(shards, name):
    n = len(shards)

    def body(*refs):
        ins, outs = refs[:n], refs[n:2 * n]
        send, recv, local = refs[2 * n:]
        x, y, c = _position()
        me = 2 * x + y
        pending = []
        for t in range(n):
            own = pltpu.make_async_copy(ins[t], outs[t].at[me], local.at[t])
            own.start()
            pending.append(own.wait)
            for k, (px, py) in enumerate(_other_chips(x, y)):
                out = _remote(ins[t], outs[t].at[me], send.at[t, k], recv.at[t, k], (px, py, c))
                out.start()
                pending.append(out.wait_send)
                pending.append(_remote(ins[t], outs[t].at[2 * px + py], send.at[t, k], recv.at[t, k], (px, py, c)).wait_recv)
        for wait in pending:
            wait()

    return pl.pallas_call(
        body, in_specs=[ANY] * n, out_specs=[ANY] * n,
        out_shape=[SDS((N_CHIPS,) + a.shape, a.dtype) for a in shards],
        scratch_shapes=[pltpu.SemaphoreType.DMA((n, N_CHIPS - 1)), pltpu.SemaphoreType.DMA((n, N_CHIPS - 1)),
                        pltpu.SemaphoreType.DMA((n,))],
        name=name)(*shards)


def swap_core_halves(grads, name):
    n = len(grads)

    def body(*refs):
        ins, outs = refs[:n], refs[n:2 * n]
        send, recv = refs[2 * n:]
        x, y, c = _position()
        pending = []
        for t in range(n):
            out = _remote(ins[t].at[:, 1 - c], outs[t], send.at[t], recv.at[t], (x, y, 1 - c))
            out.start()
            pending.append(out.wait)
        for wait in pending:
            wait()

    return pl.pallas_call(
        body, in_specs=[ANY] * n, out_specs=[ANY] * n,
        out_shape=[SDS((a.shape[0],) + a.shape[2:], a.dtype) for a in grads],
        scratch_shapes=[pltpu.SemaphoreType.DMA((n,)), pltpu.SemaphoreType.DMA((n,))],
        name=name)(*grads)


def scatter_chips(parts, name):
    n = len(parts)

    def body(*refs):
        ins, outs = refs[:n], refs[n:2 * n]
        send, recv = refs[2 * n:]
        x, y, c = _position()
        pending = []
        for t in range(n):
            for k, (px, py) in enumerate(_other_chips(x, y)):
                out = _remote(ins[t].at[2 * px + py], outs[t].at[k], send.at[t, k], recv.at[t, k], (px, py, c))
                out.start()
                pending.append(out.wait)
        for wait in pending:
            wait()

    return pl.pallas_call(
        body, in_specs=[ANY] * n, out_specs=[ANY] * n,
        out_shape=[SDS((N_CHIPS - 1,) + a.shape[1:], a.dtype) for a in parts],
        scratch_shapes=[pltpu.SemaphoreType.DMA((n, N_CHIPS - 1)), pltpu.SemaphoreType.DMA((n, N_CHIPS - 1))],
        name=name)(*parts)


def share_core_halves(halves, name):
    n = len(halves)

    def body(*refs):
        ins, outs = refs[:n], refs[n:2 * n]
        send, recv, local = refs[2 * n:]
        x, y, c = _position()
        pending = []
        for t in range(n):
            own = pltpu.make_async_copy(ins[t], outs[t].at[c], local.at[t])
            own.start()
            pending.append(own.wait)
            out = _remote(ins[t], outs[t].at[c], send.at[t], recv.at[t], (x, y, 1 - c))
            out.start()
            pending.append(out.wait_send)
            pending.append(_remote(ins[t], outs[t].at[1 - c], send.at[t], recv.at[t], (x, y, 1 - c)).wait_recv)
        for wait in pending:
            wait()

    return pl.pallas_call(
        body, in_specs=[ANY] * n, out_specs=[ANY] * n,
        out_shape=[SDS((2,) + a.shape, a.dtype) for a in halves],
        scratch_shapes=[pltpu.SemaphoreType.DMA((n,)), pltpu.SemaphoreType.DMA((n,)), pltpu.SemaphoreType.DMA((n,))],
        name=name)(*halves)


def gather_devices(v, name):
    flips = [(dx, dy, dc) for dx in (0, 1) for dy in (0, 1) for dc in (0, 1)][1:]

    def body(v_ref, o_ref, send, recv, local):
        x, y, c = _position()
        me = 4 * x + 2 * y + c
        own = pltpu.make_async_copy(v_ref, o_ref.at[me], local)
        own.start()
        pending = [own.wait]
        for k, (dx, dy, dc) in enumerate(flips):
            px, py, pc = x ^ dx, y ^ dy, c ^ dc
            out = _remote(v_ref, o_ref.at[me], send.at[k], recv.at[k], (px, py, pc))
            out.start()
            pending.append(out.wait_send)
            pending.append(_remote(v_ref, o_ref.at[4 * px + 2 * py + pc], send.at[k], recv.at[k], (px, py, pc)).wait_recv)
        for wait in pending:
            wait()

    return pl.pallas_call(
        body, in_specs=[ANY], out_specs=ANY, out_shape=SDS((8,) + v.shape, v.dtype),
        scratch_shapes=[pltpu.SemaphoreType.DMA((7,)), pltpu.SemaphoreType.DMA((7,)), pltpu.SemaphoreType.DMA],
        name=name)(v)


SMALL = ("norm_g", "conv_b_pw1", "conv_w_dw", "conv_b_dw", "conv_ln_g", "conv_ln_b", "conv_b_pw2", "ffn_w_dw")


def _pack_rows(arrs, rows):
    flat = jnp.concatenate([a.reshape(-1, LANES) for a in arrs], axis=0)
    return jnp.pad(flat, ((0, rows - flat.shape[0]), (0, 0)))


def _unpack_rows(packed, shapes):
    out, at = [], 0
    for shp in shapes:
        size = 1
        for dim in shp:
            size *= dim
        rows = size // LANES
        out.append(packed[..., at:at + rows, :].reshape(packed.shape[:-2] + tuple(shp)))
        at += rows
    return out


def _join_last(t):
    t = jnp.moveaxis(t, 0, -2)
    return t.reshape(t.shape[:-2] + (t.shape[-2] * t.shape[-1],))


def _split_last(t):
    t = t.reshape(t.shape[:-1] + (N_CHIPS, t.shape[-1] // N_CHIPS))
    return jnp.moveaxis(t, -2, 0)


def _rope_tables(positions):
    half = ROT_DIM // 2
    inv_freq = ROPE_THETA ** (-jnp.arange(half, dtype=f32) / half)
    ang = positions.astype(f32).reshape(-1, 1) * inv_freq
    cos, sin = jnp.cos(ang), jnp.sin(ang)
    s = ang.shape[0]
    rest = HEAD_DIM - ROT_DIM
    head = lambda lo, hi, fill: jnp.concatenate([lo, hi, jnp.full((s, rest), fill, f32)], axis=1)
    zero = jnp.zeros((s, half), f32)
    twice = lambda t: jnp.concatenate([t] * (LANES // HEAD_DIM), axis=1)
    return twice(head(cos, cos, 1.0)), twice(head(-sin, zero, 0.0)), twice(head(zero, sin, 0.0))


def _pad_rows(t, rows):
    return jnp.pad(t, ((0, rows - t.shape[0]), (0, 0)))


def _ffn_pack(w_dw, b_dw):
    t = jnp.concatenate([w_dw, b_dw[None]], axis=0)
    t = t.reshape(FFN_CONV + 1, 2, 2, -1)
    t = jnp.transpose(t, (2, 1, 0, 3))
    return jnp.pad(t, ((0, 0), (0, 0), (0, 8 - (FFN_CONV + 1)), (0, 0)))


def _ffn_unpack(d):
    t = jnp.transpose(d[:, :, :FFN_CONV + 1], (2, 1, 0, 3)).reshape(FFN_CONV + 1, -1)
    return t[:FFN_CONV], t[FFN_CONV]


def _ffn_block(x, g_pre, g_post, w_up, li, w_down, wb, tag):
    s = x.shape[0]
    h = prenorm(x, g_pre, f"{tag}_prenorm")
    u = mm_nn(h[None], w_up, li, None, N_CHIPS, f32, 512, w_up.shape[-1], f"{tag}_up")
    u4 = u.reshape(2, 2, s, u.shape[-1])
    a = ffn_act_fwd(u4, wb, f"{tag}_act")
    y = mm_nn(a, w_down, 0, None, 1, f32, 512, 512, f"{tag}_down")[0]
    return y, (h, u4, a)


def _ffn_block_bwd(dy, saved, w_up, li, w_down, wb, tag):
    h, u4, a = saved
    d_down = mm_tn(a, dy[None], 1, 1024, a.shape[-1], 512, f"{tag}_dwdown")
    da = mm_nt(dy[None], w_down, 0, 2, f32, 1024, a.shape[-1], w_down.shape[-1], f"{tag}_da")
    du4, dwb = ffn_act_bwd(da, u4, wb, f"{tag}_actbwd")
    du = du4.reshape((N_CHIPS,) + du4.shape[2:])
    d_up = mm_tn(h[None], du, N_CHIPS, 1024, h.shape[-1], du.shape[-1], f"{tag}_dwup")
    dh = mm_nt(du, w_up, li, 1, f32, 1024, h.shape[-1], du.shape[-1], f"{tag}_dh")[0]
    return dh, d_up, d_down, dwb


def kernel(x, positions, norm_g, attn_w_qkv, attn_w_o, conv_w_pw1, conv_b_pw1, conv_w_dw, conv_b_dw, conv_ln_g, conv_ln_b, conv_w_pw2, conv_b_pw2, ffn_w_up, ffn_w_dw, ffn_b_dw, ffn_w_down, loss_target, m_norm_g, m_attn_w_qkv, m_attn_w_o, m_conv_w_pw1, m_conv_b_pw1, m_conv_w_dw, m_conv_b_dw, m_conv_ln_g, m_conv_ln_b, m_conv_w_pw2, m_conv_b_pw2, m_ffn_w_up, m_ffn_w_dw, m_ffn_b_dw, m_ffn_w_down, v_norm_g, v_attn_w_qkv, v_attn_w_o, v_conv_w_pw1, v_conv_b_pw1, v_conv_w_dw, v_conv_b_dw, v_conv_ln_g, v_conv_ln_b, v_conv_w_pw2, v_conv_b_pw2, v_ffn_w_up, v_ffn_w_dw, v_ffn_b_dw, v_ffn_w_down):
    weights = dict(norm_g=norm_g, attn_w_qkv=attn_w_qkv, attn_w_o=attn_w_o, conv_w_pw1=conv_w_pw1, conv_b_pw1=conv_b_pw1,
                   conv_w_dw=conv_w_dw, conv_b_dw=conv_b_dw, conv_ln_g=conv_ln_g, conv_ln_b=conv_ln_b, conv_w_pw2=conv_w_pw2,
                   conv_b_pw2=conv_b_pw2, ffn_w_up=ffn_w_up, ffn_w_dw=ffn_w_dw, ffn_b_dw=ffn_b_dw, ffn_w_down=ffn_w_down)
    mom1 = dict(norm_g=m_norm_g, attn_w_qkv=m_attn_w_qkv, attn_w_o=m_attn_w_o, conv_w_pw1=m_conv_w_pw1, conv_b_pw1=m_conv_b_pw1,
                conv_w_dw=m_conv_w_dw, conv_b_dw=m_conv_b_dw, conv_ln_g=m_conv_ln_g, conv_ln_b=m_conv_ln_b, conv_w_pw2=m_conv_w_pw2,
                conv_b_pw2=m_conv_b_pw2, ffn_w_up=m_ffn_w_up, ffn_w_dw=m_ffn_w_dw, ffn_b_dw=m_ffn_b_dw, ffn_w_down=m_ffn_w_down)
    mom2 = dict(norm_g=v_norm_g, attn_w_qkv=v_attn_w_qkv, attn_w_o=v_attn_w_o, conv_w_pw1=v_conv_w_pw1, conv_b_pw1=v_conv_b_pw1,
                conv_w_dw=v_conv_w_dw, conv_b_dw=v_conv_b_dw, conv_ln_g=v_conv_ln_g, conv_ln_b=v_conv_ln_b, conv_w_pw2=v_conv_w_pw2,
                conv_b_pw2=v_conv_b_pw2, ffn_w_up=v_ffn_w_up, ffn_w_dw=v_ffn_w_dw, ffn_b_dw=v_ffn_b_dw, ffn_w_down=v_ffn_w_down)
    big = ("attn_w_qkv", "attn_w_o", "conv_w_pw1", "conv_w_pw2", "ffn_w_up", "ffn_w_down")
    xi, yi, ci = _position()
    core = ci.astype(jnp.int32).reshape(1)
    chip = (2 * xi + yi).astype(jnp.int32).reshape(1)

    x = x[0]
    target = loss_target[0]
    s, d = x.shape

    small_shapes = [weights[k].shape for k in SMALL]
    small_rows = -(-sum(weights[k].size for k in SMALL) // LANES // 8) * 8
    small_w = _pack_rows([weights[k] for k in SMALL], small_rows)
    gathered = gather_chips([weights[k].astype(bf16) for k in big] + [small_w], "gather_weights")
    gw = dict(zip(big, gathered[:-1]))
    full_small = dict(zip(SMALL, [_join_last(t) for t in _unpack_rows(gathered[-1], small_shapes)]))
    w_qkv, w_o, w_pw1, w_up = gw["attn_w_qkv"], gw["attn_w_o"], gw["conv_w_pw1"], gw["ffn_w_up"]
    w_pw2 = gw["conv_w_pw2"].reshape(1, 1, -1, d)
    w_down = [gw["ffn_w_down"][:, i].reshape(1, 1, -1, d) for i in range(2)]
    gains = full_small["norm_g"]
    gain = lambda i, k: gains[i, k][None]
    b_pw1 = full_small["conv_b_pw1"]
    conv_wdw = _pad_rows(full_small["conv_w_dw"][0], CONV_HALO)
    conv_vecs = _pad_rows(jnp.concatenate([full_small["conv_b_dw"], full_small["conv_ln_g"], full_small["conv_ln_b"]], axis=0), 8)
    b_pw2 = full_small["conv_b_pw2"]
    wbs = [_ffn_pack(full_small["ffn_w_dw"][i], ffn_b_dw[i]) for i in range(2)]
    tabs = _rope_tables(positions[0])
    tabs_t = (tabs[0], -tabs[1], -tabs[2])
    head_ones = (jnp.arange(GROUP_WIDTH)[:, None] // HEAD_DIM == jnp.arange(GROUP_WIDTH)[None, :] // HEAD_DIM).astype(bf16)

    h0 = prenorm(x, gain(0, 0), "l0_prenorm")
    qkv = mm_nn(h0[None], w_qkv, 0, None, 1, f32, 512, w_qkv.shape[-1], "qkv")[0]
    qkvr = rope_qkv(qkv, tabs, "rope")
    att = [attn_fwd(qkvr, g, dil, f"attn_fwd{g}") for g, (_, dil) in enumerate(DILATED_GROUPS)]
    os_, lses = [a[0] for a in att], [a[1] for a in att]
    mixed = mix_fwd(os_, lses, "mix")
    y0 = mm_nn(mixed[None], w_o, 0, None, 1, f32, 1024, w_o.shape[-1], "attn_out")[0]
    x1 = postnorm_residual(x, y0, gain(0, 1), "l0_postnorm")
    y1, ffn0 = _ffn_block(x1, gain(0, 2), gain(0, 3), w_up, 0, w_down[0], wbs[0], "ffn0")
    x2 = postnorm_residual(x1, y1, gain(0, 3), "l0_ffn_postnorm")

    h2 = prenorm(x2, gain(1, 0), "l1_prenorm")
    z = mm_nn(h2[None], w_pw1, 0, b_pw1, 1, f32, 512, w_pw1.shape[-1], "pw1")[0]
    cpre, sw = conv_module_fwd(z, conv_wdw, conv_vecs, "conv_fwd")
    y2 = mm_nn(sw[None], w_pw2, 0, b_pw2, 1, f32, 512, 512, "pw2")[0]
    x3 = postnorm_residual(x2, y2, gain(1, 1), "l1_postnorm")
    y3, ffn1 = _ffn_block(x3, gain(1, 2), gain(1, 3), w_up, 1, w_down[1], wbs[1], "ffn1")
    dx4, loss_row = final_loss(x3, y3, gain(1, 3), target, "loss")
    loss = lax.psum(loss_row[0, 0], ("x", "y", "c"))

    dgain = [[None] * 4 for _ in range(2)]
    dy3, dgain[1][3], _ = norm_bwd(y3, gain(1, 3), dx4, None, bf16, "l1_ffn_postnorm_bwd")
    dh, d_up1, d_down1, dwb1 = _ffn_block_bwd(dy3, ffn1, w_up, 1, w_down[1], wbs[1], "ffn1")
    dx3, dgain[1][2], _ = norm_bwd(x3, gain(1, 2), dh, dx4, f32, "l1_ffn_prenorm_bwd")

    dy2, dgain[1][1], d_b_pw2 = norm_bwd(y2, gain(1, 1), dx3, None, bf16, "l1_postnorm_bwd")
    d_pw2 = mm_tn(sw[None], dy2[None], 1, 1024, d, 512, "dw_pw2")
    dsw = mm_nt(dy2[None], w_pw2, 0, 1, f32, 1024, d, d, "d_swish")[0]
    dz, d_conv_wdw, d_conv_vecs, d_b_pw1 = conv_module_bwd(dsw, cpre, z, conv_wdw, conv_vecs, "conv_bwd")
    d_pw1 = mm_tn(h2[None], dz[None], N_CHIPS, 1024, d, w_pw1.shape[-1], "dw_pw1")
    dh = mm_nt(dz[None], w_pw1, 0, 1, f32, 1024, d, w_pw1.shape[-1], "d_h2")[0]
    dx2, dgain[1][0], _ = norm_bwd(x2, gain(1, 0), dh, dx3, f32, "l1_prenorm_bwd")

    dy1, dgain[0][3], _ = norm_bwd(y1, gain(0, 3), dx2, None, bf16, "l0_ffn_postnorm_bwd")
    dh, d_up0, d_down0, dwb0 = _ffn_block_bwd(dy1, ffn0, w_up, 0, w_down[0], wbs[0], "ffn0")
    dx1, dgain[0][2], _ = norm_bwd(x1, gain(0, 2), dh, dx2, f32, "l0_ffn_prenorm_bwd")

    dy0, dgain[0][1], _ = norm_bwd(y0, gain(0, 1), dx1, None, bf16, "l0_postnorm_bwd")
    d_wo = mm_tn(mixed[None], dy0[None], N_CHIPS, 1024, GROUP_WIDTH, w_o.shape[-1], "dw_o")
    dmixed = mm_nt(dy0[None], w_o, 0, 1, f32, 1024, GROUP_WIDTH, w_o.shape[-1], "d_mixed")[0]
    mb = mix_bwd(dmixed, os_, lses, head_ones, "mix_bwd")
    parts = [attn_bwd(qkvr, mb[g], lses[g], mb[3 + g], g, dil, f"attn_bwd{g}") for g, (_, dil) in enumerate(DILATED_GROUPS)]
    dqkv = dqkv_assemble(parts, tabs_t, "dqkv")
    d_qkv = mm_tn(h0[None], dqkv[None], N_CHIPS, 1024, d, w_qkv.shape[-1] // 3, "dw_qkv")
    dh = mm_nt(dqkv[None], w_qkv, 0, 1, f32, 1024, d, w_qkv.shape[-1] // 3, "d_h0")[0]
    grad_x, dgain[0][0], _ = norm_bwd(x, gain(0, 0), dh, dx1, f32, "l0_prenorm_bwd")

    stack_layers = lambda a, b: jnp.stack([a, b], axis=1)
    gbig = dict(
        attn_w_qkv=d_qkv, attn_w_o=d_wo, conv_w_pw1=d_pw1, conv_w_pw2=d_pw2[0].reshape(N_CHIPS, -1, d),
        ffn_w_up=stack_layers(d_up0, d_up1),
        ffn_w_down=stack_layers(d_down0[0].reshape(N_CHIPS, -1, d), d_down1[0].reshape(N_CHIPS, -1, d)))
    d_ffn_dw, d_ffn_b = zip(*[_ffn_unpack(t) for t in (dwb0, dwb1)])
    gsmall = dict(
        norm_g=jnp.stack([jnp.concatenate(row, axis=0) for row in dgain], axis=0),
        conv_b_pw1=d_b_pw1, conv_w_dw=d_conv_wdw[None, :CONV_KERNEL], conv_b_dw=d_conv_vecs[0:1], conv_ln_g=d_conv_vecs[1:2],
        conv_ln_b=d_conv_vecs[2:3], conv_b_pw2=d_b_pw2, ffn_w_dw=jnp.stack(d_ffn_dw, axis=0))
    bias_rows = ffn_b_dw.size // LANES
    small_g = jnp.concatenate([jnp.concatenate([_pack_rows([_split_last(gsmall[k])[p] for k in SMALL], small_rows)
                                                for p in range(N_CHIPS)], axis=0),
                               jnp.stack(d_ffn_b, axis=0).reshape(bias_rows, LANES)], axis=0)

    small_sum = sum_devices(gather_devices(small_g, "gather_small_grads"), "sum_small_grads")
    my_small = lax.dynamic_slice_in_dim(small_sum, chip[0] * small_rows, small_rows, axis=0)
    g_small = jnp.concatenate([my_small, small_sum[N_CHIPS * small_rows:]], axis=0)

    halves = [gbig[k].reshape(N_CHIPS, 2, -1, gbig[k].shape[-1]) for k in big]
    from_core = swap_core_halves(halves, "swap_core_halves")
    chip_parts = [add_core_halves(a, b, core, f"add_core_{k}") for k, a, b in zip(big, halves, from_core)]
    from_chips = scatter_chips(chip_parts, "scatter_chips")
    mine = [add_chip_parts(a, b, chip, f"add_chips_{k}") for k, a, b in zip(big, chip_parts, from_chips)]
    shard_grads = share_core_halves(mine, "share_core_halves")

    grads, deltas, new_m, new_v = {}, {}, {}, {}
    for k, g2 in zip(big, shard_grads):
        shp = weights[k].shape
        g2 = g2.reshape(-1, shp[-1])
        dl, nm, nv = adamw(weights[k].reshape(g2.shape), g2, mom1[k].reshape(g2.shape), mom2[k].reshape(g2.shape), f"adamw_{k}")
        grads[k], deltas[k], new_m[k], new_v[k] = (t.reshape(shp) for t in (g2, dl, nm, nv))
    pack_state = lambda src: jnp.concatenate([_pack_rows([src[k] for k in SMALL], small_rows), src["ffn_b_dw"].reshape(bias_rows, LANES)], axis=0)
    small_out = (g_small,) + tuple(adamw(pack_state(weights), g_small, pack_state(mom1), pack_state(mom2), "adamw_small"))
    for dst, packed in zip((grads, deltas, new_m, new_v), small_out):
        for k, t in zip(SMALL, _unpack_rows(packed[:small_rows], small_shapes)):
            dst[k] = t
        dst["ffn_b_dw"] = packed[small_rows:].reshape(ffn_b_dw.shape)

    order = ("norm_g", "attn_w_qkv", "attn_w_o", "conv_w_pw1", "conv_b_pw1", "conv_w_dw", "conv_b_dw", "conv_ln_g", "conv_ln_b",
             "conv_w_pw2", "conv_b_pw2", "ffn_w_up", "ffn_w_dw", "ffn_b_dw", "ffn_w_down")
    return (loss, grad_x[None], *[grads[k] for k in order], *[deltas[k] for k in order], *[new_m[k] for k in order],
            *[new_v[k] for k in order])
```

```python
import functools

import jax
import jax.numpy as jnp
from jax import lax
from jax.experimental import pallas as pl
from jax.experimental.pallas import tpu as pltpu

f32 = jnp.float32
bf16 = jnp.bfloat16
SDS = jax.ShapeDtypeStruct

EPS = 1e-6
HEAD_DIM = 64
N_SLOTS = 8
GROUP_WIDTH = N_SLOTS * HEAD_DIM
DILATED_GROUPS = ((128, 1), (512, 4), (2048, 16))
N_GROUPS = 3
SPAN = 128
ROT_DIM = HEAD_DIM // 4
ROPE_THETA = 500000.0
CONV_KERNEL = 31
CONV_HALO = 32
FFN_CONV = 3
FFN_HALO = 8
ADAM_LR, ADAM_B1, ADAM_B2, ADAM_EPS, ADAM_WD, ADAM_STEP = 0.001, 0.9, 0.999, 1e-08, 0.01, 10
LANES = 128
N_CHIPS = 4
VMEM_LIMIT_BYTES = 56 * 1024 * 1024
NEG_BIG = -1e30
MESH = pl.DeviceIdType.MESH
ANY = pl.BlockSpec(memory_space=pl.ANY)


def _params(*sem):
    return pltpu.CompilerParams(dimension_semantics=sem, vmem_limit_bytes=VMEM_LIMIT_BYTES)


def _sigmoid(v):
    return 1.0 / (1.0 + jnp.exp(-v))


def _dot_nt(a, b):
    return lax.dot_general(a, b, (((1,), (1,)), ((), ())), preferred_element_type=f32)


def _dot_tn(a, b):
    return lax.dot_general(a, b, (((0,), (0,)), ((), ())), preferred_element_type=f32)


def mm_nn(x, w, li, bias, out_blocks, out_dtype, tm, tn, name):
    nq, m, kq = x.shape
    p, _, k, n = w.shape
    assert k == nq * kq and n % tn == 0 and m % tm == 0
    on = p * n // out_blocks
    assert on % tn == 0
    nj, onj = n // tn, on // tn

    def body(*refs):
        if bias is None:
            x_ref, w_ref, o_ref, *scr = refs
            b_ref = None
        else:
            x_ref, w_ref, b_ref, o_ref, *scr = refs
        part = jnp.dot(x_ref[...], w_ref[...], preferred_element_type=f32)

        def finish(acc):
            if b_ref is not None:
                acc = acc + b_ref[...]
            o_ref[...] = acc.astype(o_ref.dtype)

        if nq == 1:
            finish(part)
        else:
            acc_ref, = scr
            q = pl.program_id(2)

            @pl.when(q == 0)
            def _():
                acc_ref[...] = part

            @pl.when(q > 0)
            def _():
                acc_ref[...] += part

            @pl.when(q == nq - 1)
            def _():
                finish(acc_ref[...])

    in_specs = [pl.BlockSpec((None, tm, kq), lambda j, i, q: (q, i, 0)),
                pl.BlockSpec((None, None, kq, tn), lambda j, i, q: (j // nj, li, q, j % nj))]
    args = [x, w]
    if bias is not None:
        in_specs.append(pl.BlockSpec((1, tn), lambda j, i, q: (0, j)))
        args.append(bias)
    return pl.pallas_call(
        body, grid=(p * nj, m // tm, nq), in_specs=in_specs,
        out_specs=pl.BlockSpec((None, tm, tn), lambda j, i, q: (j // onj, i, j % onj)),
        out_shape=SDS((out_blocks, m, on), out_dtype),
        scratch_shapes=[] if nq == 1 else [pltpu.VMEM((tm, tn), f32)],
        compiler_params=_params("parallel", "parallel", "arbitrary"), name=name)(*args)


def mm_nt(dy, w, li, out_blocks, out_dtype, tm, tk, tn, name):
    ob, m, on = dy.shape
    p, _, k, n = w.shape
    assert ob * on == p * n and n % tn == 0 and on % tn == 0 and k % tk == 0 and m % tm == 0
    kq = k // out_blocks
    assert kq % tk == 0
    nj, onj, kqj = n // tn, on // tn, kq // tk
    nr = p * nj

    def body(dy_ref, w_ref, o_ref, *scr):
        part = _dot_nt(dy_ref[...], w_ref[...])
        if nr == 1:
            o_ref[...] = part.astype(o_ref.dtype)
        else:
            acc_ref, = scr
            r = pl.program_id(2)

            @pl.when(r == 0)
            def _():
                acc_ref[...] = part

            @pl.when(r > 0)
            def _():
                acc_ref[...] += part

            @pl.when(r == nr - 1)
            def _():
                o_ref[...] = acc_ref[...].astype(o_ref.dtype)

    return pl.pallas_call(
        body, grid=(k // tk, m // tm, nr),
        in_specs=[pl.BlockSpec((None, tm, tn), lambda kt, i, r: (r // onj, i, r % onj)),
                  pl.BlockSpec((None, None, tk, tn), lambda kt, i, r: (r // nj, li, kt, r % nj))],
        out_specs=pl.BlockSpec((None, tm, tk), lambda kt, i, r: (kt // kqj, i, kt % kqj)),
        out_shape=SDS((out_blocks, m, kq), out_dtype),
        scratch_shapes=[] if nr == 1 else [pltpu.VMEM((tm, tk), f32)],
        compiler_params=_params("parallel", "parallel", "arbitrary"), name=name)(dy, w)


def mm_tn(x, dy, p, tm, tk, tn, name):
    nq, m, kq = x.shape
    ob, _, on = dy.shape
    k = nq * kq
    n = ob * on // p
    assert n % tn == 0 and on % tn == 0 and kq % tk == 0 and m % tm == 0
    nj, onj, kqj = n // tn, on // tn, kq // tk

    def body(x_ref, dy_ref, o_ref):
        part = _dot_tn(x_ref[...], dy_ref[...])
        i = pl.program_id(2)

        @pl.when(i == 0)
        def _():
            o_ref[...] = part

        @pl.when(i > 0)
        def _():
            o_ref[...] += part

    return pl.pallas_call(
        body, grid=(k // tk, p * nj, m // tm),
        in_specs=[pl.BlockSpec((None, tm, tk), lambda kt, j, i: (kt // kqj, i, kt % kqj)),
                  pl.BlockSpec((None, tm, tn), lambda kt, j, i: (j // onj, i, j % onj))],
        out_specs=pl.BlockSpec((None, tk, tn), lambda kt, j, i: (j // nj, kt, j % nj)),
        out_shape=SDS((p, k, n), f32),
        compiler_params=_params("parallel", "parallel", "arbitrary"), name=name)(x, dy)


def _row_tile(s):
    return min(s, 512)


def _rows(tr, d):
    return pl.BlockSpec((tr, d), lambda i: (i, 0))


def _fixed(r, d):
    return pl.BlockSpec((r, d), lambda i: (0, 0))


def _rms(xv):
    return lax.rsqrt(jnp.mean(xv * xv, axis=-1, keepdims=True) + EPS)


def prenorm(x, g, name):
    s, d = x.shape
    tr = _row_tile(s)

    def body(x_ref, g_ref, o_ref):
        xv = x_ref[...]
        o_ref[...] = (xv * _rms(xv) * g_ref[...]).astype(o_ref.dtype)

    return pl.pallas_call(body, grid=(s // tr,), in_specs=[_rows(tr, d), _fixed(1, d)], out_specs=_rows(tr, d),
                          out_shape=SDS((s, d), bf16), compiler_params=_params("parallel"), name=name)(x, g)


def postnorm_residual(x, y, g, name):
    s, d = x.shape
    tr = _row_tile(s)

    def body(x_ref, y_ref, g_ref, o_ref):
        yv = y_ref[...]
        o_ref[...] = x_ref[...] + yv * _rms(yv) * g_ref[...]

    return pl.pallas_call(body, grid=(s // tr,), in_specs=[_rows(tr, d), _rows(tr, d), _fixed(1, d)],
                          out_specs=_rows(tr, d), out_shape=SDS((s, d), f32), compiler_params=_params("parallel"),
                          name=name)(x, y, g)


def norm_bwd(xin, g, dout, res, out_dtype, name):
    s, d = xin.shape
    tr = _row_tile(s)

    def body(*refs):
        if res is None:
            x_ref, g_ref, do_ref, dx_ref, dg_ref, cs_ref = refs
            r_ref = None
        else:
            x_ref, g_ref, do_ref, r_ref, dx_ref, dg_ref, cs_ref = refs
        xv = x_ref[...]
        r = _rms(xv)
        xh = xv * r
        dov = do_ref[...].astype(f32)
        gy = dov * g_ref[...]
        dx = r * (gy - xh * jnp.mean(gy * xh, axis=-1, keepdims=True))
        if r_ref is not None:
            dx = dx + r_ref[...]
        dx_ref[...] = dx.astype(dx_ref.dtype)
        dg = jnp.sum(dov * xh, axis=0, keepdims=True)
        cs = jnp.sum(dx, axis=0, keepdims=True)
        i = pl.program_id(0)

        @pl.when(i == 0)
        def _():
            dg_ref[...] = dg
            cs_ref[...] = cs

        @pl.when(i > 0)
        def _():
            dg_ref[...] += dg
            cs_ref[...] += cs

    in_specs = [_rows(tr, d), _fixed(1, d), _rows(tr, d)]
    args = [xin, g, dout]
    if res is not None:
        in_specs.append(_rows(tr, d))
        args.append(res)
    return pl.pallas_call(body, grid=(s // tr,), in_specs=in_specs,
                          out_specs=[_rows(tr, d), _fixed(1, d), _fixed(1, d)],
                          out_shape=[SDS((s, d), out_dtype), SDS((1, d), f32), SDS((1, d), f32)],
                          compiler_params=_params("arbitrary"), name=name)(*args)


def final_loss(x, y, g, target, name):
    s, d = x.shape
    tr = _row_tile(s)
    nt = s // tr

    def body(x_ref, y_ref, g_ref, t_ref, dx_ref, loss_ref, acc_ref):
        yv = y_ref[...]
        diff = x_ref[...] + yv * _rms(yv) * g_ref[...] - t_ref[...]
        dx_ref[...] = diff * (1.0 / d)
        sq = jnp.sum(diff * diff, axis=0, keepdims=True)
        i = pl.program_id(0)

        @pl.when(i == 0)
        def _():
            acc_ref[...] = sq

        @pl.when(i > 0)
        def _():
            acc_ref[...] += sq

        @pl.when(i == nt - 1)
        def _():
            total = jnp.sum(acc_ref[...], axis=1, keepdims=True) * (0.5 / d)
            loss_ref[...] = jnp.broadcast_to(total, (1, LANES))

    return pl.pallas_call(body, grid=(nt,), in_specs=[_rows(tr, d), _rows(tr, d), _fixed(1, d), _rows(tr, d)],
                          out_specs=[_rows(tr, d), _fixed(1, LANES)],
                          out_shape=[SDS((s, d), f32), SDS((1, LANES), f32)],
                          scratch_shapes=[pltpu.VMEM((1, d), f32)],
                          compiler_params=_params("arbitrary"), name=name)(x, y, g, target)


def _rotate_chunk(v, tc, ta, tb):
    return v * tc + pltpu.roll(v, LANES - ROT_DIM // 2, axis=1) * ta + pltpu.roll(v, ROT_DIM // 2, axis=1) * tb


def rope_qkv(qkv, tabs, name):
    s, w = qkv.shape
    tr = min(s, 256)
    n_rot = 2 * N_GROUPS * GROUP_WIDTH // LANES

    def body(x_ref, tc_ref, ta_ref, tb_ref, o_ref):
        tc, ta, tb = tc_ref[...], ta_ref[...], tb_ref[...]
        for ch in range(w // LANES):
            cols = slice(ch * LANES, (ch + 1) * LANES)
            v = x_ref[:, cols]
            if ch < n_rot:
                v = _rotate_chunk(v, tc, ta, tb)
            o_ref[:, cols] = v.astype(o_ref.dtype)

    tab = _rows(tr, LANES)
    return pl.pallas_call(body, grid=(s // tr,), in_specs=[_rows(tr, w), tab, tab, tab], out_specs=_rows(tr, w),
                          out_shape=SDS((s, w), bf16), compiler_params=_params("parallel"), name=name)(qkv, *tabs)


def _attn_masks(j):
    row = lax.broadcasted_iota(jnp.int32, (SPAN, SPAN), 0)
    col = lax.broadcasted_iota(jnp.int32, (SPAN, SPAN), 1)
    return col <= row, jnp.logical_and(col >= row, j > 0)


def _attn_in_specs(g, n_col_blocks):
    def at(kind, prev):
        def index(r, j):
            return (jnp.maximum(j - 1, 0) if prev else j, r * n_col_blocks + kind * N_GROUPS + g)
        return pl.BlockSpec((SPAN, GROUP_WIDTH), index)
    return [at(0, False), at(1, False), at(1, True), at(2, False), at(2, True)]


def attn_fwd(qkvr, g, dil, name):
    s, w = qkvr.shape
    l = s // dil
    nb = l // SPAN
    a = qkvr.reshape(l, dil * w)

    def body(q_ref, ko_ref, kp_ref, vo_ref, vp_ref, o_ref, lse_ref):
        own, prev = _attn_masks(pl.program_id(1))
        for h in range(N_SLOTS):
            hs = slice(h * HEAD_DIM, (h + 1) * HEAD_DIM)
            q = q_ref[:, hs]
            so = jnp.where(own, _dot_nt(q, ko_ref[:, hs]) * (HEAD_DIM ** -0.5), NEG_BIG)
            sp = jnp.where(prev, _dot_nt(q, kp_ref[:, hs]) * (HEAD_DIM ** -0.5), NEG_BIG)
            mx = jnp.maximum(jnp.max(so, axis=-1, keepdims=True), jnp.max(sp, axis=-1, keepdims=True))
            po = jnp.exp(so - mx)
            pp = jnp.exp(sp - mx)
            den = jnp.sum(po, axis=-1, keepdims=True) + jnp.sum(pp, axis=-1, keepdims=True)
            acc = jnp.dot(po.astype(bf16), vo_ref[:, hs], preferred_element_type=f32)
            acc = acc + jnp.dot(pp.astype(bf16), vp_ref[:, hs], preferred_element_type=f32)
            o_ref[:, hs] = acc / den
            lse_ref[:, hs] = jnp.broadcast_to(mx + jnp.log(den), (SPAN, HEAD_DIM))

    out = pl.BlockSpec((SPAN, GROUP_WIDTH), lambda r, j: (j, r))
    o, lse = pl.pallas_call(
        body, grid=(dil, nb), in_specs=_attn_in_specs(g, w // GROUP_WIDTH), out_specs=[out, out],
        out_shape=[SDS((l, dil * GROUP_WIDTH), f32)] * 2,
        compiler_params=_params("parallel", "parallel"), name=name)(a, a, a, a, a)
    return o.reshape(s, GROUP_WIDTH), lse.reshape(s, GROUP_WIDTH)


def _group_weights(lses):
    mx = jnp.maximum(jnp.maximum(lses[0], lses[1]), lses[2])
    es = [jnp.exp(v - mx) for v in lses]
    inv = 1.0 / (es[0] + es[1] + es[2])
    return [e * inv for e in es]


def mix_fwd(os_, lses, name):
    s, w = os_[0].shape
    tr = _row_tile(s)

    def body(o0, o1, o2, l0, l1, l2, out_ref):
        wg = _group_weights([l0[...], l1[...], l2[...]])
        out_ref[...] = (wg[0] * o0[...] + wg[1] * o1[...] + wg[2] * o2[...]).astype(out_ref.dtype)

    return pl.pallas_call(body, grid=(s // tr,), in_specs=[_rows(tr, w)] * 6, out_specs=_rows(tr, w),
                          out_shape=SDS((s, w), bf16), compiler_params=_params("parallel"), name=name)(*os_, *lses)


def mix_bwd(dmixed, os_, lses, head_ones, name):
    s, w = dmixed.shape
    tr = _row_tile(s)

    def head_sum(t, ones):
        hi = t.astype(bf16)
        lo = (t - hi.astype(f32)).astype(bf16)
        return jnp.dot(hi, ones, preferred_element_type=f32) + jnp.dot(lo, ones, preferred_element_type=f32)

    def body(dm_ref, o0, o1, o2, l0, l1, l2, ones_ref, d0, d1, d2, p0, p1, p2):
        dm = dm_ref[...]
        ones = ones_ref[...]
        wg = _group_weights([l0[...], l1[...], l2[...]])
        mean = sum(wg[k] * head_sum(dm * o[...], ones) for k, o in enumerate((o0, o1, o2)))
        for k, (d_ref, p_ref) in enumerate(((d0, p0), (d1, p1), (d2, p2))):
            d_ref[...] = (wg[k] * dm).astype(d_ref.dtype)
            p_ref[...] = wg[k] * mean

    return pl.pallas_call(body, grid=(s // tr,), in_specs=[_rows(tr, w)] * 7 + [_fixed(w, w)],
                          out_specs=[_rows(tr, w)] * 6,
                          out_shape=[SDS((s, w), bf16)] * 3 + [SDS((s, w), f32)] * 3,
                          compiler_params=_params("parallel"), name=name)(dmixed, *os_, *lses, head_ones)


def attn_bwd(qkvr, do, lse, dterm, g, dil, name):
    s, w = qkvr.shape
    l = s // dil
    nb = l // SPAN
    a = qkvr.reshape(l, dil * w)
    phased = lambda t: t.reshape(l, dil * GROUP_WIDTH)

    def body(q_ref, ko_ref, kp_ref, vo_ref, vp_ref, do_ref, lse_ref, dt_ref, dq_ref, dko_ref, dkp_ref, dvo_ref, dvp_ref):
        own, prev = _attn_masks(pl.program_id(1))
        scale = HEAD_DIM ** -0.5
        for h in range(N_SLOTS):
            hs = slice(h * HEAD_DIM, (h + 1) * HEAD_DIM)
            one = slice(h * HEAD_DIM, h * HEAD_DIM + 1)
            q, ko, kp, vo, vp, dov = q_ref[:, hs], ko_ref[:, hs], kp_ref[:, hs], vo_ref[:, hs], vp_ref[:, hs], do_ref[:, hs]
            lse_h, dt_h = lse_ref[:, one], dt_ref[:, one]
            po = jnp.exp(jnp.where(own, _dot_nt(q, ko) * scale - lse_h, NEG_BIG))
            pp = jnp.exp(jnp.where(prev, _dot_nt(q, kp) * scale - lse_h, NEG_BIG))
            dso = (po * (_dot_nt(dov, vo) - dt_h) * scale).astype(bf16)
            dsp = (pp * (_dot_nt(dov, vp) - dt_h) * scale).astype(bf16)
            dq_ref[:, hs] = jnp.dot(dso, ko, preferred_element_type=f32) + jnp.dot(dsp, kp, preferred_element_type=f32)
            dko_ref[:, hs] = _dot_tn(dso, q)
            dkp_ref[:, hs] = _dot_tn(dsp, q)
            dvo_ref[:, hs] = _dot_tn(po.astype(bf16), dov)
            dvp_ref[:, hs] = _dot_tn(pp.astype(bf16), dov)

    blk = pl.BlockSpec((SPAN, GROUP_WIDTH), lambda r, j: (j, r))
    outs = pl.pallas_call(
        body, grid=(dil, nb), in_specs=_attn_in_specs(g, w // GROUP_WIDTH) + [blk, blk, blk], out_specs=[blk] * 5,
        out_shape=[SDS((l, dil * GROUP_WIDTH), f32)] * 5,
        compiler_params=_params("parallel", "parallel"), name=name)(a, a, a, a, a, phased(do), phased(lse), phased(dterm))
    return [t.reshape(s, GROUP_WIDTH) for t in outs]


def dqkv_assemble(parts, tabs, name):
    s = parts[0][0].shape[0]
    nblk = s // SPAN
    width = 3 * N_GROUPS * GROUP_WIDTH

    def body(*refs):
        ins, (tc_ref, ta_ref, tb_ref, o_ref) = refs[:5 * N_GROUPS], refs[5 * N_GROUPS:]
        tc, ta, tb = tc_ref[...], ta_ref[...], tb_ref[...]
        i = pl.program_id(0)
        for g, (_, dil) in enumerate(DILATED_GROUPS):
            dq, dko, dkp, dvo, dvp = ins[5 * g:5 * g + 5]
            has_next = i + dil < nblk
            for ch in range(GROUP_WIDTH // LANES):
                cols = slice(ch * LANES, (ch + 1) * LANES)
                base = g * GROUP_WIDTH + ch * LANES
                dk = dko[:, cols] + jnp.where(has_next, dkp[:, cols], 0.0)
                dv = dvo[:, cols] + jnp.where(has_next, dvp[:, cols], 0.0)
                o_ref[:, base:base + LANES] = _rotate_chunk(dq[:, cols], tc, ta, tb).astype(o_ref.dtype)
                kb = N_GROUPS * GROUP_WIDTH + base
                o_ref[:, kb:kb + LANES] = _rotate_chunk(dk, tc, ta, tb).astype(o_ref.dtype)
                vb = 2 * N_GROUPS * GROUP_WIDTH + base
                o_ref[:, vb:vb + LANES] = dv.astype(o_ref.dtype)

    here = _rows(SPAN, GROUP_WIDTH)
    in_specs, args = [], []
    for g, (_, dil) in enumerate(DILATED_GROUPS):
        ahead = pl.BlockSpec((SPAN, GROUP_WIDTH), functools.partial(lambda i, dil: (jnp.minimum(i + dil, nblk - 1), 0), dil=dil))
        in_specs += [here, here, ahead, here, ahead]
        args += list(parts[g])
    tab = _rows(SPAN, LANES)
    return pl.pallas_call(body, grid=(nblk,), in_specs=in_specs + [tab] * 3, out_specs=_rows(SPAN, width),
                          out_shape=SDS((s, width), bf16), compiler_params=_params("parallel"), name=name)(*args, *tabs)


def _ffn_conv(cur, halo, wb, half, scr, first, tr):
    scr[0:FFN_HALO, :] = jnp.where(first, 0.0, halo)
    scr[FFN_HALO:FFN_HALO + tr, :] = cur
    acc = wb[half, FFN_CONV:FFN_CONV + 1, :]
    for k in range(FFN_CONV):
        acc = acc + wb[half, k:k + 1, :] * scr[pl.ds(FFN_HALO - (FFN_CONV - 1) + k, tr), :]
    return acc


def ffn_act_fwd(u, wb, name):
    _, nbk, s, c = u.shape
    tr = min(s, 256)

    def body(u_ref, h_ref, wb_ref, a_ref, su, sg):
        first = pl.program_id(1) == 0
        up = _ffn_conv(u_ref[0], h_ref[0], wb_ref, 0, su, first, tr)
        gate = _ffn_conv(u_ref[1], h_ref[1], wb_ref, 1, sg, first, tr)
        a_ref[...] = (gate * _sigmoid(gate) * up).astype(a_ref.dtype)

    return pl.pallas_call(
        body, grid=(nbk, s // tr),
        in_specs=[pl.BlockSpec((2, None, tr, c), lambda p, i: (0, p, i, 0)),
                  pl.BlockSpec((2, None, FFN_HALO, c), lambda p, i: (0, p, jnp.maximum(i * (tr // FFN_HALO) - 1, 0), 0)),
                  pl.BlockSpec((None, 2, 8, c), lambda p, i: (p, 0, 0, 0))],
        out_specs=pl.BlockSpec((None, tr, c), lambda p, i: (p, i, 0)),
        out_shape=SDS((nbk, s, c), bf16),
        scratch_shapes=[pltpu.VMEM((tr + FFN_HALO, c), f32)] * 2,
        compiler_params=_params("parallel", "arbitrary"), name=name)(u, u, wb)


def ffn_act_bwd(da, u, wb, name):
    _, nbk, s, c = u.shape
    tr = min(s, 256)
    nt = s // tr

    def body(da_ref, u_ref, h_ref, wb_ref, du_ref, dwb_ref, su, sg, eu, eg):
        step = pl.program_id(1)
        first = step == nt - 1
        up = _ffn_conv(u_ref[0], h_ref[0], wb_ref, 0, su, first, tr)
        gate = _ffn_conv(u_ref[1], h_ref[1], wb_ref, 1, sg, first, tr)
        sig = _sigmoid(gate)
        dav = da_ref[...].astype(f32)
        d_up = dav * (gate * sig)
        d_gate = dav * up * (sig * (1.0 + gate * (1.0 - sig)))

        @pl.when(step == 0)
        def _():
            eu[tr:tr + FFN_HALO, :] = jnp.zeros((FFN_HALO, c), f32)
            eg[tr:tr + FFN_HALO, :] = jnp.zeros((FFN_HALO, c), f32)
            dwb_ref[...] = jnp.zeros(dwb_ref.shape, f32)

        for half, (dv, ext, xs) in enumerate(((d_up, eu, su), (d_gate, eg, sg))):
            ext[0:tr, :] = dv
            acc = jnp.zeros((tr, c), f32)
            for k in range(FFN_CONV):
                acc = acc + wb_ref[half, k:k + 1, :] * ext[pl.ds(FFN_CONV - 1 - k, tr), :]
                dwb_ref[half, k:k + 1, :] += jnp.sum(dv * xs[pl.ds(FFN_HALO - (FFN_CONV - 1) + k, tr), :], axis=0, keepdims=True)
            dwb_ref[half, FFN_CONV:FFN_CONV + 1, :] += jnp.sum(dv, axis=0, keepdims=True)
            du_ref[half] = acc.astype(du_ref.dtype)
            ext[tr:tr + FFN_HALO, :] = ext[0:FFN_HALO, :]

    rev = lambda i: nt - 1 - i
    return pl.pallas_call(
        body, grid=(nbk, nt),
        in_specs=[pl.BlockSpec((None, tr, c), lambda p, i: (p, rev(i), 0)),
                  pl.BlockSpec((2, None, tr, c), lambda p, i: (0, p, rev(i), 0)),
                  pl.BlockSpec((2, None, FFN_HALO, c), lambda p, i: (0, p, jnp.maximum(rev(i) * (tr // FFN_HALO) - 1, 0), 0)),
                  pl.BlockSpec((None, 2, 8, c), lambda p, i: (p, 0, 0, 0))],
        out_specs=[pl.BlockSpec((2, None, tr, c), lambda p, i: (0, p, rev(i), 0)),
                   pl.BlockSpec((None, 2, 8, c), lambda p, i: (p, 0, 0, 0))],
        out_shape=[SDS((2, nbk, s, c), bf16), SDS((nbk, 2, 8, c), f32)],
        scratch_shapes=[pltpu.VMEM((tr + FFN_HALO, c), f32)] * 4,
        compiler_params=_params("parallel", "arbitrary"), name=name)(da, u, u, wb)


def _glu(zv, c):
    return zv[:, :c] * _sigmoid(zv[:, c:])


def _conv_fill(z_ref, h_ref, scr, first, tr, c):
    scr[0:CONV_HALO, :] = jnp.where(first, 0.0, _glu(h_ref[...], c))
    scr[CONV_HALO:CONV_HALO + tr, :] = _glu(z_ref[...], c)


def _layernorm_parts(cv):
    mu = jnp.mean(cv, axis=-1, keepdims=True)
    cen = cv - mu
    rstd = lax.rsqrt(jnp.mean(cen * cen, axis=-1, keepdims=True) + EPS)
    return cen * rstd, rstd


def conv_module_fwd(z, wdw, vecs, name):
    s, c2 = z.shape
    c = c2 // 2
    tr = min(s, 256)

    def body(z_ref, h_ref, w_ref, v_ref, c_ref, s_ref, scr):
        _conv_fill(z_ref, h_ref, scr, pl.program_id(0) == 0, tr, c)
        acc = jnp.broadcast_to(v_ref[0:1, :], (tr, c))
        for j in range(CONV_KERNEL):
            acc = acc + w_ref[j:j + 1, :] * scr[pl.ds(CONV_HALO - (CONV_KERNEL - 1) + j, tr), :]
        c_ref[...] = acc
        chat, _ = _layernorm_parts(acc)
        ln = chat * v_ref[1:2, :] + v_ref[2:3, :]
        s_ref[...] = (ln * _sigmoid(ln)).astype(s_ref.dtype)

    return pl.pallas_call(
        body, grid=(s // tr,),
        in_specs=[_rows(tr, c2), pl.BlockSpec((CONV_HALO, c2), lambda i: (jnp.maximum(i * (tr // CONV_HALO) - 1, 0), 0)),
                  _fixed(CONV_HALO, c), _fixed(8, c)],
        out_specs=[_rows(tr, c), _rows(tr, c)], out_shape=[SDS((s, c), f32), SDS((s, c), bf16)],
        scratch_shapes=[pltpu.VMEM((tr + CONV_HALO, c), f32)],
        compiler_params=_params("arbitrary"), name=name)(z, z, wdw, vecs)


def conv_module_bwd(ds, cpre, z, wdw, vecs, name):
    s, c2 = z.shape
    c = c2 // 2
    tr = min(s, 256)
    nt = s // tr

    def body(ds_ref, c_ref, z_ref, h_ref, w_ref, v_ref, dz_ref, dw_ref, dv_ref, db_ref, scr, ext):
        step = pl.program_id(0)
        _conv_fill(z_ref, h_ref, scr, step == nt - 1, tr, c)
        chat, rstd = _layernorm_parts(c_ref[...])
        gain = v_ref[1:2, :]
        ln = chat * gain + v_ref[2:3, :]
        sig = _sigmoid(ln)
        dln = ds_ref[...].astype(f32) * (sig * (1.0 + ln * (1.0 - sig)))
        gy = dln * gain
        dc = rstd * (gy - jnp.mean(gy, axis=-1, keepdims=True) - chat * jnp.mean(gy * chat, axis=-1, keepdims=True))

        @pl.when(step == 0)
        def _():
            ext[tr:tr + CONV_HALO, :] = jnp.zeros((CONV_HALO, c), f32)
            dw_ref[...] = jnp.zeros(dw_ref.shape, f32)
            dv_ref[...] = jnp.zeros(dv_ref.shape, f32)
            db_ref[...] = jnp.zeros(db_ref.shape, f32)

        dv_ref[0:1, :] += jnp.sum(dc, axis=0, keepdims=True)
        dv_ref[1:2, :] += jnp.sum(dln * chat, axis=0, keepdims=True)
        dv_ref[2:3, :] += jnp.sum(dln, axis=0, keepdims=True)
        ext[0:tr, :] = dc
        du = jnp.zeros((tr, c), f32)
        for j in range(CONV_KERNEL):
            du = du + w_ref[j:j + 1, :] * ext[pl.ds(CONV_KERNEL - 1 - j, tr), :]
            dw_ref[j:j + 1, :] += jnp.sum(dc * scr[pl.ds(CONV_HALO - (CONV_KERNEL - 1) + j, tr), :], axis=0, keepdims=True)
        ext[tr:tr + CONV_HALO, :] = ext[0:CONV_HALO, :]
        zv = z_ref[...]
        a, sg = zv[:, :c], _sigmoid(zv[:, c:])
        da = du * sg
        dg = du * a * (sg * (1.0 - sg))
        dz_ref[:, :c] = da.astype(dz_ref.dtype)
        dz_ref[:, c:] = dg.astype(dz_ref.dtype)
        db_ref[:, :c] += jnp.sum(da, axis=0, keepdims=True)
        db_ref[:, c:] += jnp.sum(dg, axis=0, keepdims=True)

    rev = lambda i: nt - 1 - i
    back = lambda d: pl.BlockSpec((tr, d), lambda i: (rev(i), 0))
    return pl.pallas_call(
        body, grid=(nt,),
        in_specs=[back(c), back(c), back(c2),
                  pl.BlockSpec((CONV_HALO, c2), lambda i: (jnp.maximum(rev(i) * (tr // CONV_HALO) - 1, 0), 0)),
                  _fixed(CONV_HALO, c), _fixed(8, c)],
        out_specs=[back(c2), _fixed(CONV_HALO, c), _fixed(8, c), _fixed(1, c2)],
        out_shape=[SDS((s, c2), bf16), SDS((CONV_HALO, c), f32), SDS((8, c), f32), SDS((1, c2), f32)],
        scratch_shapes=[pltpu.VMEM((tr + CONV_HALO, c), f32)] * 2,
        compiler_params=_params("arbitrary"), name=name)(ds, cpre, z, z, wdw, vecs)


def _tile2d(r, n):
    tn = n if n <= 2048 else 1024
    tr = r
    while tr * tn * 4 > (1 << 21) and tr % 16 == 0:
        tr //= 2
    assert r % tr == 0 and n % tn == 0
    return tr, tn


def adamw(w, g, m, v, name):
    r, n = w.shape
    tr, tn = _tile2d(r, n)

    def body(w_ref, g_ref, m_ref, v_ref, d_ref, nm_ref, nv_ref):
        gv = g_ref[...]
        nm = ADAM_B1 * m_ref[...] + (1.0 - ADAM_B1) * gv
        nv = ADAM_B2 * v_ref[...] + (1.0 - ADAM_B2) * (gv * gv)
        m_hat = nm / (1.0 - ADAM_B1 ** ADAM_STEP)
        v_hat = nv / (1.0 - ADAM_B2 ** ADAM_STEP)
        d_ref[...] = -ADAM_LR * (m_hat / (jnp.sqrt(v_hat) + ADAM_EPS) + ADAM_WD * w_ref[...])
        nm_ref[...] = nm
        nv_ref[...] = nv

    blk = pl.BlockSpec((tr, tn), lambda i, j: (i, j))
    return pl.pallas_call(body, grid=(r // tr, n // tn), in_specs=[blk] * 4, out_specs=[blk] * 3,
                          out_shape=[SDS((r, n), f32)] * 3, compiler_params=_params("parallel", "parallel"),
                          name=name)(w, g, m, v)


def add_core_halves(grad, got, where, name):
    _, _, rh, n = grad.shape
    tr, tn = _tile2d(rh, n)

    def body(w_ref, a_ref, b_ref, o_ref):
        o_ref[...] = (a_ref[...] + b_ref[...]).astype(o_ref.dtype)

    return pl.pallas_call(
        body,
        grid_spec=pltpu.PrefetchScalarGridSpec(
            num_scalar_prefetch=1, grid=(N_CHIPS, rh // tr, n // tn),
            in_specs=[pl.BlockSpec((None, None, tr, tn), lambda p, i, j, w_ref: (p, w_ref[0], i, j)),
                      pl.BlockSpec((None, tr, tn), lambda p, i, j, w_ref: (p, i, j))],
            out_specs=pl.BlockSpec((None, tr, tn), lambda p, i, j, w_ref: (p, i, j))),
        out_shape=SDS((N_CHIPS, rh, n), bf16), compiler_params=_params("parallel", "parallel", "parallel"),
        name=name)(where, grad, got)


def add_chip_parts(grad, got_core, got_chips, where, name):
    _, _, rh, n = grad.shape
    tr, tn = _tile2d(rh, n)

    def body(w_ref, a_ref, b_ref, g_ref, o_ref):
        acc = a_ref[...] + b_ref[...]
        for k in range(N_CHIPS - 1):
            acc = acc + g_ref[k].astype(f32)
        o_ref[...] = acc

    return pl.pallas_call(
        body,
        grid_spec=pltpu.PrefetchScalarGridSpec(
            num_scalar_prefetch=1, grid=(rh // tr, n // tn),
            in_specs=[pl.BlockSpec((None, None, tr, tn), lambda i, j, w_ref: (w_ref[1], w_ref[0], i, j)),
                      pl.BlockSpec((None, tr, tn), lambda i, j, w_ref: (w_ref[1], i, j)),
                      pl.BlockSpec((N_CHIPS - 1, tr, tn), lambda i, j, w_ref: (0, i, j))],
            out_specs=pl.BlockSpec((None, tr, tn), lambda i, j, w_ref: (w_ref[0], i, j))),
        out_shape=SDS((2, rh, n), f32), compiler_params=_params("parallel", "parallel"),
        name=name)(where, grad, got_core, got_chips)


def sum_devices(parts, name):
    nd, r, n = parts.shape

    def body(p_ref, o_ref):
        acc = p_ref[0]
        for k in range(1, nd):
            acc = acc + p_ref[k]
        o_ref[...] = acc

    return pl.pallas_call(body, out_shape=SDS((r, n), f32), name=name)(parts)


def _position():
    return lax.axis_index("x"), lax.axis_index("y"), lax.axis_index("c")


def _other_chips(x, y):
    return [(1 - x, y), (x, 1 - y), (1 - x, 1 - y)]


def _remote(src, dst, send, recv, to):
    return pltpu.make_async_remote_copy(src_ref=src, dst_ref=dst, send_sem=send, recv_sem=recv, device_id=to,
                                        device_id_type=MESH)


def gather_chips(bufs, name):
    n = len(bufs)
    nk = N_CHIPS - 1

    def body(*refs):
        bufs_ = refs[n:2 * n]
        send, recv = refs[2 * n:]
        x, y, c = _position()
        me = 2 * x + y
        chips = _other_chips(x, y)
        sends = []
        for t in range(n):
            for k, (px, py) in enumerate(chips):
                out = _remote(bufs_[t].at[me, c], bufs_[t].at[me, c], send.at[t, k], recv.at[t, k], (px, py, c))
                out.start()
                sends.append(out)
        for t in range(n):
            for k, (px, py) in enumerate(chips):
                piece = bufs_[t].at[2 * px + py, c]
                _remote(piece, piece, send.at[t, k], recv.at[t, k], (px, py, c)).wait_recv()
                on = _remote(piece, piece, send.at[t, nk + k], recv.at[t, nk + k], (x, y, 1 - c))
                on.start()
                sends.append(on)
        for t in range(n):
            for k, (px, py) in enumerate(chips):
                piece = bufs_[t].at[2 * px + py, 1 - c]
                _remote(piece, piece, send.at[t, nk + k], recv.at[t, nk + k], (x, y, 1 - c)).wait_recv()
        for cp in sends:
            cp.wait_send()

    return pl.pallas_call(
        body, in_specs=[ANY] * n, out_specs=[ANY] * n,
        out_shape=[SDS(a.shape, a.dtype) for a in bufs],
        input_output_aliases={t: t for t in range(n)},
        scratch_shapes=[pltpu.SemaphoreType.DMA((n, 2 * nk)), pltpu.SemaphoreType.DMA((n, 2 * nk))],
        name=name)(*bufs)


def swap_core_halves(grads, name):
    n = len(grads)

    def body(*refs):
        ins, outs = refs[:n], refs[n:2 * n]
        send, recv = refs[2 * n:]
        x, y, c = _position()
        pending = []
        for t in range(n):
            out = _remote(ins[t].at[:, 1 - c], outs[t], send.at[t], recv.at[t], (x, y, 1 - c))
            out.start()
            pending.append(out.wait)
        for wait in pending:
            wait()

    return pl.pallas_call(
        body, in_specs=[ANY] * n, out_specs=[ANY] * n,
        out_shape=[SDS((a.shape[0],) + a.shape[2:], a.dtype) for a in grads],
        scratch_shapes=[pltpu.SemaphoreType.DMA((n,)), pltpu.SemaphoreType.DMA((n,))],
        name=name)(*grads)


def scatter_chips(parts, name):
    n = len(parts)

    def body(*refs):
        ins, outs = refs[:n], refs[n:2 * n]
        send, recv = refs[2 * n:]
        x, y, c = _position()
        pending = []
        for t in range(n):
            for k, (px, py) in enumerate(_other_chips(x, y)):
                out = _remote(ins[t].at[2 * px + py], outs[t].at[k], send.at[t, k], recv.at[t, k], (px, py, c))
                out.start()
                pending.append(out.wait)
        for wait in pending:
            wait()

    return pl.pallas_call(
        body, in_specs=[ANY] * n, out_specs=[ANY] * n,
        out_shape=[SDS((N_CHIPS - 1,) + a.shape[1:], a.dtype) for a in parts],
        scratch_shapes=[pltpu.SemaphoreType.DMA((n, N_CHIPS - 1)), pltpu.SemaphoreType.DMA((n, N_CHIPS - 1))],
        name=name)(*parts)


def share_core_halves(bufs, name):
    n = len(bufs)

    def body(*refs):
        bufs_ = refs[n:2 * n]
        send, recv = refs[2 * n:]
        x, y, c = _position()
        pending = []
        for t in range(n):
            out = _remote(bufs_[t].at[c], bufs_[t].at[c], send.at[t], recv.at[t], (x, y, 1 - c))
            out.start()
            pending.append(out.wait_send)
            other = bufs_[t].at[1 - c]
            pending.append(_remote(other, other, send.at[t], recv.at[t], (x, y, 1 - c)).wait_recv)
        for wait in pending:
            wait()

    return pl.pallas_call(
        body, in_specs=[ANY] * n, out_specs=[ANY] * n, out_shape=[SDS(a.shape, a.dtype) for a in bufs],
        input_output_aliases={t: t for t in range(n)},
        scratch_shapes=[pltpu.SemaphoreType.DMA((n,)), pltpu.SemaphoreType.DMA((n,))],
        name=name)(*bufs)


def gather_devices(v, name):
    flips = [(dx, dy, dc) for dx in (0, 1) for dy in (0, 1) for dc in (0, 1)][1:]

    def body(v_ref, o_ref, send, recv, local):
        x, y, c = _position()
        me = 4 * x + 2 * y + c
        own = pltpu.make_async_copy(v_ref, o_ref.at[me], local)
        own.start()
        pending = [own.wait]
        for k, (dx, dy, dc) in enumerate(flips):
            px, py, pc = x ^ dx, y ^ dy, c ^ dc
            out = _remote(v_ref, o_ref.at[me], send.at[k], recv.at[k], (px, py, pc))
            out.start()
            pending.append(out.wait_send)
            pending.append(_remote(v_ref, o_ref.at[4 * px + 2 * py + pc], send.at[k], recv.at[k], (px, py, pc)).wait_recv)
        for wait in pending:
            wait()

    return pl.pallas_call(
        body, in_specs=[ANY], out_specs=ANY, out_shape=SDS((8,) + v.shape, v.dtype),
        scratch_shapes=[pltpu.SemaphoreType.DMA((7,)), pltpu.SemaphoreType.DMA((7,)), pltpu.SemaphoreType.DMA],
        name=name)(v)


SMALL = ("norm_g", "conv_b_pw1", "conv_w_dw", "conv_b_dw", "conv_ln_g", "conv_ln_b", "conv_b_pw2", "ffn_w_dw")


def _pack_rows(arrs, rows):
    flat = jnp.concatenate([a.reshape(-1, LANES) for a in arrs], axis=0)
    return jnp.pad(flat, ((0, rows - flat.shape[0]), (0, 0)))


def _unpack_rows(packed, shapes):
    out, at = [], 0
    for shp in shapes:
        size = 1
        for dim in shp:
            size *= dim
        rows = size // LANES
        out.append(packed[..., at:at + rows, :].reshape(packed.shape[:-2] + tuple(shp)))
        at += rows
    return out


def _join_last(t):
    t = jnp.moveaxis(t, 0, -2)
    return t.reshape(t.shape[:-2] + (t.shape[-2] * t.shape[-1],))


def _split_last(t):
    t = t.reshape(t.shape[:-1] + (N_CHIPS, t.shape[-1] // N_CHIPS))
    return jnp.moveaxis(t, -2, 0)


def _rope_tables(positions):
    half = ROT_DIM // 2
    inv_freq = ROPE_THETA ** (-jnp.arange(half, dtype=f32) / half)
    ang = positions.astype(f32).reshape(-1, 1) * inv_freq
    cos, sin = jnp.cos(ang), jnp.sin(ang)
    s = ang.shape[0]
    rest = HEAD_DIM - ROT_DIM
    head = lambda lo, hi, fill: jnp.concatenate([lo, hi, jnp.full((s, rest), fill, f32)], axis=1)
    zero = jnp.zeros((s, half), f32)
    twice = lambda t: jnp.concatenate([t] * (LANES // HEAD_DIM), axis=1)
    return twice(head(cos, cos, 1.0)), twice(head(-sin, zero, 0.0)), twice(head(zero, sin, 0.0))


def _pad_rows(t, rows):
    return jnp.pad(t, ((0, rows - t.shape[0]), (0, 0)))


def _ffn_pack(w_dw, b_dw):
    t = jnp.concatenate([w_dw, b_dw[None]], axis=0)
    t = t.reshape(FFN_CONV + 1, 2, 2, -1)
    t = jnp.transpose(t, (2, 1, 0, 3))
    return jnp.pad(t, ((0, 0), (0, 0), (0, 8 - (FFN_CONV + 1)), (0, 0)))


def _ffn_unpack(d):
    t = jnp.transpose(d[:, :, :FFN_CONV + 1], (2, 1, 0, 3)).reshape(FFN_CONV + 1, -1)
    return t[:FFN_CONV], t[FFN_CONV]


def _ffn_block(x, g_pre, g_post, w_up, li, w_down, wb, tag):
    s = x.shape[0]
    h = prenorm(x, g_pre, f"{tag}_prenorm")
    u = mm_nn(h[None], w_up, li, None, N_CHIPS, f32, 512, w_up.shape[-1], f"{tag}_up")
    u4 = u.reshape(2, 2, s, u.shape[-1])
    a = ffn_act_fwd(u4, wb, f"{tag}_act")
    y = mm_nn(a, w_down, 0, None, 1, f32, 512, 512, f"{tag}_down")[0]
    return y, (h, u4, a)


def _ffn_block_bwd(dy, saved, w_up, li, w_down, wb, tag):
    h, u4, a = saved
    d_down = mm_tn(a, dy[None], 1, 1024, a.shape[-1], 512, f"{tag}_dwdown")
    da = mm_nt(dy[None], w_down, 0, 2, f32, 1024, a.shape[-1], w_down.shape[-1], f"{tag}_da")
    du4, dwb = ffn_act_bwd(da, u4, wb, f"{tag}_actbwd")
    du = du4.reshape((N_CHIPS,) + du4.shape[2:])
    d_up = mm_tn(h[None], du, N_CHIPS, 1024, h.shape[-1], du.shape[-1], f"{tag}_dwup")
    dh = mm_nt(du, w_up, li, 1, f32, 1024, h.shape[-1], du.shape[-1], f"{tag}_dh")[0]
    return dh, d_up, d_down, dwb


def kernel(x, positions, norm_g, attn_w_qkv, attn_w_o, conv_w_pw1, conv_b_pw1, conv_w_dw, conv_b_dw, conv_ln_g, conv_ln_b, conv_w_pw2, conv_b_pw2, ffn_w_up, ffn_w_dw, ffn_b_dw, ffn_w_down, loss_target, m_norm_g, m_attn_w_qkv, m_attn_w_o, m_conv_w_pw1, m_conv_b_pw1, m_conv_w_dw, m_conv_b_dw, m_conv_ln_g, m_conv_ln_b, m_conv_w_pw2, m_conv_b_pw2, m_ffn_w_up, m_ffn_w_dw, m_ffn_b_dw, m_ffn_w_down, v_norm_g, v_attn_w_qkv, v_attn_w_o, v_conv_w_pw1, v_conv_b_pw1, v_conv_w_dw, v_conv_b_dw, v_conv_ln_g, v_conv_ln_b, v_conv_w_pw2, v_conv_b_pw2, v_ffn_w_up, v_ffn_w_dw, v_ffn_b_dw, v_ffn_w_down):
    weights = dict(norm_g=norm_g, attn_w_qkv=attn_w_qkv, attn_w_o=attn_w_o, conv_w_pw1=conv_w_pw1, conv_b_pw1=conv_b_pw1,
                   conv_w_dw=conv_w_dw, conv_b_dw=conv_b_dw, conv_ln_g=conv_ln_g, conv_ln_b=conv_ln_b, conv_w_pw2=conv_w_pw2,
                   conv_b_pw2=conv_b_pw2, ffn_w_up=ffn_w_up, ffn_w_dw=ffn_w_dw, ffn_b_dw=ffn_b_dw, ffn_w_down=ffn_w_down)
    mom1 = dict(norm_g=m_norm_g, attn_w_qkv=m_attn_w_qkv, attn_w_o=m_attn_w_o, conv_w_pw1=m_conv_w_pw1, conv_b_pw1=m_conv_b_pw1,
                conv_w_dw=m_conv_w_dw, conv_b_dw=m_conv_b_dw, conv_ln_g=m_conv_ln_g, conv_ln_b=m_conv_ln_b, conv_w_pw2=m_conv_w_pw2,
                conv_b_pw2=m_conv_b_pw2, ffn_w_up=m_ffn_w_up, ffn_w_dw=m_ffn_w_dw, ffn_b_dw=m_ffn_b_dw, ffn_w_down=m_ffn_w_down)
    mom2 = dict(norm_g=v_norm_g, attn_w_qkv=v_attn_w_qkv, attn_w_o=v_attn_w_o, conv_w_pw1=v_conv_w_pw1, conv_b_pw1=v_conv_b_pw1,
                conv_w_dw=v_conv_w_dw, conv_b_dw=v_conv_b_dw, conv_ln_g=v_conv_ln_g, conv_ln_b=v_conv_ln_b, conv_w_pw2=v_conv_w_pw2,
                conv_b_pw2=v_conv_b_pw2, ffn_w_up=v_ffn_w_up, ffn_w_dw=v_ffn_w_dw, ffn_b_dw=v_ffn_b_dw, ffn_w_down=v_ffn_w_down)
    big = ("attn_w_qkv", "attn_w_o", "conv_w_pw1", "conv_w_pw2", "ffn_w_up", "ffn_w_down")
    xi, yi, ci = _position()
    chip = (2 * xi + yi).astype(jnp.int32).reshape(1)
    where = jnp.stack([ci, 2 * xi + yi]).astype(jnp.int32)

    x = x[0]
    target = loss_target[0]
    s, d = x.shape

    small_shapes = [weights[k].shape for k in SMALL]
    small_rows = -(-sum(weights[k].size for k in SMALL) // LANES // 8) * 8
    small_w = _pack_rows([weights[k] for k in SMALL], small_rows)
    def own_slot(shard):
        halves = shard.reshape(1, 2, -1, shard.shape[-1])
        return lax.dynamic_update_slice(lax.empty((N_CHIPS,) + halves.shape[1:], shard.dtype), halves, (chip[0], 0, 0, 0))

    gathered = gather_chips([own_slot(weights[k].astype(bf16)) for k in big] + [own_slot(small_w)], "gather_weights")
    gw = {k: t.reshape((N_CHIPS,) + weights[k].shape) for k, t in zip(big, gathered[:-1])}
    full_small = dict(zip(SMALL, [_join_last(t) for t in _unpack_rows(gathered[-1].reshape(N_CHIPS, small_rows, LANES), small_shapes)]))
    w_qkv, w_o, w_pw1, w_up = gw["attn_w_qkv"], gw["attn_w_o"], gw["conv_w_pw1"], gw["ffn_w_up"]
    w_pw2 = gw["conv_w_pw2"].reshape(1, 1, -1, d)
    w_down = [gw["ffn_w_down"][:, i].reshape(1, 1, -1, d) for i in range(2)]
    gains = full_small["norm_g"]
    gain = lambda i, k: gains[i, k][None]
    b_pw1 = full_small["conv_b_pw1"]
    conv_wdw = _pad_rows(full_small["conv_w_dw"][0], CONV_HALO)
    conv_vecs = _pad_rows(jnp.concatenate([full_small["conv_b_dw"], full_small["conv_ln_g"], full_small["conv_ln_b"]], axis=0), 8)
    b_pw2 = full_small["conv_b_pw2"]
    wbs = [_ffn_pack(full_small["ffn_w_dw"][i], ffn_b_dw[i]) for i in range(2)]
    tabs = _rope_tables(positions[0])
    tabs_t = (tabs[0], -tabs[1], -tabs[2])
    head_ones = (jnp.arange(GROUP_WIDTH)[:, None] // HEAD_DIM == jnp.arange(GROUP_WIDTH)[None, :] // HEAD_DIM).astype(bf16)

    h0 = prenorm(x, gain(0, 0), "l0_prenorm")
    qkv = mm_nn(h0[None], w_qkv, 0, None, 1, f32, 512, w_qkv.shape[-1], "qkv")[0]
    qkvr = rope_qkv(qkv, tabs, "rope")
    att = [attn_fwd(qkvr, g, dil, f"attn_fwd{g}") for g, (_, dil) in enumerate(DILATED_GROUPS)]
    os_, lses = [a[0] for a in att], [a[1] for a in att]
    mixed = mix_fwd(os_, lses, "mix")
    y0 = mm_nn(mixed[None], w_o, 0, None, 1, f32, 1024, w_o.shape[-1], "attn_out")[0]
    x1 = postnorm_residual(x, y0, gain(0, 1), "l0_postnorm")
    y1, ffn0 = _ffn_block(x1, gain(0, 2), gain(0, 3), w_up, 0, w_down[0], wbs[0], "ffn0")
    x2 = postnorm_residual(x1, y1, gain(0, 3), "l0_ffn_postnorm")

    h2 = prenorm(x2, gain(1, 0), "l1_prenorm")
    z = mm_nn(h2[None], w_pw1, 0, b_pw1, 1, f32, 512, w_pw1.shape[-1], "pw1")[0]
    cpre, sw = conv_module_fwd(z, conv_wdw, conv_vecs, "conv_fwd")
    y2 = mm_nn(sw[None], w_pw2, 0, b_pw2, 1, f32, 512, 512, "pw2")[0]
    x3 = postnorm_residual(x2, y2, gain(1, 1), "l1_postnorm")
    y3, ffn1 = _ffn_block(x3, gain(1, 2), gain(1, 3), w_up, 1, w_down[1], wbs[1], "ffn1")
    dx4, loss_row = final_loss(x3, y3, gain(1, 3), target, "loss")
    loss = lax.psum(loss_row[0, 0], ("x", "y", "c"))

    dgain = [[None] * 4 for _ in range(2)]
    dy3, dgain[1][3], _ = norm_bwd(y3, gain(1, 3), dx4, None, bf16, "l1_ffn_postnorm_bwd")
    dh, d_up1, d_down1, dwb1 = _ffn_block_bwd(dy3, ffn1, w_up, 1, w_down[1], wbs[1], "ffn1")
    dx3, dgain[1][2], _ = norm_bwd(x3, gain(1, 2), dh, dx4, f32, "l1_ffn_prenorm_bwd")

    dy2, dgain[1][1], d_b_pw2 = norm_bwd(y2, gain(1, 1), dx3, None, bf16, "l1_postnorm_bwd")
    d_pw2 = mm_tn(sw[None], dy2[None], 1, 1024, d, 512, "dw_pw2")
    dsw = mm_nt(dy2[None], w_pw2, 0, 1, f32, 1024, d, d, "d_swish")[0]
    dz, d_conv_wdw, d_conv_vecs, d_b_pw1 = conv_module_bwd(dsw, cpre, z, conv_wdw, conv_vecs, "conv_bwd")
    d_pw1 = mm_tn(h2[None], dz[None], N_CHIPS, 1024, d, w_pw1.shape[-1], "dw_pw1")
    dh = mm_nt(dz[None], w_pw1, 0, 1, f32, 1024, d, w_pw1.shape[-1], "d_h2")[0]
    dx2, dgain[1][0], _ = norm_bwd(x2, gain(1, 0), dh, dx3, f32, "l1_prenorm_bwd")

    dy1, dgain[0][3], _ = norm_bwd(y1, gain(0, 3), dx2, None, bf16, "l0_ffn_postnorm_bwd")
    dh, d_up0, d_down0, dwb0 = _ffn_block_bwd(dy1, ffn0, w_up, 0, w_down[0], wbs[0], "ffn0")
    dx1, dgain[0][2], _ = norm_bwd(x1, gain(0, 2), dh, dx2, f32, "l0_ffn_prenorm_bwd")

    dy0, dgain[0][1], _ = norm_bwd(y0, gain(0, 1), dx1, None, bf16, "l0_postnorm_bwd")
    d_wo = mm_tn(mixed[None], dy0[None], N_CHIPS, 1024, GROUP_WIDTH, w_o.shape[-1], "dw_o")
    dmixed = mm_nt(dy0[None], w_o, 0, 1, f32, 1024, GROUP_WIDTH, w_o.shape[-1], "d_mixed")[0]
    mb = mix_bwd(dmixed, os_, lses, head_ones, "mix_bwd")
    parts = [attn_bwd(qkvr, mb[g], lses[g], mb[3 + g], g, dil, f"attn_bwd{g}") for g, (_, dil) in enumerate(DILATED_GROUPS)]
    dqkv = dqkv_assemble(parts, tabs_t, "dqkv")
    d_qkv = mm_tn(h0[None], dqkv[None], N_CHIPS, 1024, d, w_qkv.shape[-1] // 3, "dw_qkv")
    dh = mm_nt(dqkv[None], w_qkv, 0, 1, f32, 1024, d, w_qkv.shape[-1] // 3, "d_h0")[0]
    grad_x, dgain[0][0], _ = norm_bwd(x, gain(0, 0), dh, dx1, f32, "l0_prenorm_bwd")

    stack_layers = lambda a, b: jnp.stack([a, b], axis=1)
    gbig = dict(
        attn_w_qkv=d_qkv, attn_w_o=d_wo, conv_w_pw1=d_pw1, conv_w_pw2=d_pw2[0].reshape(N_CHIPS, -1, d),
        ffn_w_up=stack_layers(d_up0, d_up1),
        ffn_w_down=stack_layers(d_down0[0].reshape(N_CHIPS, -1, d), d_down1[0].reshape(N_CHIPS, -1, d)))
    d_ffn_dw, d_ffn_b = zip(*[_ffn_unpack(t) for t in (dwb0, dwb1)])
    gsmall = dict(
        norm_g=jnp.stack([jnp.concatenate(row, axis=0) for row in dgain], axis=0),
        conv_b_pw1=d_b_pw1, conv_w_dw=d_conv_wdw[None, :CONV_KERNEL], conv_b_dw=d_conv_vecs[0:1], conv_ln_g=d_conv_vecs[1:2],
        conv_ln_b=d_conv_vecs[2:3], conv_b_pw2=d_b_pw2, ffn_w_dw=jnp.stack(d_ffn_dw, axis=0))
    bias_rows = ffn_b_dw.size // LANES
    small_g = jnp.concatenate([jnp.concatenate([_pack_rows([_split_last(gsmall[k])[p] for k in SMALL], small_rows)
                                                for p in range(N_CHIPS)], axis=0),
                               jnp.stack(d_ffn_b, axis=0).reshape(bias_rows, LANES)], axis=0)

    small_sum = sum_devices(gather_devices(small_g, "gather_small_grads"), "sum_small_grads")
    my_small = lax.dynamic_slice_in_dim(small_sum, chip[0] * small_rows, small_rows, axis=0)
    g_small = jnp.concatenate([my_small, small_sum[N_CHIPS * small_rows:]], axis=0)

    halves = [gbig[k].reshape(N_CHIPS, 2, -1, gbig[k].shape[-1]) for k in big]
    from_core = swap_core_halves(halves, "swap_core_halves")
    chip_parts = [add_core_halves(a, b, where, f"add_core_{k}") for k, a, b in zip(big, halves, from_core)]
    from_chips = scatter_chips(chip_parts, "scatter_chips")
    mine = [add_chip_parts(a, b, r, where, f"add_chips_{k}") for k, a, b, r in zip(big, halves, from_core, from_chips)]
    shard_grads = share_core_halves(mine, "share_core_halves")

    grads, deltas, new_m, new_v = {}, {}, {}, {}
    for k, g2 in zip(big, shard_grads):
        shp = weights[k].shape
        g2 = g2.reshape(-1, shp[-1])
        dl, nm, nv = adamw(weights[k].reshape(g2.shape), g2, mom1[k].reshape(g2.shape), mom2[k].reshape(g2.shape), f"adamw_{k}")
        grads[k], deltas[k], new_m[k], new_v[k] = (t.reshape(shp) for t in (g2, dl, nm, nv))
    pack_state = lambda src: jnp.concatenate([_pack_rows([src[k] for k in SMALL], small_rows), src["ffn_b_dw"].reshape(bias_rows, LANES)], axis=0)
    small_out = (g_small,) + tuple(adamw(pack_state(weights), g_small, pack_state(mom1), pack_state(mom2), "adamw_small"))
    for dst, packed in zip((grads, deltas, new_m, new_v), small_out):
        for k, t in zip(SMALL, _unpack_rows(packed[:small_rows], small_shapes)):
            dst[k] = t
        dst["ffn_b_dw"] = packed[small_rows:].reshape(ffn_b_dw.shape)

    order = ("norm_g", "attn_w_qkv", "attn_w_o", "conv_w_pw1", "conv_b_pw1", "conv_w_dw", "conv_b_dw", "conv_ln_g", "conv_ln_b",
             "conv_w_pw2", "conv_b_pw2", "ffn_w_up", "ffn_w_dw", "ffn_b_dw", "ffn_w_down")
    return (loss, grad_x[None], *[grads[k] for k in order], *[deltas[k] for k in order], *[new_m[k] for k in order],
            *[new_v[k] for k in order])
```

```python
import functools

import jax
import jax.numpy as jnp
from jax import lax
from jax.experimental import pallas as pl
from jax.experimental.pallas import tpu as pltpu

f32 = jnp.float32
bf16 = jnp.bfloat16
SDS = jax.ShapeDtypeStruct

EPS = 1e-6
HEAD_DIM = 64
N_SLOTS = 8
GROUP_WIDTH = N_SLOTS * HEAD_DIM
DILATED_GROUPS = ((128, 1), (512, 4), (2048, 16))
N_GROUPS = 3
SPAN = 128
ROT_DIM = HEAD_DIM // 4
ROPE_THETA = 500000.0
CONV_KERNEL = 31
CONV_HALO = 32
FFN_CONV = 3
FFN_HALO = 16
FFN_TAIL = 8
ADAM_LR, ADAM_B1, ADAM_B2, ADAM_EPS, ADAM_WD, ADAM_STEP = 0.001, 0.9, 0.999, 1e-08, 0.01, 10
LANES = 128
N_CHIPS = 4
VMEM_LIMIT_BYTES = 56 * 1024 * 1024
NEG_BIG = -1e30
MESH = pl.DeviceIdType.MESH
ANY = pl.BlockSpec(memory_space=pl.ANY)


def _params(*sem):
    return pltpu.CompilerParams(dimension_semantics=sem, vmem_limit_bytes=VMEM_LIMIT_BYTES)


def _sigmoid(v):
    return 1.0 / (1.0 + jnp.exp(-v))


def _dot_nt(a, b):
    return lax.dot_general(a, b, (((1,), (1,)), ((), ())), preferred_element_type=f32)


def _dot_tn(a, b):
    return lax.dot_general(a, b, (((0,), (0,)), ((), ())), preferred_element_type=f32)


def mm_nn(x, w, li, bias, out_blocks, out_dtype, tm, tn, name):
    nq, m, kq = x.shape
    p, _, k, n = w.shape
    assert k == nq * kq and n % tn == 0 and m % tm == 0
    on = p * n // out_blocks
    assert on % tn == 0
    nj, onj = n // tn, on // tn

    def body(*refs):
        if bias is None:
            x_ref, w_ref, o_ref, *scr = refs
            b_ref = None
        else:
            x_ref, w_ref, b_ref, o_ref, *scr = refs
        part = jnp.dot(x_ref[...], w_ref[...], preferred_element_type=f32)

        def finish(acc):
            if b_ref is not None:
                acc = acc + b_ref[...]
            o_ref[...] = acc.astype(o_ref.dtype)

        if nq == 1:
            finish(part)
        else:
            acc_ref, = scr
            q = pl.program_id(2)

            @pl.when(q == 0)
            def _():
                acc_ref[...] = part

            @pl.when(q > 0)
            def _():
                acc_ref[...] += part

            @pl.when(q == nq - 1)
            def _():
                finish(acc_ref[...])

    in_specs = [pl.BlockSpec((None, tm, kq), lambda j, i, q: (q, i, 0)),
                pl.BlockSpec((None, None, kq, tn), lambda j, i, q: (j // nj, li, q, j % nj))]
    args = [x, w]
    if bias is not None:
        in_specs.append(pl.BlockSpec((1, tn), lambda j, i, q: (0, j)))
        args.append(bias)
    return pl.pallas_call(
        body, grid=(p * nj, m // tm, nq), in_specs=in_specs,
        out_specs=pl.BlockSpec((None, tm, tn), lambda j, i, q: (j // onj, i, j % onj)),
        out_shape=SDS((out_blocks, m, on), out_dtype),
        scratch_shapes=[] if nq == 1 else [pltpu.VMEM((tm, tn), f32)],
        compiler_params=_params("parallel", "parallel", "arbitrary"), name=name)(*args)


def mm_nt(dy, w, li, out_blocks, out_dtype, tm, tk, tn, name):
    ob, m, on = dy.shape
    p, _, k, n = w.shape
    assert ob * on == p * n and n % tn == 0 and on % tn == 0 and k % tk == 0 and m % tm == 0
    kq = k // out_blocks
    assert kq % tk == 0
    nj, onj, kqj = n // tn, on // tn, kq // tk
    nr = p * nj

    def body(dy_ref, w_ref, o_ref, *scr):
        part = _dot_nt(dy_ref[...], w_ref[...])
        if nr == 1:
            o_ref[...] = part.astype(o_ref.dtype)
        else:
            acc_ref, = scr
            r = pl.program_id(2)

            @pl.when(r == 0)
            def _():
                acc_ref[...] = part

            @pl.when(r > 0)
            def _():
                acc_ref[...] += part

            @pl.when(r == nr - 1)
            def _():
                o_ref[...] = acc_ref[...].astype(o_ref.dtype)

    return pl.pallas_call(
        body, grid=(k // tk, m // tm, nr),
        in_specs=[pl.BlockSpec((None, tm, tn), lambda kt, i, r: (r // onj, i, r % onj)),
                  pl.BlockSpec((None, None, tk, tn), lambda kt, i, r: (r // nj, li, kt, r % nj))],
        out_specs=pl.BlockSpec((None, tm, tk), lambda kt, i, r: (kt // kqj, i, kt % kqj)),
        out_shape=SDS((out_blocks, m, kq), out_dtype),
        scratch_shapes=[] if nr == 1 else [pltpu.VMEM((tm, tk), f32)],
        compiler_params=_params("parallel", "parallel", "arbitrary"), name=name)(dy, w)


def mm_tn(x, dy, p, tm, tk, tn, name):
    nq, m, kq = x.shape
    ob, _, on = dy.shape
    k = nq * kq
    n = ob * on // p
    assert n % tn == 0 and on % tn == 0 and kq % tk == 0 and m % tm == 0
    nj, onj, kqj = n // tn, on // tn, kq // tk

    def body(x_ref, dy_ref, o_ref):
        part = _dot_tn(x_ref[...], dy_ref[...])
        i = pl.program_id(2)

        @pl.when(i == 0)
        def _():
            o_ref[...] = part

        @pl.when(i > 0)
        def _():
            o_ref[...] += part

    return pl.pallas_call(
        body, grid=(k // tk, p * nj, m // tm),
        in_specs=[pl.BlockSpec((None, tm, tk), lambda kt, j, i: (kt // kqj, i, kt % kqj)),
                  pl.BlockSpec((None, tm, tn), lambda kt, j, i: (j // onj, i, j % onj))],
        out_specs=pl.BlockSpec((None, tk, tn), lambda kt, j, i: (j // nj, kt, j % nj)),
        out_shape=SDS((p, k, n), f32),
        compiler_params=_params("parallel", "parallel", "arbitrary"), name=name)(x, dy)


def _row_tile(s):
    return min(s, 512)


def _rows(tr, d):
    return pl.BlockSpec((tr, d), lambda i: (i, 0))


def _fixed(r, d):
    return pl.BlockSpec((r, d), lambda i: (0, 0))


def _rms(xv):
    return lax.rsqrt(jnp.mean(xv * xv, axis=-1, keepdims=True) + EPS)


def prenorm(x, g, name):
    s, d = x.shape
    tr = _row_tile(s)

    def body(x_ref, g_ref, o_ref):
        xv = x_ref[...]
        o_ref[...] = (xv * _rms(xv) * g_ref[...]).astype(o_ref.dtype)

    return pl.pallas_call(body, grid=(s // tr,), in_specs=[_rows(tr, d), _fixed(1, d)], out_specs=_rows(tr, d),
                          out_shape=SDS((s, d), bf16), compiler_params=_params("parallel"), name=name)(x, g)


def postnorm_residual(x, y, g, name):
    s, d = x.shape
    tr = _row_tile(s)

    def body(x_ref, y_ref, g_ref, o_ref):
        yv = y_ref[...]
        o_ref[...] = x_ref[...] + yv * _rms(yv) * g_ref[...]

    return pl.pallas_call(body, grid=(s // tr,), in_specs=[_rows(tr, d), _rows(tr, d), _fixed(1, d)],
                          out_specs=_rows(tr, d), out_shape=SDS((s, d), f32), compiler_params=_params("parallel"),
                          name=name)(x, y, g)


def norm_bwd(xin, g, dout, res, out_dtype, name):
    s, d = xin.shape
    tr = _row_tile(s)

    def body(*refs):
        if res is None:
            x_ref, g_ref, do_ref, dx_ref, dg_ref, cs_ref = refs
            r_ref = None
        else:
            x_ref, g_ref, do_ref, r_ref, dx_ref, dg_ref, cs_ref = refs
        xv = x_ref[...]
        r = _rms(xv)
        xh = xv * r
        dov = do_ref[...].astype(f32)
        gy = dov * g_ref[...]
        dx = r * (gy - xh * jnp.mean(gy * xh, axis=-1, keepdims=True))
        if r_ref is not None:
            dx = dx + r_ref[...]
        dx_ref[...] = dx.astype(dx_ref.dtype)
        dg = jnp.sum(dov * xh, axis=0, keepdims=True)
        cs = jnp.sum(dx, axis=0, keepdims=True)
        i = pl.program_id(0)

        @pl.when(i == 0)
        def _():
            dg_ref[...] = dg
            cs_ref[...] = cs

        @pl.when(i > 0)
        def _():
            dg_ref[...] += dg
            cs_ref[...] += cs

    in_specs = [_rows(tr, d), _fixed(1, d), _rows(tr, d)]
    args = [xin, g, dout]
    if res is not None:
        in_specs.append(_rows(tr, d))
        args.append(res)
    return pl.pallas_call(body, grid=(s // tr,), in_specs=in_specs,
                          out_specs=[_rows(tr, d), _fixed(1, d), _fixed(1, d)],
                          out_shape=[SDS((s, d), out_dtype), SDS((1, d), f32), SDS((1, d), f32)],
                          compiler_params=_params("arbitrary"), name=name)(*args)


def final_loss(x, y, g, target, name):
    s, d = x.shape
    tr = _row_tile(s)
    nt = s // tr

    def body(x_ref, y_ref, g_ref, t_ref, dx_ref, loss_ref, acc_ref):
        yv = y_ref[...]
        diff = x_ref[...] + yv * _rms(yv) * g_ref[...] - t_ref[...]
        dx_ref[...] = diff * (1.0 / d)
        sq = jnp.sum(diff * diff, axis=0, keepdims=True)
        i = pl.program_id(0)

        @pl.when(i == 0)
        def _():
            acc_ref[...] = sq

        @pl.when(i > 0)
        def _():
            acc_ref[...] += sq

        @pl.when(i == nt - 1)
        def _():
            total = jnp.sum(acc_ref[...], axis=1, keepdims=True) * (0.5 / d)
            loss_ref[...] = jnp.broadcast_to(total, (1, LANES))

    return pl.pallas_call(body, grid=(nt,), in_specs=[_rows(tr, d), _rows(tr, d), _fixed(1, d), _rows(tr, d)],
                          out_specs=[_rows(tr, d), _fixed(1, LANES)],
                          out_shape=[SDS((s, d), f32), SDS((1, LANES), f32)],
                          scratch_shapes=[pltpu.VMEM((1, d), f32)],
                          compiler_params=_params("arbitrary"), name=name)(x, y, g, target)


def _rotate_chunk(v, tc, ta, tb):
    return v * tc + pltpu.roll(v, LANES - ROT_DIM // 2, axis=1) * ta + pltpu.roll(v, ROT_DIM // 2, axis=1) * tb


def rope_qkv(qkv, tabs, name):
    s, w = qkv.shape
    tr = min(s, 256)
    n_rot = 2 * N_GROUPS * GROUP_WIDTH // LANES

    def body(x_ref, tc_ref, ta_ref, tb_ref, o_ref):
        tc, ta, tb = tc_ref[...], ta_ref[...], tb_ref[...]
        for ch in range(w // LANES):
            cols = slice(ch * LANES, (ch + 1) * LANES)
            v = x_ref[:, cols]
            if ch < n_rot:
                v = _rotate_chunk(v, tc, ta, tb)
            o_ref[:, cols] = v.astype(o_ref.dtype)

    tab = _rows(tr, LANES)
    return pl.pallas_call(body, grid=(s // tr,), in_specs=[_rows(tr, w), tab, tab, tab], out_specs=_rows(tr, w),
                          out_shape=SDS((s, w), bf16), compiler_params=_params("parallel"), name=name)(qkv, *tabs)


def _attn_masks(j):
    row = lax.broadcasted_iota(jnp.int32, (SPAN, SPAN), 0)
    col = lax.broadcasted_iota(jnp.int32, (SPAN, SPAN), 1)
    return col <= row, jnp.logical_and(col >= row, j > 0)


def _attn_in_specs(g, n_col_blocks):
    def at(kind, prev):
        def index(r, j):
            return (jnp.maximum(j - 1, 0) if prev else j, r * n_col_blocks + kind * N_GROUPS + g)
        return pl.BlockSpec((SPAN, GROUP_WIDTH), index)
    return [at(0, False), at(1, False), at(1, True), at(2, False), at(2, True)]


def attn_fwd(qkvr, g, dil, name):
    s, w = qkvr.shape
    l = s // dil
    nb = l // SPAN
    a = qkvr.reshape(l, dil * w)

    def body(q_ref, ko_ref, kp_ref, vo_ref, vp_ref, o_ref, lse_ref):
        own, prev = _attn_masks(pl.program_id(1))
        for h in range(N_SLOTS):
            hs = slice(h * HEAD_DIM, (h + 1) * HEAD_DIM)
            q = q_ref[:, hs]
            so = jnp.where(own, _dot_nt(q, ko_ref[:, hs]) * (HEAD_DIM ** -0.5), NEG_BIG)
            sp = jnp.where(prev, _dot_nt(q, kp_ref[:, hs]) * (HEAD_DIM ** -0.5), NEG_BIG)
            mx = jnp.maximum(jnp.max(so, axis=-1, keepdims=True), jnp.max(sp, axis=-1, keepdims=True))
            po = jnp.exp(so - mx)
            pp = jnp.exp(sp - mx)
            den = jnp.sum(po, axis=-1, keepdims=True) + jnp.sum(pp, axis=-1, keepdims=True)
            acc = jnp.dot(po.astype(bf16), vo_ref[:, hs], preferred_element_type=f32)
            acc = acc + jnp.dot(pp.astype(bf16), vp_ref[:, hs], preferred_element_type=f32)
            o_ref[:, hs] = acc / den
            lse_ref[:, hs] = jnp.broadcast_to(mx + jnp.log(den), (SPAN, HEAD_DIM))

    out = pl.BlockSpec((SPAN, GROUP_WIDTH), lambda r, j: (j, r))
    o, lse = pl.pallas_call(
        body, grid=(dil, nb), in_specs=_attn_in_specs(g, w // GROUP_WIDTH), out_specs=[out, out],
        out_shape=[SDS((l, dil * GROUP_WIDTH), f32)] * 2,
        compiler_params=_params("parallel", "parallel"), name=name)(a, a, a, a, a)
    return o.reshape(s, GROUP_WIDTH), lse.reshape(s, GROUP_WIDTH)


def _group_weights(lses):
    mx = jnp.maximum(jnp.maximum(lses[0], lses[1]), lses[2])
    es = [jnp.exp(v - mx) for v in lses]
    inv = 1.0 / (es[0] + es[1] + es[2])
    return [e * inv for e in es]


def mix_fwd(os_, lses, name):
    s, w = os_[0].shape
    tr = _row_tile(s)

    def body(o0, o1, o2, l0, l1, l2, out_ref):
        wg = _group_weights([l0[...], l1[...], l2[...]])
        out_ref[...] = (wg[0] * o0[...] + wg[1] * o1[...] + wg[2] * o2[...]).astype(out_ref.dtype)

    return pl.pallas_call(body, grid=(s // tr,), in_specs=[_rows(tr, w)] * 6, out_specs=_rows(tr, w),
                          out_shape=SDS((s, w), bf16), compiler_params=_params("parallel"), name=name)(*os_, *lses)


def mix_bwd(dmixed, os_, lses, head_ones, name):
    s, w = dmixed.shape
    tr = _row_tile(s)

    def head_sum(t, ones):
        hi = t.astype(bf16)
        lo = (t - hi.astype(f32)).astype(bf16)
        return jnp.dot(hi, ones, preferred_element_type=f32) + jnp.dot(lo, ones, preferred_element_type=f32)

    def body(dm_ref, o0, o1, o2, l0, l1, l2, ones_ref, d0, d1, d2, p0, p1, p2):
        dm = dm_ref[...]
        ones = ones_ref[...]
        wg = _group_weights([l0[...], l1[...], l2[...]])
        mean = sum(wg[k] * head_sum(dm * o[...], ones) for k, o in enumerate((o0, o1, o2)))
        for k, (d_ref, p_ref) in enumerate(((d0, p0), (d1, p1), (d2, p2))):
            d_ref[...] = (wg[k] * dm).astype(d_ref.dtype)
            p_ref[...] = wg[k] * mean

    return pl.pallas_call(body, grid=(s // tr,), in_specs=[_rows(tr, w)] * 7 + [_fixed(w, w)],
                          out_specs=[_rows(tr, w)] * 6,
                          out_shape=[SDS((s, w), bf16)] * 3 + [SDS((s, w), f32)] * 3,
                          compiler_params=_params("parallel"), name=name)(dmixed, *os_, *lses, head_ones)


def attn_bwd(qkvr, do, lse, dterm, g, dil, name):
    s, w = qkvr.shape
    l = s // dil
    nb = l // SPAN
    a = qkvr.reshape(l, dil * w)
    phased = lambda t: t.reshape(l, dil * GROUP_WIDTH)

    def body(q_ref, ko_ref, kp_ref, vo_ref, vp_ref, do_ref, lse_ref, dt_ref, dq_ref, dko_ref, dkp_ref, dvo_ref, dvp_ref):
        own, prev = _attn_masks(pl.program_id(1))
        scale = HEAD_DIM ** -0.5
        for h in range(N_SLOTS):
            hs = slice(h * HEAD_DIM, (h + 1) * HEAD_DIM)
            one = slice(h * HEAD_DIM, h * HEAD_DIM + 1)
            q, ko, kp, vo, vp, dov = q_ref[:, hs], ko_ref[:, hs], kp_ref[:, hs], vo_ref[:, hs], vp_ref[:, hs], do_ref[:, hs]
            lse_h, dt_h = lse_ref[:, one], dt_ref[:, one]
            po = jnp.exp(jnp.where(own, _dot_nt(q, ko) * scale - lse_h, NEG_BIG))
            pp = jnp.exp(jnp.where(prev, _dot_nt(q, kp) * scale - lse_h, NEG_BIG))
            dso = (po * (_dot_nt(dov, vo) - dt_h) * scale).astype(bf16)
            dsp = (pp * (_dot_nt(dov, vp) - dt_h) * scale).astype(bf16)
            dq_ref[:, hs] = jnp.dot(dso, ko, preferred_element_type=f32) + jnp.dot(dsp, kp, preferred_element_type=f32)
            dko_ref[:, hs] = _dot_tn(dso, q)
            dkp_ref[:, hs] = _dot_tn(dsp, q)
            dvo_ref[:, hs] = _dot_tn(po.astype(bf16), dov)
            dvp_ref[:, hs] = _dot_tn(pp.astype(bf16), dov)

    blk = pl.BlockSpec((SPAN, GROUP_WIDTH), lambda r, j: (j, r))
    outs = pl.pallas_call(
        body, grid=(dil, nb), in_specs=_attn_in_specs(g, w // GROUP_WIDTH) + [blk, blk, blk], out_specs=[blk] * 5,
        out_shape=[SDS((l, dil * GROUP_WIDTH), f32)] * 5,
        compiler_params=_params("parallel", "parallel"), name=name)(a, a, a, a, a, phased(do), phased(lse), phased(dterm))
    return [t.reshape(s, GROUP_WIDTH) for t in outs]


def dqkv_assemble(parts, tabs, name):
    s = parts[0][0].shape[0]
    nblk = s // SPAN
    width = 3 * N_GROUPS * GROUP_WIDTH

    def body(*refs):
        ins, (tc_ref, ta_ref, tb_ref, o_ref) = refs[:5 * N_GROUPS], refs[5 * N_GROUPS:]
        tc, ta, tb = tc_ref[...], ta_ref[...], tb_ref[...]
        i = pl.program_id(0)
        for g, (_, dil) in enumerate(DILATED_GROUPS):
            dq, dko, dkp, dvo, dvp = ins[5 * g:5 * g + 5]
            has_next = i + dil < nblk
            for ch in range(GROUP_WIDTH // LANES):
                cols = slice(ch * LANES, (ch + 1) * LANES)
                base = g * GROUP_WIDTH + ch * LANES
                dk = dko[:, cols] + jnp.where(has_next, dkp[:, cols], 0.0)
                dv = dvo[:, cols] + jnp.where(has_next, dvp[:, cols], 0.0)
                o_ref[:, base:base + LANES] = _rotate_chunk(dq[:, cols], tc, ta, tb).astype(o_ref.dtype)
                kb = N_GROUPS * GROUP_WIDTH + base
                o_ref[:, kb:kb + LANES] = _rotate_chunk(dk, tc, ta, tb).astype(o_ref.dtype)
                vb = 2 * N_GROUPS * GROUP_WIDTH + base
                o_ref[:, vb:vb + LANES] = dv.astype(o_ref.dtype)

    here = _rows(SPAN, GROUP_WIDTH)
    in_specs, args = [], []
    for g, (_, dil) in enumerate(DILATED_GROUPS):
        ahead = pl.BlockSpec((SPAN, GROUP_WIDTH), functools.partial(lambda i, dil: (jnp.minimum(i + dil, nblk - 1), 0), dil=dil))
        in_specs += [here, here, ahead, here, ahead]
        args += list(parts[g])
    tab = _rows(SPAN, LANES)
    return pl.pallas_call(body, grid=(nblk,), in_specs=in_specs + [tab] * 3, out_specs=_rows(SPAN, width),
                          out_shape=SDS((s, width), bf16), compiler_params=_params("parallel"), name=name)(*args, *tabs)


def ffn_act_fwd(u, wb, name):
    _, nbk, s, c = u.shape
    tr = min(s, 256)

    def body(u_ref, h_ref, wb_ref, a_ref, ug_ref, scr):
        first = pl.program_id(1) == 0
        conv = []
        for half in range(2):
            scr[0:FFN_HALO, :] = jnp.where(first, 0.0, h_ref[half].astype(f32))
            scr[FFN_HALO:FFN_HALO + tr, :] = u_ref[half].astype(f32)
            acc = wb_ref[half, FFN_CONV:FFN_CONV + 1, :]
            for k in range(FFN_CONV):
                acc = acc + wb_ref[half, k:k + 1, :] * scr[pl.ds(FFN_HALO - (FFN_CONV - 1) + k, tr), :]
            ug_ref[half] = acc.astype(ug_ref.dtype)
            conv.append(acc)
        up, gate = conv
        a_ref[...] = (gate * _sigmoid(gate) * up).astype(a_ref.dtype)

    both = pl.BlockSpec((2, None, tr, c), lambda p, i: (0, p, i, 0))
    return pl.pallas_call(
        body, grid=(nbk, s // tr),
        in_specs=[both,
                  pl.BlockSpec((2, None, FFN_HALO, c), lambda p, i: (0, p, jnp.maximum(i * (tr // FFN_HALO) - 1, 0), 0)),
                  pl.BlockSpec((None, 2, 8, c), lambda p, i: (p, 0, 0, 0))],
        out_specs=[pl.BlockSpec((None, tr, c), lambda p, i: (p, i, 0)), both],
        out_shape=[SDS((nbk, s, c), bf16), SDS(u.shape, bf16)],
        scratch_shapes=[pltpu.VMEM((tr + FFN_HALO, c), f32)],
        compiler_params=_params("parallel", "arbitrary"), name=name)(u, u, wb)


def ffn_act_bwd(da, ug, u, wb, name):
    _, nbk, s, c = u.shape
    tr = min(s, 256)
    nt = s // tr

    def body(da_ref, ug_ref, u_ref, wb_ref, du_ref, dwb_ref, eu, eg):
        step = pl.program_id(1)
        up, gate = ug_ref[0].astype(f32), ug_ref[1].astype(f32)
        sig = _sigmoid(gate)
        dav = da_ref[...].astype(f32)
        d_up = dav * (gate * sig)
        d_gate = dav * up * (sig * (1.0 + gate * (1.0 - sig)))

        @pl.when(step == 0)
        def _():
            eu[tr:tr + FFN_TAIL, :] = jnp.zeros((FFN_TAIL, c), f32)
            eg[tr:tr + FFN_TAIL, :] = jnp.zeros((FFN_TAIL, c), f32)
            dwb_ref[...] = jnp.zeros(dwb_ref.shape, f32)

        for half, (dv, ext) in enumerate(((d_up, eu), (d_gate, eg))):
            ext[0:tr, :] = dv
            xv = u_ref[half].astype(f32)
            acc = jnp.zeros((tr, c), f32)
            for k in range(FFN_CONV):
                ahead = dv if k == FFN_CONV - 1 else ext[pl.ds(FFN_CONV - 1 - k, tr), :]
                acc = acc + wb_ref[half, k:k + 1, :] * ahead
                dwb_ref[half, k:k + 1, :] += jnp.sum(xv * ahead, axis=0, keepdims=True)
            dwb_ref[half, FFN_CONV:FFN_CONV + 1, :] += jnp.sum(dv, axis=0, keepdims=True)
            du_ref[half] = acc.astype(du_ref.dtype)
            ext[tr:tr + FFN_TAIL, :] = ext[0:FFN_TAIL, :]

    rev = lambda i: nt - 1 - i
    both = pl.BlockSpec((2, None, tr, c), lambda p, i: (0, p, rev(i), 0))
    return pl.pallas_call(
        body, grid=(nbk, nt),
        in_specs=[pl.BlockSpec((None, tr, c), lambda p, i: (p, rev(i), 0)), both, both,
                  pl.BlockSpec((None, 2, 8, c), lambda p, i: (p, 0, 0, 0))],
        out_specs=[both, pl.BlockSpec((None, 2, 8, c), lambda p, i: (p, 0, 0, 0))],
        out_shape=[SDS((2, nbk, s, c), bf16), SDS((nbk, 2, 8, c), f32)],
        scratch_shapes=[pltpu.VMEM((tr + FFN_TAIL, c), f32)] * 2,
        compiler_params=_params("parallel", "arbitrary"), name=name)(da, ug, u, wb)


def _glu(zv, c):
    return zv[:, :c] * _sigmoid(zv[:, c:])


def _conv_fill(z_ref, h_ref, scr, first, tr, c):
    scr[0:CONV_HALO, :] = jnp.where(first, 0.0, _glu(h_ref[...], c))
    scr[CONV_HALO:CONV_HALO + tr, :] = _glu(z_ref[...], c)


def _conv_taps(b):
    return [(a, CONV_KERNEL - 1 - 8 * a - b) for a in range(CONV_HALO // 8) if CONV_KERNEL - 1 - 8 * a - b >= 0]


def _layernorm_parts(cv):
    mu = jnp.mean(cv, axis=-1, keepdims=True)
    cen = cv - mu
    rstd = lax.rsqrt(jnp.mean(cen * cen, axis=-1, keepdims=True) + EPS)
    return cen * rstd, rstd


def conv_module_fwd(z, wdw, vecs, name):
    s, c2 = z.shape
    c = c2 // 2
    tr = min(s, 256)

    def body(z_ref, h_ref, w_ref, v_ref, c_ref, s_ref, scr, zb):
        _conv_fill(z_ref, h_ref, scr, pl.program_id(0) == 0, tr, c)
        acc = jnp.broadcast_to(v_ref[0:1, :], (tr, c))
        for b in range(8):
            part = None
            for a, j in _conv_taps(b):
                term = w_ref[j:j + 1, :] * scr[pl.ds(CONV_HALO - 8 - 8 * a, tr + 8), :]
                part = term if part is None else part + term
            if b == 0:
                acc = acc + part[8:]
            else:
                zb[...] = part
                acc = acc + zb[pl.ds(8 - b, tr), :]
        c_ref[...] = acc
        chat, _ = _layernorm_parts(acc)
        ln = chat * v_ref[1:2, :] + v_ref[2:3, :]
        s_ref[...] = (ln * _sigmoid(ln)).astype(s_ref.dtype)

    return pl.pallas_call(
        body, grid=(s // tr,),
        in_specs=[_rows(tr, c2), pl.BlockSpec((CONV_HALO, c2), lambda i: (jnp.maximum(i * (tr // CONV_HALO) - 1, 0), 0)),
                  _fixed(CONV_HALO, c), _fixed(8, c)],
        out_specs=[_rows(tr, c), _rows(tr, c)], out_shape=[SDS((s, c), f32), SDS((s, c), bf16)],
        scratch_shapes=[pltpu.VMEM((tr + CONV_HALO, c), f32), pltpu.VMEM((tr + 8, c), f32)],
        compiler_params=_params("arbitrary"), name=name)(z, z, wdw, vecs)


def conv_module_bwd(ds, cpre, z, wdw, vecs, name):
    s, c2 = z.shape
    c = c2 // 2
    tr = min(s, 256)
    nt = s // tr

    def body(ds_ref, c_ref, z_ref, w_ref, v_ref, dz_ref, dw_ref, dv_ref, db_ref, ext, dsh):
        step = pl.program_id(0)
        chat, rstd = _layernorm_parts(c_ref[...])
        gain = v_ref[1:2, :]
        ln = chat * gain + v_ref[2:3, :]
        sig = _sigmoid(ln)
        dln = ds_ref[...].astype(f32) * (sig * (1.0 + ln * (1.0 - sig)))
        gy = dln * gain
        dc = rstd * (gy - jnp.mean(gy, axis=-1, keepdims=True) - chat * jnp.mean(gy * chat, axis=-1, keepdims=True))

        @pl.when(step == 0)
        def _():
            ext[tr:tr + CONV_HALO, :] = jnp.zeros((CONV_HALO, c), f32)
            dw_ref[...] = jnp.zeros(dw_ref.shape, f32)
            dv_ref[...] = jnp.zeros(dv_ref.shape, f32)
            db_ref[...] = jnp.zeros(db_ref.shape, f32)

        dv_ref[0:1, :] += jnp.sum(dc, axis=0, keepdims=True)
        dv_ref[1:2, :] += jnp.sum(dln * chat, axis=0, keepdims=True)
        dv_ref[2:3, :] += jnp.sum(dln, axis=0, keepdims=True)
        ext[0:tr, :] = dc
        zv = z_ref[...]
        a, sg = zv[:, :c], _sigmoid(zv[:, c:])
        uv = a * sg
        du = jnp.zeros((tr, c), f32)
        for b in range(8):
            src = ext
            if b:
                dsh[...] = ext[pl.ds(b, tr + CONV_HALO - 8), :]
                src = dsh
            for a8, j in _conv_taps(b):
                ahead = src[pl.ds(8 * a8, tr), :]
                du = du + w_ref[j:j + 1, :] * ahead
                dw_ref[j:j + 1, :] += jnp.sum(uv * ahead, axis=0, keepdims=True)
        ext[tr:tr + CONV_HALO, :] = ext[0:CONV_HALO, :]
        da = du * sg
        dg = du * a * (sg * (1.0 - sg))
        dz_ref[:, :c] = da.astype(dz_ref.dtype)
        dz_ref[:, c:] = dg.astype(dz_ref.dtype)
        db_ref[:, :c] += jnp.sum(da, axis=0, keepdims=True)
        db_ref[:, c:] += jnp.sum(dg, axis=0, keepdims=True)

    rev = lambda i: nt - 1 - i
    back = lambda d: pl.BlockSpec((tr, d), lambda i: (rev(i), 0))
    return pl.pallas_call(
        body, grid=(nt,),
        in_specs=[back(c), back(c), back(c2), _fixed(CONV_HALO, c), _fixed(8, c)],
        out_specs=[back(c2), _fixed(CONV_HALO, c), _fixed(8, c), _fixed(1, c2)],
        out_shape=[SDS((s, c2), bf16), SDS((CONV_HALO, c), f32), SDS((8, c), f32), SDS((1, c2), f32)],
        scratch_shapes=[pltpu.VMEM((tr + CONV_HALO, c), f32), pltpu.VMEM((tr + CONV_HALO - 8, c), f32)],
        compiler_params=_params("arbitrary"), name=name)(ds, cpre, z, wdw, vecs)


def _tile2d(r, n):
    tn = n if n <= 2048 else 1024
    tr = r
    while tr * tn * 4 > (1 << 21) and tr % 16 == 0:
        tr //= 2
    assert r % tr == 0 and n % tn == 0
    return tr, tn


def adamw(w, g, m, v, name):
    r, n = w.shape
    tr, tn = _tile2d(r, n)

    def body(w_ref, g_ref, m_ref, v_ref, d_ref, nm_ref, nv_ref):
        gv = g_ref[...]
        nm = ADAM_B1 * m_ref[...] + (1.0 - ADAM_B1) * gv
        nv = ADAM_B2 * v_ref[...] + (1.0 - ADAM_B2) * (gv * gv)
        m_hat = nm / (1.0 - ADAM_B1 ** ADAM_STEP)
        v_hat = nv / (1.0 - ADAM_B2 ** ADAM_STEP)
        d_ref[...] = -ADAM_LR * (m_hat / (jnp.sqrt(v_hat) + ADAM_EPS) + ADAM_WD * w_ref[...])
        nm_ref[...] = nm
        nv_ref[...] = nv

    blk = pl.BlockSpec((tr, tn), lambda i, j: (i, j))
    return pl.pallas_call(body, grid=(r // tr, n // tn), in_specs=[blk] * 4, out_specs=[blk] * 3,
                          out_shape=[SDS((r, n), f32)] * 3, compiler_params=_params("parallel", "parallel"),
                          name=name)(w, g, m, v)


def add_core_halves(grad, got, where, name):
    _, _, rh, n = grad.shape
    tr, tn = _tile2d(rh, n)

    def body(w_ref, a_ref, b_ref, o_ref):
        o_ref[...] = (a_ref[...] + b_ref[...]).astype(o_ref.dtype)

    return pl.pallas_call(
        body,
        grid_spec=pltpu.PrefetchScalarGridSpec(
            num_scalar_prefetch=1, grid=(N_CHIPS, rh // tr, n // tn),
            in_specs=[pl.BlockSpec((None, None, tr, tn), lambda p, i, j, w_ref: (p, w_ref[0], i, j)),
                      pl.BlockSpec((None, tr, tn), lambda p, i, j, w_ref: (p, i, j))],
            out_specs=pl.BlockSpec((None, tr, tn), lambda p, i, j, w_ref: (p, i, j))),
        out_shape=SDS((N_CHIPS, rh, n), bf16), compiler_params=_params("parallel", "parallel", "parallel"),
        name=name)(where, grad, got)


def add_chip_parts(grad, got_core, got_chips, where, name):
    _, _, rh, n = grad.shape
    tr, tn = _tile2d(rh, n)

    def body(w_ref, a_ref, b_ref, g_ref, o_ref):
        acc = a_ref[...] + b_ref[...]
        for k in range(N_CHIPS - 1):
            acc = acc + g_ref[k].astype(f32)
        o_ref[...] = acc

    return pl.pallas_call(
        body,
        grid_spec=pltpu.PrefetchScalarGridSpec(
            num_scalar_prefetch=1, grid=(rh // tr, n // tn),
            in_specs=[pl.BlockSpec((None, None, tr, tn), lambda i, j, w_ref: (w_ref[1], w_ref[0], i, j)),
                      pl.BlockSpec((None, tr, tn), lambda i, j, w_ref: (w_ref[1], i, j)),
                      pl.BlockSpec((N_CHIPS - 1, tr, tn), lambda i, j, w_ref: (0, i, j))],
            out_specs=pl.BlockSpec((None, tr, tn), lambda i, j, w_ref: (w_ref[0], i, j))),
        out_shape=SDS((2, rh, n), f32), compiler_params=_params("parallel", "parallel"),
        name=name)(where, grad, got_core, got_chips)


def sum_devices(parts, name):
    nd, r, n = parts.shape

    def body(p_ref, o_ref):
        acc = p_ref[0]
        for k in range(1, nd):
            acc = acc + p_ref[k]
        o_ref[...] = acc

    return pl.pallas_call(body, out_shape=SDS((r, n), f32), name=name)(parts)


def _position():
    return lax.axis_index("x"), lax.axis_index("y"), lax.axis_index("c")


def _other_chips(x, y):
    return [(1 - x, y), (x, 1 - y), (1 - x, 1 - y)]


def _remote(src, dst, send, recv, to):
    return pltpu.make_async_remote_copy(src_ref=src, dst_ref=dst, send_sem=send, recv_sem=recv, device_id=to,
                                        device_id_type=MESH)


def gather_chips(bufs, name):
    n = len(bufs)
    nk = N_CHIPS - 1

    def body(*refs):
        bufs_ = refs[n:2 * n]
        send, recv = refs[2 * n:]
        x, y, c = _position()
        me = 2 * x + y
        chips = _other_chips(x, y)
        sends = []
        for t in range(n):
            for k, (px, py) in enumerate(chips):
                out = _remote(bufs_[t].at[me, c], bufs_[t].at[me, c], send.at[t, k], recv.at[t, k], (px, py, c))
                out.start()
                sends.append(out)
        for t in range(n):
            for k, (px, py) in enumerate(chips):
                piece = bufs_[t].at[2 * px + py, c]
                _remote(piece, piece, send.at[t, k], recv.at[t, k], (px, py, c)).wait_recv()
                on = _remote(piece, piece, send.at[t, nk + k], recv.at[t, nk + k], (x, y, 1 - c))
                on.start()
                sends.append(on)
        for t in range(n):
            for k, (px, py) in enumerate(chips):
                piece = bufs_[t].at[2 * px + py, 1 - c]
                _remote(piece, piece, send.at[t, nk + k], recv.at[t, nk + k], (x, y, 1 - c)).wait_recv()
        for cp in sends:
            cp.wait_send()

    return pl.pallas_call(
        body, in_specs=[ANY] * n, out_specs=[ANY] * n,
        out_shape=[SDS(a.shape, a.dtype) for a in bufs],
        input_output_aliases={t: t for t in range(n)},
        scratch_shapes=[pltpu.SemaphoreType.DMA((n, 2 * nk)), pltpu.SemaphoreType.DMA((n, 2 * nk))],
        name=name)(*bufs)


def swap_core_halves(grads, name):
    n = len(grads)

    def body(*refs):
        ins, outs = refs[:n], refs[n:2 * n]
        send, recv = refs[2 * n:]
        x, y, c = _position()
        pending = []
        for t in range(n):
            out = _remote(ins[t].at[:, 1 - c], outs[t], send.at[t], recv.at[t], (x, y, 1 - c))
            out.start()
            pending.append(out.wait)
        for wait in pending:
            wait()

    return pl.pallas_call(
        body, in_specs=[ANY] * n, out_specs=[ANY] * n,
        out_shape=[SDS((a.shape[0],) + a.shape[2:], a.dtype) for a in grads],
        scratch_shapes=[pltpu.SemaphoreType.DMA((n,)), pltpu.SemaphoreType.DMA((n,))],
        name=name)(*grads)


def scatter_chips(parts, name):
    n = len(parts)

    def body(*refs):
        ins, outs = refs[:n], refs[n:2 * n]
        send, recv = refs[2 * n:]
        x, y, c = _position()
        pending = []
        for t in range(n):
            for k, (px, py) in enumerate(_other_chips(x, y)):
                out = _remote(ins[t].at[2 * px + py], outs[t].at[k], send.at[t, k], recv.at[t, k], (px, py, c))
                out.start()
                pending.append(out.wait)
        for wait in pending:
            wait()

    return pl.pallas_call(
        body, in_specs=[ANY] * n, out_specs=[ANY] * n,
        out_shape=[SDS((N_CHIPS - 1,) + a.shape[1:], a.dtype) for a in parts],
        scratch_shapes=[pltpu.SemaphoreType.DMA((n, N_CHIPS - 1)), pltpu.SemaphoreType.DMA((n, N_CHIPS - 1))],
        name=name)(*parts)


def share_core_halves(bufs, name):
    n = len(bufs)

    def body(*refs):
        bufs_ = refs[n:2 * n]
        send, recv = refs[2 * n:]
        x, y, c = _position()
        pending = []
        for t in range(n):
            out = _remote(bufs_[t].at[c], bufs_[t].at[c], send.at[t], recv.at[t], (x, y, 1 - c))
            out.start()
            pending.append(out.wait_send)
            other = bufs_[t].at[1 - c]
            pending.append(_remote(other, other, send.at[t], recv.at[t], (x, y, 1 - c)).wait_recv)
        for wait in pending:
            wait()

    return pl.pallas_call(
        body, in_specs=[ANY] * n, out_specs=[ANY] * n, out_shape=[SDS(a.shape, a.dtype) for a in bufs],
        input_output_aliases={t: t for t in range(n)},
        scratch_shapes=[pltpu.SemaphoreType.DMA((n,)), pltpu.SemaphoreType.DMA((n,))],
        name=name)(*bufs)


def gather_devices(v, name):
    flips = [(dx, dy, dc) for dx in (0, 1) for dy in (0, 1) for dc in (0, 1)][1:]

    def body(v_ref, o_ref, send, recv, local):
        x, y, c = _position()
        me = 4 * x + 2 * y + c
        own = pltpu.make_async_copy(v_ref, o_ref.at[me], local)
        own.start()
        pending = [own.wait]
        for k, (dx, dy, dc) in enumerate(flips):
            px, py, pc = x ^ dx, y ^ dy, c ^ dc
            out = _remote(v_ref, o_ref.at[me], send.at[k], recv.at[k], (px, py, pc))
            out.start()
            pending.append(out.wait_send)
            pending.append(_remote(v_ref, o_ref.at[4 * px + 2 * py + pc], send.at[k], recv.at[k], (px, py, pc)).wait_recv)
        for wait in pending:
            wait()

    return pl.pallas_call(
        body, in_specs=[ANY], out_specs=ANY, out_shape=SDS((8,) + v.shape, v.dtype),
        scratch_shapes=[pltpu.SemaphoreType.DMA((7,)), pltpu.SemaphoreType.DMA((7,)), pltpu.SemaphoreType.DMA],
        name=name)(v)


SMALL = ("norm_g", "conv_b_pw1", "conv_w_dw", "conv_b_dw", "conv_ln_g", "conv_ln_b", "conv_b_pw2", "ffn_w_dw")


def _pack_rows(arrs, rows):
    flat = jnp.concatenate([a.reshape(-1, LANES) for a in arrs], axis=0)
    return jnp.pad(flat, ((0, rows - flat.shape[0]), (0, 0)))


def _unpack_rows(packed, shapes):
    out, at = [], 0
    for shp in shapes:
        size = 1
        for dim in shp:
            size *= dim
        rows = size // LANES
        out.append(packed[..., at:at + rows, :].reshape(packed.shape[:-2] + tuple(shp)))
        at += rows
    return out


def _join_last(t):
    t = jnp.moveaxis(t, 0, -2)
    return t.reshape(t.shape[:-2] + (t.shape[-2] * t.shape[-1],))


def _split_last(t):
    t = t.reshape(t.shape[:-1] + (N_CHIPS, t.shape[-1] // N_CHIPS))
    return jnp.moveaxis(t, -2, 0)


def _rope_tables(positions):
    half = ROT_DIM // 2
    inv_freq = ROPE_THETA ** (-jnp.arange(half, dtype=f32) / half)
    ang = positions.astype(f32).reshape(-1, 1) * inv_freq
    cos, sin = jnp.cos(ang), jnp.sin(ang)
    s = ang.shape[0]
    rest = HEAD_DIM - ROT_DIM
    head = lambda lo, hi, fill: jnp.concatenate([lo, hi, jnp.full((s, rest), fill, f32)], axis=1)
    zero = jnp.zeros((s, half), f32)
    twice = lambda t: jnp.concatenate([t] * (LANES // HEAD_DIM), axis=1)
    return twice(head(cos, cos, 1.0)), twice(head(-sin, zero, 0.0)), twice(head(zero, sin, 0.0))


def _pad_rows(t, rows):
    return jnp.pad(t, ((0, rows - t.shape[0]), (0, 0)))


def _ffn_pack(w_dw, b_dw):
    t = jnp.concatenate([w_dw, b_dw[None]], axis=0)
    t = t.reshape(FFN_CONV + 1, 2, 2, -1)
    t = jnp.transpose(t, (2, 1, 0, 3))
    return jnp.pad(t, ((0, 0), (0, 0), (0, 8 - (FFN_CONV + 1)), (0, 0)))


def _ffn_unpack(d):
    t = jnp.transpose(d[:, :, :FFN_CONV + 1], (2, 1, 0, 3)).reshape(FFN_CONV + 1, -1)
    return t[:FFN_CONV], t[FFN_CONV]


def _ffn_block(x, g_pre, g_post, w_up, li, w_down, wb, tag):
    s = x.shape[0]
    h = prenorm(x, g_pre, f"{tag}_prenorm")
    u = mm_nn(h[None], w_up, li, None, N_CHIPS, bf16, 512, w_up.shape[-1], f"{tag}_up")
    u4 = u.reshape(2, 2, s, u.shape[-1])
    a, ug = ffn_act_fwd(u4, wb, f"{tag}_act")
    y = mm_nn(a, w_down, 0, None, 1, f32, 512, 512, f"{tag}_down")[0]
    return y, (h, u4, ug, a)


def _ffn_block_bwd(dy, saved, w_up, li, w_down, wb, tag):
    h, u4, ug, a = saved
    d_down = mm_tn(a, dy[None], 1, 1024, a.shape[-1], 512, f"{tag}_dwdown")
    da = mm_nt(dy[None], w_down, 0, 2, bf16, 1024, a.shape[-1], w_down.shape[-1], f"{tag}_da")
    du4, dwb = ffn_act_bwd(da, ug, u4, wb, f"{tag}_actbwd")
    du = du4.reshape((N_CHIPS,) + du4.shape[2:])
    d_up = mm_tn(h[None], du, N_CHIPS, 1024, h.shape[-1], du.shape[-1], f"{tag}_dwup")
    dh = mm_nt(du, w_up, li, 1, f32, 1024, h.shape[-1], du.shape[-1], f"{tag}_dh")[0]
    return dh, d_up, d_down, dwb


def kernel(x, positions, norm_g, attn_w_qkv, attn_w_o, conv_w_pw1, conv_b_pw1, conv_w_dw, conv_b_dw, conv_ln_g, conv_ln_b, conv_w_pw2, conv_b_pw2, ffn_w_up, ffn_w_dw, ffn_b_dw, ffn_w_down, loss_target, m_norm_g, m_attn_w_qkv, m_attn_w_o, m_conv_w_pw1, m_conv_b_pw1, m_conv_w_dw, m_conv_b_dw, m_conv_ln_g, m_conv_ln_b, m_conv_w_pw2, m_conv_b_pw2, m_ffn_w_up, m_ffn_w_dw, m_ffn_b_dw, m_ffn_w_down, v_norm_g, v_attn_w_qkv, v_attn_w_o, v_conv_w_pw1, v_conv_b_pw1, v_conv_w_dw, v_conv_b_dw, v_conv_ln_g, v_conv_ln_b, v_conv_w_pw2, v_conv_b_pw2, v_ffn_w_up, v_ffn_w_dw, v_ffn_b_dw, v_ffn_w_down):
    weights = dict(norm_g=norm_g, attn_w_qkv=attn_w_qkv, attn_w_o=attn_w_o, conv_w_pw1=conv_w_pw1, conv_b_pw1=conv_b_pw1,
                   conv_w_dw=conv_w_dw, conv_b_dw=conv_b_dw, conv_ln_g=conv_ln_g, conv_ln_b=conv_ln_b, conv_w_pw2=conv_w_pw2,
                   conv_b_pw2=conv_b_pw2, ffn_w_up=ffn_w_up, ffn_w_dw=ffn_w_dw, ffn_b_dw=ffn_b_dw, ffn_w_down=ffn_w_down)
    mom1 = dict(norm_g=m_norm_g, attn_w_qkv=m_attn_w_qkv, attn_w_o=m_attn_w_o, conv_w_pw1=m_conv_w_pw1, conv_b_pw1=m_conv_b_pw1,
                conv_w_dw=m_conv_w_dw, conv_b_dw=m_conv_b_dw, conv_ln_g=m_conv_ln_g, conv_ln_b=m_conv_ln_b, conv_w_pw2=m_conv_w_pw2,
                conv_b_pw2=m_conv_b_pw2, ffn_w_up=m_ffn_w_up, ffn_w_dw=m_ffn_w_dw, ffn_b_dw=m_ffn_b_dw, ffn_w_down=m_ffn_w_down)
    mom2 = dict(norm_g=v_norm_g, attn_w_qkv=v_attn_w_qkv, attn_w_o=v_attn_w_o, conv_w_pw1=v_conv_w_pw1, conv_b_pw1=v_conv_b_pw1,
                conv_w_dw=v_conv_w_dw, conv_b_dw=v_conv_b_dw, conv_ln_g=v_conv_ln_g, conv_ln_b=v_conv_ln_b, conv_w_pw2=v_conv_w_pw2,
                conv_b_pw2=v_conv_b_pw2, ffn_w_up=v_ffn_w_up, ffn_w_dw=v_ffn_w_dw, ffn_b_dw=v_ffn_b_dw, ffn_w_down=v_ffn_w_down)
    big = ("attn_w_qkv", "attn_w_o", "conv_w_pw1", "conv_w_pw2", "ffn_w_up", "ffn_w_down")
    xi, yi, ci = _position()
    chip = (2 * xi + yi).astype(jnp.int32).reshape(1)
    where = jnp.stack([ci, 2 * xi + yi]).astype(jnp.int32)

    x = x[0]
    target = loss_target[0]
    s, d = x.shape

    small_shapes = [weights[k].shape for k in SMALL]
    small_rows = -(-sum(weights[k].size for k in SMALL) // LANES // 8) * 8
    small_w = _pack_rows([weights[k] for k in SMALL], small_rows)
    def own_slot(shard):
        halves = shard.reshape(1, 2, -1, shard.shape[-1])
        return lax.dynamic_update_slice(lax.empty((N_CHIPS,) + halves.shape[1:], shard.dtype), halves, (chip[0], 0, 0, 0))

    gathered = gather_chips([own_slot(weights[k].astype(bf16)) for k in big] + [own_slot(small_w)], "gather_weights")
    gw = {k: t.reshape((N_CHIPS,) + weights[k].shape) for k, t in zip(big, gathered[:-1])}
    full_small = dict(zip(SMALL, [_join_last(t) for t in _unpack_rows(gathered[-1].reshape(N_CHIPS, small_rows, LANES), small_shapes)]))
    w_qkv, w_o, w_pw1, w_up = gw["attn_w_qkv"], gw["attn_w_o"], gw["conv_w_pw1"], gw["ffn_w_up"]
    w_pw2 = gw["conv_w_pw2"].reshape(1, 1, -1, d)
    w_down = [gw["ffn_w_down"][:, i].reshape(1, 1, -1, d) for i in range(2)]
    gains = full_small["norm_g"]
    gain = lambda i, k: gains[i, k][None]
    b_pw1 = full_small["conv_b_pw1"]
    conv_wdw = _pad_rows(full_small["conv_w_dw"][0], CONV_HALO)
    conv_vecs = _pad_rows(jnp.concatenate([full_small["conv_b_dw"], full_small["conv_ln_g"], full_small["conv_ln_b"]], axis=0), 8)
    b_pw2 = full_small["conv_b_pw2"]
    wbs = [_ffn_pack(full_small["ffn_w_dw"][i], ffn_b_dw[i]) for i in range(2)]
    tabs = _rope_tables(positions[0])
    tabs_t = (tabs[0], -tabs[1], -tabs[2])
    head_ones = (jnp.arange(GROUP_WIDTH)[:, None] // HEAD_DIM == jnp.arange(GROUP_WIDTH)[None, :] // HEAD_DIM).astype(bf16)

    h0 = prenorm(x, gain(0, 0), "l0_prenorm")
    qkv = mm_nn(h0[None], w_qkv, 0, None, 1, f32, 512, w_qkv.shape[-1], "qkv")[0]
    qkvr = rope_qkv(qkv, tabs, "rope")
    att = [attn_fwd(qkvr, g, dil, f"attn_fwd{g}") for g, (_, dil) in enumerate(DILATED_GROUPS)]
    os_, lses = [a[0] for a in att], [a[1] for a in att]
    mixed = mix_fwd(os_, lses, "mix")
    y0 = mm_nn(mixed[None], w_o, 0, None, 1, f32, 1024, w_o.shape[-1], "attn_out")[0]
    x1 = postnorm_residual(x, y0, gain(0, 1), "l0_postnorm")
    y1, ffn0 = _ffn_block(x1, gain(0, 2), gain(0, 3), w_up, 0, w_down[0], wbs[0], "ffn0")
    x2 = postnorm_residual(x1, y1, gain(0, 3), "l0_ffn_postnorm")

    h2 = prenorm(x2, gain(1, 0), "l1_prenorm")
    z = mm_nn(h2[None], w_pw1, 0, b_pw1, 1, f32, 1024, w_pw1.shape[-1], "pw1")[0]
    cpre, sw = conv_module_fwd(z, conv_wdw, conv_vecs, "conv_fwd")
    y2 = mm_nn(sw[None], w_pw2, 0, b_pw2, 1, f32, 512, 512, "pw2")[0]
    x3 = postnorm_residual(x2, y2, gain(1, 1), "l1_postnorm")
    y3, ffn1 = _ffn_block(x3, gain(1, 2), gain(1, 3), w_up, 1, w_down[1], wbs[1], "ffn1")
    dx4, loss_row = final_loss(x3, y3, gain(1, 3), target, "loss")
    loss = lax.psum(loss_row[0, 0], ("x", "y", "c"))

    dgain = [[None] * 4 for _ in range(2)]
    dy3, dgain[1][3], _ = norm_bwd(y3, gain(1, 3), dx4, None, bf16, "l1_ffn_postnorm_bwd")
    dh, d_up1, d_down1, dwb1 = _ffn_block_bwd(dy3, ffn1, w_up, 1, w_down[1], wbs[1], "ffn1")
    dx3, dgain[1][2], _ = norm_bwd(x3, gain(1, 2), dh, dx4, f32, "l1_ffn_prenorm_bwd")

    dy2, dgain[1][1], d_b_pw2 = norm_bwd(y2, gain(1, 1), dx3, None, bf16, "l1_postnorm_bwd")
    d_pw2 = mm_tn(sw[None], dy2[None], 1, 1024, d, 512, "dw_pw2")
    dsw = mm_nt(dy2[None], w_pw2, 0, 1, f32, 1024, d, d, "d_swish")[0]
    dz, d_conv_wdw, d_conv_vecs, d_b_pw1 = conv_module_bwd(dsw, cpre, z, conv_wdw, conv_vecs, "conv_bwd")
    d_pw1 = mm_tn(h2[None], dz[None], N_CHIPS, 1024, d, w_pw1.shape[-1], "dw_pw1")
    dh = mm_nt(dz[None], w_pw1, 0, 1, f32, 1024, d, w_pw1.shape[-1], "d_h2")[0]
    dx2, dgain[1][0], _ = norm_bwd(x2, gain(1, 0), dh, dx3, f32, "l1_prenorm_bwd")

    dy1, dgain[0][3], _ = norm_bwd(y1, gain(0, 3), dx2, None, bf16, "l0_ffn_postnorm_bwd")
    dh, d_up0, d_down0, dwb0 = _ffn_block_bwd(dy1, ffn0, w_up, 0, w_down[0], wbs[0], "ffn0")
    dx1, dgain[0][2], _ = norm_bwd(x1, gain(0, 2), dh, dx2, f32, "l0_ffn_prenorm_bwd")

    dy0, dgain[0][1], _ = norm_bwd(y0, gain(0, 1), dx1, None, bf16, "l0_postnorm_bwd")
    d_wo = mm_tn(mixed[None], dy0[None], N_CHIPS, 1024, GROUP_WIDTH, w_o.shape[-1], "dw_o")
    dmixed = mm_nt(dy0[None], w_o, 0, 1, f32, 1024, GROUP_WIDTH, w_o.shape[-1], "d_mixed")[0]
    mb = mix_bwd(dmixed, os_, lses, head_ones, "mix_bwd")
    parts = [attn_bwd(qkvr, mb[g], lses[g], mb[3 + g], g, dil, f"attn_bwd{g}") for g, (_, dil) in enumerate(DILATED_GROUPS)]
    dqkv = dqkv_assemble(parts, tabs_t, "dqkv")
    d_qkv = mm_tn(h0[None], dqkv[None], N_CHIPS, 1024, d, w_qkv.shape[-1], "dw_qkv")
    dh = mm_nt(dqkv[None], w_qkv, 0, 1, f32, 1024, d, w_qkv.shape[-1], "d_h0")[0]
    grad_x, dgain[0][0], _ = norm_bwd(x, gain(0, 0), dh, dx1, f32, "l0_prenorm_bwd")

    stack_layers = lambda a, b: jnp.stack([a, b], axis=1)
    gbig = dict(
        attn_w_qkv=d_qkv, attn_w_o=d_wo, conv_w_pw1=d_pw1, conv_w_pw2=d_pw2[0].reshape(N_CHIPS, -1, d),
        ffn_w_up=stack_layers(d_up0, d_up1),
        ffn_w_down=stack_layers(d_down0[0].reshape(N_CHIPS, -1, d), d_down1[0].reshape(N_CHIPS, -1, d)))
    d_ffn_dw, d_ffn_b = zip(*[_ffn_unpack(t) for t in (dwb0, dwb1)])
    gsmall = dict(
        norm_g=jnp.stack([jnp.concatenate(row, axis=0) for row in dgain], axis=0),
        conv_b_pw1=d_b_pw1, conv_w_dw=d_conv_wdw[None, :CONV_KERNEL], conv_b_dw=d_conv_vecs[0:1], conv_ln_g=d_conv_vecs[1:2],
        conv_ln_b=d_conv_vecs[2:3], conv_b_pw2=d_b_pw2, ffn_w_dw=jnp.stack(d_ffn_dw, axis=0))
    bias_rows = ffn_b_dw.size // LANES
    small_g = jnp.concatenate([jnp.concatenate([_pack_rows([_split_last(gsmall[k])[p] for k in SMALL], small_rows)
                                                for p in range(N_CHIPS)], axis=0),
                               jnp.stack(d_ffn_b, axis=0).reshape(bias_rows, LANES)], axis=0)

    small_sum = sum_devices(gather_devices(small_g, "gather_small_grads"), "sum_small_grads")
    my_small = lax.dynamic_slice_in_dim(small_sum, chip[0] * small_rows, small_rows, axis=0)
    g_small = jnp.concatenate([my_small, small_sum[N_CHIPS * small_rows:]], axis=0)

    halves = [gbig[k].reshape(N_CHIPS, 2, -1, gbig[k].shape[-1]) for k in big]
    from_core = swap_core_halves(halves, "swap_core_halves")
    chip_parts = [add_core_halves(a, b, where, f"add_core_{k}") for k, a, b in zip(big, halves, from_core)]
    from_chips = scatter_chips(chip_parts, "scatter_chips")
    mine = [add_chip_parts(a, b, r, where, f"add_chips_{k}") for k, a, b, r in zip(big, halves, from_core, from_chips)]
    shard_grads = share_core_halves(mine, "share_core_halves")

    grads, deltas, new_m, new_v = {}, {}, {}, {}
    for k, g2 in zip(big, shard_grads):
        shp = weights[k].shape
        g2 = g2.reshape(-1, shp[-1])
        dl, nm, nv = adamw(weights[k].reshape(g2.shape), g2, mom1[k].reshape(g2.shape), mom2[k].reshape(g2.shape), f"adamw_{k}")
        grads[k], deltas[k], new_m[k], new_v[k] = (t.reshape(shp) for t in (g2, dl, nm, nv))
    pack_state = lambda src: jnp.concatenate([_pack_rows([src[k] for k in SMALL], small_rows), src["ffn_b_dw"].reshape(bias_rows, LANES)], axis=0)
    small_out = (g_small,) + tuple(adamw(pack_state(weights), g_small, pack_state(mom1), pack_state(mom2), "adamw_small"))
    for dst, packed in zip((grads, deltas, new_m, new_v), small_out):
        for k, t in zip(SMALL, _unpack_rows(packed[:small_rows], small_shapes)):
            dst[k] = t
        dst["ffn_b_dw"] = packed[small_rows:].reshape(ffn_b_dw.shape)

    order = ("norm_g", "attn_w_qkv", "attn_w_o", "conv_w_pw1", "conv_b_pw1", "conv_w_dw", "conv_b_dw", "conv_ln_g", "conv_ln_b",
             "conv_w_pw2", "conv_b_pw2", "ffn_w_up", "ffn_w_dw", "ffn_b_dw", "ffn_w_down")
    return (loss, grad_x[None], *[grads[k] for k in order], *[deltas[k] for k in order], *[new_m[k] for k in order],
            *[new_v[k] for k in order])
```

```python
import functools

import jax
import jax.numpy as jnp
from jax import lax
from jax.experimental import pallas as pl
from jax.experimental.pallas import tpu as pltpu

f32 = jnp.float32
bf16 = jnp.bfloat16
SDS = jax.ShapeDtypeStruct

EPS = 1e-6
HEAD_DIM = 64
N_SLOTS = 8
GROUP_WIDTH = N_SLOTS * HEAD_DIM
DILATED_GROUPS = ((128, 1), (512, 4), (2048, 16))
N_GROUPS = 3
SPAN = 128
ROT_DIM = HEAD_DIM // 4
ROPE_THETA = 500000.0
CONV_KERNEL = 31
CONV_HALO = 32
FFN_CONV = 3
FFN_HALO = 16
FFN_TAIL = 8
STRIP_ROWS = 64
ADAM_LR, ADAM_B1, ADAM_B2, ADAM_EPS, ADAM_WD, ADAM_STEP = 0.001, 0.9, 0.999, 1e-08, 0.01, 10
LANES = 128
N_CHIPS = 4
VMEM_LIMIT_BYTES = 56 * 1024 * 1024
NEG_BIG = -1e30
MESH = pl.DeviceIdType.MESH
ANY = pl.BlockSpec(memory_space=pl.ANY)


def _params(*sem):
    return pltpu.CompilerParams(dimension_semantics=sem, vmem_limit_bytes=VMEM_LIMIT_BYTES)


def _sigmoid(v):
    return 1.0 / (1.0 + jnp.exp(-v))


def _dot_nt(a, b):
    return lax.dot_general(a, b, (((1,), (1,)), ((), ())), preferred_element_type=f32)


def _dot_tn(a, b):
    return lax.dot_general(a, b, (((0,), (0,)), ((), ())), preferred_element_type=f32)


def mm_nn(x, w, li, bias, out_blocks, out_dtype, tm, tn, name):
    nq, m, kq = x.shape
    p, _, k, n = w.shape
    assert k == nq * kq and n % tn == 0 and m % tm == 0
    on = p * n // out_blocks
    assert on % tn == 0
    nj, onj = n // tn, on // tn

    def body(*refs):
        if bias is None:
            x_ref, w_ref, o_ref, *scr = refs
            b_ref = None
        else:
            x_ref, w_ref, b_ref, o_ref, *scr = refs
        part = jnp.dot(x_ref[...], w_ref[...], preferred_element_type=f32)

        def finish(acc):
            if b_ref is not None:
                acc = acc + b_ref[...]
            o_ref[...] = acc.astype(o_ref.dtype)

        if nq == 1:
            finish(part)
        else:
            acc_ref, = scr
            q = pl.program_id(2)

            @pl.when(q == 0)
            def _():
                acc_ref[...] = part

            @pl.when(q > 0)
            def _():
                acc_ref[...] += part

            @pl.when(q == nq - 1)
            def _():
                finish(acc_ref[...])

    in_specs = [pl.BlockSpec((None, tm, kq), lambda j, i, q: (q, i, 0)),
                pl.BlockSpec((None, None, kq, tn), lambda j, i, q: (j // nj, li, q, j % nj))]
    args = [x, w]
    if bias is not None:
        in_specs.append(pl.BlockSpec((1, tn), lambda j, i, q: (0, j)))
        args.append(bias)
    return pl.pallas_call(
        body, grid=(p * nj, m // tm, nq), in_specs=in_specs,
        out_specs=pl.BlockSpec((None, tm, tn), lambda j, i, q: (j // onj, i, j % onj)),
        out_shape=SDS((out_blocks, m, on), out_dtype),
        scratch_shapes=[] if nq == 1 else [pltpu.VMEM((tm, tn), f32)],
        compiler_params=_params("parallel", "parallel", "arbitrary"), name=name)(*args)


def mm_nt(dy, w, li, out_blocks, out_dtype, tm, tk, tn, name):
    ob, m, on = dy.shape
    p, _, k, n = w.shape
    assert ob * on == p * n and n % tn == 0 and on % tn == 0 and k % tk == 0 and m % tm == 0
    kq = k // out_blocks
    assert kq % tk == 0
    nj, onj, kqj = n // tn, on // tn, kq // tk
    nr = p * nj

    def body(dy_ref, w_ref, o_ref, *scr):
        part = _dot_nt(dy_ref[...], w_ref[...])
        if nr == 1:
            o_ref[...] = part.astype(o_ref.dtype)
        else:
            acc_ref, = scr
            r = pl.program_id(2)

            @pl.when(r == 0)
            def _():
                acc_ref[...] = part

            @pl.when(r > 0)
            def _():
                acc_ref[...] += part

            @pl.when(r == nr - 1)
            def _():
                o_ref[...] = acc_ref[...].astype(o_ref.dtype)

    return pl.pallas_call(
        body, grid=(k // tk, m // tm, nr),
        in_specs=[pl.BlockSpec((None, tm, tn), lambda kt, i, r: (r // onj, i, r % onj)),
                  pl.BlockSpec((None, None, tk, tn), lambda kt, i, r: (r // nj, li, kt, r % nj))],
        out_specs=pl.BlockSpec((None, tm, tk), lambda kt, i, r: (kt // kqj, i, kt % kqj)),
        out_shape=SDS((out_blocks, m, kq), out_dtype),
        scratch_shapes=[] if nr == 1 else [pltpu.VMEM((tm, tk), f32)],
        compiler_params=_params("parallel", "parallel", "arbitrary"), name=name)(dy, w)


def mm_tn(x, dy, p, tm, tk, tn, name):
    nq, m, kq = x.shape
    ob, _, on = dy.shape
    k = nq * kq
    n = ob * on // p
    assert n % tn == 0 and on % tn == 0 and kq % tk == 0 and m % tm == 0
    nj, onj, kqj = n // tn, on // tn, kq // tk

    def body(x_ref, dy_ref, o_ref):
        part = _dot_tn(x_ref[...], dy_ref[...])
        i = pl.program_id(2)

        @pl.when(i == 0)
        def _():
            o_ref[...] = part

        @pl.when(i > 0)
        def _():
            o_ref[...] += part

    return pl.pallas_call(
        body, grid=(k // tk, p * nj, m // tm),
        in_specs=[pl.BlockSpec((None, tm, tk), lambda kt, j, i: (kt // kqj, i, kt % kqj)),
                  pl.BlockSpec((None, tm, tn), lambda kt, j, i: (j // onj, i, j % onj))],
        out_specs=pl.BlockSpec((None, tk, tn), lambda kt, j, i: (j // nj, kt, j % nj)),
        out_shape=SDS((p, k, n), f32),
        compiler_params=_params("parallel", "parallel", "arbitrary"), name=name)(x, dy)


def _row_tile(s):
    return min(s, 512)


def _rows(tr, d):
    return pl.BlockSpec((tr, d), lambda i: (i, 0))


def _fixed(r, d):
    return pl.BlockSpec((r, d), lambda i: (0, 0))


def _rms(xv):
    return lax.rsqrt(jnp.mean(xv * xv, axis=-1, keepdims=True) + EPS)


def prenorm(x, g, name):
    s, d = x.shape
    tr = _row_tile(s)

    def body(x_ref, g_ref, o_ref):
        xv = x_ref[...]
        o_ref[...] = (xv * _rms(xv) * g_ref[...]).astype(o_ref.dtype)

    return pl.pallas_call(body, grid=(s // tr,), in_specs=[_rows(tr, d), _fixed(1, d)], out_specs=_rows(tr, d),
                          out_shape=SDS((s, d), bf16), compiler_params=_params("parallel"), name=name)(x, g)


def postnorm_residual(x, y, g, name):
    s, d = x.shape
    tr = _row_tile(s)

    def body(x_ref, y_ref, g_ref, o_ref):
        yv = y_ref[...]
        o_ref[...] = x_ref[...] + yv * _rms(yv) * g_ref[...]

    return pl.pallas_call(body, grid=(s // tr,), in_specs=[_rows(tr, d), _rows(tr, d), _fixed(1, d)],
                          out_specs=_rows(tr, d), out_shape=SDS((s, d), f32), compiler_params=_params("parallel"),
                          name=name)(x, y, g)


def norm_bwd(xin, g, dout, res, out_dtype, name):
    s, d = xin.shape
    tr = _row_tile(s)

    def body(*refs):
        if res is None:
            x_ref, g_ref, do_ref, dx_ref, dg_ref, cs_ref = refs
            r_ref = None
        else:
            x_ref, g_ref, do_ref, r_ref, dx_ref, dg_ref, cs_ref = refs
        xv = x_ref[...]
        r = _rms(xv)
        xh = xv * r
        dov = do_ref[...].astype(f32)
        gy = dov * g_ref[...]
        dx = r * (gy - xh * jnp.mean(gy * xh, axis=-1, keepdims=True))
        if r_ref is not None:
            dx = dx + r_ref[...]
        dx_ref[...] = dx.astype(dx_ref.dtype)
        dg = jnp.sum(dov * xh, axis=0, keepdims=True)
        cs = jnp.sum(dx, axis=0, keepdims=True)
        i = pl.program_id(0)

        @pl.when(i == 0)
        def _():
            dg_ref[...] = dg
            cs_ref[...] = cs

        @pl.when(i > 0)
        def _():
            dg_ref[...] += dg
            cs_ref[...] += cs

    in_specs = [_rows(tr, d), _fixed(1, d), _rows(tr, d)]
    args = [xin, g, dout]
    if res is not None:
        in_specs.append(_rows(tr, d))
        args.append(res)
    return pl.pallas_call(body, grid=(s // tr,), in_specs=in_specs,
                          out_specs=[_rows(tr, d), _fixed(1, d), _fixed(1, d)],
                          out_shape=[SDS((s, d), out_dtype), SDS((1, d), f32), SDS((1, d), f32)],
                          compiler_params=_params("arbitrary"), name=name)(*args)


def final_loss(x, y, g, target, name):
    s, d = x.shape
    tr = _row_tile(s)
    nt = s // tr

    def body(x_ref, y_ref, g_ref, t_ref, dx_ref, loss_ref, acc_ref):
        yv = y_ref[...]
        diff = x_ref[...] + yv * _rms(yv) * g_ref[...] - t_ref[...]
        dx_ref[...] = diff * (1.0 / d)
        sq = jnp.sum(diff * diff, axis=0, keepdims=True)
        i = pl.program_id(0)

        @pl.when(i == 0)
        def _():
            acc_ref[...] = sq

        @pl.when(i > 0)
        def _():
            acc_ref[...] += sq

        @pl.when(i == nt - 1)
        def _():
            total = jnp.sum(acc_ref[...], axis=1, keepdims=True) * (0.5 / d)
            loss_ref[...] = jnp.broadcast_to(total, (1, LANES))

    return pl.pallas_call(body, grid=(nt,), in_specs=[_rows(tr, d), _rows(tr, d), _fixed(1, d), _rows(tr, d)],
                          out_specs=[_rows(tr, d), _fixed(1, LANES)],
                          out_shape=[SDS((s, d), f32), SDS((1, LANES), f32)],
                          scratch_shapes=[pltpu.VMEM((1, d), f32)],
                          compiler_params=_params("arbitrary"), name=name)(x, y, g, target)


def _rotate_chunk(v, tc, ta, tb):
    return v * tc + pltpu.roll(v, LANES - ROT_DIM // 2, axis=1) * ta + pltpu.roll(v, ROT_DIM // 2, axis=1) * tb


def rope_qkv(qkv, tabs, name):
    s, w = qkv.shape
    tr = min(s, 256)
    n_rot = 2 * N_GROUPS * GROUP_WIDTH // LANES

    def body(x_ref, tc_ref, ta_ref, tb_ref, o_ref):
        tc, ta, tb = tc_ref[...], ta_ref[...], tb_ref[...]
        for ch in range(w // LANES):
            cols = slice(ch * LANES, (ch + 1) * LANES)
            v = x_ref[:, cols]
            if ch < n_rot:
                v = _rotate_chunk(v, tc, ta, tb)
            o_ref[:, cols] = v.astype(o_ref.dtype)

    tab = _rows(tr, LANES)
    return pl.pallas_call(body, grid=(s // tr,), in_specs=[_rows(tr, w), tab, tab, tab], out_specs=_rows(tr, w),
                          out_shape=SDS((s, w), bf16), compiler_params=_params("parallel"), name=name)(qkv, *tabs)


def _attn_mask(j):
    row = lax.broadcasted_iota(jnp.int32, (SPAN, 2 * SPAN), 0)
    col = lax.broadcasted_iota(jnp.int32, (SPAN, 2 * SPAN), 1)
    prev = jnp.logical_and(jnp.logical_and(col < SPAN, col >= row), j > 0)
    return jnp.logical_or(prev, jnp.logical_and(col >= SPAN, col - SPAN <= row))


def _attn_in_specs(g, n_col_blocks):
    def at(kind, prev):
        def index(r, j):
            return (jnp.maximum(j - 1, 0) if prev else j, r * n_col_blocks + kind * N_GROUPS + g)
        return pl.BlockSpec((SPAN, GROUP_WIDTH), index)
    return [at(0, False), at(1, False), at(1, True), at(2, False), at(2, True)]


def attn_fwd(qkvr, g, dil, name):
    s, w = qkvr.shape
    l = s // dil
    nb = l // SPAN
    a = qkvr.reshape(l, dil * w)

    def body(q_ref, ko_ref, kp_ref, vo_ref, vp_ref, o_ref, lse_ref, k_scr, v_scr):
        mask = _attn_mask(pl.program_id(1))
        k_scr[0:SPAN, :] = kp_ref[...]
        k_scr[SPAN:2 * SPAN, :] = ko_ref[...]
        v_scr[0:SPAN, :] = vp_ref[...]
        v_scr[SPAN:2 * SPAN, :] = vo_ref[...]
        for h in range(N_SLOTS):
            hs = slice(h * HEAD_DIM, (h + 1) * HEAD_DIM)
            sc = jnp.where(mask, _dot_nt(q_ref[:, hs], k_scr[:, hs]) * (HEAD_DIM ** -0.5), NEG_BIG)
            mx = jnp.max(sc, axis=-1, keepdims=True)
            p = jnp.exp(sc - mx)
            den = jnp.sum(p, axis=-1, keepdims=True)
            o_ref[:, hs] = jnp.dot(p.astype(bf16), v_scr[:, hs], preferred_element_type=f32) / den
            lse_ref[:, hs] = jnp.broadcast_to(mx + jnp.log(den), (SPAN, HEAD_DIM))

    out = pl.BlockSpec((SPAN, GROUP_WIDTH), lambda r, j: (j, r))
    o, lse = pl.pallas_call(
        body, grid=(dil, nb), in_specs=_attn_in_specs(g, w // GROUP_WIDTH), out_specs=[out, out],
        out_shape=[SDS((l, dil * GROUP_WIDTH), f32)] * 2,
        scratch_shapes=[pltpu.VMEM((2 * SPAN, GROUP_WIDTH), bf16)] * 2,
        compiler_params=_params("parallel", "parallel"), name=name)(a, a, a, a, a)
    return o.reshape(s, GROUP_WIDTH), lse.reshape(s, GROUP_WIDTH)


def _group_weights(lses):
    mx = jnp.maximum(jnp.maximum(lses[0], lses[1]), lses[2])
    es = [jnp.exp(v - mx) for v in lses]
    inv = 1.0 / (es[0] + es[1] + es[2])
    return [e * inv for e in es]


def mix_fwd(os_, lses, name):
    s, w = os_[0].shape
    tr = _row_tile(s)

    def body(o0, o1, o2, l0, l1, l2, out_ref):
        wg = _group_weights([l0[...], l1[...], l2[...]])
        out_ref[...] = (wg[0] * o0[...] + wg[1] * o1[...] + wg[2] * o2[...]).astype(out_ref.dtype)

    return pl.pallas_call(body, grid=(s // tr,), in_specs=[_rows(tr, w)] * 6, out_specs=_rows(tr, w),
                          out_shape=SDS((s, w), bf16), compiler_params=_params("parallel"), name=name)(*os_, *lses)


def mix_bwd(dmixed, os_, lses, head_ones, name):
    s, w = dmixed.shape
    tr = _row_tile(s)

    def head_sum(t, ones):
        hi = t.astype(bf16)
        lo = (t - hi.astype(f32)).astype(bf16)
        return jnp.dot(hi, ones, preferred_element_type=f32) + jnp.dot(lo, ones, preferred_element_type=f32)

    def body(dm_ref, o0, o1, o2, l0, l1, l2, ones_ref, d0, d1, d2, p0, p1, p2):
        dm = dm_ref[...]
        ones = ones_ref[...]
        wg = _group_weights([l0[...], l1[...], l2[...]])
        mean = sum(wg[k] * head_sum(dm * o[...], ones) for k, o in enumerate((o0, o1, o2)))
        for k, (d_ref, p_ref) in enumerate(((d0, p0), (d1, p1), (d2, p2))):
            d_ref[...] = (wg[k] * dm).astype(d_ref.dtype)
            p_ref[...] = wg[k] * mean

    return pl.pallas_call(body, grid=(s // tr,), in_specs=[_rows(tr, w)] * 7 + [_fixed(w, w)],
                          out_specs=[_rows(tr, w)] * 6,
                          out_shape=[SDS((s, w), bf16)] * 3 + [SDS((s, w), f32)] * 3,
                          compiler_params=_params("parallel"), name=name)(dmixed, *os_, *lses, head_ones)


def attn_bwd(qkvr, do, lse, dterm, g, dil, name):
    s, w = qkvr.shape
    l = s // dil
    nb = l // SPAN
    a = qkvr.reshape(l, dil * w)
    phased = lambda t: t.reshape(l, dil * GROUP_WIDTH)

    def body(q_ref, ko_ref, kp_ref, vo_ref, vp_ref, do_ref, lse_ref, dt_ref, dq_ref, dko_ref, dkp_ref, dvo_ref, dvp_ref,
             k_scr, v_scr):
        mask = _attn_mask(pl.program_id(1))
        scale = HEAD_DIM ** -0.5
        k_scr[0:SPAN, :] = kp_ref[...]
        k_scr[SPAN:2 * SPAN, :] = ko_ref[...]
        v_scr[0:SPAN, :] = vp_ref[...]
        v_scr[SPAN:2 * SPAN, :] = vo_ref[...]
        for h in range(N_SLOTS):
            hs = slice(h * HEAD_DIM, (h + 1) * HEAD_DIM)
            one = slice(h * HEAD_DIM, h * HEAD_DIM + 1)
            q, kk, dov = q_ref[:, hs], k_scr[:, hs], do_ref[:, hs]
            p = jnp.exp(jnp.where(mask, _dot_nt(q, kk) * scale - lse_ref[:, one], NEG_BIG))
            ds = (p * (_dot_nt(dov, v_scr[:, hs]) - dt_ref[:, one]) * scale).astype(bf16)
            dq_ref[:, hs] = jnp.dot(ds, kk, preferred_element_type=f32)
            dk = _dot_tn(ds, q)
            dv = _dot_tn(p.astype(bf16), dov)
            dkp_ref[:, hs] = dk[:SPAN]
            dko_ref[:, hs] = dk[SPAN:]
            dvp_ref[:, hs] = dv[:SPAN]
            dvo_ref[:, hs] = dv[SPAN:]

    blk = pl.BlockSpec((SPAN, GROUP_WIDTH), lambda r, j: (j, r))
    outs = pl.pallas_call(
        body, grid=(dil, nb), in_specs=_attn_in_specs(g, w // GROUP_WIDTH) + [blk, blk, blk], out_specs=[blk] * 5,
        out_shape=[SDS((l, dil * GROUP_WIDTH), f32)] * 5,
        scratch_shapes=[pltpu.VMEM((2 * SPAN, GROUP_WIDTH), bf16)] * 2,
        compiler_params=_params("parallel", "parallel"), name=name)(a, a, a, a, a, phased(do), phased(lse), phased(dterm))
    return [t.reshape(s, GROUP_WIDTH) for t in outs]


def dqkv_assemble(parts, tabs, name):
    s = parts[0][0].shape[0]
    nblk = s // SPAN
    width = 3 * N_GROUPS * GROUP_WIDTH

    def body(*refs):
        ins, (tc_ref, ta_ref, tb_ref, o_ref) = refs[:5 * N_GROUPS], refs[5 * N_GROUPS:]
        tc, ta, tb = tc_ref[...], ta_ref[...], tb_ref[...]
        i = pl.program_id(0)
        for g, (_, dil) in enumerate(DILATED_GROUPS):
            dq, dko, dkp, dvo, dvp = ins[5 * g:5 * g + 5]
            has_next = i + dil < nblk
            for ch in range(GROUP_WIDTH // LANES):
                cols = slice(ch * LANES, (ch + 1) * LANES)
                base = g * GROUP_WIDTH + ch * LANES
                dk = dko[:, cols] + jnp.where(has_next, dkp[:, cols], 0.0)
                dv = dvo[:, cols] + jnp.where(has_next, dvp[:, cols], 0.0)
                o_ref[:, base:base + LANES] = _rotate_chunk(dq[:, cols], tc, ta, tb).astype(o_ref.dtype)
                kb = N_GROUPS * GROUP_WIDTH + base
                o_ref[:, kb:kb + LANES] = _rotate_chunk(dk, tc, ta, tb).astype(o_ref.dtype)
                vb = 2 * N_GROUPS * GROUP_WIDTH + base
                o_ref[:, vb:vb + LANES] = dv.astype(o_ref.dtype)

    here = _rows(SPAN, GROUP_WIDTH)
    in_specs, args = [], []
    for g, (_, dil) in enumerate(DILATED_GROUPS):
        ahead = pl.BlockSpec((SPAN, GROUP_WIDTH), functools.partial(lambda i, dil: (jnp.minimum(i + dil, nblk - 1), 0), dil=dil))
        in_specs += [here, here, ahead, here, ahead]
        args += list(parts[g])
    tab = _rows(SPAN, LANES)
    return pl.pallas_call(body, grid=(nblk,), in_specs=in_specs + [tab] * 3, out_specs=_rows(SPAN, width),
                          out_shape=SDS((s, width), bf16), compiler_params=_params("parallel"), name=name)(*args, *tabs)


def ffn_act_fwd(u, wb, name):
    _, nbk, s, c = u.shape
    tr = min(s, 256)

    def body(u_ref, h_ref, wb_ref, a_ref, ug_ref, su, sg):
        first = pl.program_id(1) == 0
        for lc in range(c // LANES):
            ln = slice(lc * LANES, (lc + 1) * LANES)
            for half, scr in enumerate((su, sg)):
                scr[0:FFN_HALO, ln] = jnp.where(first, 0.0, h_ref[half, :, ln].astype(f32))
            for r0 in range(0, tr, STRIP_ROWS):
                rows = slice(r0, r0 + STRIP_ROWS)
                conv = []
                for half, scr in enumerate((su, sg)):
                    xv = u_ref[half, rows, ln].astype(f32)
                    scr[FFN_HALO + r0:FFN_HALO + r0 + STRIP_ROWS, ln] = xv
                    acc = wb_ref[half, FFN_CONV:FFN_CONV + 1, ln] + wb_ref[half, FFN_CONV - 1:FFN_CONV, ln] * xv
                    for k in range(FFN_CONV - 1):
                        acc = acc + wb_ref[half, k:k + 1, ln] * scr[pl.ds(FFN_HALO + r0 - (FFN_CONV - 1) + k, STRIP_ROWS), ln]
                    ug_ref[half, rows, ln] = acc.astype(ug_ref.dtype)
                    conv.append(acc)
                up, gate = conv
                a_ref[rows, ln] = (gate * _sigmoid(gate) * up).astype(a_ref.dtype)

    both = pl.BlockSpec((2, None, tr, c), lambda p, i: (0, p, i, 0))
    return pl.pallas_call(
        body, grid=(nbk, s // tr),
        in_specs=[both,
                  pl.BlockSpec((2, None, FFN_HALO, c), lambda p, i: (0, p, jnp.maximum(i * (tr // FFN_HALO) - 1, 0), 0)),
                  pl.BlockSpec((None, 2, 8, c), lambda p, i: (p, 0, 0, 0))],
        out_specs=[pl.BlockSpec((None, tr, c), lambda p, i: (p, i, 0)), both],
        out_shape=[SDS((nbk, s, c), bf16), SDS(u.shape, bf16)],
        scratch_shapes=[pltpu.VMEM((tr + FFN_HALO, c), f32)] * 2,
        compiler_params=_params("parallel", "arbitrary"), name=name)(u, u, wb)


def ffn_act_bwd(da, ug, u, wb, name):
    _, nbk, s, c = u.shape
    tr = min(s, 256)
    nt = s // tr

    def body(da_ref, ug_ref, u_ref, wb_ref, du_ref, dwb_ref, eu, eg):
        step = pl.program_id(1)

        @pl.when(step == 0)
        def _():
            eu[tr:tr + FFN_TAIL, :] = jnp.zeros((FFN_TAIL, c), f32)
            eg[tr:tr + FFN_TAIL, :] = jnp.zeros((FFN_TAIL, c), f32)
            dwb_ref[...] = jnp.zeros(dwb_ref.shape, f32)

        fold = lambda t: jnp.sum(t.reshape(STRIP_ROWS // 8, 8, LANES), axis=0)
        for lc in range(c // LANES):
            ln = slice(lc * LANES, (lc + 1) * LANES)
            sums = [[jnp.zeros((8, LANES), f32) for _ in range(FFN_CONV + 1)] for _ in range(2)]
            for r0 in reversed(range(0, tr, STRIP_ROWS)):
                rows = slice(r0, r0 + STRIP_ROWS)
                up, gate = ug_ref[0, rows, ln].astype(f32), ug_ref[1, rows, ln].astype(f32)
                sig = _sigmoid(gate)
                dav = da_ref[rows, ln].astype(f32)
                grads = (dav * (gate * sig), dav * up * (sig * (1.0 + gate * (1.0 - sig))))
                for half, ext in enumerate((eu, eg)):
                    dv = grads[half]
                    ext[rows, ln] = dv
                    xv = u_ref[half, rows, ln].astype(f32)
                    acc = None
                    for k in range(FFN_CONV):
                        ahead = dv if k == FFN_CONV - 1 else ext[pl.ds(r0 + FFN_CONV - 1 - k, STRIP_ROWS), ln]
                        term = wb_ref[half, k:k + 1, ln] * ahead
                        acc = term if acc is None else acc + term
                        sums[half][k] = sums[half][k] + fold(xv * ahead)
                    sums[half][FFN_CONV] = sums[half][FFN_CONV] + fold(dv)
                    du_ref[half, rows, ln] = acc.astype(du_ref.dtype)
            for half, ext in enumerate((eu, eg)):
                ext[tr:tr + FFN_TAIL, ln] = ext[0:FFN_TAIL, ln]
                for k in range(FFN_CONV + 1):
                    dwb_ref[half, k:k + 1, ln] += jnp.sum(sums[half][k], axis=0, keepdims=True)

    rev = lambda i: nt - 1 - i
    both = pl.BlockSpec((2, None, tr, c), lambda p, i: (0, p, rev(i), 0))
    return pl.pallas_call(
        body, grid=(nbk, nt),
        in_specs=[pl.BlockSpec((None, tr, c), lambda p, i: (p, rev(i), 0)), both, both,
                  pl.BlockSpec((None, 2, 8, c), lambda p, i: (p, 0, 0, 0))],
        out_specs=[both, pl.BlockSpec((None, 2, 8, c), lambda p, i: (p, 0, 0, 0))],
        out_shape=[SDS((2, nbk, s, c), bf16), SDS((nbk, 2, 8, c), f32)],
        scratch_shapes=[pltpu.VMEM((tr + FFN_TAIL, c), f32)] * 2,
        compiler_params=_params("parallel", "arbitrary"), name=name)(da, ug, u, wb)


def _glu(zv, c):
    return zv[:, :c] * _sigmoid(zv[:, c:])


def _conv_fill(z_ref, h_ref, scr, first, tr, c):
    scr[0:CONV_HALO, :] = jnp.where(first, 0.0, _glu(h_ref[...], c))
    scr[CONV_HALO:CONV_HALO + tr, :] = _glu(z_ref[...], c)


def _conv_taps(b):
    return [(a, CONV_KERNEL - 1 - 8 * a - b) for a in range(CONV_HALO // 8) if CONV_KERNEL - 1 - 8 * a - b >= 0]


def _layernorm_parts(cv):
    mu = jnp.mean(cv, axis=-1, keepdims=True)
    cen = cv - mu
    rstd = lax.rsqrt(jnp.mean(cen * cen, axis=-1, keepdims=True) + EPS)
    return cen * rstd, rstd


def conv_module_fwd(z, wdw, vecs, name):
    s, c2 = z.shape
    c = c2 // 2
    tr = min(s, 256)

    def body(z_ref, h_ref, w_ref, v_ref, c_ref, s_ref, scr, zb):
        _conv_fill(z_ref, h_ref, scr, pl.program_id(0) == 0, tr, c)
        acc = jnp.broadcast_to(v_ref[0:1, :], (tr, c))
        for b in range(8):
            part = None
            for a, j in _conv_taps(b):
                term = w_ref[j:j + 1, :] * scr[pl.ds(CONV_HALO - 8 - 8 * a, tr + 8), :]
                part = term if part is None else part + term
            if b == 0:
                acc = acc + part[8:]
            else:
                zb[...] = part
                acc = acc + zb[pl.ds(8 - b, tr), :]
        c_ref[...] = acc
        chat, _ = _layernorm_parts(acc)
        ln = chat * v_ref[1:2, :] + v_ref[2:3, :]
        s_ref[...] = (ln * _sigmoid(ln)).astype(s_ref.dtype)

    return pl.pallas_call(
        body, grid=(s // tr,),
        in_specs=[_rows(tr, c2), pl.BlockSpec((CONV_HALO, c2), lambda i: (jnp.maximum(i * (tr // CONV_HALO) - 1, 0), 0)),
                  _fixed(CONV_HALO, c), _fixed(8, c)],
        out_specs=[_rows(tr, c), _rows(tr, c)], out_shape=[SDS((s, c), f32), SDS((s, c), bf16)],
        scratch_shapes=[pltpu.VMEM((tr + CONV_HALO, c), f32), pltpu.VMEM((tr + 8, c), f32)],
        compiler_params=_params("arbitrary"), name=name)(z, z, wdw, vecs)


def conv_module_bwd(ds, cpre, z, wdw, vecs, name):
    s, c2 = z.shape
    c = c2 // 2
    tr = min(s, 256)
    nt = s // tr

    def body(ds_ref, c_ref, z_ref, w_ref, v_ref, dz_ref, dw_ref, dv_ref, db_ref, ext, dsh):
        step = pl.program_id(0)
        chat, rstd = _layernorm_parts(c_ref[...])
        gain = v_ref[1:2, :]
        ln = chat * gain + v_ref[2:3, :]
        sig = _sigmoid(ln)
        dln = ds_ref[...].astype(f32) * (sig * (1.0 + ln * (1.0 - sig)))
        gy = dln * gain
        dc = rstd * (gy - jnp.mean(gy, axis=-1, keepdims=True) - chat * jnp.mean(gy * chat, axis=-1, keepdims=True))

        @pl.when(step == 0)
        def _():
            ext[tr:tr + CONV_HALO, :] = jnp.zeros((CONV_HALO, c), f32)
            dw_ref[...] = jnp.zeros(dw_ref.shape, f32)
            dv_ref[...] = jnp.zeros(dv_ref.shape, f32)
            db_ref[...] = jnp.zeros(db_ref.shape, f32)

        dv_ref[0:1, :] += jnp.sum(dc, axis=0, keepdims=True)
        dv_ref[1:2, :] += jnp.sum(dln * chat, axis=0, keepdims=True)
        dv_ref[2:3, :] += jnp.sum(dln, axis=0, keepdims=True)
        ext[0:tr, :] = dc
        zv = z_ref[...]
        a, sg = zv[:, :c], _sigmoid(zv[:, c:])
        uv = a * sg
        du = jnp.zeros((tr, c), f32)
        for b in range(8):
            src = ext
            if b:
                dsh[...] = ext[pl.ds(b, tr + CONV_HALO - 8), :]
                src = dsh
            for a8, j in _conv_taps(b):
                ahead = src[pl.ds(8 * a8, tr), :]
                du = du + w_ref[j:j + 1, :] * ahead
                dw_ref[j:j + 1, :] += jnp.sum(uv * ahead, axis=0, keepdims=True)
        ext[tr:tr + CONV_HALO, :] = ext[0:CONV_HALO, :]
        da = du * sg
        dg = du * a * (sg * (1.0 - sg))
        dz_ref[:, :c] = da.astype(dz_ref.dtype)
        dz_ref[:, c:] = dg.astype(dz_ref.dtype)
        db_ref[:, :c] += jnp.sum(da, axis=0, keepdims=True)
        db_ref[:, c:] += jnp.sum(dg, axis=0, keepdims=True)

    rev = lambda i: nt - 1 - i
    back = lambda d: pl.BlockSpec((tr, d), lambda i: (rev(i), 0))
    return pl.pallas_call(
        body, grid=(nt,),
        in_specs=[back(c), back(c), back(c2), _fixed(CONV_HALO, c), _fixed(8, c)],
        out_specs=[back(c2), _fixed(CONV_HALO, c), _fixed(8, c), _fixed(1, c2)],
        out_shape=[SDS((s, c2), bf16), SDS((CONV_HALO, c), f32), SDS((8, c), f32), SDS((1, c2), f32)],
        scratch_shapes=[pltpu.VMEM((tr + CONV_HALO, c), f32), pltpu.VMEM((tr + CONV_HALO - 8, c), f32)],
        compiler_params=_params("arbitrary"), name=name)(ds, cpre, z, wdw, vecs)


def _tile2d(r, n):
    tn = n if n <= 2048 else 1024
    tr = r
    while tr * tn * 4 > (1 << 21) and tr % 16 == 0:
        tr //= 2
    assert r % tr == 0 and n % tn == 0
    return tr, tn


def adamw(w, g, m, v, name):
    r, n = w.shape
    tr, tn = _tile2d(r, n)

    def body(w_ref, g_ref, m_ref, v_ref, d_ref, nm_ref, nv_ref):
        gv = g_ref[...]
        nm = ADAM_B1 * m_ref[...] + (1.0 - ADAM_B1) * gv
        nv = ADAM_B2 * v_ref[...] + (1.0 - ADAM_B2) * (gv * gv)
        m_hat = nm / (1.0 - ADAM_B1 ** ADAM_STEP)
        v_hat = nv / (1.0 - ADAM_B2 ** ADAM_STEP)
        d_ref[...] = -ADAM_LR * (m_hat / (jnp.sqrt(v_hat) + ADAM_EPS) + ADAM_WD * w_ref[...])
        nm_ref[...] = nm
        nv_ref[...] = nv

    blk = pl.BlockSpec((tr, tn), lambda i, j: (i, j))
    return pl.pallas_call(body, grid=(r // tr, n // tn), in_specs=[blk] * 4, out_specs=[blk] * 3,
                          out_shape=[SDS((r, n), f32)] * 3, compiler_params=_params("parallel", "parallel"),
                          name=name)(w, g, m, v)


def add_core_halves(grad, got, where, name):
    _, _, rh, n = grad.shape
    tr, tn = _tile2d(rh, n)

    def body(w_ref, a_ref, b_ref, o_ref):
        o_ref[...] = (a_ref[...] + b_ref[...]).astype(o_ref.dtype)

    return pl.pallas_call(
        body,
        grid_spec=pltpu.PrefetchScalarGridSpec(
            num_scalar_prefetch=1, grid=(N_CHIPS, rh // tr, n // tn),
            in_specs=[pl.BlockSpec((None, None, tr, tn), lambda p, i, j, w_ref: (p, w_ref[0], i, j)),
                      pl.BlockSpec((None, tr, tn), lambda p, i, j, w_ref: (p, i, j))],
            out_specs=pl.BlockSpec((None, tr, tn), lambda p, i, j, w_ref: (p, i, j))),
        out_shape=SDS((N_CHIPS, rh, n), bf16), compiler_params=_params("parallel", "parallel", "parallel"),
        name=name)(where, grad, got)


def add_chip_parts(grad, got_core, got_chips, where, name):
    _, _, rh, n = grad.shape
    tr, tn = _tile2d(rh, n)

    def body(w_ref, a_ref, b_ref, g_ref, o_ref):
        acc = a_ref[...] + b_ref[...]
        for k in range(N_CHIPS - 1):
            acc = acc + g_ref[k].astype(f32)
        o_ref[...] = acc

    return pl.pallas_call(
        body,
        grid_spec=pltpu.PrefetchScalarGridSpec(
            num_scalar_prefetch=1, grid=(rh // tr, n // tn),
            in_specs=[pl.BlockSpec((None, None, tr, tn), lambda i, j, w_ref: (w_ref[1], w_ref[0], i, j)),
                      pl.BlockSpec((None, tr, tn), lambda i, j, w_ref: (w_ref[1], i, j)),
                      pl.BlockSpec((N_CHIPS - 1, tr, tn), lambda i, j, w_ref: (0, i, j))],
            out_specs=pl.BlockSpec((None, tr, tn), lambda i, j, w_ref: (w_ref[0], i, j))),
        out_shape=SDS((2, rh, n), f32), compiler_params=_params("parallel", "parallel"),
        name=name)(where, grad, got_core, got_chips)


def sum_devices(parts, name):
    nd, r, n = parts.shape

    def body(p_ref, o_ref):
        acc = p_ref[0]
        for k in range(1, nd):
            acc = acc + p_ref[k]
        o_ref[...] = acc

    return pl.pallas_call(body, out_shape=SDS((r, n), f32), name=name)(parts)


def _position():
    return lax.axis_index("x"), lax.axis_index("y"), lax.axis_index("c")


def _other_chips(x, y):
    return [(1 - x, y), (x, 1 - y), (1 - x, 1 - y)]


def _remote(src, dst, send, recv, to):
    return pltpu.make_async_remote_copy(src_ref=src, dst_ref=dst, send_sem=send, recv_sem=recv, device_id=to,
                                        device_id_type=MESH)


def gather_chips(bufs, name):
    n = len(bufs)
    nk = N_CHIPS - 1

    def body(*refs):
        bufs_ = refs[n:2 * n]
        send, recv = refs[2 * n:]
        x, y, c = _position()
        me = 2 * x + y
        chips = _other_chips(x, y)
        sends = []
        for t in range(n):
            for k, (px, py) in enumerate(chips):
                out = _remote(bufs_[t].at[me, c], bufs_[t].at[me, c], send.at[t, k], recv.at[t, k], (px, py, c))
                out.start()
                sends.append(out)
        for t in range(n):
            for k, (px, py) in enumerate(chips):
                piece = bufs_[t].at[2 * px + py, c]
                _remote(piece, piece, send.at[t, k], recv.at[t, k], (px, py, c)).wait_recv()
                on = _remote(piece, piece, send.at[t, nk + k], recv.at[t, nk + k], (x, y, 1 - c))
                on.start()
                sends.append(on)
        for t in range(n):
            for k, (px, py) in enumerate(chips):
                piece = bufs_[t].at[2 * px + py, 1 - c]
                _remote(piece, piece, send.at[t, nk + k], recv.at[t, nk + k], (x, y, 1 - c)).wait_recv()
        for cp in sends:
            cp.wait_send()

    return pl.pallas_call(
        body, in_specs=[ANY] * n, out_specs=[ANY] * n,
        out_shape=[SDS(a.shape, a.dtype) for a in bufs],
        input_output_aliases={t: t for t in range(n)},
        scratch_shapes=[pltpu.SemaphoreType.DMA((n, 2 * nk)), pltpu.SemaphoreType.DMA((n, 2 * nk))],
        name=name)(*bufs)


def swap_core_halves(grads, name):
    n = len(grads)

    def body(*refs):
        ins, outs = refs[:n], refs[n:2 * n]
        send, recv = refs[2 * n:]
        x, y, c = _position()
        pending = []
        for t in range(n):
            out = _remote(ins[t].at[:, 1 - c], outs[t], send.at[t], recv.at[t], (x, y, 1 - c))
            out.start()
            pending.append(out.wait)
        for wait in pending:
            wait()

    return pl.pallas_call(
        body, in_specs=[ANY] * n, out_specs=[ANY] * n,
        out_shape=[SDS((a.shape[0],) + a.shape[2:], a.dtype) for a in grads],
        scratch_shapes=[pltpu.SemaphoreType.DMA((n,)), pltpu.SemaphoreType.DMA((n,))],
        name=name)(*grads)


def scatter_chips(parts, name):
    n = len(parts)

    def body(*refs):
        ins, outs = refs[:n], refs[n:2 * n]
        send, recv = refs[2 * n:]
        x, y, c = _position()
        pending = []
        for t in range(n):
            for k, (px, py) in enumerate(_other_chips(x, y)):
                out = _remote(ins[t].at[2 * px + py], outs[t].at[k], send.at[t, k], recv.at[t, k], (px, py, c))
                out.start()
                pending.append(out.wait)
        for wait in pending:
            wait()

    return pl.pallas_call(
        body, in_specs=[ANY] * n, out_specs=[ANY] * n,
        out_shape=[SDS((N_CHIPS - 1,) + a.shape[1:], a.dtype) for a in parts],
        scratch_shapes=[pltpu.SemaphoreType.DMA((n, N_CHIPS - 1)), pltpu.SemaphoreType.DMA((n, N_CHIPS - 1))],
        name=name)(*parts)


def share_core_halves(bufs, name):
    n = len(bufs)

    def body(*refs):
        bufs_ = refs[n:2 * n]
        send, recv = refs[2 * n:]
        x, y, c = _position()
        pending = []
        for t in range(n):
            out = _remote(bufs_[t].at[c], bufs_[t].at[c], send.at[t], recv.at[t], (x, y, 1 - c))
            out.start()
            pending.append(out.wait_send)
            other = bufs_[t].at[1 - c]
            pending.append(_remote(other, other, send.at[t], recv.at[t], (x, y, 1 - c)).wait_recv)
        for wait in pending:
            wait()

    return pl.pallas_call(
        body, in_specs=[ANY] * n, out_specs=[ANY] * n, out_shape=[SDS(a.shape, a.dtype) for a in bufs],
        input_output_aliases={t: t for t in range(n)},
        scratch_shapes=[pltpu.SemaphoreType.DMA((n,)), pltpu.SemaphoreType.DMA((n,))],
        name=name)(*bufs)


def gather_devices(v, name):
    flips = [(dx, dy, dc) for dx in (0, 1) for dy in (0, 1) for dc in (0, 1)][1:]

    def body(v_ref, o_ref, send, recv, local):
        x, y, c = _position()
        me = 4 * x + 2 * y + c
        own = pltpu.make_async_copy(v_ref, o_ref.at[me], local)
        own.start()
        pending = [own.wait]
        for k, (dx, dy, dc) in enumerate(flips):
            px, py, pc = x ^ dx, y ^ dy, c ^ dc
            out = _remote(v_ref, o_ref.at[me], send.at[k], recv.at[k], (px, py, pc))
            out.start()
            pending.append(out.wait_send)
            pending.append(_remote(v_ref, o_ref.at[4 * px + 2 * py + pc], send.at[k], recv.at[k], (px, py, pc)).wait_recv)
        for wait in pending:
            wait()

    return pl.pallas_call(
        body, in_specs=[ANY], out_specs=ANY, out_shape=SDS((8,) + v.shape, v.dtype),
        scratch_shapes=[pltpu.SemaphoreType.DMA((7,)), pltpu.SemaphoreType.DMA((7,)), pltpu.SemaphoreType.DMA],
        name=name)(v)


SMALL = ("norm_g", "conv_b_pw1", "conv_w_dw", "conv_b_dw", "conv_ln_g", "conv_ln_b", "conv_b_pw2", "ffn_w_dw")


def _pack_rows(arrs, rows):
    flat = jnp.concatenate([a.reshape(-1, LANES) for a in arrs], axis=0)
    return jnp.pad(flat, ((0, rows - flat.shape[0]), (0, 0)))


def _unpack_rows(packed, shapes):
    out, at = [], 0
    for shp in shapes:
        size = 1
        for dim in shp:
            size *= dim
        rows = size // LANES
        out.append(packed[..., at:at + rows, :].reshape(packed.shape[:-2] + tuple(shp)))
        at += rows
    return out


def _join_last(t):
    t = jnp.moveaxis(t, 0, -2)
    return t.reshape(t.shape[:-2] + (t.shape[-2] * t.shape[-1],))


def _split_last(t):
    t = t.reshape(t.shape[:-1] + (N_CHIPS, t.shape[-1] // N_CHIPS))
    return jnp.moveaxis(t, -2, 0)


def _rope_tables(positions):
    half = ROT_DIM // 2
    inv_freq = ROPE_THETA ** (-jnp.arange(half, dtype=f32) / half)
    ang = positions.astype(f32).reshape(-1, 1) * inv_freq
    cos, sin = jnp.cos(ang), jnp.sin(ang)
    s = ang.shape[0]
    rest = HEAD_DIM - ROT_DIM
    head = lambda lo, hi, fill: jnp.concatenate([lo, hi, jnp.full((s, rest), fill, f32)], axis=1)
    zero = jnp.zeros((s, half), f32)
    twice = lambda t: jnp.concatenate([t] * (LANES // HEAD_DIM), axis=1)
    return twice(head(cos, cos, 1.0)), twice(head(-sin, zero, 0.0)), twice(head(zero, sin, 0.0))


def _pad_rows(t, rows):
    return jnp.pad(t, ((0, rows - t.shape[0]), (0, 0)))


def _ffn_pack(w_dw, b_dw):
    t = jnp.concatenate([w_dw, b_dw[None]], axis=0)
    t = t.reshape(FFN_CONV + 1, 2, 2, -1)
    t = jnp.transpose(t, (2, 1, 0, 3))
    return jnp.pad(t, ((0, 0), (0, 0), (0, 8 - (FFN_CONV + 1)), (0, 0)))


def _ffn_unpack(d):
    t = jnp.transpose(d[:, :, :FFN_CONV + 1], (2, 1, 0, 3)).reshape(FFN_CONV + 1, -1)
    return t[:FFN_CONV], t[FFN_CONV]


def _ffn_block(x, g_pre, g_post, w_up, li, w_down, wb, tag):
    s = x.shape[0]
    h = prenorm(x, g_pre, f"{tag}_prenorm")
    u = mm_nn(h[None], w_up, li, None, N_CHIPS, bf16, 512, w_up.shape[-1], f"{tag}_up")
    u4 = u.reshape(2, 2, s, u.shape[-1])
    a, ug = ffn_act_fwd(u4, wb, f"{tag}_act")
    y = mm_nn(a, w_down, 0, None, 1, f32, 512, 512, f"{tag}_down")[0]
    return y, (h, u4, ug, a)


def _ffn_block_bwd(dy, saved, w_up, li, w_down, wb, tag):
    h, u4, ug, a = saved
    d_down = mm_tn(a, dy[None], 1, 1024, a.shape[-1], 512, f"{tag}_dwdown")
    da = mm_nt(dy[None], w_down, 0, 2, bf16, 1024, a.shape[-1], w_down.shape[-1], f"{tag}_da")
    du4, dwb = ffn_act_bwd(da, ug, u4, wb, f"{tag}_actbwd")
    du = du4.reshape((N_CHIPS,) + du4.shape[2:])
    d_up = mm_tn(h[None], du, N_CHIPS, 1024, h.shape[-1], du.shape[-1], f"{tag}_dwup")
    dh = mm_nt(du, w_up, li, 1, f32, 1024, h.shape[-1], du.shape[-1], f"{tag}_dh")[0]
    return dh, d_up, d_down, dwb


def kernel(x, positions, norm_g, attn_w_qkv, attn_w_o, conv_w_pw1, conv_b_pw1, conv_w_dw, conv_b_dw, conv_ln_g, conv_ln_b, conv_w_pw2, conv_b_pw2, ffn_w_up, ffn_w_dw, ffn_b_dw, ffn_w_down, loss_target, m_norm_g, m_attn_w_qkv, m_attn_w_o, m_conv_w_pw1, m_conv_b_pw1, m_conv_w_dw, m_conv_b_dw, m_conv_ln_g, m_conv_ln_b, m_conv_w_pw2, m_conv_b_pw2, m_ffn_w_up, m_ffn_w_dw, m_ffn_b_dw, m_ffn_w_down, v_norm_g, v_attn_w_qkv, v_attn_w_o, v_conv_w_pw1, v_conv_b_pw1, v_conv_w_dw, v_conv_b_dw, v_conv_ln_g, v_conv_ln_b, v_conv_w_pw2, v_conv_b_pw2, v_ffn_w_up, v_ffn_w_dw, v_ffn_b_dw, v_ffn_w_down):
    weights = dict(norm_g=norm_g, attn_w_qkv=attn_w_qkv, attn_w_o=attn_w_o, conv_w_pw1=conv_w_pw1, conv_b_pw1=conv_b_pw1,
                   conv_w_dw=conv_w_dw, conv_b_dw=conv_b_dw, conv_ln_g=conv_ln_g, conv_ln_b=conv_ln_b, conv_w_pw2=conv_w_pw2,
                   conv_b_pw2=conv_b_pw2, ffn_w_up=ffn_w_up, ffn_w_dw=ffn_w_dw, ffn_b_dw=ffn_b_dw, ffn_w_down=ffn_w_down)
    mom1 = dict(norm_g=m_norm_g, attn_w_qkv=m_attn_w_qkv, attn_w_o=m_attn_w_o, conv_w_pw1=m_conv_w_pw1, conv_b_pw1=m_conv_b_pw1,
                conv_w_dw=m_conv_w_dw, conv_b_dw=m_conv_b_dw, conv_ln_g=m_conv_ln_g, conv_ln_b=m_conv_ln_b, conv_w_pw2=m_conv_w_pw2,
                conv_b_pw2=m_conv_b_pw2, ffn_w_up=m_ffn_w_up, ffn_w_dw=m_ffn_w_dw, ffn_b_dw=m_ffn_b_dw, ffn_w_down=m_ffn_w_down)
    mom2 = dict(norm_g=v_norm_g, attn_w_qkv=v_attn_w_qkv, attn_w_o=v_attn_w_o, conv_w_pw1=v_conv_w_pw1, conv_b_pw1=v_conv_b_pw1,
                conv_w_dw=v_conv_w_dw, conv_b_dw=v_conv_b_dw, conv_ln_g=v_conv_ln_g, conv_ln_b=v_conv_ln_b, conv_w_pw2=v_conv_w_pw2,
                conv_b_pw2=v_conv_b_pw2, ffn_w_up=v_ffn_w_up, ffn_w_dw=v_ffn_w_dw, ffn_b_dw=v_ffn_b_dw, ffn_w_down=v_ffn_w_down)
    big = ("attn_w_qkv", "attn_w_o", "conv_w_pw1", "conv_w_pw2", "ffn_w_up", "ffn_w_down")
    xi, yi, ci = _position()
    chip = (2 * xi + yi).astype(jnp.int32).reshape(1)
    where = jnp.stack([ci, 2 * xi + yi]).astype(jnp.int32)

    x = x[0]
    target = loss_target[0]
    s, d = x.shape

    small_shapes = [weights[k].shape for k in SMALL]
    small_rows = -(-sum(weights[k].size for k in SMALL) // LANES // 8) * 8
    small_w = _pack_rows([weights[k] for k in SMALL], small_rows)
    def own_slot(shard):
        halves = shard.reshape(1, 2, -1, shard.shape[-1])
        return lax.dynamic_update_slice(lax.empty((N_CHIPS,) + halves.shape[1:], shard.dtype), halves, (chip[0], 0, 0, 0))

    gathered = gather_chips([own_slot(weights[k].astype(bf16)) for k in big] + [own_slot(small_w)], "gather_weights")
    gw = {k: t.reshape((N_CHIPS,) + weights[k].shape) for k, t in zip(big, gathered[:-1])}
    full_small = dict(zip(SMALL, [_join_last(t) for t in _unpack_rows(gathered[-1].reshape(N_CHIPS, small_rows, LANES), small_shapes)]))
    w_qkv, w_o, w_pw1, w_up = gw["attn_w_qkv"], gw["attn_w_o"], gw["conv_w_pw1"], gw["ffn_w_up"]
    w_pw2 = gw["conv_w_pw2"].reshape(1, 1, -1, d)
    w_down = [gw["ffn_w_down"][:, i].reshape(1, 1, -1, d) for i in range(2)]
    gains = full_small["norm_g"]
    gain = lambda i, k: gains[i, k][None]
    b_pw1 = full_small["conv_b_pw1"]
    conv_wdw = _pad_rows(full_small["conv_w_dw"][0], CONV_HALO)
    conv_vecs = _pad_rows(jnp.concatenate([full_small["conv_b_dw"], full_small["conv_ln_g"], full_small["conv_ln_b"]], axis=0), 8)
    b_pw2 = full_small["conv_b_pw2"]
    wbs = [_ffn_pack(full_small["ffn_w_dw"][i], ffn_b_dw[i]) for i in range(2)]
    tabs = _rope_tables(positions[0])
    tabs_t = (tabs[0], -tabs[1], -tabs[2])
    head_ones = (jnp.arange(GROUP_WIDTH)[:, None] // HEAD_DIM == jnp.arange(GROUP_WIDTH)[None, :] // HEAD_DIM).astype(bf16)

    h0 = prenorm(x, gain(0, 0), "l0_prenorm")
    qkv = mm_nn(h0[None], w_qkv, 0, None, 1, f32, 512, w_qkv.shape[-1], "qkv")[0]
    qkvr = rope_qkv(qkv, tabs, "rope")
    att = [attn_fwd(qkvr, g, dil, f"attn_fwd{g}") for g, (_, dil) in enumerate(DILATED_GROUPS)]
    os_, lses = [a[0] for a in att], [a[1] for a in att]
    mixed = mix_fwd(os_, lses, "mix")
    y0 = mm_nn(mixed[None], w_o, 0, None, 1, f32, 1024, w_o.shape[-1], "attn_out")[0]
    x1 = postnorm_residual(x, y0, gain(0, 1), "l0_postnorm")
    y1, ffn0 = _ffn_block(x1, gain(0, 2), gain(0, 3), w_up, 0, w_down[0], wbs[0], "ffn0")
    x2 = postnorm_residual(x1, y1, gain(0, 3), "l0_ffn_postnorm")

    h2 = prenorm(x2, gain(1, 0), "l1_prenorm")
    z = mm_nn(h2[None], w_pw1, 0, b_pw1, 1, f32, 1024, w_pw1.shape[-1], "pw1")[0]
    cpre, sw = conv_module_fwd(z, conv_wdw, conv_vecs, "conv_fwd")
    y2 = mm_nn(sw[None], w_pw2, 0, b_pw2, 1, f32, 512, 512, "pw2")[0]
    x3 = postnorm_residual(x2, y2, gain(1, 1), "l1_postnorm")
    y3, ffn1 = _ffn_block(x3, gain(1, 2), gain(1, 3), w_up, 1, w_down[1], wbs[1], "ffn1")
    dx4, loss_row = final_loss(x3, y3, gain(1, 3), target, "loss")
    loss = lax.psum(loss_row[0, 0], ("x", "y", "c"))

    dgain = [[None] * 4 for _ in range(2)]
    dy3, dgain[1][3], _ = norm_bwd(y3, gain(1, 3), dx4, None, bf16, "l1_ffn_postnorm_bwd")
    dh, d_up1, d_down1, dwb1 = _ffn_block_bwd(dy3, ffn1, w_up, 1, w_down[1], wbs[1], "ffn1")
    dx3, dgain[1][2], _ = norm_bwd(x3, gain(1, 2), dh, dx4, f32, "l1_ffn_prenorm_bwd")

    dy2, dgain[1][1], d_b_pw2 = norm_bwd(y2, gain(1, 1), dx3, None, bf16, "l1_postnorm_bwd")
    d_pw2 = mm_tn(sw[None], dy2[None], 1, 1024, d, 512, "dw_pw2")
    dsw = mm_nt(dy2[None], w_pw2, 0, 1, f32, 1024, d, d, "d_swish")[0]
    dz, d_conv_wdw, d_conv_vecs, d_b_pw1 = conv_module_bwd(dsw, cpre, z, conv_wdw, conv_vecs, "conv_bwd")
    d_pw1 = mm_tn(h2[None], dz[None], N_CHIPS, 1024, d, w_pw1.shape[-1], "dw_pw1")
    dh = mm_nt(dz[None], w_pw1, 0, 1, f32, 1024, d, w_pw1.shape[-1], "d_h2")[0]
    dx2, dgain[1][0], _ = norm_bwd(x2, gain(1, 0), dh, dx3, f32, "l1_prenorm_bwd")

    dy1, dgain[0][3], _ = norm_bwd(y1, gain(0, 3), dx2, None, bf16, "l0_ffn_postnorm_bwd")
    dh, d_up0, d_down0, dwb0 = _ffn_block_bwd(dy1, ffn0, w_up, 0, w_down[0], wbs[0], "ffn0")
    dx1, dgain[0][2], _ = norm_bwd(x1, gain(0, 2), dh, dx2, f32, "l0_ffn_prenorm_bwd")

    dy0, dgain[0][1], _ = norm_bwd(y0, gain(0, 1), dx1, None, bf16, "l0_postnorm_bwd")
    d_wo = mm_tn(mixed[None], dy0[None], N_CHIPS, 1024, GROUP_WIDTH, w_o.shape[-1], "dw_o")
    dmixed = mm_nt(dy0[None], w_o, 0, 1, f32, 1024, GROUP_WIDTH, w_o.shape[-1], "d_mixed")[0]
    mb = mix_bwd(dmixed, os_, lses, head_ones, "mix_bwd")
    parts = [attn_bwd(qkvr, mb[g], lses[g], mb[3 + g], g, dil, f"attn_bwd{g}") for g, (_, dil) in enumerate(DILATED_GROUPS)]
    dqkv = dqkv_assemble(parts, tabs_t, "dqkv")
    d_qkv = mm_tn(h0[None], dqkv[None], N_CHIPS, 1024, d, w_qkv.shape[-1], "dw_qkv")
    dh = mm_nt(dqkv[None], w_qkv, 0, 1, f32, 1024, d, w_qkv.shape[-1], "d_h0")[0]
    grad_x, dgain[0][0], _ = norm_bwd(x, gain(0, 0), dh, dx1, f32, "l0_prenorm_bwd")

    stack_layers = lambda a, b: jnp.stack([a, b], axis=1)
    gbig = dict(
        attn_w_qkv=d_qkv, attn_w_o=d_wo, conv_w_pw1=d_pw1, conv_w_pw2=d_pw2[0].reshape(N_CHIPS, -1, d),
        ffn_w_up=stack_layers(d_up0, d_up1),
        ffn_w_down=stack_layers(d_down0[0].reshape(N_CHIPS, -1, d), d_down1[0].reshape(N_CHIPS, -1, d)))
    d_ffn_dw, d_ffn_b = zip(*[_ffn_unpack(t) for t in (dwb0, dwb1)])
    gsmall = dict(
        norm_g=jnp.stack([jnp.concatenate(row, axis=0) for row in dgain], axis=0),
        conv_b_pw1=d_b_pw1, conv_w_dw=d_conv_wdw[None, :CONV_KERNEL], conv_b_dw=d_conv_vecs[0:1], conv_ln_g=d_conv_vecs[1:2],
        conv_ln_b=d_conv_vecs[2:3], conv_b_pw2=d_b_pw2, ffn_w_dw=jnp.stack(d_ffn_dw, axis=0))
    bias_rows = ffn_b_dw.size // LANES
    small_g = jnp.concatenate([jnp.concatenate([_pack_rows([_split_last(gsmall[k])[p] for k in SMALL], small_rows)
                                                for p in range(N_CHIPS)], axis=0),
                               jnp.stack(d_ffn_b, axis=0).reshape(bias_rows, LANES)], axis=0)

    small_sum = sum_devices(gather_devices(small_g, "gather_small_grads"), "sum_small_grads")
    my_small = lax.dynamic_slice_in_dim(small_sum, chip[0] * small_rows, small_rows, axis=0)
    g_small = jnp.concatenate([my_small, small_sum[N_CHIPS * small_rows:]], axis=0)

    halves = [gbig[k].reshape(N_CHIPS, 2, -1, gbig[k].shape[-1]) for k in big]
    from_core = swap_core_halves(halves, "swap_core_halves")
    chip_parts = [add_core_halves(a, b, where, f"add_core_{k}") for k, a, b in zip(big, halves, from_core)]
    from_chips = scatter_chips(chip_parts, "scatter_chips")
    mine = [add_chip_parts(a, b, r, where, f"add_chips_{k}") for k, a, b, r in zip(big, halves, from_core, from_chips)]
    shard_grads = share_core_halves(mine, "share_core_halves")

    grads, deltas, new_m, new_v = {}, {}, {}, {}
    for k, g2 in zip(big, shard_grads):
        shp = weights[k].shape
        g2 = g2.reshape(-1, shp[-1])
        dl, nm, nv = adamw(weights[k].reshape(g2.shape), g2, mom1[k].reshape(g2.shape), mom2[k].reshape(g2.shape), f"adamw_{k}")
        grads[k], deltas[k], new_m[k], new_v[k] = (t.reshape(shp) for t in (g2, dl, nm, nv))
    pack_state = lambda src: jnp.concatenate([_pack_rows([src[k] for k in SMALL], small_rows), src["ffn_b_dw"].reshape(bias_rows, LANES)], axis=0)
    small_out = (g_small,) + tuple(adamw(pack_state(weights), g_small, pack_state(mom1), pack_state(mom2), "adamw_small"))
    for dst, packed in zip((grads, deltas, new_m, new_v), small_out):
        for k, t in zip(SMALL, _unpack_rows(packed[:small_rows], small_shapes)):
            dst[k] = t
        dst["ffn_b_dw"] = packed[small_rows:].reshape(ffn_b_dw.shape)

    order = ("norm_g", "attn_w_qkv", "attn_w_o", "conv_w_pw1", "conv_b_pw1", "conv_w_dw", "conv_b_dw", "conv_ln_g", "conv_ln_b",
             "conv_w_pw2", "conv_b_pw2", "ffn_w_up", "ffn_w_dw", "ffn_b_dw", "ffn_w_down")
    return (loss, grad_x[None], *[grads[k] for k in order], *[deltas[k] for k in order], *[new_m[k] for k in order],
            *[new_v[k] for k in order])
```

```python
import functools

import jax
import jax.numpy as jnp
from jax import lax
from jax.experimental import pallas as pl
from jax.experimental.pallas import tpu as pltpu

f32 = jnp.float32
bf16 = jnp.bfloat16
SDS = jax.ShapeDtypeStruct

EPS = 1e-6
HEAD_DIM = 64
N_SLOTS = 8
GROUP_WIDTH = N_SLOTS * HEAD_DIM
DILATED_GROUPS = ((128, 1), (512, 4), (2048, 16))
N_GROUPS = 3
SPAN = 128
ROT_DIM = HEAD_DIM // 4
ROPE_THETA = 500000.0
CONV_KERNEL = 31
CONV_HALO = 32
FFN_CONV = 3
FFN_HALO = 16
FFN_TAIL = 8
STRIP_ROWS = 64
ADAM_LR, ADAM_B1, ADAM_B2, ADAM_EPS, ADAM_WD, ADAM_STEP = 0.001, 0.9, 0.999, 1e-08, 0.01, 10
LANES = 128
N_CHIPS = 4
VMEM_LIMIT_BYTES = 56 * 1024 * 1024
NEG_BIG = -1e30
MESH = pl.DeviceIdType.MESH
ANY = pl.BlockSpec(memory_space=pl.ANY)


def _params(*sem):
    return pltpu.CompilerParams(dimension_semantics=sem, vmem_limit_bytes=VMEM_LIMIT_BYTES)


def _sigmoid(v):
    return 1.0 / (1.0 + jnp.exp(-v))


def _dot_nt(a, b):
    return lax.dot_general(a, b, (((1,), (1,)), ((), ())), preferred_element_type=f32)


def _dot_tn(a, b):
    return lax.dot_general(a, b, (((0,), (0,)), ((), ())), preferred_element_type=f32)


def mm_nn(x, w, li, bias, out_blocks, out_dtype, tm, tn, name):
    nq, m, kq = x.shape
    p, _, k, n = w.shape
    assert k == nq * kq and n % tn == 0 and m % tm == 0
    on = p * n // out_blocks
    assert on % tn == 0
    nj, onj = n // tn, on // tn

    def body(*refs):
        if bias is None:
            x_ref, w_ref, o_ref = refs
            b_ref = None
        else:
            x_ref, w_ref, b_ref, o_ref = refs
        acc = jnp.dot(x_ref[0], w_ref[0:kq, :], preferred_element_type=f32)
        for q in range(1, nq):
            acc = acc + jnp.dot(x_ref[q], w_ref[q * kq:(q + 1) * kq, :], preferred_element_type=f32)
        if b_ref is not None:
            acc = acc + b_ref[...]
        o_ref[...] = acc.astype(o_ref.dtype)

    in_specs = [pl.BlockSpec((nq, tm, kq), lambda j, i: (0, i, 0)),
                pl.BlockSpec((None, None, k, tn), lambda j, i: (j // nj, li, 0, j % nj))]
    args = [x, w]
    if bias is not None:
        in_specs.append(pl.BlockSpec((1, tn), lambda j, i: (0, j)))
        args.append(bias)
    return pl.pallas_call(
        body, grid=(p * nj, m // tm), in_specs=in_specs,
        out_specs=pl.BlockSpec((None, tm, tn), lambda j, i: (j // onj, i, j % onj)),
        out_shape=SDS((out_blocks, m, on), out_dtype),
        compiler_params=_params("parallel", "parallel"), name=name)(*args)


def mm_nt(dy, w, li, out_blocks, out_dtype, tm, tk, tn, name):
    ob, m, on = dy.shape
    p, _, k, n = w.shape
    assert ob * on == p * n and n % tn == 0 and on % tn == 0 and k % tk == 0 and m % tm == 0
    kq = k // out_blocks
    assert kq % tk == 0
    nj, onj, kqj = n // tn, on // tn, kq // tk
    nr = p * nj

    def body(dy_ref, w_ref, o_ref, *scr):
        part = _dot_nt(dy_ref[...], w_ref[...])
        if nr == 1:
            o_ref[...] = part.astype(o_ref.dtype)
        else:
            acc_ref, = scr
            r = pl.program_id(2)

            @pl.when(r == 0)
            def _():
                acc_ref[...] = part

            @pl.when(r > 0)
            def _():
                acc_ref[...] += part

            @pl.when(r == nr - 1)
            def _():
                o_ref[...] = acc_ref[...].astype(o_ref.dtype)

    return pl.pallas_call(
        body, grid=(k // tk, m // tm, nr),
        in_specs=[pl.BlockSpec((None, tm, tn), lambda kt, i, r: (r // onj, i, r % onj)),
                  pl.BlockSpec((None, None, tk, tn), lambda kt, i, r: (r // nj, li, kt, r % nj))],
        out_specs=pl.BlockSpec((None, tm, tk), lambda kt, i, r: (kt // kqj, i, kt % kqj)),
        out_shape=SDS((out_blocks, m, kq), out_dtype),
        scratch_shapes=[] if nr == 1 else [pltpu.VMEM((tm, tk), f32)],
        compiler_params=_params("parallel", "parallel", "arbitrary"), name=name)(dy, w)


def mm_tn(x, dy, p, tm, tk, tn, name):
    nq, m, kq = x.shape
    ob, _, on = dy.shape
    k = nq * kq
    n = ob * on // p
    assert n % tn == 0 and on % tn == 0 and kq % tk == 0 and m % tm == 0
    nj, onj, kqj = n // tn, on // tn, kq // tk

    def body(x_ref, dy_ref, o_ref):
        part = _dot_tn(x_ref[...], dy_ref[...])
        i = pl.program_id(2)

        @pl.when(i == 0)
        def _():
            o_ref[...] = part

        @pl.when(i > 0)
        def _():
            o_ref[...] += part

    return pl.pallas_call(
        body, grid=(k // tk, p * nj, m // tm),
        in_specs=[pl.BlockSpec((None, tm, tk), lambda kt, j, i: (kt // kqj, i, kt % kqj)),
                  pl.BlockSpec((None, tm, tn), lambda kt, j, i: (j // onj, i, j % onj))],
        out_specs=pl.BlockSpec((None, tk, tn), lambda kt, j, i: (j // nj, kt, j % nj)),
        out_shape=SDS((p, k, n), f32),
        compiler_params=_params("parallel", "parallel", "arbitrary"), name=name)(x, dy)


def _row_tile(s):
    return min(s, 512)


def _rows(tr, d):
    return pl.BlockSpec((tr, d), lambda i: (i, 0))


def _fixed(r, d):
    return pl.BlockSpec((r, d), lambda i: (0, 0))


def _rms(xv):
    return lax.rsqrt(jnp.mean(xv * xv, axis=-1, keepdims=True) + EPS)


def prenorm(x, g, name):
    s, d = x.shape
    tr = _row_tile(s)

    def body(x_ref, g_ref, o_ref):
        xv = x_ref[...]
        o_ref[...] = (xv * _rms(xv) * g_ref[...]).astype(o_ref.dtype)

    return pl.pallas_call(body, grid=(s // tr,), in_specs=[_rows(tr, d), _fixed(1, d)], out_specs=_rows(tr, d),
                          out_shape=SDS((s, d), bf16), compiler_params=_params("parallel"), name=name)(x, g)


def postnorm_residual(x, y, g, name):
    s, d = x.shape
    tr = _row_tile(s)

    def body(x_ref, y_ref, g_ref, o_ref):
        yv = y_ref[...]
        o_ref[...] = x_ref[...] + yv * _rms(yv) * g_ref[...]

    return pl.pallas_call(body, grid=(s // tr,), in_specs=[_rows(tr, d), _rows(tr, d), _fixed(1, d)],
                          out_specs=_rows(tr, d), out_shape=SDS((s, d), f32), compiler_params=_params("parallel"),
                          name=name)(x, y, g)


def norm_bwd(xin, g, dout, res, out_dtype, name):
    s, d = xin.shape
    tr = _row_tile(s)

    def body(*refs):
        if res is None:
            x_ref, g_ref, do_ref, dx_ref, dg_ref, cs_ref = refs
            r_ref = None
        else:
            x_ref, g_ref, do_ref, r_ref, dx_ref, dg_ref, cs_ref = refs
        xv = x_ref[...]
        r = _rms(xv)
        xh = xv * r
        dov = do_ref[...].astype(f32)
        gy = dov * g_ref[...]
        dx = r * (gy - xh * jnp.mean(gy * xh, axis=-1, keepdims=True))
        if r_ref is not None:
            dx = dx + r_ref[...]
        dx_ref[...] = dx.astype(dx_ref.dtype)
        dg = jnp.sum(dov * xh, axis=0, keepdims=True)
        cs = jnp.sum(dx, axis=0, keepdims=True)
        i = pl.program_id(0)

        @pl.when(i == 0)
        def _():
            dg_ref[...] = dg
            cs_ref[...] = cs

        @pl.when(i > 0)
        def _():
            dg_ref[...] += dg
            cs_ref[...] += cs

    in_specs = [_rows(tr, d), _fixed(1, d), _rows(tr, d)]
    args = [xin, g, dout]
    if res is not None:
        in_specs.append(_rows(tr, d))
        args.append(res)
    return pl.pallas_call(body, grid=(s // tr,), in_specs=in_specs,
                          out_specs=[_rows(tr, d), _fixed(1, d), _fixed(1, d)],
                          out_shape=[SDS((s, d), out_dtype), SDS((1, d), f32), SDS((1, d), f32)],
                          compiler_params=_params("arbitrary"), name=name)(*args)


def final_loss(x, y, g, target, name):
    s, d = x.shape
    tr = _row_tile(s)
    nt = s // tr

    def body(x_ref, y_ref, g_ref, t_ref, dx_ref, loss_ref, acc_ref):
        yv = y_ref[...]
        diff = x_ref[...] + yv * _rms(yv) * g_ref[...] - t_ref[...]
        dx_ref[...] = diff * (1.0 / d)
        sq = jnp.sum(diff * diff, axis=0, keepdims=True)
        i = pl.program_id(0)

        @pl.when(i == 0)
        def _():
            acc_ref[...] = sq

        @pl.when(i > 0)
        def _():
            acc_ref[...] += sq

        @pl.when(i == nt - 1)
        def _():
            total = jnp.sum(acc_ref[...], axis=1, keepdims=True) * (0.5 / d)
            loss_ref[...] = jnp.broadcast_to(total, (1, LANES))

    return pl.pallas_call(body, grid=(nt,), in_specs=[_rows(tr, d), _rows(tr, d), _fixed(1, d), _rows(tr, d)],
                          out_specs=[_rows(tr, d), _fixed(1, LANES)],
                          out_shape=[SDS((s, d), f32), SDS((1, LANES), f32)],
                          scratch_shapes=[pltpu.VMEM((1, d), f32)],
                          compiler_params=_params("arbitrary"), name=name)(x, y, g, target)


def _rotate_chunk(v, tc, ta, tb):
    return v * tc + pltpu.roll(v, LANES - ROT_DIM // 2, axis=1) * ta + pltpu.roll(v, ROT_DIM // 2, axis=1) * tb


def qkv_rope(h, w, tabs, name):
    s, d = h.shape
    tm = min(s, 512)
    n_kinds = 3

    def body(x_ref, w_ref, tc_ref, ta_ref, tb_ref, o_ref):
        acc = jnp.dot(x_ref[...], w_ref[...], preferred_element_type=f32)
        kind = pl.program_id(0) // N_GROUPS

        @pl.when(kind < 2)
        def _():
            tc, ta, tb = tc_ref[...], ta_ref[...], tb_ref[...]
            for ch in range(GROUP_WIDTH // LANES):
                cols = slice(ch * LANES, (ch + 1) * LANES)
                o_ref[:, cols] = _rotate_chunk(acc[:, cols], tc, ta, tb).astype(o_ref.dtype)

        @pl.when(kind == 2)
        def _():
            o_ref[...] = acc.astype(o_ref.dtype)

    tab = pl.BlockSpec((tm, LANES), lambda j, i: (i, 0))
    return pl.pallas_call(
        body, grid=(n_kinds * N_GROUPS, s // tm),
        in_specs=[pl.BlockSpec((tm, d), lambda j, i: (i, 0)), pl.BlockSpec((d, GROUP_WIDTH), lambda j, i: (0, j)), tab, tab, tab],
        out_specs=pl.BlockSpec((None, tm, GROUP_WIDTH), lambda j, i: (j % N_GROUPS, i, j // N_GROUPS)),
        out_shape=SDS((N_GROUPS, s, n_kinds * GROUP_WIDTH), bf16),
        compiler_params=_params("parallel", "parallel"), name=name)(h, w, *tabs)


def _attn_mask(j):
    row = lax.broadcasted_iota(jnp.int32, (SPAN, 2 * SPAN), 0)
    col = lax.broadcasted_iota(jnp.int32, (SPAN, 2 * SPAN), 1)
    prev = jnp.logical_and(jnp.logical_and(col < SPAN, col >= row), j > 0)
    return jnp.logical_or(prev, jnp.logical_and(col >= SPAN, col - SPAN <= row))


def _attn_in_specs(gi):
    def at(kind, prev):
        def index(r, j):
            return (gi, jnp.maximum(j - 1, 0) if prev else j, r * 3 + kind)
        return pl.BlockSpec((None, SPAN, GROUP_WIDTH), index)
    return [at(0, False), at(1, False), at(1, True), at(2, False), at(2, True)]


def _phases(qkvg, g, dil):
    if dil == 1:
        return qkvg, g
    _, s, w = qkvg.shape
    return qkvg[g].reshape(1, s // dil, dil * w), 0


def attn_fwd(qkvg, g, dil, name):
    a, gi = _phases(qkvg, g, dil)
    s = qkvg.shape[1]
    l = s // dil
    nb = l // SPAN

    def body(q_ref, ko_ref, kp_ref, vo_ref, vp_ref, o_ref, lse_ref, k_scr, v_scr):
        mask = _attn_mask(pl.program_id(1))
        k_scr[0:SPAN, :] = kp_ref[...]
        k_scr[SPAN:2 * SPAN, :] = ko_ref[...]
        v_scr[0:SPAN, :] = vp_ref[...]
        v_scr[SPAN:2 * SPAN, :] = vo_ref[...]
        for h in range(N_SLOTS):
            hs = slice(h * HEAD_DIM, (h + 1) * HEAD_DIM)
            sc = jnp.where(mask, _dot_nt(q_ref[:, hs], k_scr[:, hs]) * (HEAD_DIM ** -0.5), NEG_BIG)
            mx = jnp.max(sc, axis=-1, keepdims=True)
            p = jnp.exp(sc - mx)
            den = jnp.sum(p, axis=-1, keepdims=True)
            o_ref[:, hs] = jnp.dot(p.astype(bf16), v_scr[:, hs], preferred_element_type=f32) / den
            lse_ref[:, hs] = jnp.broadcast_to(mx + jnp.log(den), (SPAN, HEAD_DIM))

    out = pl.BlockSpec((SPAN, GROUP_WIDTH), lambda r, j: (j, r))
    o, lse = pl.pallas_call(
        body, grid=(dil, nb), in_specs=_attn_in_specs(gi), out_specs=[out, out],
        out_shape=[SDS((l, dil * GROUP_WIDTH), f32)] * 2,
        scratch_shapes=[pltpu.VMEM((2 * SPAN, GROUP_WIDTH), bf16)] * 2,
        compiler_params=_params("parallel", "parallel"), name=name)(a, a, a, a, a)
    return o.reshape(s, GROUP_WIDTH), lse.reshape(s, GROUP_WIDTH), lse


def _group_weights(lses):
    mx = jnp.maximum(jnp.maximum(lses[0], lses[1]), lses[2])
    es = [jnp.exp(v - mx) for v in lses]
    inv = 1.0 / (es[0] + es[1] + es[2])
    return [e * inv for e in es]


def mix_fwd(os_, lses, name):
    s, w = os_[0].shape
    tr = _row_tile(s)

    def body(o0, o1, o2, l0, l1, l2, out_ref):
        wg = _group_weights([l0[...], l1[...], l2[...]])
        out_ref[...] = (wg[0] * o0[...] + wg[1] * o1[...] + wg[2] * o2[...]).astype(out_ref.dtype)

    return pl.pallas_call(body, grid=(s // tr,), in_specs=[_rows(tr, w)] * 6, out_specs=_rows(tr, w),
                          out_shape=SDS((s, w), bf16), compiler_params=_params("parallel"), name=name)(*os_, *lses)


def mix_bwd(dmixed, os_, lses, head_ones, name):
    s, w = dmixed.shape
    tr = _row_tile(s)

    def head_sum(t, ones):
        hi = t.astype(bf16)
        lo = (t - hi.astype(f32)).astype(bf16)
        return jnp.dot(hi, ones, preferred_element_type=f32) + jnp.dot(lo, ones, preferred_element_type=f32)

    def body(dm_ref, o0, o1, o2, l0, l1, l2, ones_ref, d0, d1, d2, p0, p1, p2):
        dm = dm_ref[...]
        ones = ones_ref[...]
        wg = _group_weights([l0[...], l1[...], l2[...]])
        mean = sum(wg[k] * head_sum(dm * o[...], ones) for k, o in enumerate((o0, o1, o2)))
        for k, (d_ref, p_ref) in enumerate(((d0, p0), (d1, p1), (d2, p2))):
            d_ref[...] = (wg[k] * dm).astype(d_ref.dtype)
            p_ref[...] = wg[k] * mean

    return pl.pallas_call(body, grid=(s // tr,), in_specs=[_rows(tr, w)] * 7 + [_fixed(w, w)],
                          out_specs=[_rows(tr, w)] * 6,
                          out_shape=[SDS((s, w), bf16)] * 3 + [SDS((s, w), f32)] * 3,
                          compiler_params=_params("parallel"), name=name)(dmixed, *os_, *lses, head_ones)


def attn_bwd(qkvg, do, lse_phased, dterm, g, dil, name):
    a, gi = _phases(qkvg, g, dil)
    s = qkvg.shape[1]
    l = s // dil
    nb = l // SPAN
    phased = lambda t: t.reshape(l, dil * GROUP_WIDTH)

    def body(q_ref, ko_ref, kp_ref, vo_ref, vp_ref, do_ref, lse_ref, dt_ref, dq_ref, dko_ref, dkp_ref, dvo_ref, dvp_ref,
             k_scr, v_scr):
        mask = _attn_mask(pl.program_id(1))
        scale = HEAD_DIM ** -0.5
        k_scr[0:SPAN, :] = kp_ref[...]
        k_scr[SPAN:2 * SPAN, :] = ko_ref[...]
        v_scr[0:SPAN, :] = vp_ref[...]
        v_scr[SPAN:2 * SPAN, :] = vo_ref[...]
        for h in range(N_SLOTS):
            hs = slice(h * HEAD_DIM, (h + 1) * HEAD_DIM)
            one = slice(h * HEAD_DIM, h * HEAD_DIM + 1)
            q, kk, dov = q_ref[:, hs], k_scr[:, hs], do_ref[:, hs]
            p = jnp.exp(jnp.where(mask, _dot_nt(q, kk) * scale - lse_ref[:, one], NEG_BIG))
            ds = (p * (_dot_nt(dov, v_scr[:, hs]) - dt_ref[:, one]) * scale).astype(bf16)
            dq_ref[:, hs] = jnp.dot(ds, kk, preferred_element_type=f32).astype(dq_ref.dtype)
            dk = _dot_tn(ds, q).astype(dko_ref.dtype)
            dv = _dot_tn(p.astype(bf16), dov).astype(dvo_ref.dtype)
            dkp_ref[:, hs] = dk[:SPAN]
            dko_ref[:, hs] = dk[SPAN:]
            dvp_ref[:, hs] = dv[:SPAN]
            dvo_ref[:, hs] = dv[SPAN:]

    blk = pl.BlockSpec((SPAN, GROUP_WIDTH), lambda r, j: (j, r))
    outs = pl.pallas_call(
        body, grid=(dil, nb), in_specs=_attn_in_specs(gi) + [blk, blk, blk], out_specs=[blk] * 5,
        out_shape=[SDS((l, dil * GROUP_WIDTH), bf16)] * 5,
        scratch_shapes=[pltpu.VMEM((2 * SPAN, GROUP_WIDTH), bf16)] * 2,
        compiler_params=_params("parallel", "parallel"), name=name)(a, a, a, a, a, phased(do), lse_phased, phased(dterm))
    return [t.reshape(s, GROUP_WIDTH) for t in outs]


def dqkv_assemble(parts, tabs, name):
    s = parts[0][0].shape[0]
    nblk = s // SPAN
    width = 3 * N_GROUPS * GROUP_WIDTH

    def body(*refs):
        ins, (tc_ref, ta_ref, tb_ref, o_ref) = refs[:5 * N_GROUPS], refs[5 * N_GROUPS:]
        tc, ta, tb = tc_ref[...], ta_ref[...], tb_ref[...]
        i = pl.program_id(0)
        for g, (_, dil) in enumerate(DILATED_GROUPS):
            dq, dko, dkp, dvo, dvp = ins[5 * g:5 * g + 5]
            has_next = i + dil < nblk
            for ch in range(GROUP_WIDTH // LANES):
                cols = slice(ch * LANES, (ch + 1) * LANES)
                base = g * GROUP_WIDTH + ch * LANES
                dk = dko[:, cols].astype(f32) + jnp.where(has_next, dkp[:, cols].astype(f32), 0.0)
                dv = dvo[:, cols].astype(f32) + jnp.where(has_next, dvp[:, cols].astype(f32), 0.0)
                o_ref[:, base:base + LANES] = _rotate_chunk(dq[:, cols].astype(f32), tc, ta, tb).astype(o_ref.dtype)
                kb = N_GROUPS * GROUP_WIDTH + base
                o_ref[:, kb:kb + LANES] = _rotate_chunk(dk, tc, ta, tb).astype(o_ref.dtype)
                vb = 2 * N_GROUPS * GROUP_WIDTH + base
                o_ref[:, vb:vb + LANES] = dv.astype(o_ref.dtype)

    here = _rows(SPAN, GROUP_WIDTH)
    in_specs, args = [], []
    for g, (_, dil) in enumerate(DILATED_GROUPS):
        ahead = pl.BlockSpec((SPAN, GROUP_WIDTH), functools.partial(lambda i, dil: (jnp.minimum(i + dil, nblk - 1), 0), dil=dil))
        in_specs += [here, here, ahead, here, ahead]
        args += list(parts[g])
    tab = _rows(SPAN, LANES)
    return pl.pallas_call(body, grid=(nblk,), in_specs=in_specs + [tab] * 3, out_specs=_rows(SPAN, width),
                          out_shape=SDS((s, width), bf16), compiler_params=_params("parallel"), name=name)(*args, *tabs)


def ffn_act_fwd(u, wb, name):
    _, nbk, s, c = u.shape
    tr = min(s, 256)

    def body(u_ref, h_ref, wb_ref, a_ref, ug_ref, su, sg):
        first = pl.program_id(1) == 0
        for lc in range(c // LANES):
            ln = slice(lc * LANES, (lc + 1) * LANES)
            for half, scr in enumerate((su, sg)):
                scr[0:FFN_HALO, ln] = jnp.where(first, 0.0, h_ref[half, :, ln].astype(f32))
            for r0 in range(0, tr, STRIP_ROWS):
                rows = slice(r0, r0 + STRIP_ROWS)
                conv = []
                for half, scr in enumerate((su, sg)):
                    xv = u_ref[half, rows, ln].astype(f32)
                    scr[FFN_HALO + r0:FFN_HALO + r0 + STRIP_ROWS, ln] = xv
                    acc = wb_ref[half, FFN_CONV:FFN_CONV + 1, ln] + wb_ref[half, FFN_CONV - 1:FFN_CONV, ln] * xv
                    for k in range(FFN_CONV - 1):
                        acc = acc + wb_ref[half, k:k + 1, ln] * scr[pl.ds(FFN_HALO + r0 - (FFN_CONV - 1) + k, STRIP_ROWS), ln]
                    ug_ref[half, rows, ln] = acc.astype(ug_ref.dtype)
                    conv.append(acc)
                up, gate = conv
                a_ref[rows, ln] = (gate * _sigmoid(gate) * up).astype(a_ref.dtype)

    both = pl.BlockSpec((2, None, tr, c), lambda p, i: (0, p, i, 0))
    return pl.pallas_call(
        body, grid=(nbk, s // tr),
        in_specs=[both,
                  pl.BlockSpec((2, None, FFN_HALO, c), lambda p, i: (0, p, jnp.maximum(i * (tr // FFN_HALO) - 1, 0), 0)),
                  pl.BlockSpec((None, 2, 8, c), lambda p, i: (p, 0, 0, 0))],
        out_specs=[pl.BlockSpec((None, tr, c), lambda p, i: (p, i, 0)), both],
        out_shape=[SDS((nbk, s, c), bf16), SDS(u.shape, bf16)],
        scratch_shapes=[pltpu.VMEM((tr + FFN_HALO, c), f32)] * 2,
        compiler_params=_params("parallel", "arbitrary"), name=name)(u, u, wb)


def ffn_act_bwd(da, ug, u, wb, name):
    _, nbk, s, c = u.shape
    tr = min(s, 256)
    nt = s // tr

    def body(da_ref, ug_ref, u_ref, wb_ref, du_ref, dwb_ref, eu, eg):
        step = pl.program_id(1)

        @pl.when(step == 0)
        def _():
            eu[tr:tr + FFN_TAIL, :] = jnp.zeros((FFN_TAIL, c), f32)
            eg[tr:tr + FFN_TAIL, :] = jnp.zeros((FFN_TAIL, c), f32)
            dwb_ref[...] = jnp.zeros(dwb_ref.shape, f32)

        fold = lambda t: jnp.sum(t.reshape(STRIP_ROWS // 8, 8, LANES), axis=0)
        for lc in range(c // LANES):
            ln = slice(lc * LANES, (lc + 1) * LANES)
            sums = [[jnp.zeros((8, LANES), f32) for _ in range(FFN_CONV + 1)] for _ in range(2)]
            for r0 in reversed(range(0, tr, STRIP_ROWS)):
                rows = slice(r0, r0 + STRIP_ROWS)
                up, gate = ug_ref[0, rows, ln].astype(f32), ug_ref[1, rows, ln].astype(f32)
                sig = _sigmoid(gate)
                dav = da_ref[rows, ln].astype(f32)
                grads = (dav * (gate * sig), dav * up * (sig * (1.0 + gate * (1.0 - sig))))
                for half, ext in enumerate((eu, eg)):
                    dv = grads[half]
                    ext[rows, ln] = dv
                    xv = u_ref[half, rows, ln].astype(f32)
                    acc = None
                    for k in range(FFN_CONV):
                        ahead = dv if k == FFN_CONV - 1 else ext[pl.ds(r0 + FFN_CONV - 1 - k, STRIP_ROWS), ln]
                        term = wb_ref[half, k:k + 1, ln] * ahead
                        acc = term if acc is None else acc + term
                        sums[half][k] = sums[half][k] + fold(xv * ahead)
                    sums[half][FFN_CONV] = sums[half][FFN_CONV] + fold(dv)
                    du_ref[half, rows, ln] = acc.astype(du_ref.dtype)
            for half, ext in enumerate((eu, eg)):
                ext[tr:tr + FFN_TAIL, ln] = ext[0:FFN_TAIL, ln]
                for k in range(FFN_CONV + 1):
                    dwb_ref[half, k:k + 1, ln] += jnp.sum(sums[half][k], axis=0, keepdims=True)

    rev = lambda i: nt - 1 - i
    both = pl.BlockSpec((2, None, tr, c), lambda p, i: (0, p, rev(i), 0))
    return pl.pallas_call(
        body, grid=(nbk, nt),
        in_specs=[pl.BlockSpec((None, tr, c), lambda p, i: (p, rev(i), 0)), both, both,
                  pl.BlockSpec((None, 2, 8, c), lambda p, i: (p, 0, 0, 0))],
        out_specs=[both, pl.BlockSpec((None, 2, 8, c), lambda p, i: (p, 0, 0, 0))],
        out_shape=[SDS((2, nbk, s, c), bf16), SDS((nbk, 2, 8, c), f32)],
        scratch_shapes=[pltpu.VMEM((tr + FFN_TAIL, c), f32)] * 2,
        compiler_params=_params("parallel", "arbitrary"), name=name)(da, ug, u, wb)


def _glu(zv, c):
    return zv[:, :c] * _sigmoid(zv[:, c:])


def _conv_fill(z_ref, h_ref, scr, first, tr, c):
    scr[0:CONV_HALO, :] = jnp.where(first, 0.0, _glu(h_ref[...], c))
    scr[CONV_HALO:CONV_HALO + tr, :] = _glu(z_ref[...], c)


def _conv_taps(b):
    return [(a, CONV_KERNEL - 1 - 8 * a - b) for a in range(CONV_HALO // 8) if CONV_KERNEL - 1 - 8 * a - b >= 0]


def _layernorm_parts(cv):
    mu = jnp.mean(cv, axis=-1, keepdims=True)
    cen = cv - mu
    rstd = lax.rsqrt(jnp.mean(cen * cen, axis=-1, keepdims=True) + EPS)
    return cen * rstd, rstd


def conv_module_fwd(z, wdw, vecs, name):
    s, c2 = z.shape
    c = c2 // 2
    tr = min(s, 256)

    def body(z_ref, h_ref, w_ref, v_ref, c_ref, s_ref, scr, zb):
        _conv_fill(z_ref, h_ref, scr, pl.program_id(0) == 0, tr, c)
        acc = jnp.broadcast_to(v_ref[0:1, :], (tr, c))
        for b in range(8):
            part = None
            for a, j in _conv_taps(b):
                term = w_ref[j:j + 1, :] * scr[pl.ds(CONV_HALO - 8 - 8 * a, tr + 8), :]
                part = term if part is None else part + term
            if b == 0:
                acc = acc + part[8:]
            else:
                zb[...] = part
                acc = acc + zb[pl.ds(8 - b, tr), :]
        c_ref[...] = acc
        chat, _ = _layernorm_parts(acc)
        ln = chat * v_ref[1:2, :] + v_ref[2:3, :]
        s_ref[...] = (ln * _sigmoid(ln)).astype(s_ref.dtype)

    return pl.pallas_call(
        body, grid=(s // tr,),
        in_specs=[_rows(tr, c2), pl.BlockSpec((CONV_HALO, c2), lambda i: (jnp.maximum(i * (tr // CONV_HALO) - 1, 0), 0)),
                  _fixed(CONV_HALO, c), _fixed(8, c)],
        out_specs=[_rows(tr, c), _rows(tr, c)], out_shape=[SDS((s, c), f32), SDS((s, c), bf16)],
        scratch_shapes=[pltpu.VMEM((tr + CONV_HALO, c), f32), pltpu.VMEM((tr + 8, c), f32)],
        compiler_params=_params("arbitrary"), name=name)(z, z, wdw, vecs)


def conv_module_bwd(ds, cpre, z, wdw, vecs, name):
    s, c2 = z.shape
    c = c2 // 2
    tr = min(s, 256)
    nt = s // tr

    def body(ds_ref, c_ref, z_ref, w_ref, v_ref, dz_ref, dw_ref, dv_ref, db_ref, ext, dwp):
        step = pl.program_id(0)

        @pl.when(step == 0)
        def _():
            ext[tr:tr + CONV_HALO, :] = jnp.zeros((CONV_HALO, c), f32)
            dwp[...] = jnp.zeros(dwp.shape, f32)
            dv_ref[...] = jnp.zeros(dv_ref.shape, f32)
            db_ref[...] = jnp.zeros(db_ref.shape, f32)

        gain, bias = v_ref[1:2, :], v_ref[2:3, :]
        fold16 = lambda t: t[:8] + t[8:]
        sums = [jnp.zeros((8, c), f32) for _ in range(3)]
        for r0 in range(0, tr, 16):
            rows = slice(r0, r0 + 16)
            chat, rstd = _layernorm_parts(c_ref[rows, :])
            ln = chat * gain + bias
            sig = _sigmoid(ln)
            dln = ds_ref[rows, :].astype(f32) * (sig * (1.0 + ln * (1.0 - sig)))
            gy = dln * gain
            dc = rstd * (gy - jnp.mean(gy, axis=-1, keepdims=True) - chat * jnp.mean(gy * chat, axis=-1, keepdims=True))
            ext[rows, :] = dc
            for k, t in enumerate((dc, dln * chat, dln)):
                sums[k] = sums[k] + fold16(t)
        for k in range(3):
            dv_ref[k:k + 1, :] += jnp.sum(sums[k], axis=0, keepdims=True)

        fold = lambda t: jnp.sum(t.reshape(STRIP_ROWS // 8, 8, LANES), axis=0)
        for lc in range(c // LANES):
            ln_a = slice(lc * LANES, (lc + 1) * LANES)
            ln_g = slice(c + lc * LANES, c + (lc + 1) * LANES)
            dbs = [jnp.zeros((8, LANES), f32) for _ in range(2)]
            for r0 in range(0, tr, STRIP_ROWS):
                rows = slice(r0, r0 + STRIP_ROWS)
                a, sg = z_ref[rows, ln_a], _sigmoid(z_ref[rows, ln_g])
                uv = a * sg
                du = jnp.zeros((STRIP_ROWS, LANES), f32)
                for b in range(8):
                    src = ext[pl.ds(r0 + b, STRIP_ROWS + CONV_HALO - 8), ln_a]
                    for a8, j in _conv_taps(b):
                        ahead = src[8 * a8:8 * a8 + STRIP_ROWS]
                        du = du + w_ref[j:j + 1, ln_a] * ahead
                        dwp[8 * j:8 * j + 8, ln_a] += fold(uv * ahead)
                da = du * sg
                dg = du * a * (sg * (1.0 - sg))
                dz_ref[rows, ln_a] = da.astype(dz_ref.dtype)
                dz_ref[rows, ln_g] = dg.astype(dz_ref.dtype)
                dbs = [dbs[0] + fold(da), dbs[1] + fold(dg)]
            db_ref[:, ln_a] += jnp.sum(dbs[0], axis=0, keepdims=True)
            db_ref[:, ln_g] += jnp.sum(dbs[1], axis=0, keepdims=True)
        ext[tr:tr + CONV_HALO, :] = ext[0:CONV_HALO, :]

        @pl.when(step == nt - 1)
        def _():
            for j in range(CONV_HALO):
                dw_ref[j:j + 1, :] = jnp.sum(dwp[8 * j:8 * j + 8, :], axis=0, keepdims=True)

    rev = lambda i: nt - 1 - i
    back = lambda d: pl.BlockSpec((tr, d), lambda i: (rev(i), 0))
    return pl.pallas_call(
        body, grid=(nt,),
        in_specs=[back(c), back(c), back(c2), _fixed(CONV_HALO, c), _fixed(8, c)],
        out_specs=[back(c2), _fixed(CONV_HALO, c), _fixed(8, c), _fixed(1, c2)],
        out_shape=[SDS((s, c2), bf16), SDS((CONV_HALO, c), f32), SDS((8, c), f32), SDS((1, c2), f32)],
        scratch_shapes=[pltpu.VMEM((tr + CONV_HALO, c), f32), pltpu.VMEM((8 * CONV_HALO, c), f32)],
        compiler_params=_params("arbitrary"), name=name)(ds, cpre, z, wdw, vecs)


def _tile2d(r, n):
    tn = n if n <= 2048 else 1024
    tr = r
    while tr * tn * 4 > (1 << 21) and tr % 16 == 0:
        tr //= 2
    assert r % tr == 0 and n % tn == 0
    return tr, tn


def adamw(w, g, m, v, name):
    r, n = w.shape
    tr, tn = _tile2d(r, n)

    def body(w_ref, g_ref, m_ref, v_ref, d_ref, nm_ref, nv_ref):
        gv = g_ref[...]
        nm = ADAM_B1 * m_ref[...] + (1.0 - ADAM_B1) * gv
        nv = ADAM_B2 * v_ref[...] + (1.0 - ADAM_B2) * (gv * gv)
        m_hat = nm / (1.0 - ADAM_B1 ** ADAM_STEP)
        v_hat = nv / (1.0 - ADAM_B2 ** ADAM_STEP)
        d_ref[...] = -ADAM_LR * (m_hat / (jnp.sqrt(v_hat) + ADAM_EPS) + ADAM_WD * w_ref[...])
        nm_ref[...] = nm
        nv_ref[...] = nv

    blk = pl.BlockSpec((tr, tn), lambda i, j: (i, j))
    return pl.pallas_call(body, grid=(r // tr, n // tn), in_specs=[blk] * 4, out_specs=[blk] * 3,
                          out_shape=[SDS((r, n), f32)] * 3, compiler_params=_params("parallel", "parallel"),
                          name=name)(w, g, m, v)


def add_core_halves(grad, got, where, name):
    _, _, rh, n = grad.shape
    tr, tn = _tile2d(rh, n)

    def body(w_ref, a_ref, b_ref, o_ref):
        o_ref[...] = (a_ref[...] + b_ref[...]).astype(o_ref.dtype)

    return pl.pallas_call(
        body,
        grid_spec=pltpu.PrefetchScalarGridSpec(
            num_scalar_prefetch=1, grid=(N_CHIPS, rh // tr, n // tn),
            in_specs=[pl.BlockSpec((None, None, tr, tn), lambda p, i, j, w_ref: (p, w_ref[0], i, j)),
                      pl.BlockSpec((None, tr, tn), lambda p, i, j, w_ref: (p, i, j))],
            out_specs=pl.BlockSpec((None, tr, tn), lambda p, i, j, w_ref: (p, i, j))),
        out_shape=SDS((N_CHIPS, rh, n), bf16), compiler_params=_params("parallel", "parallel", "parallel"),
        name=name)(where, grad, got)


def add_chip_parts(grad, got_core, got_chips, where, name):
    _, _, rh, n = grad.shape
    tr, tn = _tile2d(rh, n)

    def body(w_ref, a_ref, b_ref, g_ref, o_ref):
        acc = a_ref[...] + b_ref[...]
        for k in range(N_CHIPS - 1):
            acc = acc + g_ref[k].astype(f32)
        o_ref[...] = acc

    return pl.pallas_call(
        body,
        grid_spec=pltpu.PrefetchScalarGridSpec(
            num_scalar_prefetch=1, grid=(rh // tr, n // tn),
            in_specs=[pl.BlockSpec((None, None, tr, tn), lambda i, j, w_ref: (w_ref[1], w_ref[0], i, j)),
                      pl.BlockSpec((None, tr, tn), lambda i, j, w_ref: (w_ref[1], i, j)),
                      pl.BlockSpec((N_CHIPS - 1, tr, tn), lambda i, j, w_ref: (0, i, j))],
            out_specs=pl.BlockSpec((None, tr, tn), lambda i, j, w_ref: (w_ref[0], i, j))),
        out_shape=SDS((2, rh, n), f32), compiler_params=_params("parallel", "parallel"),
        name=name)(where, grad, got_core, got_chips)


def sum_devices(parts, name):
    nd, r, n = parts.shape

    def body(p_ref, o_ref):
        acc = p_ref[0]
        for k in range(1, nd):
            acc = acc + p_ref[k]
        o_ref[...] = acc

    return pl.pallas_call(body, out_shape=SDS((r, n), f32), name=name)(parts)


def _position():
    return lax.axis_index("x"), lax.axis_index("y"), lax.axis_index("c")


def _other_chips(x, y):
    return [(1 - x, y), (x, 1 - y), (1 - x, 1 - y)]


def _remote(src, dst, send, recv, to):
    return pltpu.make_async_remote_copy(src_ref=src, dst_ref=dst, send_sem=send, recv_sem=recv, device_id=to,
                                        device_id_type=MESH)


def gather_chips(bufs, name):
    n = len(bufs)
    nk = N_CHIPS - 1

    def body(*refs):
        bufs_ = refs[n:2 * n]
        send, recv = refs[2 * n:]
        x, y, c = _position()
        me = 2 * x + y
        chips = _other_chips(x, y)
        sends = []
        for t in range(n):
            for k, (px, py) in enumerate(chips):
                out = _remote(bufs_[t].at[me, c], bufs_[t].at[me, c], send.at[t, k], recv.at[t, k], (px, py, c))
                out.start()
                sends.append(out)
        for t in range(n):
            for k, (px, py) in enumerate(chips):
                piece = bufs_[t].at[2 * px + py, c]
                _remote(piece, piece, send.at[t, k], recv.at[t, k], (px, py, c)).wait_recv()
                on = _remote(piece, piece, send.at[t, nk + k], recv.at[t, nk + k], (x, y, 1 - c))
                on.start()
                sends.append(on)
        for t in range(n):
            for k, (px, py) in enumerate(chips):
                piece = bufs_[t].at[2 * px + py, 1 - c]
                _remote(piece, piece, send.at[t, nk + k], recv.at[t, nk + k], (x, y, 1 - c)).wait_recv()
        for cp in sends:
            cp.wait_send()

    return pl.pallas_call(
        body, in_specs=[ANY] * n, out_specs=[ANY] * n,
        out_shape=[SDS(a.shape, a.dtype) for a in bufs],
        input_output_aliases={t: t for t in range(n)},
        scratch_shapes=[pltpu.SemaphoreType.DMA((n, 2 * nk)), pltpu.SemaphoreType.DMA((n, 2 * nk))],
        name=name)(*bufs)


def swap_core_halves(grads, name):
    n = len(grads)

    def body(*refs):
        ins, outs = refs[:n], refs[n:2 * n]
        send, recv = refs[2 * n:]
        x, y, c = _position()
        pending = []
        for t in range(n):
            out = _remote(ins[t].at[:, 1 - c], outs[t], send.at[t], recv.at[t], (x, y, 1 - c))
            out.start()
            pending.append(out.wait)
        for wait in pending:
            wait()

    return pl.pallas_call(
        body, in_specs=[ANY] * n, out_specs=[ANY] * n,
        out_shape=[SDS((a.shape[0],) + a.shape[2:], a.dtype) for a in grads],
        scratch_shapes=[pltpu.SemaphoreType.DMA((n,)), pltpu.SemaphoreType.DMA((n,))],
        name=name)(*grads)


def scatter_chips(parts, name):
    n = len(parts)

    def body(*refs):
        ins, outs = refs[:n], refs[n:2 * n]
        send, recv = refs[2 * n:]
        x, y, c = _position()
        pending = []
        for t in range(n):
            for k, (px, py) in enumerate(_other_chips(x, y)):
                out = _remote(ins[t].at[2 * px + py], outs[t].at[k], send.at[t, k], recv.at[t, k], (px, py, c))
                out.start()
                pending.append(out.wait)
        for wait in pending:
            wait()

    return pl.pallas_call(
        body, in_specs=[ANY] * n, out_specs=[ANY] * n,
        out_shape=[SDS((N_CHIPS - 1,) + a.shape[1:], a.dtype) for a in parts],
        scratch_shapes=[pltpu.SemaphoreType.DMA((n, N_CHIPS - 1)), pltpu.SemaphoreType.DMA((n, N_CHIPS - 1))],
        name=name)(*parts)


def share_core_halves(bufs, name):
    n = len(bufs)

    def body(*refs):
        bufs_ = refs[n:2 * n]
        send, recv = refs[2 * n:]
        x, y, c = _position()
        pending = []
        for t in range(n):
            out = _remote(bufs_[t].at[c], bufs_[t].at[c], send.at[t], recv.at[t], (x, y, 1 - c))
            out.start()
            pending.append(out.wait_send)
            other = bufs_[t].at[1 - c]
            pending.append(_remote(other, other, send.at[t], recv.at[t], (x, y, 1 - c)).wait_recv)
        for wait in pending:
            wait()

    return pl.pallas_call(
        body, in_specs=[ANY] * n, out_specs=[ANY] * n, out_shape=[SDS(a.shape, a.dtype) for a in bufs],
        input_output_aliases={t: t for t in range(n)},
        scratch_shapes=[pltpu.SemaphoreType.DMA((n,)), pltpu.SemaphoreType.DMA((n,))],
        name=name)(*bufs)


def gather_devices(v, name):
    flips = [(dx, dy, dc) for dx in (0, 1) for dy in (0, 1) for dc in (0, 1)][1:]

    def body(v_ref, o_ref, send, recv, local):
        x, y, c = _position()
        me = 4 * x + 2 * y + c
        own = pltpu.make_async_copy(v_ref, o_ref.at[me], local)
        own.start()
        pending = [own.wait]
        for k, (dx, dy, dc) in enumerate(flips):
            px, py, pc = x ^ dx, y ^ dy, c ^ dc
            out = _remote(v_ref, o_ref.at[me], send.at[k], recv.at[k], (px, py, pc))
            out.start()
            pending.append(out.wait_send)
            pending.append(_remote(v_ref, o_ref.at[4 * px + 2 * py + pc], send.at[k], recv.at[k], (px, py, pc)).wait_recv)
        for wait in pending:
            wait()

    return pl.pallas_call(
        body, in_specs=[ANY], out_specs=ANY, out_shape=SDS((8,) + v.shape, v.dtype),
        scratch_shapes=[pltpu.SemaphoreType.DMA((7,)), pltpu.SemaphoreType.DMA((7,)), pltpu.SemaphoreType.DMA],
        name=name)(v)


SMALL = ("norm_g", "conv_b_pw1", "conv_w_dw", "conv_b_dw", "conv_ln_g", "conv_ln_b", "conv_b_pw2", "ffn_w_dw")


def _pack_rows(arrs, rows):
    flat = jnp.concatenate([a.reshape(-1, LANES) for a in arrs], axis=0)
    return jnp.pad(flat, ((0, rows - flat.shape[0]), (0, 0)))


def _unpack_rows(packed, shapes):
    out, at = [], 0
    for shp in shapes:
        size = 1
        for dim in shp:
            size *= dim
        rows = size // LANES
        out.append(packed[..., at:at + rows, :].reshape(packed.shape[:-2] + tuple(shp)))
        at += rows
    return out


def _join_last(t):
    t = jnp.moveaxis(t, 0, -2)
    return t.reshape(t.shape[:-2] + (t.shape[-2] * t.shape[-1],))


def _split_last(t):
    t = t.reshape(t.shape[:-1] + (N_CHIPS, t.shape[-1] // N_CHIPS))
    return jnp.moveaxis(t, -2, 0)


def _rope_tables(positions):
    half = ROT_DIM // 2
    inv_freq = ROPE_THETA ** (-jnp.arange(half, dtype=f32) / half)
    ang = positions.astype(f32).reshape(-1, 1) * inv_freq
    cos, sin = jnp.cos(ang), jnp.sin(ang)
    s = ang.shape[0]
    rest = HEAD_DIM - ROT_DIM
    head = lambda lo, hi, fill: jnp.concatenate([lo, hi, jnp.full((s, rest), fill, f32)], axis=1)
    zero = jnp.zeros((s, half), f32)
    twice = lambda t: jnp.concatenate([t] * (LANES // HEAD_DIM), axis=1)
    return twice(head(cos, cos, 1.0)), twice(head(-sin, zero, 0.0)), twice(head(zero, sin, 0.0))


def _pad_rows(t, rows):
    return jnp.pad(t, ((0, rows - t.shape[0]), (0, 0)))


def _ffn_pack(w_dw, b_dw):
    t = jnp.concatenate([w_dw, b_dw[None]], axis=0)
    t = t.reshape(FFN_CONV + 1, 2, 2, -1)
    t = jnp.transpose(t, (2, 1, 0, 3))
    return jnp.pad(t, ((0, 0), (0, 0), (0, 8 - (FFN_CONV + 1)), (0, 0)))


def _ffn_unpack(d):
    t = jnp.transpose(d[:, :, :FFN_CONV + 1], (2, 1, 0, 3)).reshape(FFN_CONV + 1, -1)
    return t[:FFN_CONV], t[FFN_CONV]


def _ffn_block(x, g_pre, g_post, w_up, li, w_down, wb, tag):
    s = x.shape[0]
    h = prenorm(x, g_pre, f"{tag}_prenorm")
    u = mm_nn(h[None], w_up, li, None, N_CHIPS, bf16, 512, w_up.shape[-1], f"{tag}_up")
    u4 = u.reshape(2, 2, s, u.shape[-1])
    a, ug = ffn_act_fwd(u4, wb, f"{tag}_act")
    y = mm_nn(a, w_down, 0, None, 1, f32, 512, 512, f"{tag}_down")[0]
    return y, (h, u4, ug, a)


def _ffn_block_bwd(dy, saved, w_up, li, w_down, wb, tag):
    h, u4, ug, a = saved
    d_down = mm_tn(a, dy[None], 1, 1024, a.shape[-1], 512, f"{tag}_dwdown")
    da = mm_nt(dy[None], w_down, 0, 2, bf16, 1024, a.shape[-1], w_down.shape[-1], f"{tag}_da")
    du4, dwb = ffn_act_bwd(da, ug, u4, wb, f"{tag}_actbwd")
    du = du4.reshape((N_CHIPS,) + du4.shape[2:])
    d_up = mm_tn(h[None], du, N_CHIPS, 1024, h.shape[-1], du.shape[-1], f"{tag}_dwup")
    dh = mm_nt(du, w_up, li, 1, f32, 1024, h.shape[-1], du.shape[-1], f"{tag}_dh")[0]
    return dh, d_up, d_down, dwb


def kernel(x, positions, norm_g, attn_w_qkv, attn_w_o, conv_w_pw1, conv_b_pw1, conv_w_dw, conv_b_dw, conv_ln_g, conv_ln_b, conv_w_pw2, conv_b_pw2, ffn_w_up, ffn_w_dw, ffn_b_dw, ffn_w_down, loss_target, m_norm_g, m_attn_w_qkv, m_attn_w_o, m_conv_w_pw1, m_conv_b_pw1, m_conv_w_dw, m_conv_b_dw, m_conv_ln_g, m_conv_ln_b, m_conv_w_pw2, m_conv_b_pw2, m_ffn_w_up, m_ffn_w_dw, m_ffn_b_dw, m_ffn_w_down, v_norm_g, v_attn_w_qkv, v_attn_w_o, v_conv_w_pw1, v_conv_b_pw1, v_conv_w_dw, v_conv_b_dw, v_conv_ln_g, v_conv_ln_b, v_conv_w_pw2, v_conv_b_pw2, v_ffn_w_up, v_ffn_w_dw, v_ffn_b_dw, v_ffn_w_down):
    weights = dict(norm_g=norm_g, attn_w_qkv=attn_w_qkv, attn_w_o=attn_w_o, conv_w_pw1=conv_w_pw1, conv_b_pw1=conv_b_pw1,
                   conv_w_dw=conv_w_dw, conv_b_dw=conv_b_dw, conv_ln_g=conv_ln_g, conv_ln_b=conv_ln_b, conv_w_pw2=conv_w_pw2,
                   conv_b_pw2=conv_b_pw2, ffn_w_up=ffn_w_up, ffn_w_dw=ffn_w_dw, ffn_b_dw=ffn_b_dw, ffn_w_down=ffn_w_down)
    mom1 = dict(norm_g=m_norm_g, attn_w_qkv=m_attn_w_qkv, attn_w_o=m_attn_w_o, conv_w_pw1=m_conv_w_pw1, conv_b_pw1=m_conv_b_pw1,
                conv_w_dw=m_conv_w_dw, conv_b_dw=m_conv_b_dw, conv_ln_g=m_conv_ln_g, conv_ln_b=m_conv_ln_b, conv_w_pw2=m_conv_w_pw2,
                conv_b_pw2=m_conv_b_pw2, ffn_w_up=m_ffn_w_up, ffn_w_dw=m_ffn_w_dw, ffn_b_dw=m_ffn_b_dw, ffn_w_down=m_ffn_w_down)
    mom2 = dict(norm_g=v_norm_g, attn_w_qkv=v_attn_w_qkv, attn_w_o=v_attn_w_o, conv_w_pw1=v_conv_w_pw1, conv_b_pw1=v_conv_b_pw1,
                conv_w_dw=v_conv_w_dw, conv_b_dw=v_conv_b_dw, conv_ln_g=v_conv_ln_g, conv_ln_b=v_conv_ln_b, conv_w_pw2=v_conv_w_pw2,
                conv_b_pw2=v_conv_b_pw2, ffn_w_up=v_ffn_w_up, ffn_w_dw=v_ffn_w_dw, ffn_b_dw=v_ffn_b_dw, ffn_w_down=v_ffn_w_down)
    big = ("attn_w_qkv", "attn_w_o", "conv_w_pw1", "conv_w_pw2", "ffn_w_up", "ffn_w_down")
    xi, yi, ci = _position()
    chip = (2 * xi + yi).astype(jnp.int32).reshape(1)
    where = jnp.stack([ci, 2 * xi + yi]).astype(jnp.int32)

    x = x[0]
    target = loss_target[0]
    s, d = x.shape

    small_shapes = [weights[k].shape for k in SMALL]
    small_rows = -(-sum(weights[k].size for k in SMALL) // LANES // 8) * 8
    small_w = _pack_rows([weights[k] for k in SMALL], small_rows)
    def own_slot(shard):
        halves = shard.reshape(1, 2, -1, shard.shape[-1])
        return lax.dynamic_update_slice(lax.empty((N_CHIPS,) + halves.shape[1:], shard.dtype), halves, (chip[0], 0, 0, 0))

    gathered = gather_chips([own_slot(weights[k].astype(bf16)) for k in big] + [own_slot(small_w)], "gather_weights")
    gw = {k: t.reshape((N_CHIPS,) + weights[k].shape) for k, t in zip(big, gathered[:-1])}
    full_small = dict(zip(SMALL, [_join_last(t) for t in _unpack_rows(gathered[-1].reshape(N_CHIPS, small_rows, LANES), small_shapes)]))
    w_qkv, w_o, w_pw1, w_up = gw["attn_w_qkv"], gw["attn_w_o"], gw["conv_w_pw1"], gw["ffn_w_up"]
    w_pw2 = gw["conv_w_pw2"].reshape(1, 1, -1, d)
    w_down = [gw["ffn_w_down"][:, i].reshape(1, 1, -1, d) for i in range(2)]
    gains = full_small["norm_g"]
    gain = lambda i, k: gains[i, k][None]
    b_pw1 = full_small["conv_b_pw1"]
    conv_wdw = _pad_rows(full_small["conv_w_dw"][0], CONV_HALO)
    conv_vecs = _pad_rows(jnp.concatenate([full_small["conv_b_dw"], full_small["conv_ln_g"], full_small["conv_ln_b"]], axis=0), 8)
    b_pw2 = full_small["conv_b_pw2"]
    wbs = [_ffn_pack(full_small["ffn_w_dw"][i], ffn_b_dw[i]) for i in range(2)]
    tabs = _rope_tables(positions[0])
    tabs_t = (tabs[0], -tabs[1], -tabs[2])
    head_ones = (jnp.arange(GROUP_WIDTH)[:, None] // HEAD_DIM == jnp.arange(GROUP_WIDTH)[None, :] // HEAD_DIM).astype(bf16)

    h0 = prenorm(x, gain(0, 0), "l0_prenorm")
    w_qkv_flat = jnp.transpose(w_qkv[:, 0], (1, 0, 2)).reshape(d, -1)
    qkvg = qkv_rope(h0, w_qkv_flat, tabs, "qkv")
    att = [attn_fwd(qkvg, g, dil, f"attn_fwd{g}") for g, (_, dil) in enumerate(DILATED_GROUPS)]
    os_, lses, lses_phased = ([a[k] for a in att] for k in range(3))
    mixed = mix_fwd(os_, lses, "mix")
    y0 = mm_nn(mixed[None], w_o, 0, None, 1, f32, 1024, w_o.shape[-1], "attn_out")[0]
    x1 = postnorm_residual(x, y0, gain(0, 1), "l0_postnorm")
    y1, ffn0 = _ffn_block(x1, gain(0, 2), gain(0, 3), w_up, 0, w_down[0], wbs[0], "ffn0")
    x2 = postnorm_residual(x1, y1, gain(0, 3), "l0_ffn_postnorm")

    h2 = prenorm(x2, gain(1, 0), "l1_prenorm")
    z = mm_nn(h2[None], w_pw1, 0, b_pw1, 1, f32, 1024, w_pw1.shape[-1], "pw1")[0]
    cpre, sw = conv_module_fwd(z, conv_wdw, conv_vecs, "conv_fwd")
    y2 = mm_nn(sw[None], w_pw2, 0, b_pw2, 1, f32, 512, 512, "pw2")[0]
    x3 = postnorm_residual(x2, y2, gain(1, 1), "l1_postnorm")
    y3, ffn1 = _ffn_block(x3, gain(1, 2), gain(1, 3), w_up, 1, w_down[1], wbs[1], "ffn1")
    dx4, loss_row = final_loss(x3, y3, gain(1, 3), target, "loss")
    loss = lax.psum(loss_row[0, 0], ("x", "y", "c"))

    dgain = [[None] * 4 for _ in range(2)]
    dy3, dgain[1][3], _ = norm_bwd(y3, gain(1, 3), dx4, None, bf16, "l1_ffn_postnorm_bwd")
    dh, d_up1, d_down1, dwb1 = _ffn_block_bwd(dy3, ffn1, w_up, 1, w_down[1], wbs[1], "ffn1")
    dx3, dgain[1][2], _ = norm_bwd(x3, gain(1, 2), dh, dx4, f32, "l1_ffn_prenorm_bwd")

    dy2, dgain[1][1], d_b_pw2 = norm_bwd(y2, gain(1, 1), dx3, None, bf16, "l1_postnorm_bwd")
    d_pw2 = mm_tn(sw[None], dy2[None], 1, 1024, d, 512, "dw_pw2")
    dsw = mm_nt(dy2[None], w_pw2, 0, 1, f32, 1024, d, d, "d_swish")[0]
    dz, d_conv_wdw, d_conv_vecs, d_b_pw1 = conv_module_bwd(dsw, cpre, z, conv_wdw, conv_vecs, "conv_bwd")
    d_pw1 = mm_tn(h2[None], dz[None], N_CHIPS, 1024, d, w_pw1.shape[-1], "dw_pw1")
    dh = mm_nt(dz[None], w_pw1, 0, 1, f32, 1024, d, w_pw1.shape[-1], "d_h2")[0]
    dx2, dgain[1][0], _ = norm_bwd(x2, gain(1, 0), dh, dx3, f32, "l1_prenorm_bwd")

    dy1, dgain[0][3], _ = norm_bwd(y1, gain(0, 3), dx2, None, bf16, "l0_ffn_postnorm_bwd")
    dh, d_up0, d_down0, dwb0 = _ffn_block_bwd(dy1, ffn0, w_up, 0, w_down[0], wbs[0], "ffn0")
    dx1, dgain[0][2], _ = norm_bwd(x1, gain(0, 2), dh, dx2, f32, "l0_ffn_prenorm_bwd")

    dy0, dgain[0][1], _ = norm_bwd(y0, gain(0, 1), dx1, None, bf16, "l0_postnorm_bwd")
    d_wo = mm_tn(mixed[None], dy0[None], N_CHIPS, 1024, GROUP_WIDTH, w_o.shape[-1], "dw_o")
    dmixed = mm_nt(dy0[None], w_o, 0, 1, f32, 1024, GROUP_WIDTH, w_o.shape[-1], "d_mixed")[0]
    mb = mix_bwd(dmixed, os_, lses, head_ones, "mix_bwd")
    parts = [attn_bwd(qkvg, mb[g], lses_phased[g], mb[3 + g], g, dil, f"attn_bwd{g}") for g, (_, dil) in enumerate(DILATED_GROUPS)]
    dqkv = dqkv_assemble(parts, tabs_t, "dqkv")
    d_qkv = mm_tn(h0[None], dqkv[None], N_CHIPS, 1024, d, w_qkv.shape[-1], "dw_qkv")
    dh = mm_nt(dqkv[None], w_qkv, 0, 1, f32, 1024, d, w_qkv.shape[-1], "d_h0")[0]
    grad_x, dgain[0][0], _ = norm_bwd(x, gain(0, 0), dh, dx1, f32, "l0_prenorm_bwd")

    stack_layers = lambda a, b: jnp.stack([a, b], axis=1)
    gbig = dict(
        attn_w_qkv=d_qkv, attn_w_o=d_wo, conv_w_pw1=d_pw1, conv_w_pw2=d_pw2[0].reshape(N_CHIPS, -1, d),
        ffn_w_up=stack_layers(d_up0, d_up1),
        ffn_w_down=stack_layers(d_down0[0].reshape(N_CHIPS, -1, d), d_down1[0].reshape(N_CHIPS, -1, d)))
    d_ffn_dw, d_ffn_b = zip(*[_ffn_unpack(t) for t in (dwb0, dwb1)])
    gsmall = dict(
        norm_g=jnp.stack([jnp.concatenate(row, axis=0) for row in dgain], axis=0),
        conv_b_pw1=d_b_pw1, conv_w_dw=d_conv_wdw[None, :CONV_KERNEL], conv_b_dw=d_conv_vecs[0:1], conv_ln_g=d_conv_vecs[1:2],
        conv_ln_b=d_conv_vecs[2:3], conv_b_pw2=d_b_pw2, ffn_w_dw=jnp.stack(d_ffn_dw, axis=0))
    bias_rows = ffn_b_dw.size // LANES
    small_g = jnp.concatenate([jnp.concatenate([_pack_rows([_split_last(gsmall[k])[p] for k in SMALL], small_rows)
                                                for p in range(N_CHIPS)], axis=0),
                               jnp.stack(d_ffn_b, axis=0).reshape(bias_rows, LANES)], axis=0)

    small_sum = sum_devices(gather_devices(small_g, "gather_small_grads"), "sum_small_grads")
    my_small = lax.dynamic_slice_in_dim(small_sum, chip[0] * small_rows, small_rows, axis=0)
    g_small = jnp.concatenate([my_small, small_sum[N_CHIPS * small_rows:]], axis=0)

    halves = [gbig[k].reshape(N_CHIPS, 2, -1, gbig[k].shape[-1]) for k in big]
    from_core = swap_core_halves(halves, "swap_core_halves")
    chip_parts = [add_core_halves(a, b, where, f"add_core_{k}") for k, a, b in zip(big, halves, from_core)]
    from_chips = scatter_chips(chip_parts, "scatter_chips")
    mine = [add_chip_parts(a, b, r, where, f"add_chips_{k}") for k, a, b, r in zip(big, halves, from_core, from_chips)]
    shard_grads = share_core_halves(mine, "share_core_halves")

    grads, deltas, new_m, new_v = {}, {}, {}, {}
    for k, g2 in zip(big, shard_grads):
        shp = weights[k].shape
        g2 = g2.reshape(-1, shp[-1])
        dl, nm, nv = adamw(weights[k].reshape(g2.shape), g2, mom1[k].reshape(g2.shape), mom2[k].reshape(g2.shape), f"adamw_{k}")
        grads[k], deltas[k], new_m[k], new_v[k] = (t.reshape(shp) for t in (g2, dl, nm, nv))
    pack_state = lambda src: jnp.concatenate([_pack_rows([src[k] for k in SMALL], small_rows), src["ffn_b_dw"].reshape(bias_rows, LANES)], axis=0)
    small_out = (g_small,) + tuple(adamw(pack_state(weights), g_small, pack_state(mom1), pack_state(mom2), "adamw_small"))
    for dst, packed in zip((grads, deltas, new_m, new_v), small_out):
        for k, t in zip(SMALL, _unpack_rows(packed[:small_rows], small_shapes)):
            dst[k] = t
        dst["ffn_b_dw"] = packed[small_rows:].reshape(ffn_b_dw.shape)

    order = ("norm_g", "attn_w_qkv", "attn_w_o", "conv_w_pw1", "conv_b_pw1", "conv_w_dw", "conv_b_dw", "conv_ln_g", "conv_ln_b",
             "conv_w_pw2", "conv_b_pw2", "ffn_w_up", "ffn_w_dw", "ffn_b_dw", "ffn_w_down")
    return (loss, grad_x[None], *[grads[k] for k in order], *[deltas[k] for k in order], *[new_m[k] for k in order],
            *[new_v[k] for k in order])
```

```python
import functools
from typing import Callable, NamedTuple

import jax
import jax.numpy as jnp
from jax import lax
from jax.experimental import pallas as pl
from jax.experimental.pallas import tpu as pltpu

f32 = jnp.float32
bf16 = jnp.bfloat16
SDS = jax.ShapeDtypeStruct

EPS = 1e-6
HEAD_DIM = 64
N_SLOTS = 8
GROUP_WIDTH = N_SLOTS * HEAD_DIM
DILATED_GROUPS = ((128, 1), (512, 4), (2048, 16))
N_GROUPS = 3
SPAN = 128
ROT_DIM = HEAD_DIM // 4
ROPE_THETA = 500000.0
CONV_KERNEL = 31
CONV_HALO = 32
FFN_CONV = 3
FFN_HALO = 16
FFN_TAIL = 8
STRIP_ROWS = 64
ADAM_LR, ADAM_B1, ADAM_B2, ADAM_EPS, ADAM_WD, ADAM_STEP = 0.001, 0.9, 0.999, 1e-08, 0.01, 10
LANES = 128
N_CHIPS = 4
VMEM_LIMIT_BYTES = 56 * 1024 * 1024
NEG_BIG = -1e30
MESH = pl.DeviceIdType.MESH
ANY = pl.BlockSpec(memory_space=pl.ANY)


def _params(*sem):
    return pltpu.CompilerParams(dimension_semantics=sem, vmem_limit_bytes=VMEM_LIMIT_BYTES)


def _sigmoid(v):
    return 1.0 / (1.0 + jnp.exp(-v))


def _dot_nt(a, b):
    return lax.dot_general(a, b, (((1,), (1,)), ((), ())), preferred_element_type=f32)


def _dot_tn(a, b):
    return lax.dot_general(a, b, (((0,), (0,)), ((), ())), preferred_element_type=f32)


def mm_nn(x, w, li, bias, out_blocks, out_dtype, tm, tn, name):
    nq, m, kq = x.shape
    p, _, k, n = w.shape
    assert k == nq * kq and n % tn == 0 and m % tm == 0
    on = p * n // out_blocks
    assert on % tn == 0
    nj, onj = n // tn, on // tn

    def body(*refs):
        if bias is None:
            x_ref, w_ref, o_ref = refs
            b_ref = None
        else:
            x_ref, w_ref, b_ref, o_ref = refs
        acc = jnp.dot(x_ref[0], w_ref[0:kq, :], preferred_element_type=f32)
        for q in range(1, nq):
            acc = acc + jnp.dot(x_ref[q], w_ref[q * kq:(q + 1) * kq, :], preferred_element_type=f32)
        if b_ref is not None:
            acc = acc + b_ref[...]
        o_ref[...] = acc.astype(o_ref.dtype)

    in_specs = [pl.BlockSpec((nq, tm, kq), lambda j, i: (0, i, 0)),
                pl.BlockSpec((None, None, k, tn), lambda j, i: (j // nj, li, 0, j % nj))]
    args = [x, w]
    if bias is not None:
        in_specs.append(pl.BlockSpec((1, tn), lambda j, i: (0, j)))
        args.append(bias)
    return pl.pallas_call(
        body, grid=(p * nj, m // tm), in_specs=in_specs,
        out_specs=pl.BlockSpec((None, tm, tn), lambda j, i: (j // onj, i, j % onj)),
        out_shape=SDS((out_blocks, m, on), out_dtype),
        compiler_params=_params("parallel", "parallel"), name=name)(*args)


def mm_nt(dy, w, li, out_blocks, out_dtype, tm, tk, tn, name):
    ob, m, on = dy.shape
    p, _, k, n = w.shape
    assert ob * on == p * n and n % tn == 0 and on % tn == 0 and k % tk == 0 and m % tm == 0
    kq = k // out_blocks
    assert kq % tk == 0
    nj, onj, kqj = n // tn, on // tn, kq // tk
    nr = p * nj

    def body(dy_ref, w_ref, o_ref, *scr):
        part = _dot_nt(dy_ref[...], w_ref[...])
        if nr == 1:
            o_ref[...] = part.astype(o_ref.dtype)
        else:
            acc_ref, = scr
            r = pl.program_id(2)

            @pl.when(r == 0)
            def _():
                acc_ref[...] = part

            @pl.when(r > 0)
            def _():
                acc_ref[...] += part

            @pl.when(r == nr - 1)
            def _():
                o_ref[...] = acc_ref[...].astype(o_ref.dtype)

    return pl.pallas_call(
        body, grid=(k // tk, m // tm, nr),
        in_specs=[pl.BlockSpec((None, tm, tn), lambda kt, i, r: (r // onj, i, r % onj)),
                  pl.BlockSpec((None, None, tk, tn), lambda kt, i, r: (r // nj, li, kt, r % nj))],
        out_specs=pl.BlockSpec((None, tm, tk), lambda kt, i, r: (kt // kqj, i, kt % kqj)),
        out_shape=SDS((out_blocks, m, kq), out_dtype),
        scratch_shapes=[] if nr == 1 else [pltpu.VMEM((tm, tk), f32)],
        compiler_params=_params("parallel", "parallel", "arbitrary"), name=name)(dy, w)


def mm_tn(x, dy, p, tm, tk, tn, name):
    nq, m, kq = x.shape
    ob, _, on = dy.shape
    k = nq * kq
    n = ob * on // p
    assert n % tn == 0 and on % tn == 0 and kq % tk == 0 and m % tm == 0
    nj, onj, kqj = n // tn, on // tn, kq // tk

    def body(x_ref, dy_ref, o_ref):
        part = _dot_tn(x_ref[...], dy_ref[...])
        i = pl.program_id(2)

        @pl.when(i == 0)
        def _():
            o_ref[...] = part

        @pl.when(i > 0)
        def _():
            o_ref[...] += part

    return pl.pallas_call(
        body, grid=(k // tk, p * nj, m // tm),
        in_specs=[pl.BlockSpec((None, tm, tk), lambda kt, j, i: (kt // kqj, i, kt % kqj)),
                  pl.BlockSpec((None, tm, tn), lambda kt, j, i: (j // onj, i, j % onj))],
        out_specs=pl.BlockSpec((None, tk, tn), lambda kt, j, i: (j // nj, kt, j % nj)),
        out_shape=SDS((p, k, n), f32),
        compiler_params=_params("parallel", "parallel", "arbitrary"), name=name)(x, dy)


def _row_tile(s):
    return min(s, 512)


def _rows(tr, d):
    return pl.BlockSpec((tr, d), lambda i: (i, 0))


def _fixed(r, d):
    return pl.BlockSpec((r, d), lambda i: (0, 0))


def _rms(xv):
    return lax.rsqrt(jnp.mean(xv * xv, axis=-1, keepdims=True) + EPS)


def prenorm(x, g, name):
    s, d = x.shape
    tr = _row_tile(s)

    def body(x_ref, g_ref, o_ref):
        xv = x_ref[...]
        o_ref[...] = (xv * _rms(xv) * g_ref[...]).astype(o_ref.dtype)

    return pl.pallas_call(body, grid=(s // tr,), in_specs=[_rows(tr, d), _fixed(1, d)], out_specs=_rows(tr, d),
                          out_shape=SDS((s, d), bf16), compiler_params=_params("parallel"), name=name)(x, g)


def postnorm_residual(x, y, g, name):
    s, d = x.shape
    tr = _row_tile(s)

    def body(x_ref, y_ref, g_ref, o_ref):
        yv = y_ref[...]
        o_ref[...] = x_ref[...] + yv * _rms(yv) * g_ref[...]

    return pl.pallas_call(body, grid=(s // tr,), in_specs=[_rows(tr, d), _rows(tr, d), _fixed(1, d)],
                          out_specs=_rows(tr, d), out_shape=SDS((s, d), f32), compiler_params=_params("parallel"),
                          name=name)(x, y, g)


def norm_bwd(xin, g, dout, res, out_dtype, name):
    s, d = xin.shape
    tr = _row_tile(s)

    def body(*refs):
        if res is None:
            x_ref, g_ref, do_ref, dx_ref, dg_ref, cs_ref = refs
            r_ref = None
        else:
            x_ref, g_ref, do_ref, r_ref, dx_ref, dg_ref, cs_ref = refs
        xv = x_ref[...]
        r = _rms(xv)
        xh = xv * r
        dov = do_ref[...].astype(f32)
        gy = dov * g_ref[...]
        dx = r * (gy - xh * jnp.mean(gy * xh, axis=-1, keepdims=True))
        if r_ref is not None:
            dx = dx + r_ref[...]
        dx_ref[...] = dx.astype(dx_ref.dtype)
        dg = jnp.sum(dov * xh, axis=0, keepdims=True)
        cs = jnp.sum(dx, axis=0, keepdims=True)
        i = pl.program_id(0)

        @pl.when(i == 0)
        def _():
            dg_ref[...] = dg
            cs_ref[...] = cs

        @pl.when(i > 0)
        def _():
            dg_ref[...] += dg
            cs_ref[...] += cs

    in_specs = [_rows(tr, d), _fixed(1, d), _rows(tr, d)]
    args = [xin, g, dout]
    if res is not None:
        in_specs.append(_rows(tr, d))
        args.append(res)
    return pl.pallas_call(body, grid=(s // tr,), in_specs=in_specs,
                          out_specs=[_rows(tr, d), _fixed(1, d), _fixed(1, d)],
                          out_shape=[SDS((s, d), out_dtype), SDS((1, d), f32), SDS((1, d), f32)],
                          compiler_params=_params("arbitrary"), name=name)(*args)


def final_loss(x, y, g, target, name):
    s, d = x.shape
    tr = _row_tile(s)
    nt = s // tr

    def body(x_ref, y_ref, g_ref, t_ref, dx_ref, loss_ref, acc_ref):
        yv = y_ref[...]
        diff = x_ref[...] + yv * _rms(yv) * g_ref[...] - t_ref[...]
        dx_ref[...] = diff * (1.0 / d)
        sq = jnp.sum(diff * diff, axis=0, keepdims=True)
        i = pl.program_id(0)

        @pl.when(i == 0)
        def _():
            acc_ref[...] = sq

        @pl.when(i > 0)
        def _():
            acc_ref[...] += sq

        @pl.when(i == nt - 1)
        def _():
            total = jnp.sum(acc_ref[...], axis=1, keepdims=True) * (0.5 / d)
            loss_ref[...] = jnp.broadcast_to(total, (1, LANES))

    return pl.pallas_call(body, grid=(nt,), in_specs=[_rows(tr, d), _rows(tr, d), _fixed(1, d), _rows(tr, d)],
                          out_specs=[_rows(tr, d), _fixed(1, LANES)],
                          out_shape=[SDS((s, d), f32), SDS((1, LANES), f32)],
                          scratch_shapes=[pltpu.VMEM((1, d), f32)],
                          compiler_params=_params("arbitrary"), name=name)(x, y, g, target)


def _rotate_chunk(v, tc, ta, tb):
    return v * tc + pltpu.roll(v, LANES - ROT_DIM // 2, axis=1) * ta + pltpu.roll(v, ROT_DIM // 2, axis=1) * tb


def qkv_rope(h, w, tabs, name, rider=None):
    s, d = h.shape
    tm = min(s, 512)
    n_kinds = 3
    grid = (n_kinds * N_GROUPS, s // tm)

    def body(*refs):
        first = jnp.logical_and(pl.program_id(0) == 0, pl.program_id(1) == 0)
        last = jnp.logical_and(pl.program_id(0) == grid[0] - 1, pl.program_id(1) == grid[1] - 1)
        (x_ref, w_ref, tc_ref, ta_ref, tb_ref, o_ref), ride_start, ride_finish = _rider_run(rider, refs, 5, 1, first, last)
        ride_start()
        kind = pl.program_id(0) // N_GROUPS

        @pl.when(kind < 2)
        def _():
            tc, ta, tb = tc_ref[...], ta_ref[...], tb_ref[...]
            half = GROUP_WIDTH // 2
            for part in range(2):
                acc = jnp.dot(x_ref[...], w_ref[:, part * half:(part + 1) * half], preferred_element_type=f32)
                for ch in range(half // LANES):
                    cols = slice(ch * LANES, (ch + 1) * LANES)
                    dst = slice(part * half + ch * LANES, part * half + (ch + 1) * LANES)
                    o_ref[:, dst] = _rotate_chunk(acc[:, cols], tc, ta, tb).astype(o_ref.dtype)

        @pl.when(kind == 2)
        def _():
            o_ref[...] = jnp.dot(x_ref[...], w_ref[...], preferred_element_type=f32).astype(o_ref.dtype)

        ride_finish()

    tab = pl.BlockSpec((tm, LANES), lambda j, i: (i, 0))
    r_in, r_out, r_shapes, r_scratch, r_alias, r_args = _rider_specs(rider, 5, 1)
    outs = pl.pallas_call(
        body, grid=grid,
        in_specs=[pl.BlockSpec((tm, d), lambda j, i: (i, 0)), pl.BlockSpec((d, GROUP_WIDTH), lambda j, i: (0, j)), tab, tab, tab] + r_in,
        out_specs=[pl.BlockSpec((None, tm, GROUP_WIDTH), lambda j, i: (j % N_GROUPS, i, j // N_GROUPS))] + r_out,
        out_shape=[SDS((N_GROUPS, s, n_kinds * GROUP_WIDTH), bf16)] + r_shapes,
        scratch_shapes=r_scratch, input_output_aliases=r_alias,
        compiler_params=_params("arbitrary", "arbitrary"), name=name)(h, w, *tabs, *r_args)
    return outs[0], outs[1:]


def _attn_mask(j):
    row = lax.broadcasted_iota(jnp.int32, (SPAN, 2 * SPAN), 0)
    col = lax.broadcasted_iota(jnp.int32, (SPAN, 2 * SPAN), 1)
    prev = jnp.logical_and(jnp.logical_and(col < SPAN, col >= row), j > 0)
    return jnp.logical_or(prev, jnp.logical_and(col >= SPAN, col - SPAN <= row))


def _attn_in_specs(gi):
    def at(kind, prev):
        def index(r, j):
            return (gi, jnp.maximum(j - 1, 0) if prev else j, r * 3 + kind)
        return pl.BlockSpec((None, SPAN, GROUP_WIDTH), index)
    return [at(0, False), at(1, False), at(1, True), at(2, False), at(2, True)]


def _phases(qkvg, g, dil):
    if dil == 1:
        return qkvg, g
    _, s, w = qkvg.shape
    return qkvg[g].reshape(1, s // dil, dil * w), 0


def attn_fwd(qkvg, g, dil, name):
    a, gi = _phases(qkvg, g, dil)
    s = qkvg.shape[1]
    l = s // dil
    nb = l // SPAN

    def body(q_ref, ko_ref, kp_ref, vo_ref, vp_ref, o_ref, lse_ref, k_scr, v_scr):
        mask = _attn_mask(pl.program_id(1))
        k_scr[0:SPAN, :] = kp_ref[...]
        k_scr[SPAN:2 * SPAN, :] = ko_ref[...]
        v_scr[0:SPAN, :] = vp_ref[...]
        v_scr[SPAN:2 * SPAN, :] = vo_ref[...]
        for h in range(N_SLOTS):
            hs = slice(h * HEAD_DIM, (h + 1) * HEAD_DIM)
            sc = jnp.where(mask, _dot_nt(q_ref[:, hs], k_scr[:, hs]) * (HEAD_DIM ** -0.5), NEG_BIG)
            mx = jnp.max(sc, axis=-1, keepdims=True)
            p = jnp.exp(sc - mx)
            den = jnp.sum(p, axis=-1, keepdims=True)
            o_ref[:, hs] = jnp.dot(p.astype(bf16), v_scr[:, hs], preferred_element_type=f32) / den
            lse_ref[:, hs] = jnp.broadcast_to(mx + jnp.log(den), (SPAN, HEAD_DIM))

    out = pl.BlockSpec((SPAN, GROUP_WIDTH), lambda r, j: (j, r))
    o, lse = pl.pallas_call(
        body, grid=(dil, nb), in_specs=_attn_in_specs(gi), out_specs=[out, out],
        out_shape=[SDS((l, dil * GROUP_WIDTH), f32)] * 2,
        scratch_shapes=[pltpu.VMEM((2 * SPAN, GROUP_WIDTH), bf16)] * 2,
        compiler_params=_params("parallel", "parallel"), name=name)(a, a, a, a, a)
    return o.reshape(s, GROUP_WIDTH), lse.reshape(s, GROUP_WIDTH), lse


def _group_weights(lses):
    mx = jnp.maximum(jnp.maximum(lses[0], lses[1]), lses[2])
    es = [jnp.exp(v - mx) for v in lses]
    inv = 1.0 / (es[0] + es[1] + es[2])
    return [e * inv for e in es]


def mix_fwd(os_, lses, name):
    s, w = os_[0].shape
    tr = _row_tile(s)

    def body(o0, o1, o2, l0, l1, l2, out_ref):
        wg = _group_weights([l0[...], l1[...], l2[...]])
        out_ref[...] = (wg[0] * o0[...] + wg[1] * o1[...] + wg[2] * o2[...]).astype(out_ref.dtype)

    return pl.pallas_call(body, grid=(s // tr,), in_specs=[_rows(tr, w)] * 6, out_specs=_rows(tr, w),
                          out_shape=SDS((s, w), bf16), compiler_params=_params("parallel"), name=name)(*os_, *lses)


def mix_bwd(dmixed, os_, lses, head_ones, name):
    s, w = dmixed.shape
    tr = _row_tile(s)

    def head_sum(t, ones):
        hi = t.astype(bf16)
        lo = (t - hi.astype(f32)).astype(bf16)
        return jnp.dot(hi, ones, preferred_element_type=f32) + jnp.dot(lo, ones, preferred_element_type=f32)

    def body(dm_ref, o0, o1, o2, l0, l1, l2, ones_ref, d0, d1, d2, p0, p1, p2):
        dm = dm_ref[...]
        ones = ones_ref[...]
        wg = _group_weights([l0[...], l1[...], l2[...]])
        mean = sum(wg[k] * head_sum(dm * o[...], ones) for k, o in enumerate((o0, o1, o2)))
        for k, (d_ref, p_ref) in enumerate(((d0, p0), (d1, p1), (d2, p2))):
            d_ref[...] = (wg[k] * dm).astype(d_ref.dtype)
            p_ref[...] = wg[k] * mean

    return pl.pallas_call(body, grid=(s // tr,), in_specs=[_rows(tr, w)] * 7 + [_fixed(w, w)],
                          out_specs=[_rows(tr, w)] * 6,
                          out_shape=[SDS((s, w), bf16)] * 3 + [SDS((s, w), f32)] * 3,
                          compiler_params=_params("parallel"), name=name)(dmixed, *os_, *lses, head_ones)


def attn_bwd(qkvg, do, lse_phased, dterm, g, dil, name):
    a, gi = _phases(qkvg, g, dil)
    s = qkvg.shape[1]
    l = s // dil
    nb = l // SPAN
    phased = lambda t: t.reshape(l, dil * GROUP_WIDTH)

    def body(q_ref, ko_ref, kp_ref, vo_ref, vp_ref, do_ref, lse_ref, dt_ref, dq_ref, dko_ref, dkp_ref, dvo_ref, dvp_ref,
             k_scr, v_scr):
        mask = _attn_mask(pl.program_id(1))
        scale = HEAD_DIM ** -0.5
        k_scr[0:SPAN, :] = kp_ref[...]
        k_scr[SPAN:2 * SPAN, :] = ko_ref[...]
        v_scr[0:SPAN, :] = vp_ref[...]
        v_scr[SPAN:2 * SPAN, :] = vo_ref[...]
        for h in range(N_SLOTS):
            hs = slice(h * HEAD_DIM, (h + 1) * HEAD_DIM)
            one = slice(h * HEAD_DIM, h * HEAD_DIM + 1)
            q, kk, dov = q_ref[:, hs], k_scr[:, hs], do_ref[:, hs]
            p = jnp.exp(jnp.where(mask, _dot_nt(q, kk) * scale - lse_ref[:, one], NEG_BIG))
            ds = (p * (_dot_nt(dov, v_scr[:, hs]) - dt_ref[:, one]) * scale).astype(bf16)
            dq_ref[:, hs] = jnp.dot(ds, kk, preferred_element_type=f32).astype(dq_ref.dtype)
            dk = _dot_tn(ds, q).astype(dko_ref.dtype)
            dv = _dot_tn(p.astype(bf16), dov).astype(dvo_ref.dtype)
            dkp_ref[:, hs] = dk[:SPAN]
            dko_ref[:, hs] = dk[SPAN:]
            dvp_ref[:, hs] = dv[:SPAN]
            dvo_ref[:, hs] = dv[SPAN:]

    blk = pl.BlockSpec((SPAN, GROUP_WIDTH), lambda r, j: (j, r))
    outs = pl.pallas_call(
        body, grid=(dil, nb), in_specs=_attn_in_specs(gi) + [blk, blk, blk], out_specs=[blk] * 5,
        out_shape=[SDS((l, dil * GROUP_WIDTH), bf16)] * 5,
        scratch_shapes=[pltpu.VMEM((2 * SPAN, GROUP_WIDTH), bf16)] * 2,
        compiler_params=_params("parallel", "parallel"), name=name)(a, a, a, a, a, phased(do), lse_phased, phased(dterm))
    return [t.reshape(s, GROUP_WIDTH) for t in outs]


def dqkv_assemble(parts, tabs, name):
    s = parts[0][0].shape[0]
    nblk = s // SPAN
    width = 3 * N_GROUPS * GROUP_WIDTH

    def body(*refs):
        ins, (tc_ref, ta_ref, tb_ref, o_ref) = refs[:5 * N_GROUPS], refs[5 * N_GROUPS:]
        tc, ta, tb = tc_ref[...], ta_ref[...], tb_ref[...]
        i = pl.program_id(0)
        for g, (_, dil) in enumerate(DILATED_GROUPS):
            dq, dko, dkp, dvo, dvp = ins[5 * g:5 * g + 5]
            has_next = i + dil < nblk
            for ch in range(GROUP_WIDTH // LANES):
                cols = slice(ch * LANES, (ch + 1) * LANES)
                base = g * GROUP_WIDTH + ch * LANES
                dk = dko[:, cols].astype(f32) + jnp.where(has_next, dkp[:, cols].astype(f32), 0.0)
                dv = dvo[:, cols].astype(f32) + jnp.where(has_next, dvp[:, cols].astype(f32), 0.0)
                o_ref[:, base:base + LANES] = _rotate_chunk(dq[:, cols].astype(f32), tc, ta, tb).astype(o_ref.dtype)
                kb = N_GROUPS * GROUP_WIDTH + base
                o_ref[:, kb:kb + LANES] = _rotate_chunk(dk, tc, ta, tb).astype(o_ref.dtype)
                vb = 2 * N_GROUPS * GROUP_WIDTH + base
                o_ref[:, vb:vb + LANES] = dv.astype(o_ref.dtype)

    here = _rows(SPAN, GROUP_WIDTH)
    in_specs, args = [], []
    for g, (_, dil) in enumerate(DILATED_GROUPS):
        ahead = pl.BlockSpec((SPAN, GROUP_WIDTH), functools.partial(lambda i, dil: (jnp.minimum(i + dil, nblk - 1), 0), dil=dil))
        in_specs += [here, here, ahead, here, ahead]
        args += list(parts[g])
    tab = _rows(SPAN, LANES)
    return pl.pallas_call(body, grid=(nblk,), in_specs=in_specs + [tab] * 3, out_specs=_rows(SPAN, width),
                          out_shape=SDS((s, width), bf16), compiler_params=_params("parallel"), name=name)(*args, *tabs)


def ffn_act_fwd(u, wb, name):
    _, nbk, s, c = u.shape
    tr = min(s, 256)

    def body(u_ref, h_ref, wb_ref, a_ref, ug_ref, su, sg):
        first = pl.program_id(1) == 0
        for lc in range(c // LANES):
            ln = slice(lc * LANES, (lc + 1) * LANES)
            for half, scr in enumerate((su, sg)):
                scr[0:FFN_HALO, ln] = jnp.where(first, 0.0, h_ref[half, :, ln].astype(f32))
            for r0 in range(0, tr, STRIP_ROWS):
                rows = slice(r0, r0 + STRIP_ROWS)
                conv = []
                for half, scr in enumerate((su, sg)):
                    xv = u_ref[half, rows, ln].astype(f32)
                    scr[FFN_HALO + r0:FFN_HALO + r0 + STRIP_ROWS, ln] = xv
                    acc = wb_ref[half, FFN_CONV:FFN_CONV + 1, ln] + wb_ref[half, FFN_CONV - 1:FFN_CONV, ln] * xv
                    for k in range(FFN_CONV - 1):
                        acc = acc + wb_ref[half, k:k + 1, ln] * scr[pl.ds(FFN_HALO + r0 - (FFN_CONV - 1) + k, STRIP_ROWS), ln]
                    ug_ref[half, rows, ln] = acc.astype(ug_ref.dtype)
                    conv.append(acc)
                up, gate = conv
                a_ref[rows, ln] = (gate * _sigmoid(gate) * up).astype(a_ref.dtype)

    both = pl.BlockSpec((2, None, tr, c), lambda p, i: (0, p, i, 0))
    return pl.pallas_call(
        body, grid=(nbk, s // tr),
        in_specs=[both,
                  pl.BlockSpec((2, None, FFN_HALO, c), lambda p, i: (0, p, jnp.maximum(i * (tr // FFN_HALO) - 1, 0), 0)),
                  pl.BlockSpec((None, 2, 8, c), lambda p, i: (p, 0, 0, 0))],
        out_specs=[pl.BlockSpec((None, tr, c), lambda p, i: (p, i, 0)), both],
        out_shape=[SDS((nbk, s, c), bf16), SDS(u.shape, bf16)],
        scratch_shapes=[pltpu.VMEM((tr + FFN_HALO, c), f32)] * 2,
        compiler_params=_params("parallel", "arbitrary"), name=name)(u, u, wb)


def ffn_act_bwd(da, ug, u, wb, name, rider=None):
    _, nbk, s, c = u.shape
    tr = min(s, 256)
    nt = s // tr

    def body(*refs):
        step = pl.program_id(1)
        first = jnp.logical_and(pl.program_id(0) == 0, step == 0)
        last = jnp.logical_and(pl.program_id(0) == nbk - 1, step == nt - 1)
        (da_ref, ug_ref, u_ref, wb_ref, du_ref, dwb_ref, eu, eg), ride_start, ride_finish = _rider_run(rider, refs, 4, 2, first, last)
        ride_start()

        @pl.when(step == 0)
        def _():
            eu[tr:tr + FFN_TAIL, :] = jnp.zeros((FFN_TAIL, c), f32)
            eg[tr:tr + FFN_TAIL, :] = jnp.zeros((FFN_TAIL, c), f32)
            dwb_ref[...] = jnp.zeros(dwb_ref.shape, f32)

        fold = lambda t: jnp.sum(t.reshape(STRIP_ROWS // 8, 8, LANES), axis=0)
        for lc in range(c // LANES):
            ln = slice(lc * LANES, (lc + 1) * LANES)
            sums = [[jnp.zeros((8, LANES), f32) for _ in range(FFN_CONV + 1)] for _ in range(2)]
            for r0 in reversed(range(0, tr, STRIP_ROWS)):
                rows = slice(r0, r0 + STRIP_ROWS)
                up, gate = ug_ref[0, rows, ln].astype(f32), ug_ref[1, rows, ln].astype(f32)
                sig = _sigmoid(gate)
                dav = da_ref[rows, ln].astype(f32)
                grads = (dav * (gate * sig), dav * up * (sig * (1.0 + gate * (1.0 - sig))))
                for half, ext in enumerate((eu, eg)):
                    dv = grads[half]
                    ext[rows, ln] = dv
                    xv = u_ref[half, rows, ln].astype(f32)
                    acc = None
                    for k in range(FFN_CONV):
                        ahead = dv if k == FFN_CONV - 1 else ext[pl.ds(r0 + FFN_CONV - 1 - k, STRIP_ROWS), ln]
                        term = wb_ref[half, k:k + 1, ln] * ahead
                        acc = term if acc is None else acc + term
                        sums[half][k] = sums[half][k] + fold(xv * ahead)
                    sums[half][FFN_CONV] = sums[half][FFN_CONV] + fold(dv)
                    du_ref[half, rows, ln] = acc.astype(du_ref.dtype)
            for half, ext in enumerate((eu, eg)):
                ext[tr:tr + FFN_TAIL, ln] = ext[0:FFN_TAIL, ln]
                for k in range(FFN_CONV + 1):
                    dwb_ref[half, k:k + 1, ln] += jnp.sum(sums[half][k], axis=0, keepdims=True)
        ride_finish()

    rev = lambda i: nt - 1 - i
    both = pl.BlockSpec((2, None, tr, c), lambda p, i: (0, p, rev(i), 0))
    r_in, r_out, r_shapes, r_scratch, r_alias, r_args = _rider_specs(rider, 4, 2)
    outs = pl.pallas_call(
        body, grid=(nbk, nt),
        in_specs=[pl.BlockSpec((None, tr, c), lambda p, i: (p, rev(i), 0)), both, both,
                  pl.BlockSpec((None, 2, 8, c), lambda p, i: (p, 0, 0, 0))] + r_in,
        out_specs=[both, pl.BlockSpec((None, 2, 8, c), lambda p, i: (p, 0, 0, 0))] + r_out,
        out_shape=[SDS((2, nbk, s, c), bf16), SDS((nbk, 2, 8, c), f32)] + r_shapes,
        scratch_shapes=[pltpu.VMEM((tr + FFN_TAIL, c), f32)] * 2 + r_scratch, input_output_aliases=r_alias,
        compiler_params=_params("arbitrary", "arbitrary"), name=name)(da, ug, u, wb, *r_args)
    return outs[0], outs[1], outs[2:]


def _glu(zv, c):
    return zv[:, :c] * _sigmoid(zv[:, c:])


def _conv_fill(z_ref, h_ref, scr, first, tr, c):
    scr[0:CONV_HALO, :] = jnp.where(first, 0.0, _glu(h_ref[...], c))
    scr[CONV_HALO:CONV_HALO + tr, :] = _glu(z_ref[...], c)


def _conv_taps(b):
    return [(a, CONV_KERNEL - 1 - 8 * a - b) for a in range(CONV_HALO // 8) if CONV_KERNEL - 1 - 8 * a - b >= 0]


def _layernorm_parts(cv):
    mu = jnp.mean(cv, axis=-1, keepdims=True)
    cen = cv - mu
    rstd = lax.rsqrt(jnp.mean(cen * cen, axis=-1, keepdims=True) + EPS)
    return cen * rstd, rstd


def conv_module_fwd(z, wdw, vecs, name):
    s, c2 = z.shape
    c = c2 // 2
    tr = min(s, 256)

    def body(z_ref, h_ref, w_ref, v_ref, c_ref, s_ref, scr, zb):
        _conv_fill(z_ref, h_ref, scr, pl.program_id(0) == 0, tr, c)
        acc = jnp.broadcast_to(v_ref[0:1, :], (tr, c))
        for b in range(8):
            part = None
            for a, j in _conv_taps(b):
                term = w_ref[j:j + 1, :] * scr[pl.ds(CONV_HALO - 8 - 8 * a, tr + 8), :]
                part = term if part is None else part + term
            if b == 0:
                acc = acc + part[8:]
            else:
                zb[...] = part
                acc = acc + zb[pl.ds(8 - b, tr), :]
        c_ref[...] = acc
        chat, _ = _layernorm_parts(acc)
        ln = chat * v_ref[1:2, :] + v_ref[2:3, :]
        s_ref[...] = (ln * _sigmoid(ln)).astype(s_ref.dtype)

    return pl.pallas_call(
        body, grid=(s // tr,),
        in_specs=[_rows(tr, c2), pl.BlockSpec((CONV_HALO, c2), lambda i: (jnp.maximum(i * (tr // CONV_HALO) - 1, 0), 0)),
                  _fixed(CONV_HALO, c), _fixed(8, c)],
        out_specs=[_rows(tr, c), _rows(tr, c)], out_shape=[SDS((s, c), f32), SDS((s, c), bf16)],
        scratch_shapes=[pltpu.VMEM((tr + CONV_HALO, c), f32), pltpu.VMEM((tr + 8, c), f32)],
        compiler_params=_params("arbitrary"), name=name)(z, z, wdw, vecs)


def conv_module_bwd(ds, cpre, z, wdw, vecs, name):
    s, c2 = z.shape
    c = c2 // 2
    tr = min(s, 256)
    nt = s // tr

    def body(ds_ref, c_ref, z_ref, w_ref, v_ref, dz_ref, dw_ref, dv_ref, db_ref, ext, dwp):
        step = pl.program_id(0)

        @pl.when(step == 0)
        def _():
            ext[tr:tr + CONV_HALO, :] = jnp.zeros((CONV_HALO, c), f32)
            dwp[...] = jnp.zeros(dwp.shape, f32)
            dv_ref[...] = jnp.zeros(dv_ref.shape, f32)
            db_ref[...] = jnp.zeros(db_ref.shape, f32)

        gain, bias = v_ref[1:2, :], v_ref[2:3, :]
        fold16 = lambda t: t[:8] + t[8:]
        sums = [jnp.zeros((8, c), f32) for _ in range(3)]
        for r0 in range(0, tr, 16):
            rows = slice(r0, r0 + 16)
            chat, rstd = _layernorm_parts(c_ref[rows, :])
            ln = chat * gain + bias
            sig = _sigmoid(ln)
            dln = ds_ref[rows, :].astype(f32) * (sig * (1.0 + ln * (1.0 - sig)))
            gy = dln * gain
            dc = rstd * (gy - jnp.mean(gy, axis=-1, keepdims=True) - chat * jnp.mean(gy * chat, axis=-1, keepdims=True))
            ext[rows, :] = dc
            for k, t in enumerate((dc, dln * chat, dln)):
                sums[k] = sums[k] + fold16(t)
        for k in range(3):
            dv_ref[k:k + 1, :] += jnp.sum(sums[k], axis=0, keepdims=True)

        fold = lambda t: jnp.sum(t.reshape(STRIP_ROWS // 8, 8, LANES), axis=0)
        for lc in range(c // LANES):
            ln_a = slice(lc * LANES, (lc + 1) * LANES)
            ln_g = slice(c + lc * LANES, c + (lc + 1) * LANES)
            dbs = [jnp.zeros((8, LANES), f32) for _ in range(2)]
            for r0 in range(0, tr, STRIP_ROWS):
                rows = slice(r0, r0 + STRIP_ROWS)
                a, sg = z_ref[rows, ln_a], _sigmoid(z_ref[rows, ln_g])
                uv = a * sg
                du = jnp.zeros((STRIP_ROWS, LANES), f32)
                for b in range(8):
                    src = ext[pl.ds(r0 + b, STRIP_ROWS + CONV_HALO - 8), ln_a]
                    for a8, j in _conv_taps(b):
                        ahead = src[8 * a8:8 * a8 + STRIP_ROWS]
                        du = du + w_ref[j:j + 1, ln_a] * ahead
                        dwp[8 * j:8 * j + 8, ln_a] += fold(uv * ahead)
                da = du * sg
                dg = du * a * (sg * (1.0 - sg))
                dz_ref[rows, ln_a] = da.astype(dz_ref.dtype)
                dz_ref[rows, ln_g] = dg.astype(dz_ref.dtype)
                dbs = [dbs[0] + fold(da), dbs[1] + fold(dg)]
            db_ref[:, ln_a] += jnp.sum(dbs[0], axis=0, keepdims=True)
            db_ref[:, ln_g] += jnp.sum(dbs[1], axis=0, keepdims=True)
        ext[tr:tr + CONV_HALO, :] = ext[0:CONV_HALO, :]

        @pl.when(step == nt - 1)
        def _():
            for j in range(CONV_HALO):
                dw_ref[j:j + 1, :] = jnp.sum(dwp[8 * j:8 * j + 8, :], axis=0, keepdims=True)

    rev = lambda i: nt - 1 - i
    back = lambda d: pl.BlockSpec((tr, d), lambda i: (rev(i), 0))
    return pl.pallas_call(
        body, grid=(nt,),
        in_specs=[back(c), back(c), back(c2), _fixed(CONV_HALO, c), _fixed(8, c)],
        out_specs=[back(c2), _fixed(CONV_HALO, c), _fixed(8, c), _fixed(1, c2)],
        out_shape=[SDS((s, c2), bf16), SDS((CONV_HALO, c), f32), SDS((8, c), f32), SDS((1, c2), f32)],
        scratch_shapes=[pltpu.VMEM((tr + CONV_HALO, c), f32), pltpu.VMEM((8 * CONV_HALO, c), f32)],
        compiler_params=_params("arbitrary"), name=name)(ds, cpre, z, wdw, vecs)


def _tile2d(r, n):
    tn = n if n <= 2048 else 1024
    tr = r
    while tr * tn * 4 > (1 << 21) and tr % 16 == 0:
        tr //= 2
    assert r % tr == 0 and n % tn == 0
    return tr, tn


def adamw(w, g, m, v, name):
    r, n = w.shape
    tr, tn = _tile2d(r, n)

    def body(w_ref, g_ref, m_ref, v_ref, d_ref, nm_ref, nv_ref):
        gv = g_ref[...]
        nm = ADAM_B1 * m_ref[...] + (1.0 - ADAM_B1) * gv
        nv = ADAM_B2 * v_ref[...] + (1.0 - ADAM_B2) * (gv * gv)
        m_hat = nm / (1.0 - ADAM_B1 ** ADAM_STEP)
        v_hat = nv / (1.0 - ADAM_B2 ** ADAM_STEP)
        d_ref[...] = -ADAM_LR * (m_hat / (jnp.sqrt(v_hat) + ADAM_EPS) + ADAM_WD * w_ref[...])
        nm_ref[...] = nm
        nv_ref[...] = nv

    blk = pl.BlockSpec((tr, tn), lambda i, j: (i, j))
    return pl.pallas_call(body, grid=(r // tr, n // tn), in_specs=[blk] * 4, out_specs=[blk] * 3,
                          out_shape=[SDS((r, n), f32)] * 3, compiler_params=_params("parallel", "parallel"),
                          name=name)(w, g, m, v)


def add_core_halves(grad, got, where, name):
    _, _, rh, n = grad.shape
    tr, tn = _tile2d(rh, n)

    def body(w_ref, a_ref, b_ref, o_ref):
        o_ref[...] = (a_ref[...] + b_ref[...]).astype(o_ref.dtype)

    return pl.pallas_call(
        body,
        grid_spec=pltpu.PrefetchScalarGridSpec(
            num_scalar_prefetch=1, grid=(N_CHIPS, rh // tr, n // tn),
            in_specs=[pl.BlockSpec((None, None, tr, tn), lambda p, i, j, w_ref: (p, w_ref[0], i, j)),
                      pl.BlockSpec((None, tr, tn), lambda p, i, j, w_ref: (p, i, j))],
            out_specs=pl.BlockSpec((None, tr, tn), lambda p, i, j, w_ref: (p, i, j))),
        out_shape=SDS((N_CHIPS, rh, n), bf16), compiler_params=_params("parallel", "parallel", "parallel"),
        name=name)(where, grad, got)


def add_chip_parts(grad, got_core, got_chips, where, name):
    _, _, rh, n = grad.shape
    tr, tn = _tile2d(rh, n)

    def body(w_ref, a_ref, b_ref, g_ref, o_ref):
        acc = a_ref[...] + b_ref[...]
        for k in range(N_CHIPS - 1):
            acc = acc + g_ref[k].astype(f32)
        o_ref[...] = acc

    return pl.pallas_call(
        body,
        grid_spec=pltpu.PrefetchScalarGridSpec(
            num_scalar_prefetch=1, grid=(rh // tr, n // tn),
            in_specs=[pl.BlockSpec((None, None, tr, tn), lambda i, j, w_ref: (w_ref[1], w_ref[0], i, j)),
                      pl.BlockSpec((None, tr, tn), lambda i, j, w_ref: (w_ref[1], i, j)),
                      pl.BlockSpec((N_CHIPS - 1, tr, tn), lambda i, j, w_ref: (0, i, j))],
            out_specs=pl.BlockSpec((None, tr, tn), lambda i, j, w_ref: (w_ref[0], i, j))),
        out_shape=SDS((2, rh, n), f32), compiler_params=_params("parallel", "parallel"),
        name=name)(where, grad, got_core, got_chips)


def sum_devices(parts, name):
    nd, r, n = parts.shape

    def body(p_ref, o_ref):
        acc = p_ref[0]
        for k in range(1, nd):
            acc = acc + p_ref[k]
        o_ref[...] = acc

    return pl.pallas_call(body, out_shape=SDS((r, n), f32), name=name)(parts)


def _position():
    return lax.axis_index("x"), lax.axis_index("y"), lax.axis_index("c")


def _other_chips(x, y):
    return [(1 - x, y), (x, 1 - y), (1 - x, 1 - y)]


def _remote(src, dst, send, recv, to):
    return pltpu.make_async_remote_copy(src_ref=src, dst_ref=dst, send_sem=send, recv_sem=recv, device_id=to,
                                        device_id_type=MESH)


def gather_chips(bufs, name):
    n = len(bufs)
    nk = N_CHIPS - 1

    def body(*refs):
        bufs_ = refs[n:2 * n]
        send, recv = refs[2 * n:]
        x, y, c = _position()
        me = 2 * x + y
        chips = _other_chips(x, y)
        sends = []
        for t in range(n):
            for k, (px, py) in enumerate(chips):
                out = _remote(bufs_[t].at[me, c], bufs_[t].at[me, c], send.at[t, k], recv.at[t, k], (px, py, c))
                out.start()
                sends.append(out)
        for t in range(n):
            for k, (px, py) in enumerate(chips):
                piece = bufs_[t].at[2 * px + py, c]
                _remote(piece, piece, send.at[t, k], recv.at[t, k], (px, py, c)).wait_recv()
                on = _remote(piece, piece, send.at[t, nk + k], recv.at[t, nk + k], (x, y, 1 - c))
                on.start()
                sends.append(on)
        for t in range(n):
            for k, (px, py) in enumerate(chips):
                piece = bufs_[t].at[2 * px + py, 1 - c]
                _remote(piece, piece, send.at[t, nk + k], recv.at[t, nk + k], (x, y, 1 - c)).wait_recv()
        for cp in sends:
            cp.wait_send()

    return pl.pallas_call(
        body, in_specs=[ANY] * n, out_specs=[ANY] * n,
        out_shape=[SDS(a.shape, a.dtype) for a in bufs],
        input_output_aliases={t: t for t in range(n)},
        scratch_shapes=[pltpu.SemaphoreType.DMA((n, 2 * nk)), pltpu.SemaphoreType.DMA((n, 2 * nk))],
        name=name)(*bufs)


class Rider(NamedTuple):
    operands: tuple
    n_aliased: int
    out_shapes: tuple
    scratch: tuple
    start: Callable
    finish: Callable


def _rider_specs(rider, n_inputs, n_outputs):
    if rider is None:
        return [], [], [], [], {}, []
    aliased = [SDS(a.shape, a.dtype) for a in rider.operands[:rider.n_aliased]]
    outs = aliased + list(rider.out_shapes)
    aliases = {n_inputs + t: n_outputs + t for t in range(rider.n_aliased)}
    return [ANY] * len(rider.operands), [ANY] * len(outs), outs, list(rider.scratch), aliases, list(rider.operands)


def _rider_run(rider, refs, n_inputs, n_outputs, first, last):
    if rider is None:
        return refs, lambda: None, lambda: None
    n_op = len(rider.operands)
    n_out = rider.n_aliased + len(rider.out_shapes)
    own_in, r_in = refs[:n_inputs], refs[n_inputs:n_inputs + n_op]
    own_out = refs[n_inputs + n_op:n_inputs + n_op + n_outputs]
    r_out = refs[n_inputs + n_op + n_outputs:n_inputs + n_op + n_outputs + n_out]
    rest = refs[n_inputs + n_op + n_outputs + n_out:]
    n_sem = len(rider.scratch)
    sems, own_scratch = rest[len(rest) - n_sem:], rest[:len(rest) - n_sem]

    def start():
        pl.when(first)(lambda: rider.start(r_in, r_out, sems))

    def finish():
        pl.when(last)(lambda: rider.finish(r_in, r_out, sems))

    return list(own_in) + list(own_out) + list(own_scratch), start, finish


def gather_rider(bufs):
    n = len(bufs)
    nk = N_CHIPS - 1

    def ici(bufs_, send, recv, x, y, c, t, k, px, py, own):
        piece = bufs_[t].at[2 * x + y if own else 2 * px + py, c]
        return _remote(piece, piece, send.at[t, k], recv.at[t, k], (px, py, c))

    def d2d(bufs_, send, recv, x, y, c, t, k, px, py, mine):
        piece = bufs_[t].at[2 * px + py, c if mine else 1 - c]
        return _remote(piece, piece, send.at[t, nk + k], recv.at[t, nk + k], (x, y, 1 - c))

    def start(r_in, r_out, sems):
        send, recv = sems
        x, y, c = _position()
        for t in range(n):
            for k, (px, py) in enumerate(_other_chips(x, y)):
                ici(r_out, send, recv, x, y, c, t, k, px, py, True).start()

    def finish(r_in, r_out, sems):
        send, recv = sems
        x, y, c = _position()
        chips = _other_chips(x, y)
        for t in range(n):
            for k, (px, py) in enumerate(chips):
                ici(r_out, send, recv, x, y, c, t, k, px, py, False).wait_recv()
                d2d(r_out, send, recv, x, y, c, t, k, px, py, True).start()
        for t in range(n):
            for k, (px, py) in enumerate(chips):
                d2d(r_out, send, recv, x, y, c, t, k, px, py, False).wait_recv()
        for t in range(n):
            for k, (px, py) in enumerate(chips):
                ici(r_out, send, recv, x, y, c, t, k, px, py, True).wait_send()
                d2d(r_out, send, recv, x, y, c, t, k, px, py, True).wait_send()

    sems = (pltpu.SemaphoreType.DMA((n, 2 * nk)), pltpu.SemaphoreType.DMA((n, 2 * nk)))
    return Rider(tuple(bufs), n, (), sems, start, finish)


def scatter_rider(parts):
    n = len(parts)

    def copies(r_in, r_out, sems):
        send, recv = sems
        x, y, c = _position()
        return [_remote(r_in[t].at[2 * px + py], r_out[t].at[k], send.at[t, k], recv.at[t, k], (px, py, c))
                for t in range(n) for k, (px, py) in enumerate(_other_chips(x, y))]

    def start(r_in, r_out, sems):
        for cp in copies(r_in, r_out, sems):
            cp.start()

    def finish(r_in, r_out, sems):
        for cp in copies(r_in, r_out, sems):
            cp.wait()

    sems = (pltpu.SemaphoreType.DMA((n, N_CHIPS - 1)), pltpu.SemaphoreType.DMA((n, N_CHIPS - 1)))
    return Rider(tuple(parts), 0, tuple(SDS((N_CHIPS - 1,) + a.shape[1:], a.dtype) for a in parts), sems, start, finish)


def swap_core_halves(grads, name):
    n = len(grads)

    def body(*refs):
        ins, outs = refs[:n], refs[n:2 * n]
        send, recv = refs[2 * n:]
        x, y, c = _position()
        pending = []
        for t in range(n):
            out = _remote(ins[t].at[:, 1 - c], outs[t], send.at[t], recv.at[t], (x, y, 1 - c))
            out.start()
            pending.append(out.wait)
        for wait in pending:
            wait()

    return pl.pallas_call(
        body, in_specs=[ANY] * n, out_specs=[ANY] * n,
        out_shape=[SDS((a.shape[0],) + a.shape[2:], a.dtype) for a in grads],
        scratch_shapes=[pltpu.SemaphoreType.DMA((n,)), pltpu.SemaphoreType.DMA((n,))],
        name=name)(*grads)


def scatter_chips(parts, name):
    n = len(parts)

    def body(*refs):
        ins, outs = refs[:n], refs[n:2 * n]
        send, recv = refs[2 * n:]
        x, y, c = _position()
        pending = []
        for t in range(n):
            for k, (px, py) in enumerate(_other_chips(x, y)):
                out = _remote(ins[t].at[2 * px + py], outs[t].at[k], send.at[t, k], recv.at[t, k], (px, py, c))
                out.start()
                pending.append(out.wait)
        for wait in pending:
            wait()

    return pl.pallas_call(
        body, in_specs=[ANY] * n, out_specs=[ANY] * n,
        out_shape=[SDS((N_CHIPS - 1,) + a.shape[1:], a.dtype) for a in parts],
        scratch_shapes=[pltpu.SemaphoreType.DMA((n, N_CHIPS - 1)), pltpu.SemaphoreType.DMA((n, N_CHIPS - 1))],
        name=name)(*parts)


def share_core_halves(bufs, name):
    n = len(bufs)

    def body(*refs):
        bufs_ = refs[n:2 * n]
        send, recv = refs[2 * n:]
        x, y, c = _position()
        pending = []
        for t in range(n):
            out = _remote(bufs_[t].at[c], bufs_[t].at[c], send.at[t], recv.at[t], (x, y, 1 - c))
            out.start()
            pending.append(out.wait_send)
            other = bufs_[t].at[1 - c]
            pending.append(_remote(other, other, send.at[t], recv.at[t], (x, y, 1 - c)).wait_recv)
        for wait in pending:
            wait()

    return pl.pallas_call(
        body, in_specs=[ANY] * n, out_specs=[ANY] * n, out_shape=[SDS(a.shape, a.dtype) for a in bufs],
        input_output_aliases={t: t for t in range(n)},
        scratch_shapes=[pltpu.SemaphoreType.DMA((n,)), pltpu.SemaphoreType.DMA((n,))],
        name=name)(*bufs)


def gather_devices(v, name):
    flips = [(dx, dy, dc) for dx in (0, 1) for dy in (0, 1) for dc in (0, 1)][1:]

    def body(v_ref, o_ref, send, recv, local):
        x, y, c = _position()
        me = 4 * x + 2 * y + c
        own = pltpu.make_async_copy(v_ref, o_ref.at[me], local)
        own.start()
        pending = [own.wait]
        for k, (dx, dy, dc) in enumerate(flips):
            px, py, pc = x ^ dx, y ^ dy, c ^ dc
            out = _remote(v_ref, o_ref.at[me], send.at[k], recv.at[k], (px, py, pc))
            out.start()
            pending.append(out.wait_send)
            pending.append(_remote(v_ref, o_ref.at[4 * px + 2 * py + pc], send.at[k], recv.at[k], (px, py, pc)).wait_recv)
        for wait in pending:
            wait()

    return pl.pallas_call(
        body, in_specs=[ANY], out_specs=ANY, out_shape=SDS((8,) + v.shape, v.dtype),
        scratch_shapes=[pltpu.SemaphoreType.DMA((7,)), pltpu.SemaphoreType.DMA((7,)), pltpu.SemaphoreType.DMA],
        name=name)(v)


SMALL = ("norm_g", "conv_b_pw1", "conv_w_dw", "conv_b_dw", "conv_ln_g", "conv_ln_b", "conv_b_pw2", "ffn_w_dw")


def _pack_rows(arrs, rows):
    flat = jnp.concatenate([a.reshape(-1, LANES) for a in arrs], axis=0)
    return jnp.pad(flat, ((0, rows - flat.shape[0]), (0, 0)))


def _unpack_rows(packed, shapes):
    out, at = [], 0
    for shp in shapes:
        size = 1
        for dim in shp:
            size *= dim
        rows = size // LANES
        out.append(packed[..., at:at + rows, :].reshape(packed.shape[:-2] + tuple(shp)))
        at += rows
    return out


def _join_last(t):
    t = jnp.moveaxis(t, 0, -2)
    return t.reshape(t.shape[:-2] + (t.shape[-2] * t.shape[-1],))


def _split_last(t):
    t = t.reshape(t.shape[:-1] + (N_CHIPS, t.shape[-1] // N_CHIPS))
    return jnp.moveaxis(t, -2, 0)


def _rope_tables(positions):
    half = ROT_DIM // 2
    inv_freq = ROPE_THETA ** (-jnp.arange(half, dtype=f32) / half)
    ang = positions.astype(f32).reshape(-1, 1) * inv_freq
    cos, sin = jnp.cos(ang), jnp.sin(ang)
    s = ang.shape[0]
    rest = HEAD_DIM - ROT_DIM
    head = lambda lo, hi, fill: jnp.concatenate([lo, hi, jnp.full((s, rest), fill, f32)], axis=1)
    zero = jnp.zeros((s, half), f32)
    twice = lambda t: jnp.concatenate([t] * (LANES // HEAD_DIM), axis=1)
    return twice(head(cos, cos, 1.0)), twice(head(-sin, zero, 0.0)), twice(head(zero, sin, 0.0))


def _pad_rows(t, rows):
    return jnp.pad(t, ((0, rows - t.shape[0]), (0, 0)))


def _ffn_pack(w_dw, b_dw):
    t = jnp.concatenate([w_dw, b_dw[None]], axis=0)
    t = t.reshape(FFN_CONV + 1, 2, 2, -1)
    t = jnp.transpose(t, (2, 1, 0, 3))
    return jnp.pad(t, ((0, 0), (0, 0), (0, 8 - (FFN_CONV + 1)), (0, 0)))


def _ffn_unpack(d):
    t = jnp.transpose(d[:, :, :FFN_CONV + 1], (2, 1, 0, 3)).reshape(FFN_CONV + 1, -1)
    return t[:FFN_CONV], t[FFN_CONV]


def _ffn_block(x, g_pre, g_post, w_up, li, w_down, wb, tag):
    s = x.shape[0]
    h = prenorm(x, g_pre, f"{tag}_prenorm")
    u = mm_nn(h[None], w_up, li, None, N_CHIPS, bf16, 512, w_up.shape[-1], f"{tag}_up")
    u4 = u.reshape(2, 2, s, u.shape[-1])
    a, ug = ffn_act_fwd(u4, wb, f"{tag}_act")
    y = mm_nn(a, w_down, 0, None, 1, f32, 512, 512, f"{tag}_down")[0]
    return y, (h, u4, ug, a)


def _ffn_block_bwd(dy, saved, w_up, li, w_down, wb, tag, rider=None):
    h, u4, ug, a = saved
    d_down = mm_tn(a, dy[None], 1, 1024, a.shape[-1], 512, f"{tag}_dwdown")
    da = mm_nt(dy[None], w_down, 0, 2, bf16, 1024, a.shape[-1], w_down.shape[-1], f"{tag}_da")
    du4, dwb, rode = ffn_act_bwd(da, ug, u4, wb, f"{tag}_actbwd", rider)
    du = du4.reshape((N_CHIPS,) + du4.shape[2:])
    d_up = mm_tn(h[None], du, N_CHIPS, 1024, h.shape[-1], du.shape[-1], f"{tag}_dwup")
    dh = mm_nt(du, w_up, li, 1, f32, 1024, h.shape[-1], du.shape[-1], f"{tag}_dh")[0]
    return dh, d_up, d_down, dwb, rode


def kernel(x, positions, norm_g, attn_w_qkv, attn_w_o, conv_w_pw1, conv_b_pw1, conv_w_dw, conv_b_dw, conv_ln_g, conv_ln_b, conv_w_pw2, conv_b_pw2, ffn_w_up, ffn_w_dw, ffn_b_dw, ffn_w_down, loss_target, m_norm_g, m_attn_w_qkv, m_attn_w_o, m_conv_w_pw1, m_conv_b_pw1, m_conv_w_dw, m_conv_b_dw, m_conv_ln_g, m_conv_ln_b, m_conv_w_pw2, m_conv_b_pw2, m_ffn_w_up, m_ffn_w_dw, m_ffn_b_dw, m_ffn_w_down, v_norm_g, v_attn_w_qkv, v_attn_w_o, v_conv_w_pw1, v_conv_b_pw1, v_conv_w_dw, v_conv_b_dw, v_conv_ln_g, v_conv_ln_b, v_conv_w_pw2, v_conv_b_pw2, v_ffn_w_up, v_ffn_w_dw, v_ffn_b_dw, v_ffn_w_down):
    weights = dict(norm_g=norm_g, attn_w_qkv=attn_w_qkv, attn_w_o=attn_w_o, conv_w_pw1=conv_w_pw1, conv_b_pw1=conv_b_pw1,
                   conv_w_dw=conv_w_dw, conv_b_dw=conv_b_dw, conv_ln_g=conv_ln_g, conv_ln_b=conv_ln_b, conv_w_pw2=conv_w_pw2,
                   conv_b_pw2=conv_b_pw2, ffn_w_up=ffn_w_up, ffn_w_dw=ffn_w_dw, ffn_b_dw=ffn_b_dw, ffn_w_down=ffn_w_down)
    mom1 = dict(norm_g=m_norm_g, attn_w_qkv=m_attn_w_qkv, attn_w_o=m_attn_w_o, conv_w_pw1=m_conv_w_pw1, conv_b_pw1=m_conv_b_pw1,
                conv_w_dw=m_conv_w_dw, conv_b_dw=m_conv_b_dw, conv_ln_g=m_conv_ln_g, conv_ln_b=m_conv_ln_b, conv_w_pw2=m_conv_w_pw2,
                conv_b_pw2=m_conv_b_pw2, ffn_w_up=m_ffn_w_up, ffn_w_dw=m_ffn_w_dw, ffn_b_dw=m_ffn_b_dw, ffn_w_down=m_ffn_w_down)
    mom2 = dict(norm_g=v_norm_g, attn_w_qkv=v_attn_w_qkv, attn_w_o=v_attn_w_o, conv_w_pw1=v_conv_w_pw1, conv_b_pw1=v_conv_b_pw1,
                conv_w_dw=v_conv_w_dw, conv_b_dw=v_conv_b_dw, conv_ln_g=v_conv_ln_g, conv_ln_b=v_conv_ln_b, conv_w_pw2=v_conv_w_pw2,
                conv_b_pw2=v_conv_b_pw2, ffn_w_up=v_ffn_w_up, ffn_w_dw=v_ffn_w_dw, ffn_b_dw=v_ffn_b_dw, ffn_w_down=v_ffn_w_down)
    big = ("attn_w_qkv", "attn_w_o", "conv_w_pw1", "conv_w_pw2", "ffn_w_up", "ffn_w_down")
    xi, yi, ci = _position()
    chip = (2 * xi + yi).astype(jnp.int32).reshape(1)
    where = jnp.stack([ci, 2 * xi + yi]).astype(jnp.int32)

    x = x[0]
    target = loss_target[0]
    s, d = x.shape

    small_shapes = [weights[k].shape for k in SMALL]
    small_rows = -(-sum(weights[k].size for k in SMALL) // LANES // 8) * 8
    small_w = _pack_rows([weights[k] for k in SMALL], small_rows)
    def own_slot(shard):
        halves = shard.reshape(1, 2, -1, shard.shape[-1])
        return lax.dynamic_update_slice(lax.empty((N_CHIPS,) + halves.shape[1:], shard.dtype), halves, (chip[0], 0, 0, 0))

    early, late = big[:2], big[2:]
    gathered = gather_chips([own_slot(weights[k].astype(bf16)) for k in early] + [own_slot(small_w)], "gather_weights_early")
    gw = {k: t.reshape((N_CHIPS,) + weights[k].shape) for k, t in zip(early, gathered[:-1])}
    full_small = dict(zip(SMALL, [_join_last(t) for t in _unpack_rows(gathered[-1].reshape(N_CHIPS, small_rows, LANES), small_shapes)]))
    w_qkv, w_o = gw["attn_w_qkv"], gw["attn_w_o"]
    gains = full_small["norm_g"]
    gain = lambda i, k: gains[i, k][None]
    b_pw1 = full_small["conv_b_pw1"]
    conv_wdw = _pad_rows(full_small["conv_w_dw"][0], CONV_HALO)
    conv_vecs = _pad_rows(jnp.concatenate([full_small["conv_b_dw"], full_small["conv_ln_g"], full_small["conv_ln_b"]], axis=0), 8)
    b_pw2 = full_small["conv_b_pw2"]
    wbs = [_ffn_pack(full_small["ffn_w_dw"][i], ffn_b_dw[i]) for i in range(2)]
    tabs = _rope_tables(positions[0])
    tabs_t = (tabs[0], -tabs[1], -tabs[2])
    head_ones = (jnp.arange(GROUP_WIDTH)[:, None] // HEAD_DIM == jnp.arange(GROUP_WIDTH)[None, :] // HEAD_DIM).astype(bf16)

    h0 = prenorm(x, gain(0, 0), "l0_prenorm")
    w_qkv_flat = jnp.transpose(w_qkv[:, 0], (1, 0, 2)).reshape(d, -1)
    qkvg, late_gathered = qkv_rope(h0, w_qkv_flat, tabs, "qkv", gather_rider([own_slot(weights[k].astype(bf16)) for k in late]))
    gw.update({k: t.reshape((N_CHIPS,) + weights[k].shape) for k, t in zip(late, late_gathered)})
    w_pw1, w_up = gw["conv_w_pw1"], gw["ffn_w_up"]
    w_pw2 = gw["conv_w_pw2"].reshape(1, 1, -1, d)
    w_down = [gw["ffn_w_down"][:, i].reshape(1, 1, -1, d) for i in range(2)]
    att = [attn_fwd(qkvg, g, dil, f"attn_fwd{g}") for g, (_, dil) in enumerate(DILATED_GROUPS)]
    os_, lses, lses_phased = ([a[k] for a in att] for k in range(3))
    mixed = mix_fwd(os_, lses, "mix")
    y0 = mm_nn(mixed[None], w_o, 0, None, 1, f32, 1024, w_o.shape[-1], "attn_out")[0]
    x1 = postnorm_residual(x, y0, gain(0, 1), "l0_postnorm")
    y1, ffn0 = _ffn_block(x1, gain(0, 2), gain(0, 3), w_up, 0, w_down[0], wbs[0], "ffn0")
    x2 = postnorm_residual(x1, y1, gain(0, 3), "l0_ffn_postnorm")

    h2 = prenorm(x2, gain(1, 0), "l1_prenorm")
    z = mm_nn(h2[None], w_pw1, 0, b_pw1, 1, f32, 1024, w_pw1.shape[-1], "pw1")[0]
    cpre, sw = conv_module_fwd(z, conv_wdw, conv_vecs, "conv_fwd")
    y2 = mm_nn(sw[None], w_pw2, 0, b_pw2, 1, f32, 512, 512, "pw2")[0]
    x3 = postnorm_residual(x2, y2, gain(1, 1), "l1_postnorm")
    y3, ffn1 = _ffn_block(x3, gain(1, 2), gain(1, 3), w_up, 1, w_down[1], wbs[1], "ffn1")
    dx4, loss_row = final_loss(x3, y3, gain(1, 3), target, "loss")
    loss = lax.psum(loss_row[0, 0], ("x", "y", "c"))

    dgain = [[None] * 4 for _ in range(2)]
    dy3, dgain[1][3], _ = norm_bwd(y3, gain(1, 3), dx4, None, bf16, "l1_ffn_postnorm_bwd")
    dh, d_up1, d_down1, dwb1, _ = _ffn_block_bwd(dy3, ffn1, w_up, 1, w_down[1], wbs[1], "ffn1")
    dx3, dgain[1][2], _ = norm_bwd(x3, gain(1, 2), dh, dx4, f32, "l1_ffn_prenorm_bwd")

    dy2, dgain[1][1], d_b_pw2 = norm_bwd(y2, gain(1, 1), dx3, None, bf16, "l1_postnorm_bwd")
    d_pw2 = mm_tn(sw[None], dy2[None], 1, 1024, d, 512, "dw_pw2")
    dsw = mm_nt(dy2[None], w_pw2, 0, 1, f32, 1024, d, d, "d_swish")[0]
    dz, d_conv_wdw, d_conv_vecs, d_b_pw1 = conv_module_bwd(dsw, cpre, z, conv_wdw, conv_vecs, "conv_bwd")
    d_pw1 = mm_tn(h2[None], dz[None], N_CHIPS, 1024, d, w_pw1.shape[-1], "dw_pw1")
    dh = mm_nt(dz[None], w_pw1, 0, 1, f32, 1024, d, w_pw1.shape[-1], "d_h2")[0]
    dx2, dgain[1][0], _ = norm_bwd(x2, gain(1, 0), dh, dx3, f32, "l1_prenorm_bwd")

    def core_stage(named, tag):
        halves = [g.reshape(N_CHIPS, 2, -1, g.shape[-1]) for _, g in named]
        from_core = swap_core_halves(halves, f"swap_core_halves_{tag}")
        sums = [add_core_halves(a, b, where, f"add_core_{k}") for (k, _), a, b in zip(named, halves, from_core)]
        return halves, from_core, sums

    def chip_stage(named, halves, from_core, from_chips):
        return [add_chip_parts(a, b, r, where, f"add_chips_{k}") for (k, _), a, b, r in zip(named, halves, from_core, from_chips)]

    first_done = [("conv_w_pw1", d_pw1), ("conv_w_pw2", d_pw2[0].reshape(N_CHIPS, -1, d)), ("ffn_w_up1", d_up1),
                  ("ffn_w_down1", d_down1[0].reshape(N_CHIPS, -1, d))]
    halves_1, core_1, sums_1 = core_stage(first_done, "layer1")

    dy1, dgain[0][3], _ = norm_bwd(y1, gain(0, 3), dx2, None, bf16, "l0_ffn_postnorm_bwd")
    dh, d_up0, d_down0, dwb0, chips_1 = _ffn_block_bwd(dy1, ffn0, w_up, 0, w_down[0], wbs[0], "ffn0", scatter_rider(sums_1))
    mine_1 = chip_stage(first_done, halves_1, core_1, chips_1)
    dx1, dgain[0][2], _ = norm_bwd(x1, gain(0, 2), dh, dx2, f32, "l0_ffn_prenorm_bwd")

    dy0, dgain[0][1], _ = norm_bwd(y0, gain(0, 1), dx1, None, bf16, "l0_postnorm_bwd")
    d_wo = mm_tn(mixed[None], dy0[None], N_CHIPS, 1024, GROUP_WIDTH, w_o.shape[-1], "dw_o")
    dmixed = mm_nt(dy0[None], w_o, 0, 1, f32, 1024, GROUP_WIDTH, w_o.shape[-1], "d_mixed")[0]
    mb = mix_bwd(dmixed, os_, lses, head_ones, "mix_bwd")
    parts = [attn_bwd(qkvg, mb[g], lses_phased[g], mb[3 + g], g, dil, f"attn_bwd{g}") for g, (_, dil) in enumerate(DILATED_GROUPS)]
    dqkv = dqkv_assemble(parts, tabs_t, "dqkv")
    d_qkv = mm_tn(h0[None], dqkv[None], N_CHIPS, 1024, d, w_qkv.shape[-1], "dw_qkv")
    dh = mm_nt(dqkv[None], w_qkv, 0, 1, f32, 1024, d, w_qkv.shape[-1], "d_h0")[0]
    grad_x, dgain[0][0], _ = norm_bwd(x, gain(0, 0), dh, dx1, f32, "l0_prenorm_bwd")

    d_ffn_dw, d_ffn_b = zip(*[_ffn_unpack(t) for t in (dwb0, dwb1)])
    gsmall = dict(
        norm_g=jnp.stack([jnp.concatenate(row, axis=0) for row in dgain], axis=0),
        conv_b_pw1=d_b_pw1, conv_w_dw=d_conv_wdw[None, :CONV_KERNEL], conv_b_dw=d_conv_vecs[0:1], conv_ln_g=d_conv_vecs[1:2],
        conv_ln_b=d_conv_vecs[2:3], conv_b_pw2=d_b_pw2, ffn_w_dw=jnp.stack(d_ffn_dw, axis=0))
    bias_rows = ffn_b_dw.size // LANES
    small_g = jnp.concatenate([jnp.concatenate([_pack_rows([_split_last(gsmall[k])[p] for k in SMALL], small_rows)
                                                for p in range(N_CHIPS)], axis=0),
                               jnp.stack(d_ffn_b, axis=0).reshape(bias_rows, LANES)], axis=0)

    small_sum = sum_devices(gather_devices(small_g, "gather_small_grads"), "sum_small_grads")
    my_small = lax.dynamic_slice_in_dim(small_sum, chip[0] * small_rows, small_rows, axis=0)
    g_small = jnp.concatenate([my_small, small_sum[N_CHIPS * small_rows:]], axis=0)

    last_done = [("attn_w_qkv", d_qkv), ("attn_w_o", d_wo), ("ffn_w_up0", d_up0), ("ffn_w_down0", d_down0[0].reshape(N_CHIPS, -1, d))]
    halves_0, core_0, sums_0 = core_stage(last_done, "layer0")
    mine_0 = chip_stage(last_done, halves_0, core_0, scatter_chips(sums_0, "scatter_chips_layer0"))
    shared = share_core_halves(mine_1 + mine_0, "share_core_halves")
    piece = {k: t.reshape(-1, t.shape[-1]) for (k, _), t in zip(first_done + last_done, shared)}
    shard_grads = [piece["attn_w_qkv"], piece["attn_w_o"], piece["conv_w_pw1"], piece["conv_w_pw2"],
                   jnp.concatenate([piece["ffn_w_up0"], piece["ffn_w_up1"]], axis=0),
                   jnp.concatenate([piece["ffn_w_down0"], piece["ffn_w_down1"]], axis=0)]

    grads, deltas, new_m, new_v = {}, {}, {}, {}
    for k, g2 in zip(big, shard_grads):
        shp = weights[k].shape
        dl, nm, nv = adamw(weights[k].reshape(g2.shape), g2, mom1[k].reshape(g2.shape), mom2[k].reshape(g2.shape), f"adamw_{k}")
        grads[k], deltas[k], new_m[k], new_v[k] = (t.reshape(shp) for t in (g2, dl, nm, nv))
    pack_state = lambda src: jnp.concatenate([_pack_rows([src[k] for k in SMALL], small_rows), src["ffn_b_dw"].reshape(bias_rows, LANES)], axis=0)
    small_out = (g_small,) + tuple(adamw(pack_state(weights), g_small, pack_state(mom1), pack_state(mom2), "adamw_small"))
    for dst, packed in zip((grads, deltas, new_m, new_v), small_out):
        for k, t in zip(SMALL, _unpack_rows(packed[:small_rows], small_shapes)):
            dst[k] = t
        dst["ffn_b_dw"] = packed[small_rows:].reshape(ffn_b_dw.shape)

    order = ("norm_g", "attn_w_qkv", "attn_w_o", "conv_w_pw1", "conv_b_pw1", "conv_w_dw", "conv_b_dw", "conv_ln_g", "conv_ln_b",
             "conv_w_pw2", "conv_b_pw2", "ffn_w_up", "ffn_w_dw", "ffn_b_dw", "ffn_w_down")
    return (loss, grad_x[None], *[grads[k] for k in order], *[deltas[k] for k in order], *[new_m[k] for k in order],
            *[new_v[k] for k in order])
```

```python
import functools
from typing import Callable, NamedTuple

import jax
import jax.numpy as jnp
from jax import lax
from jax.experimental import pallas as pl
from jax.experimental.pallas import tpu as pltpu

f32 = jnp.float32
bf16 = jnp.bfloat16
SDS = jax.ShapeDtypeStruct

EPS = 1e-6
HEAD_DIM = 64
N_SLOTS = 8
GROUP_WIDTH = N_SLOTS * HEAD_DIM
DILATED_GROUPS = ((128, 1), (512, 4), (2048, 16))
N_GROUPS = 3
SPAN = 128
ROT_DIM = HEAD_DIM // 4
ROPE_THETA = 500000.0
CONV_KERNEL = 31
CONV_HALO = 32
FFN_CONV = 3
FFN_HALO = 16
FFN_TAIL = 8
STRIP_ROWS = 64
ADAM_LR, ADAM_B1, ADAM_B2, ADAM_EPS, ADAM_WD, ADAM_STEP = 0.001, 0.9, 0.999, 1e-08, 0.01, 10
LANES = 128
N_CHIPS = 4
VMEM_LIMIT_BYTES = 56 * 1024 * 1024
NEG_BIG = -1e30
MESH = pl.DeviceIdType.MESH
ANY = pl.BlockSpec(memory_space=pl.ANY)


def _params(*sem):
    return pltpu.CompilerParams(dimension_semantics=sem, vmem_limit_bytes=VMEM_LIMIT_BYTES)


def _sigmoid(v):
    return 1.0 / (1.0 + jnp.exp(-v))


def _dot_nt(a, b):
    return lax.dot_general(a, b, (((1,), (1,)), ((), ())), preferred_element_type=f32)


def _dot_tn(a, b):
    return lax.dot_general(a, b, (((0,), (0,)), ((), ())), preferred_element_type=f32)


def mm_nn(x, w, li, bias, out_blocks, out_dtype, tm, tn, name):
    nq, m, kq = x.shape
    p, _, k, n = w.shape
    assert k == nq * kq and n % tn == 0 and m % tm == 0
    on = p * n // out_blocks
    assert on % tn == 0
    nj, onj = n // tn, on // tn

    def body(*refs):
        if bias is None:
            x_ref, w_ref, o_ref = refs
            b_ref = None
        else:
            x_ref, w_ref, b_ref, o_ref = refs
        acc = jnp.dot(x_ref[0], w_ref[0:kq, :], preferred_element_type=f32)
        for q in range(1, nq):
            acc = acc + jnp.dot(x_ref[q], w_ref[q * kq:(q + 1) * kq, :], preferred_element_type=f32)
        if b_ref is not None:
            acc = acc + b_ref[...]
        o_ref[...] = acc.astype(o_ref.dtype)

    in_specs = [pl.BlockSpec((nq, tm, kq), lambda j, i: (0, i, 0)),
                pl.BlockSpec((None, None, k, tn), lambda j, i: (j // nj, li, 0, j % nj))]
    args = [x, w]
    if bias is not None:
        in_specs.append(pl.BlockSpec((1, tn), lambda j, i: (0, j)))
        args.append(bias)
    return pl.pallas_call(
        body, grid=(p * nj, m // tm), in_specs=in_specs,
        out_specs=pl.BlockSpec((None, tm, tn), lambda j, i: (j // onj, i, j % onj)),
        out_shape=SDS((out_blocks, m, on), out_dtype),
        compiler_params=_params("parallel", "parallel"), name=name)(*args)


def mm_nt(dy, w, li, out_blocks, out_dtype, tm, tk, tn, name):
    ob, m, on = dy.shape
    p, _, k, n = w.shape
    assert ob * on == p * n and n % tn == 0 and on % tn == 0 and k % tk == 0 and m % tm == 0
    kq = k // out_blocks
    assert kq % tk == 0
    nj, onj, kqj = n // tn, on // tn, kq // tk
    nr = p * nj

    def body(dy_ref, w_ref, o_ref, *scr):
        part = _dot_nt(dy_ref[...], w_ref[...])
        if nr == 1:
            o_ref[...] = part.astype(o_ref.dtype)
        else:
            acc_ref, = scr
            r = pl.program_id(2)

            @pl.when(r == 0)
            def _():
                acc_ref[...] = part

            @pl.when(r > 0)
            def _():
                acc_ref[...] += part

            @pl.when(r == nr - 1)
            def _():
                o_ref[...] = acc_ref[...].astype(o_ref.dtype)

    return pl.pallas_call(
        body, grid=(k // tk, m // tm, nr),
        in_specs=[pl.BlockSpec((None, tm, tn), lambda kt, i, r: (r // onj, i, r % onj)),
                  pl.BlockSpec((None, None, tk, tn), lambda kt, i, r: (r // nj, li, kt, r % nj))],
        out_specs=pl.BlockSpec((None, tm, tk), lambda kt, i, r: (kt // kqj, i, kt % kqj)),
        out_shape=SDS((out_blocks, m, kq), out_dtype),
        scratch_shapes=[] if nr == 1 else [pltpu.VMEM((tm, tk), f32)],
        compiler_params=_params("parallel", "parallel", "arbitrary"), name=name)(dy, w)


def mm_tn(x, dy, p, tm, tk, tn, name):
    nq, m, kq = x.shape
    ob, _, on = dy.shape
    k = nq * kq
    n = ob * on // p
    assert n % tn == 0 and on % tn == 0 and kq % tk == 0 and m % tm == 0
    nj, onj, kqj = n // tn, on // tn, kq // tk

    def body(x_ref, dy_ref, o_ref):
        part = _dot_tn(x_ref[...], dy_ref[...])
        i = pl.program_id(2)

        @pl.when(i == 0)
        def _():
            o_ref[...] = part

        @pl.when(i > 0)
        def _():
            o_ref[...] += part

    return pl.pallas_call(
        body, grid=(k // tk, p * nj, m // tm),
        in_specs=[pl.BlockSpec((None, tm, tk), lambda kt, j, i: (kt // kqj, i, kt % kqj)),
                  pl.BlockSpec((None, tm, tn), lambda kt, j, i: (j // onj, i, j % onj))],
        out_specs=pl.BlockSpec((None, tk, tn), lambda kt, j, i: (j // nj, kt, j % nj)),
        out_shape=SDS((p, k, n), f32),
        compiler_params=_params("parallel", "parallel", "arbitrary"), name=name)(x, dy)


def _row_tile(s):
    return min(s, 512)


def _rows(tr, d):
    return pl.BlockSpec((tr, d), lambda i: (i, 0))


def _fixed(r, d):
    return pl.BlockSpec((r, d), lambda i: (0, 0))


def _rms(xv):
    return lax.rsqrt(jnp.mean(xv * xv, axis=-1, keepdims=True) + EPS)


def prenorm(x, g, name):
    s, d = x.shape
    tr = _row_tile(s)

    def body(x_ref, g_ref, o_ref):
        xv = x_ref[...]
        o_ref[...] = (xv * _rms(xv) * g_ref[...]).astype(o_ref.dtype)

    return pl.pallas_call(body, grid=(s // tr,), in_specs=[_rows(tr, d), _fixed(1, d)], out_specs=_rows(tr, d),
                          out_shape=SDS((s, d), bf16), compiler_params=_params("parallel"), name=name)(x, g)


def postnorm_residual(x, y, g, name):
    s, d = x.shape
    tr = _row_tile(s)

    def body(x_ref, y_ref, g_ref, o_ref):
        yv = y_ref[...]
        o_ref[...] = x_ref[...] + yv * _rms(yv) * g_ref[...]

    return pl.pallas_call(body, grid=(s // tr,), in_specs=[_rows(tr, d), _rows(tr, d), _fixed(1, d)],
                          out_specs=_rows(tr, d), out_shape=SDS((s, d), f32), compiler_params=_params("parallel"),
                          name=name)(x, y, g)


def norm_bwd(xin, g, dout, res, out_dtype, name):
    s, d = xin.shape
    tr = _row_tile(s)

    def body(*refs):
        if res is None:
            x_ref, g_ref, do_ref, dx_ref, dg_ref, cs_ref = refs
            r_ref = None
        else:
            x_ref, g_ref, do_ref, r_ref, dx_ref, dg_ref, cs_ref = refs
        xv = x_ref[...]
        r = _rms(xv)
        xh = xv * r
        dov = do_ref[...].astype(f32)
        gy = dov * g_ref[...]
        dx = r * (gy - xh * jnp.mean(gy * xh, axis=-1, keepdims=True))
        if r_ref is not None:
            dx = dx + r_ref[...]
        dx_ref[...] = dx.astype(dx_ref.dtype)
        dg = jnp.sum(dov * xh, axis=0, keepdims=True)
        cs = jnp.sum(dx, axis=0, keepdims=True)
        i = pl.program_id(0)

        @pl.when(i == 0)
        def _():
            dg_ref[...] = dg
            cs_ref[...] = cs

        @pl.when(i > 0)
        def _():
            dg_ref[...] += dg
            cs_ref[...] += cs

    in_specs = [_rows(tr, d), _fixed(1, d), _rows(tr, d)]
    args = [xin, g, dout]
    if res is not None:
        in_specs.append(_rows(tr, d))
        args.append(res)
    return pl.pallas_call(body, grid=(s // tr,), in_specs=in_specs,
                          out_specs=[_rows(tr, d), _fixed(1, d), _fixed(1, d)],
                          out_shape=[SDS((s, d), out_dtype), SDS((1, d), f32), SDS((1, d), f32)],
                          compiler_params=_params("arbitrary"), name=name)(*args)


def final_loss(x, y, g, target, name):
    s, d = x.shape
    tr = _row_tile(s)
    nt = s // tr

    def body(x_ref, y_ref, g_ref, t_ref, dx_ref, loss_ref, acc_ref):
        yv = y_ref[...]
        diff = x_ref[...] + yv * _rms(yv) * g_ref[...] - t_ref[...]
        dx_ref[...] = diff * (1.0 / d)
        sq = jnp.sum(diff * diff, axis=0, keepdims=True)
        i = pl.program_id(0)

        @pl.when(i == 0)
        def _():
            acc_ref[...] = sq

        @pl.when(i > 0)
        def _():
            acc_ref[...] += sq

        @pl.when(i == nt - 1)
        def _():
            total = jnp.sum(acc_ref[...], axis=1, keepdims=True) * (0.5 / d)
            loss_ref[...] = jnp.broadcast_to(total, (1, LANES))

    return pl.pallas_call(body, grid=(nt,), in_specs=[_rows(tr, d), _rows(tr, d), _fixed(1, d), _rows(tr, d)],
                          out_specs=[_rows(tr, d), _fixed(1, LANES)],
                          out_shape=[SDS((s, d), f32), SDS((1, LANES), f32)],
                          scratch_shapes=[pltpu.VMEM((1, d), f32)],
                          compiler_params=_params("arbitrary"), name=name)(x, y, g, target)


def _rotate(v, pair, tc, ts):
    partner = jnp.dot(v.astype(bf16), pair, preferred_element_type=f32)
    return jnp.concatenate([v[:, ch * LANES:(ch + 1) * LANES] * tc + partner[:, ch * LANES:(ch + 1) * LANES] * ts
                            for ch in range(GROUP_WIDTH // LANES)], axis=1)


def qkv_rope(h, w, pair, tabs, name, rider=None):
    s, d = h.shape
    tm = min(s, 512)
    n_kinds = 3
    grid = (n_kinds * N_GROUPS, s // tm)

    def body(*refs):
        first = jnp.logical_and(pl.program_id(0) == 0, pl.program_id(1) == 0)
        last = jnp.logical_and(pl.program_id(0) == grid[0] - 1, pl.program_id(1) == grid[1] - 1)
        (x_ref, w_ref, p_ref, tc_ref, ts_ref, o_ref), ride_start, ride_finish = _rider_run(rider, refs, 5, 1, first, last)
        ride_start()
        kind = pl.program_id(0) // N_GROUPS
        acc = jnp.dot(x_ref[...], w_ref[...], preferred_element_type=f32)

        @pl.when(kind < 2)
        def _():
            o_ref[...] = _rotate(acc, p_ref[...], tc_ref[...], ts_ref[...]).astype(o_ref.dtype)

        @pl.when(kind == 2)
        def _():
            o_ref[...] = acc.astype(o_ref.dtype)

        ride_finish()

    tab = pl.BlockSpec((tm, LANES), lambda j, i: (i, 0))
    r_in, r_out, r_shapes, r_scratch, r_alias, r_args = _rider_specs(rider, 5, 1)
    outs = pl.pallas_call(
        body, grid=grid,
        in_specs=[pl.BlockSpec((tm, d), lambda j, i: (i, 0)), pl.BlockSpec((d, GROUP_WIDTH), lambda j, i: (0, j)),
                  pl.BlockSpec((GROUP_WIDTH, GROUP_WIDTH), lambda j, i: (0, 0)), tab, tab] + r_in,
        out_specs=[pl.BlockSpec((None, tm, GROUP_WIDTH), lambda j, i: (j % N_GROUPS, i, j // N_GROUPS))] + r_out,
        out_shape=[SDS((N_GROUPS, s, n_kinds * GROUP_WIDTH), bf16)] + r_shapes,
        scratch_shapes=r_scratch, input_output_aliases=r_alias,
        compiler_params=_params("arbitrary", "arbitrary"), name=name)(h, w, pair, *tabs, *r_args)
    return outs[0], outs[1:]


def _attn_mask(j):
    row = lax.broadcasted_iota(jnp.int32, (SPAN, 2 * SPAN), 0)
    col = lax.broadcasted_iota(jnp.int32, (SPAN, 2 * SPAN), 1)
    prev = jnp.logical_and(jnp.logical_and(col < SPAN, col >= row), j > 0)
    return jnp.logical_or(prev, jnp.logical_and(col >= SPAN, col - SPAN <= row))


def _attn_in_specs(gi):
    def at(kind, prev):
        def index(r, j):
            return (gi, jnp.maximum(j - 1, 0) if prev else j, r * 3 + kind)
        return pl.BlockSpec((None, SPAN, GROUP_WIDTH), index)
    return [at(0, False), at(1, False), at(1, True), at(2, False), at(2, True)]


def _phases(qkvg, g, dil):
    if dil == 1:
        return qkvg, g
    _, s, w = qkvg.shape
    return qkvg[g].reshape(1, s // dil, dil * w), 0


def attn_fwd(qkvg, g, dil, name):
    a, gi = _phases(qkvg, g, dil)
    s = qkvg.shape[1]
    l = s // dil
    nb = l // SPAN

    def body(q_ref, ko_ref, kp_ref, vo_ref, vp_ref, o_ref, lse_ref, k_scr, v_scr):
        mask = _attn_mask(pl.program_id(1))
        k_scr[0:SPAN, :] = kp_ref[...]
        k_scr[SPAN:2 * SPAN, :] = ko_ref[...]
        v_scr[0:SPAN, :] = vp_ref[...]
        v_scr[SPAN:2 * SPAN, :] = vo_ref[...]
        for h in range(N_SLOTS):
            hs = slice(h * HEAD_DIM, (h + 1) * HEAD_DIM)
            sc = jnp.where(mask, _dot_nt(q_ref[:, hs], k_scr[:, hs]) * (HEAD_DIM ** -0.5), NEG_BIG)
            mx = jnp.max(sc, axis=-1, keepdims=True)
            p = jnp.exp(sc - mx)
            den = jnp.sum(p, axis=-1, keepdims=True)
            o_ref[:, hs] = jnp.dot(p.astype(bf16), v_scr[:, hs], preferred_element_type=f32) / den
            lse_ref[:, hs] = jnp.broadcast_to(mx + jnp.log(den), (SPAN, HEAD_DIM))

    out = pl.BlockSpec((SPAN, GROUP_WIDTH), lambda r, j: (j, r))
    o, lse = pl.pallas_call(
        body, grid=(dil, nb), in_specs=_attn_in_specs(gi), out_specs=[out, out],
        out_shape=[SDS((l, dil * GROUP_WIDTH), f32)] * 2,
        scratch_shapes=[pltpu.VMEM((2 * SPAN, GROUP_WIDTH), bf16)] * 2,
        compiler_params=_params("parallel", "parallel"), name=name)(a, a, a, a, a)
    return o.reshape(s, GROUP_WIDTH), lse.reshape(s, GROUP_WIDTH), lse


def _group_weights(lses):
    mx = jnp.maximum(jnp.maximum(lses[0], lses[1]), lses[2])
    es = [jnp.exp(v - mx) for v in lses]
    inv = 1.0 / (es[0] + es[1] + es[2])
    return [e * inv for e in es]


def mix_fwd(os_, lses, name):
    s, w = os_[0].shape
    tr = _row_tile(s)

    def body(o0, o1, o2, l0, l1, l2, out_ref):
        wg = _group_weights([l0[...], l1[...], l2[...]])
        out_ref[...] = (wg[0] * o0[...] + wg[1] * o1[...] + wg[2] * o2[...]).astype(out_ref.dtype)

    return pl.pallas_call(body, grid=(s // tr,), in_specs=[_rows(tr, w)] * 6, out_specs=_rows(tr, w),
                          out_shape=SDS((s, w), bf16), compiler_params=_params("parallel"), name=name)(*os_, *lses)


def mix_bwd(dmixed, os_, lses, head_ones, name):
    s, w = dmixed.shape
    tr = _row_tile(s)

    def head_sum(t, ones):
        hi = t.astype(bf16)
        lo = (t - hi.astype(f32)).astype(bf16)
        return jnp.dot(hi, ones, preferred_element_type=f32) + jnp.dot(lo, ones, preferred_element_type=f32)

    def body(dm_ref, o0, o1, o2, l0, l1, l2, ones_ref, d0, d1, d2, p0, p1, p2):
        dm = dm_ref[...]
        ones = ones_ref[...]
        wg = _group_weights([l0[...], l1[...], l2[...]])
        mean = sum(wg[k] * head_sum(dm * o[...], ones) for k, o in enumerate((o0, o1, o2)))
        for k, (d_ref, p_ref) in enumerate(((d0, p0), (d1, p1), (d2, p2))):
            d_ref[...] = (wg[k] * dm).astype(d_ref.dtype)
            p_ref[...] = wg[k] * mean

    return pl.pallas_call(body, grid=(s // tr,), in_specs=[_rows(tr, w)] * 7 + [_fixed(w, w)],
                          out_specs=[_rows(tr, w)] * 6,
                          out_shape=[SDS((s, w), bf16)] * 3 + [SDS((s, w), f32)] * 3,
                          compiler_params=_params("parallel"), name=name)(dmixed, *os_, *lses, head_ones)


def attn_bwd(qkvg, do, lse_phased, dterm, g, dil, name, rider=None):
    a, gi = _phases(qkvg, g, dil)
    s = qkvg.shape[1]
    l = s // dil
    nb = l // SPAN
    phased = lambda t: t.reshape(l, dil * GROUP_WIDTH)

    def body(*refs):
        first = jnp.logical_and(pl.program_id(0) == 0, pl.program_id(1) == 0)
        last = jnp.logical_and(pl.program_id(0) == dil - 1, pl.program_id(1) == nb - 1)
        own, ride_start, ride_finish = _rider_run(rider, refs, 8, 5, first, last)
        (q_ref, ko_ref, kp_ref, vo_ref, vp_ref, do_ref, lse_ref, dt_ref, dq_ref, dko_ref, dkp_ref, dvo_ref, dvp_ref,
         k_scr, v_scr) = own
        ride_start()
        mask = _attn_mask(pl.program_id(1))
        scale = HEAD_DIM ** -0.5
        k_scr[0:SPAN, :] = kp_ref[...]
        k_scr[SPAN:2 * SPAN, :] = ko_ref[...]
        v_scr[0:SPAN, :] = vp_ref[...]
        v_scr[SPAN:2 * SPAN, :] = vo_ref[...]
        for h in range(N_SLOTS):
            hs = slice(h * HEAD_DIM, (h + 1) * HEAD_DIM)
            one = slice(h * HEAD_DIM, h * HEAD_DIM + 1)
            q, kk, dov = q_ref[:, hs], k_scr[:, hs], do_ref[:, hs]
            p = jnp.exp(jnp.where(mask, _dot_nt(q, kk) * scale - lse_ref[:, one], NEG_BIG))
            ds = (p * (_dot_nt(dov, v_scr[:, hs]) - dt_ref[:, one]) * scale).astype(bf16)
            dq_ref[:, hs] = jnp.dot(ds, kk, preferred_element_type=f32).astype(dq_ref.dtype)
            dk = _dot_tn(ds, q).astype(dko_ref.dtype)
            dv = _dot_tn(p.astype(bf16), dov).astype(dvo_ref.dtype)
            dkp_ref[:, hs] = dk[:SPAN]
            dko_ref[:, hs] = dk[SPAN:]
            dvp_ref[:, hs] = dv[:SPAN]
            dvo_ref[:, hs] = dv[SPAN:]
        ride_finish()

    blk = pl.BlockSpec((SPAN, GROUP_WIDTH), lambda r, j: (j, r))
    r_in, r_out, r_shapes, r_scratch, r_alias, r_args = _rider_specs(rider, 8, 5)
    outs = pl.pallas_call(
        body, grid=(dil, nb), in_specs=_attn_in_specs(gi) + [blk, blk, blk] + r_in, out_specs=[blk] * 5 + r_out,
        out_shape=[SDS((l, dil * GROUP_WIDTH), bf16)] * 5 + r_shapes,
        scratch_shapes=[pltpu.VMEM((2 * SPAN, GROUP_WIDTH), bf16)] * 2 + r_scratch, input_output_aliases=r_alias,
        compiler_params=_params("arbitrary", "arbitrary"), name=name)(a, a, a, a, a, phased(do), lse_phased, phased(dterm), *r_args)
    return [t.reshape(s, GROUP_WIDTH) for t in outs[:5]] + [outs[5:]]


def dqkv_assemble(parts, pair_t, tabs, name):
    s = parts[0][0].shape[0]
    nblk = s // SPAN
    width = 3 * N_GROUPS * GROUP_WIDTH

    def body(*refs):
        ins, (p_ref, tc_ref, ts_ref, o_ref) = refs[:5 * N_GROUPS], refs[5 * N_GROUPS:]
        pair, tc, ts = p_ref[...], tc_ref[...], ts_ref[...]
        i = pl.program_id(0)
        for g, (_, dil) in enumerate(DILATED_GROUPS):
            dq, dko, dkp, dvo, dvp = ins[5 * g:5 * g + 5]
            has_next = i + dil < nblk
            dk = dko[...].astype(f32) + jnp.where(has_next, dkp[...].astype(f32), 0.0)
            dv = dvo[...].astype(f32) + jnp.where(has_next, dvp[...].astype(f32), 0.0)
            for kind, val in enumerate((_rotate(dq[...].astype(f32), pair, tc, ts), _rotate(dk, pair, tc, ts), dv)):
                base = (kind * N_GROUPS + g) * GROUP_WIDTH
                o_ref[:, base:base + GROUP_WIDTH] = val.astype(o_ref.dtype)

    here = _rows(SPAN, GROUP_WIDTH)
    in_specs, args = [], []
    for g, (_, dil) in enumerate(DILATED_GROUPS):
        ahead = pl.BlockSpec((SPAN, GROUP_WIDTH), functools.partial(lambda i, dil: (jnp.minimum(i + dil, nblk - 1), 0), dil=dil))
        in_specs += [here, here, ahead, here, ahead]
        args += list(parts[g])
    tab = _rows(SPAN, LANES)
    return pl.pallas_call(body, grid=(nblk,), in_specs=in_specs + [_fixed(GROUP_WIDTH, GROUP_WIDTH), tab, tab],
                          out_specs=_rows(SPAN, width), out_shape=SDS((s, width), bf16),
                          compiler_params=_params("parallel"), name=name)(*args, pair_t, *tabs)


def ffn_act_fwd(u, wb, name):
    _, nbk, s, c = u.shape
    tr = min(s, 256)

    def body(u_ref, h_ref, wb_ref, a_ref, ug_ref, su, sg):
        first = pl.program_id(1) == 0
        for lc in range(c // LANES):
            ln = slice(lc * LANES, (lc + 1) * LANES)
            for half, scr in enumerate((su, sg)):
                scr[0:FFN_HALO, ln] = jnp.where(first, 0.0, h_ref[half, :, ln].astype(f32))
            for r0 in range(0, tr, STRIP_ROWS):
                rows = slice(r0, r0 + STRIP_ROWS)
                conv = []
                for half, scr in enumerate((su, sg)):
                    xv = u_ref[half, rows, ln].astype(f32)
                    scr[FFN_HALO + r0:FFN_HALO + r0 + STRIP_ROWS, ln] = xv
                    acc = wb_ref[half, FFN_CONV:FFN_CONV + 1, ln] + wb_ref[half, FFN_CONV - 1:FFN_CONV, ln] * xv
                    for k in range(FFN_CONV - 1):
                        acc = acc + wb_ref[half, k:k + 1, ln] * scr[pl.ds(FFN_HALO + r0 - (FFN_CONV - 1) + k, STRIP_ROWS), ln]
                    ug_ref[half, rows, ln] = acc.astype(ug_ref.dtype)
                    conv.append(acc)
                up, gate = conv
                a_ref[rows, ln] = (gate * _sigmoid(gate) * up).astype(a_ref.dtype)

    both = pl.BlockSpec((2, None, tr, c), lambda p, i: (0, p, i, 0))
    return pl.pallas_call(
        body, grid=(nbk, s // tr),
        in_specs=[both,
                  pl.BlockSpec((2, None, FFN_HALO, c), lambda p, i: (0, p, jnp.maximum(i * (tr // FFN_HALO) - 1, 0), 0)),
                  pl.BlockSpec((None, 2, 8, c), lambda p, i: (p, 0, 0, 0))],
        out_specs=[pl.BlockSpec((None, tr, c), lambda p, i: (p, i, 0)), both],
        out_shape=[SDS((nbk, s, c), bf16), SDS(u.shape, bf16)],
        scratch_shapes=[pltpu.VMEM((tr + FFN_HALO, c), f32)] * 2,
        compiler_params=_params("parallel", "arbitrary"), name=name)(u, u, wb)


def ffn_act_bwd(da, ug, u, wb, name, rider=None):
    _, nbk, s, c = u.shape
    tr = min(s, 256)
    nt = s // tr

    def body(*refs):
        step = pl.program_id(1)
        first = jnp.logical_and(pl.program_id(0) == 0, step == 0)
        last = jnp.logical_and(pl.program_id(0) == nbk - 1, step == nt - 1)
        (da_ref, ug_ref, u_ref, wb_ref, du_ref, dwb_ref, eu, eg), ride_start, ride_finish = _rider_run(rider, refs, 4, 2, first, last)
        ride_start()

        @pl.when(step == 0)
        def _():
            eu[tr:tr + FFN_TAIL, :] = jnp.zeros((FFN_TAIL, c), f32)
            eg[tr:tr + FFN_TAIL, :] = jnp.zeros((FFN_TAIL, c), f32)
            dwb_ref[...] = jnp.zeros(dwb_ref.shape, f32)

        fold = lambda t: jnp.sum(t.reshape(STRIP_ROWS // 8, 8, LANES), axis=0)
        for lc in range(c // LANES):
            ln = slice(lc * LANES, (lc + 1) * LANES)
            sums = [[jnp.zeros((8, LANES), f32) for _ in range(FFN_CONV + 1)] for _ in range(2)]
            for r0 in reversed(range(0, tr, STRIP_ROWS)):
                rows = slice(r0, r0 + STRIP_ROWS)
                up, gate = ug_ref[0, rows, ln].astype(f32), ug_ref[1, rows, ln].astype(f32)
                sig = _sigmoid(gate)
                dav = da_ref[rows, ln].astype(f32)
                grads = (dav * (gate * sig), dav * up * (sig * (1.0 + gate * (1.0 - sig))))
                for half, ext in enumerate((eu, eg)):
                    dv = grads[half]
                    ext[rows, ln] = dv
                    xv = u_ref[half, rows, ln].astype(f32)
                    acc = None
                    for k in range(FFN_CONV):
                        ahead = dv if k == FFN_CONV - 1 else ext[pl.ds(r0 + FFN_CONV - 1 - k, STRIP_ROWS), ln]
                        term = wb_ref[half, k:k + 1, ln] * ahead
                        acc = term if acc is None else acc + term
                        sums[half][k] = sums[half][k] + fold(xv * ahead)
                    sums[half][FFN_CONV] = sums[half][FFN_CONV] + fold(dv)
                    du_ref[half, rows, ln] = acc.astype(du_ref.dtype)
            for half, ext in enumerate((eu, eg)):
                ext[tr:tr + FFN_TAIL, ln] = ext[0:FFN_TAIL, ln]
                for k in range(FFN_CONV + 1):
                    dwb_ref[half, k:k + 1, ln] += jnp.sum(sums[half][k], axis=0, keepdims=True)
        ride_finish()

    rev = lambda i: nt - 1 - i
    both = pl.BlockSpec((2, None, tr, c), lambda p, i: (0, p, rev(i), 0))
    r_in, r_out, r_shapes, r_scratch, r_alias, r_args = _rider_specs(rider, 4, 2)
    outs = pl.pallas_call(
        body, grid=(nbk, nt),
        in_specs=[pl.BlockSpec((None, tr, c), lambda p, i: (p, rev(i), 0)), both, both,
                  pl.BlockSpec((None, 2, 8, c), lambda p, i: (p, 0, 0, 0))] + r_in,
        out_specs=[both, pl.BlockSpec((None, 2, 8, c), lambda p, i: (p, 0, 0, 0))] + r_out,
        out_shape=[SDS((2, nbk, s, c), bf16), SDS((nbk, 2, 8, c), f32)] + r_shapes,
        scratch_shapes=[pltpu.VMEM((tr + FFN_TAIL, c), f32)] * 2 + r_scratch, input_output_aliases=r_alias,
        compiler_params=_params("arbitrary", "arbitrary"), name=name)(da, ug, u, wb, *r_args)
    return outs[0], outs[1], outs[2:]


def _glu(zv, c):
    return zv[:, :c] * _sigmoid(zv[:, c:])


def _conv_fill(z_ref, h_ref, scr, first, tr, c):
    scr[0:CONV_HALO, :] = jnp.where(first, 0.0, _glu(h_ref[...], c))
    scr[CONV_HALO:CONV_HALO + tr, :] = _glu(z_ref[...], c)


def _conv_taps(b):
    return [(a, CONV_KERNEL - 1 - 8 * a - b) for a in range(CONV_HALO // 8) if CONV_KERNEL - 1 - 8 * a - b >= 0]


def _layernorm_parts(cv):
    mu = jnp.mean(cv, axis=-1, keepdims=True)
    cen = cv - mu
    rstd = lax.rsqrt(jnp.mean(cen * cen, axis=-1, keepdims=True) + EPS)
    return cen * rstd, rstd


def conv_module_fwd(z, wdw, vecs, name):
    s, c2 = z.shape
    c = c2 // 2
    tr = min(s, 256)

    def body(z_ref, h_ref, w_ref, v_ref, c_ref, s_ref, scr, zb):
        _conv_fill(z_ref, h_ref, scr, pl.program_id(0) == 0, tr, c)
        acc = jnp.broadcast_to(v_ref[0:1, :], (tr, c))
        for b in range(8):
            part = None
            for a, j in _conv_taps(b):
                term = w_ref[j:j + 1, :] * scr[pl.ds(CONV_HALO - 8 - 8 * a, tr + 8), :]
                part = term if part is None else part + term
            if b == 0:
                acc = acc + part[8:]
            else:
                zb[...] = part
                acc = acc + zb[pl.ds(8 - b, tr), :]
        c_ref[...] = acc
        chat, _ = _layernorm_parts(acc)
        ln = chat * v_ref[1:2, :] + v_ref[2:3, :]
        s_ref[...] = (ln * _sigmoid(ln)).astype(s_ref.dtype)

    return pl.pallas_call(
        body, grid=(s // tr,),
        in_specs=[_rows(tr, c2), pl.BlockSpec((CONV_HALO, c2), lambda i: (jnp.maximum(i * (tr // CONV_HALO) - 1, 0), 0)),
                  _fixed(CONV_HALO, c), _fixed(8, c)],
        out_specs=[_rows(tr, c), _rows(tr, c)], out_shape=[SDS((s, c), f32), SDS((s, c), bf16)],
        scratch_shapes=[pltpu.VMEM((tr + CONV_HALO, c), f32), pltpu.VMEM((tr + 8, c), f32)],
        compiler_params=_params("arbitrary"), name=name)(z, z, wdw, vecs)


def conv_module_bwd(ds, cpre, z, wdw, vecs, name):
    s, c2 = z.shape
    c = c2 // 2
    tr = min(s, 256)
    nt = s // tr

    def body(ds_ref, c_ref, z_ref, w_ref, v_ref, dz_ref, dw_ref, dv_ref, db_ref, ext, dwp):
        step = pl.program_id(0)

        @pl.when(step == 0)
        def _():
            ext[tr:tr + CONV_HALO, :] = jnp.zeros((CONV_HALO, c), f32)
            dwp[...] = jnp.zeros(dwp.shape, f32)
            dv_ref[...] = jnp.zeros(dv_ref.shape, f32)
            db_ref[...] = jnp.zeros(db_ref.shape, f32)

        gain, bias = v_ref[1:2, :], v_ref[2:3, :]
        fold16 = lambda t: t[:8] + t[8:]
        sums = [jnp.zeros((8, c), f32) for _ in range(3)]
        for r0 in range(0, tr, 16):
            rows = slice(r0, r0 + 16)
            chat, rstd = _layernorm_parts(c_ref[rows, :])
            ln = chat * gain + bias
            sig = _sigmoid(ln)
            dln = ds_ref[rows, :].astype(f32) * (sig * (1.0 + ln * (1.0 - sig)))
            gy = dln * gain
            dc = rstd * (gy - jnp.mean(gy, axis=-1, keepdims=True) - chat * jnp.mean(gy * chat, axis=-1, keepdims=True))
            ext[rows, :] = dc
            for k, t in enumerate((dc, dln * chat, dln)):
                sums[k] = sums[k] + fold16(t)
        for k in range(3):
            dv_ref[k:k + 1, :] += jnp.sum(sums[k], axis=0, keepdims=True)

        fold = lambda t: jnp.sum(t.reshape(STRIP_ROWS // 8, 8, LANES), axis=0)
        for lc in range(c // LANES):
            ln_a = slice(lc * LANES, (lc + 1) * LANES)
            ln_g = slice(c + lc * LANES, c + (lc + 1) * LANES)
            dbs = [jnp.zeros((8, LANES), f32) for _ in range(2)]
            for r0 in range(0, tr, STRIP_ROWS):
                rows = slice(r0, r0 + STRIP_ROWS)
                a, sg = z_ref[rows, ln_a], _sigmoid(z_ref[rows, ln_g])
                uv = a * sg
                du = jnp.zeros((STRIP_ROWS, LANES), f32)
                for b in range(8):
                    src = ext[pl.ds(r0 + b, STRIP_ROWS + CONV_HALO - 8), ln_a]
                    for a8, j in _conv_taps(b):
                        ahead = src[8 * a8:8 * a8 + STRIP_ROWS]
                        du = du + w_ref[j:j + 1, ln_a] * ahead
                        dwp[8 * j:8 * j + 8, ln_a] += fold(uv * ahead)
                da = du * sg
                dg = du * a * (sg * (1.0 - sg))
                dz_ref[rows, ln_a] = da.astype(dz_ref.dtype)
                dz_ref[rows, ln_g] = dg.astype(dz_ref.dtype)
                dbs = [dbs[0] + fold(da), dbs[1] + fold(dg)]
            db_ref[:, ln_a] += jnp.sum(dbs[0], axis=0, keepdims=True)
            db_ref[:, ln_g] += jnp.sum(dbs[1], axis=0, keepdims=True)
        ext[tr:tr + CONV_HALO, :] = ext[0:CONV_HALO, :]

        @pl.when(step == nt - 1)
        def _():
            for j in range(CONV_HALO):
                dw_ref[j:j + 1, :] = jnp.sum(dwp[8 * j:8 * j + 8, :], axis=0, keepdims=True)

    rev = lambda i: nt - 1 - i
    back = lambda d: pl.BlockSpec((tr, d), lambda i: (rev(i), 0))
    return pl.pallas_call(
        body, grid=(nt,),
        in_specs=[back(c), back(c), back(c2), _fixed(CONV_HALO, c), _fixed(8, c)],
        out_specs=[back(c2), _fixed(CONV_HALO, c), _fixed(8, c), _fixed(1, c2)],
        out_shape=[SDS((s, c2), bf16), SDS((CONV_HALO, c), f32), SDS((8, c), f32), SDS((1, c2), f32)],
        scratch_shapes=[pltpu.VMEM((tr + CONV_HALO, c), f32), pltpu.VMEM((8 * CONV_HALO, c), f32)],
        compiler_params=_params("arbitrary"), name=name)(ds, cpre, z, wdw, vecs)


def _tile2d(r, n):
    tn = n if n <= 2048 else 1024
    tr = r
    while tr * tn * 4 > (1 << 21) and tr % 16 == 0:
        tr //= 2
    assert r % tr == 0 and n % tn == 0
    return tr, tn


def adamw(w, g, m, v, name):
    r, n = w.shape
    tr, tn = _tile2d(r, n)

    def body(w_ref, g_ref, m_ref, v_ref, d_ref, nm_ref, nv_ref):
        gv = g_ref[...]
        nm = ADAM_B1 * m_ref[...] + (1.0 - ADAM_B1) * gv
        nv = ADAM_B2 * v_ref[...] + (1.0 - ADAM_B2) * (gv * gv)
        m_hat = nm / (1.0 - ADAM_B1 ** ADAM_STEP)
        v_hat = nv / (1.0 - ADAM_B2 ** ADAM_STEP)
        d_ref[...] = -ADAM_LR * (m_hat / (jnp.sqrt(v_hat) + ADAM_EPS) + ADAM_WD * w_ref[...])
        nm_ref[...] = nm
        nv_ref[...] = nv

    blk = pl.BlockSpec((tr, tn), lambda i, j: (i, j))
    return pl.pallas_call(body, grid=(r // tr, n // tn), in_specs=[blk] * 4, out_specs=[blk] * 3,
                          out_shape=[SDS((r, n), f32)] * 3, compiler_params=_params("parallel", "parallel"),
                          name=name)(w, g, m, v)


def add_core_halves(grad, got, where, name):
    _, _, rh, n = grad.shape
    tr, tn = _tile2d(rh, n)

    def body(w_ref, a_ref, b_ref, o_ref):
        o_ref[...] = (a_ref[...] + b_ref[...]).astype(o_ref.dtype)

    return pl.pallas_call(
        body,
        grid_spec=pltpu.PrefetchScalarGridSpec(
            num_scalar_prefetch=1, grid=(N_CHIPS, rh // tr, n // tn),
            in_specs=[pl.BlockSpec((None, None, tr, tn), lambda p, i, j, w_ref: (p, w_ref[0], i, j)),
                      pl.BlockSpec((None, tr, tn), lambda p, i, j, w_ref: (p, i, j))],
            out_specs=pl.BlockSpec((None, tr, tn), lambda p, i, j, w_ref: (p, i, j))),
        out_shape=SDS((N_CHIPS, rh, n), bf16), compiler_params=_params("parallel", "parallel", "parallel"),
        name=name)(where, grad, got)


def add_chip_parts(grad, got_core, got_chips, where, name):
    _, _, rh, n = grad.shape
    tr, tn = _tile2d(rh, n)

    def body(w_ref, a_ref, b_ref, g_ref, o_ref):
        acc = a_ref[...] + b_ref[...]
        for k in range(N_CHIPS - 1):
            acc = acc + g_ref[k].astype(f32)
        o_ref[...] = acc

    return pl.pallas_call(
        body,
        grid_spec=pltpu.PrefetchScalarGridSpec(
            num_scalar_prefetch=1, grid=(rh // tr, n // tn),
            in_specs=[pl.BlockSpec((None, None, tr, tn), lambda i, j, w_ref: (w_ref[1], w_ref[0], i, j)),
                      pl.BlockSpec((None, tr, tn), lambda i, j, w_ref: (w_ref[1], i, j)),
                      pl.BlockSpec((N_CHIPS - 1, tr, tn), lambda i, j, w_ref: (0, i, j))],
            out_specs=pl.BlockSpec((None, tr, tn), lambda i, j, w_ref: (w_ref[0], i, j))),
        out_shape=SDS((2, rh, n), f32), compiler_params=_params("parallel", "parallel"),
        name=name)(where, grad, got_core, got_chips)


def sum_devices(parts, name):
    nd, r, n = parts.shape

    def body(p_ref, o_ref):
        acc = p_ref[0]
        for k in range(1, nd):
            acc = acc + p_ref[k]
        o_ref[...] = acc

    return pl.pallas_call(body, out_shape=SDS((r, n), f32), name=name)(parts)


def _position():
    return lax.axis_index("x"), lax.axis_index("y"), lax.axis_index("c")


def _other_chips(x, y):
    return [(1 - x, y), (x, 1 - y), (1 - x, 1 - y)]


def _remote(src, dst, send, recv, to):
    return pltpu.make_async_remote_copy(src_ref=src, dst_ref=dst, send_sem=send, recv_sem=recv, device_id=to,
                                        device_id_type=MESH)


def gather_chips(bufs, name):
    n = len(bufs)
    nk = N_CHIPS - 1

    def body(*refs):
        bufs_ = refs[n:2 * n]
        send, recv = refs[2 * n:]
        x, y, c = _position()
        me = 2 * x + y
        chips = _other_chips(x, y)
        sends = []
        for t in range(n):
            for k, (px, py) in enumerate(chips):
                out = _remote(bufs_[t].at[me, c], bufs_[t].at[me, c], send.at[t, k], recv.at[t, k], (px, py, c))
                out.start()
                sends.append(out)
        for t in range(n):
            for k, (px, py) in enumerate(chips):
                piece = bufs_[t].at[2 * px + py, c]
                _remote(piece, piece, send.at[t, k], recv.at[t, k], (px, py, c)).wait_recv()
                on = _remote(piece, piece, send.at[t, nk + k], recv.at[t, nk + k], (x, y, 1 - c))
                on.start()
                sends.append(on)
        for t in range(n):
            for k, (px, py) in enumerate(chips):
                piece = bufs_[t].at[2 * px + py, 1 - c]
                _remote(piece, piece, send.at[t, nk + k], recv.at[t, nk + k], (x, y, 1 - c)).wait_recv()
        for cp in sends:
            cp.wait_send()

    return pl.pallas_call(
        body, in_specs=[ANY] * n, out_specs=[ANY] * n,
        out_shape=[SDS(a.shape, a.dtype) for a in bufs],
        input_output_aliases={t: t for t in range(n)},
        scratch_shapes=[pltpu.SemaphoreType.DMA((n, 2 * nk)), pltpu.SemaphoreType.DMA((n, 2 * nk))],
        name=name)(*bufs)


class Rider(NamedTuple):
    operands: tuple
    n_aliased: int
    out_shapes: tuple
    scratch: tuple
    start: Callable
    finish: Callable


def _rider_specs(rider, n_inputs, n_outputs):
    if rider is None:
        return [], [], [], [], {}, []
    aliased = [SDS(a.shape, a.dtype) for a in rider.operands[:rider.n_aliased]]
    outs = aliased + list(rider.out_shapes)
    aliases = {n_inputs + t: n_outputs + t for t in range(rider.n_aliased)}
    return [ANY] * len(rider.operands), [ANY] * len(outs), outs, list(rider.scratch), aliases, list(rider.operands)


def _rider_run(rider, refs, n_inputs, n_outputs, first, last):
    if rider is None:
        return refs, lambda: None, lambda: None
    n_op = len(rider.operands)
    n_out = rider.n_aliased + len(rider.out_shapes)
    own_in, r_in = refs[:n_inputs], refs[n_inputs:n_inputs + n_op]
    own_out = refs[n_inputs + n_op:n_inputs + n_op + n_outputs]
    r_out = refs[n_inputs + n_op + n_outputs:n_inputs + n_op + n_outputs + n_out]
    rest = refs[n_inputs + n_op + n_outputs + n_out:]
    n_sem = len(rider.scratch)
    sems, own_scratch = rest[len(rest) - n_sem:], rest[:len(rest) - n_sem]

    def start():
        pl.when(first)(lambda: rider.start(r_in, r_out, sems))

    def finish():
        pl.when(last)(lambda: rider.finish(r_in, r_out, sems))

    return list(own_in) + list(own_out) + list(own_scratch), start, finish


def gather_rider(bufs):
    n = len(bufs)
    nk = N_CHIPS - 1

    def ici(bufs_, send, recv, x, y, c, t, k, px, py, own):
        piece = bufs_[t].at[2 * x + y if own else 2 * px + py, c]
        return _remote(piece, piece, send.at[t, k], recv.at[t, k], (px, py, c))

    def d2d(bufs_, send, recv, x, y, c, t, k, px, py, mine):
        piece = bufs_[t].at[2 * px + py, c if mine else 1 - c]
        return _remote(piece, piece, send.at[t, nk + k], recv.at[t, nk + k], (x, y, 1 - c))

    def start(r_in, r_out, sems):
        send, recv = sems
        x, y, c = _position()
        for t in range(n):
            for k, (px, py) in enumerate(_other_chips(x, y)):
                ici(r_out, send, recv, x, y, c, t, k, px, py, True).start()

    def finish(r_in, r_out, sems):
        send, recv = sems
        x, y, c = _position()
        chips = _other_chips(x, y)
        for t in range(n):
            for k, (px, py) in enumerate(chips):
                ici(r_out, send, recv, x, y, c, t, k, px, py, False).wait_recv()
                d2d(r_out, send, recv, x, y, c, t, k, px, py, True).start()
        for t in range(n):
            for k, (px, py) in enumerate(chips):
                d2d(r_out, send, recv, x, y, c, t, k, px, py, False).wait_recv()
        for t in range(n):
            for k, (px, py) in enumerate(chips):
                ici(r_out, send, recv, x, y, c, t, k, px, py, True).wait_send()
                d2d(r_out, send, recv, x, y, c, t, k, px, py, True).wait_send()

    sems = (pltpu.SemaphoreType.DMA((n, 2 * nk)), pltpu.SemaphoreType.DMA((n, 2 * nk)))
    return Rider(tuple(bufs), n, (), sems, start, finish)


def scatter_rider(parts):
    n = len(parts)

    def copies(r_in, r_out, sems):
        send, recv = sems
        x, y, c = _position()
        return [_remote(r_in[t].at[2 * px + py], r_out[t].at[k], send.at[t, k], recv.at[t, k], (px, py, c))
                for t in range(n) for k, (px, py) in enumerate(_other_chips(x, y))]

    def start(r_in, r_out, sems):
        for cp in copies(r_in, r_out, sems):
            cp.start()

    def finish(r_in, r_out, sems):
        for cp in copies(r_in, r_out, sems):
            cp.wait()

    sems = (pltpu.SemaphoreType.DMA((n, N_CHIPS - 1)), pltpu.SemaphoreType.DMA((n, N_CHIPS - 1)))
    return Rider(tuple(parts), 0, tuple(SDS((N_CHIPS - 1,) + a.shape[1:], a.dtype) for a in parts), sems, start, finish)


def swap_core_halves(grads, name):
    n = len(grads)

    def body(*refs):
        ins, outs = refs[:n], refs[n:2 * n]
        send, recv = refs[2 * n:]
        x, y, c = _position()
        pending = []
        for t in range(n):
            out = _remote(ins[t].at[:, 1 - c], outs[t], send.at[t], recv.at[t], (x, y, 1 - c))
            out.start()
            pending.append(out.wait)
        for wait in pending:
            wait()

    return pl.pallas_call(
        body, in_specs=[ANY] * n, out_specs=[ANY] * n,
        out_shape=[SDS((a.shape[0],) + a.shape[2:], a.dtype) for a in grads],
        scratch_shapes=[pltpu.SemaphoreType.DMA((n,)), pltpu.SemaphoreType.DMA((n,))],
        name=name)(*grads)


def scatter_chips(parts, name):
    n = len(parts)

    def body(*refs):
        ins, outs = refs[:n], refs[n:2 * n]
        send, recv = refs[2 * n:]
        x, y, c = _position()
        pending = []
        for t in range(n):
            for k, (px, py) in enumerate(_other_chips(x, y)):
                out = _remote(ins[t].at[2 * px + py], outs[t].at[k], send.at[t, k], recv.at[t, k], (px, py, c))
                out.start()
                pending.append(out.wait)
        for wait in pending:
            wait()

    return pl.pallas_call(
        body, in_specs=[ANY] * n, out_specs=[ANY] * n,
        out_shape=[SDS((N_CHIPS - 1,) + a.shape[1:], a.dtype) for a in parts],
        scratch_shapes=[pltpu.SemaphoreType.DMA((n, N_CHIPS - 1)), pltpu.SemaphoreType.DMA((n, N_CHIPS - 1))],
        name=name)(*parts)


def share_core_halves(bufs, name):
    n = len(bufs)

    def body(*refs):
        bufs_ = refs[n:2 * n]
        send, recv = refs[2 * n:]
        x, y, c = _position()
        pending = []
        for t in range(n):
            out = _remote(bufs_[t].at[c], bufs_[t].at[c], send.at[t], recv.at[t], (x, y, 1 - c))
            out.start()
            pending.append(out.wait_send)
            other = bufs_[t].at[1 - c]
            pending.append(_remote(other, other, send.at[t], recv.at[t], (x, y, 1 - c)).wait_recv)
        for wait in pending:
            wait()

    return pl.pallas_call(
        body, in_specs=[ANY] * n, out_specs=[ANY] * n, out_shape=[SDS(a.shape, a.dtype) for a in bufs],
        input_output_aliases={t: t for t in range(n)},
        scratch_shapes=[pltpu.SemaphoreType.DMA((n,)), pltpu.SemaphoreType.DMA((n,))],
        name=name)(*bufs)


def gather_devices(v, name):
    flips = [(dx, dy, dc) for dx in (0, 1) for dy in (0, 1) for dc in (0, 1)][1:]

    def body(v_ref, o_ref, send, recv, local):
        x, y, c = _position()
        me = 4 * x + 2 * y + c
        own = pltpu.make_async_copy(v_ref, o_ref.at[me], local)
        own.start()
        pending = [own.wait]
        for k, (dx, dy, dc) in enumerate(flips):
            px, py, pc = x ^ dx, y ^ dy, c ^ dc
            out = _remote(v_ref, o_ref.at[me], send.at[k], recv.at[k], (px, py, pc))
            out.start()
            pending.append(out.wait_send)
            pending.append(_remote(v_ref, o_ref.at[4 * px + 2 * py + pc], send.at[k], recv.at[k], (px, py, pc)).wait_recv)
        for wait in pending:
            wait()

    return pl.pallas_call(
        body, in_specs=[ANY], out_specs=ANY, out_shape=SDS((8,) + v.shape, v.dtype),
        scratch_shapes=[pltpu.SemaphoreType.DMA((7,)), pltpu.SemaphoreType.DMA((7,)), pltpu.SemaphoreType.DMA],
        name=name)(v)


SMALL = ("norm_g", "conv_b_pw1", "conv_w_dw", "conv_b_dw", "conv_ln_g", "conv_ln_b", "conv_b_pw2", "ffn_w_dw")


def _pack_rows(arrs, rows):
    flat = jnp.concatenate([a.reshape(-1, LANES) for a in arrs], axis=0)
    return jnp.pad(flat, ((0, rows - flat.shape[0]), (0, 0)))


def _unpack_rows(packed, shapes):
    out, at = [], 0
    for shp in shapes:
        size = 1
        for dim in shp:
            size *= dim
        rows = size // LANES
        out.append(packed[..., at:at + rows, :].reshape(packed.shape[:-2] + tuple(shp)))
        at += rows
    return out


def _join_last(t):
    t = jnp.moveaxis(t, 0, -2)
    return t.reshape(t.shape[:-2] + (t.shape[-2] * t.shape[-1],))


def _split_last(t):
    t = t.reshape(t.shape[:-1] + (N_CHIPS, t.shape[-1] // N_CHIPS))
    return jnp.moveaxis(t, -2, 0)


def _rope_tables(positions):
    half = ROT_DIM // 2
    inv_freq = ROPE_THETA ** (-jnp.arange(half, dtype=f32) / half)
    ang = positions.astype(f32).reshape(-1, 1) * inv_freq
    cos, sin = jnp.cos(ang), jnp.sin(ang)
    s = ang.shape[0]
    rest = HEAD_DIM - ROT_DIM
    head = lambda t, fill: jnp.concatenate([t, t, jnp.full((s, rest), fill, f32)], axis=1)
    twice = lambda t: jnp.concatenate([t] * (LANES // HEAD_DIM), axis=1)
    return twice(head(cos, 1.0)), twice(head(sin, 0.0))


def _pairing_matrix():
    half = ROT_DIM // 2
    row = jnp.arange(GROUP_WIDTH)[:, None]
    col = jnp.arange(GROUP_WIDTH)[None, :]
    d = col % HEAD_DIM
    minus = jnp.logical_and(row == col + half, d < half)
    plus = jnp.logical_and(row == col - half, jnp.logical_and(d >= half, d < ROT_DIM))
    return (plus.astype(f32) - minus.astype(f32)).astype(bf16)


def _pad_rows(t, rows):
    return jnp.pad(t, ((0, rows - t.shape[0]), (0, 0)))


def _ffn_pack(w_dw, b_dw):
    t = jnp.concatenate([w_dw, b_dw[None]], axis=0)
    t = t.reshape(FFN_CONV + 1, 2, 2, -1)
    t = jnp.transpose(t, (2, 1, 0, 3))
    return jnp.pad(t, ((0, 0), (0, 0), (0, 8 - (FFN_CONV + 1)), (0, 0)))


def _ffn_unpack(d):
    t = jnp.transpose(d[:, :, :FFN_CONV + 1], (2, 1, 0, 3)).reshape(FFN_CONV + 1, -1)
    return t[:FFN_CONV], t[FFN_CONV]


def _ffn_block(x, g_pre, g_post, w_up, li, w_down, wb, tag):
    s = x.shape[0]
    h = prenorm(x, g_pre, f"{tag}_prenorm")
    u = mm_nn(h[None], w_up, li, None, N_CHIPS, bf16, 512, w_up.shape[-1], f"{tag}_up")
    u4 = u.reshape(2, 2, s, u.shape[-1])
    a, ug = ffn_act_fwd(u4, wb, f"{tag}_act")
    y = mm_nn(a, w_down, 0, None, 1, f32, 512, 512, f"{tag}_down")[0]
    return y, (h, u4, ug, a)


def _ffn_block_bwd(dy, saved, w_up, li, w_down, wb, tag, rider=None):
    h, u4, ug, a = saved
    d_down = mm_tn(a, dy[None], 1, 1024, a.shape[-1], 512, f"{tag}_dwdown")
    da = mm_nt(dy[None], w_down, 0, 2, bf16, 1024, a.shape[-1], w_down.shape[-1], f"{tag}_da")
    du4, dwb, rode = ffn_act_bwd(da, ug, u4, wb, f"{tag}_actbwd", rider)
    du = du4.reshape((N_CHIPS,) + du4.shape[2:])
    d_up = mm_tn(h[None], du, N_CHIPS, 1024, h.shape[-1], du.shape[-1], f"{tag}_dwup")
    dh = mm_nt(du, w_up, li, 1, f32, 1024, h.shape[-1], du.shape[-1], f"{tag}_dh")[0]
    return dh, d_up, d_down, dwb, rode


def kernel(x, positions, norm_g, attn_w_qkv, attn_w_o, conv_w_pw1, conv_b_pw1, conv_w_dw, conv_b_dw, conv_ln_g, conv_ln_b, conv_w_pw2, conv_b_pw2, ffn_w_up, ffn_w_dw, ffn_b_dw, ffn_w_down, loss_target, m_norm_g, m_attn_w_qkv, m_attn_w_o, m_conv_w_pw1, m_conv_b_pw1, m_conv_w_dw, m_conv_b_dw, m_conv_ln_g, m_conv_ln_b, m_conv_w_pw2, m_conv_b_pw2, m_ffn_w_up, m_ffn_w_dw, m_ffn_b_dw, m_ffn_w_down, v_norm_g, v_attn_w_qkv, v_attn_w_o, v_conv_w_pw1, v_conv_b_pw1, v_conv_w_dw, v_conv_b_dw, v_conv_ln_g, v_conv_ln_b, v_conv_w_pw2, v_conv_b_pw2, v_ffn_w_up, v_ffn_w_dw, v_ffn_b_dw, v_ffn_w_down):
    weights = dict(norm_g=norm_g, attn_w_qkv=attn_w_qkv, attn_w_o=attn_w_o, conv_w_pw1=conv_w_pw1, conv_b_pw1=conv_b_pw1,
                   conv_w_dw=conv_w_dw, conv_b_dw=conv_b_dw, conv_ln_g=conv_ln_g, conv_ln_b=conv_ln_b, conv_w_pw2=conv_w_pw2,
                   conv_b_pw2=conv_b_pw2, ffn_w_up=ffn_w_up, ffn_w_dw=ffn_w_dw, ffn_b_dw=ffn_b_dw, ffn_w_down=ffn_w_down)
    mom1 = dict(norm_g=m_norm_g, attn_w_qkv=m_attn_w_qkv, attn_w_o=m_attn_w_o, conv_w_pw1=m_conv_w_pw1, conv_b_pw1=m_conv_b_pw1,
                conv_w_dw=m_conv_w_dw, conv_b_dw=m_conv_b_dw, conv_ln_g=m_conv_ln_g, conv_ln_b=m_conv_ln_b, conv_w_pw2=m_conv_w_pw2,
                conv_b_pw2=m_conv_b_pw2, ffn_w_up=m_ffn_w_up, ffn_w_dw=m_ffn_w_dw, ffn_b_dw=m_ffn_b_dw, ffn_w_down=m_ffn_w_down)
    mom2 = dict(norm_g=v_norm_g, attn_w_qkv=v_attn_w_qkv, attn_w_o=v_attn_w_o, conv_w_pw1=v_conv_w_pw1, conv_b_pw1=v_conv_b_pw1,
                conv_w_dw=v_conv_w_dw, conv_b_dw=v_conv_b_dw, conv_ln_g=v_conv_ln_g, conv_ln_b=v_conv_ln_b, conv_w_pw2=v_conv_w_pw2,
                conv_b_pw2=v_conv_b_pw2, ffn_w_up=v_ffn_w_up, ffn_w_dw=v_ffn_w_dw, ffn_b_dw=v_ffn_b_dw, ffn_w_down=v_ffn_w_down)
    big = ("attn_w_qkv", "attn_w_o", "conv_w_pw1", "conv_w_pw2", "ffn_w_up", "ffn_w_down")
    xi, yi, ci = _position()
    chip = (2 * xi + yi).astype(jnp.int32).reshape(1)
    where = jnp.stack([ci, 2 * xi + yi]).astype(jnp.int32)

    x = x[0]
    target = loss_target[0]
    s, d = x.shape

    small_shapes = [weights[k].shape for k in SMALL]
    small_rows = -(-sum(weights[k].size for k in SMALL) // LANES // 8) * 8
    small_w = _pack_rows([weights[k] for k in SMALL], small_rows)
    def own_slot(shard):
        halves = shard.reshape(1, 2, -1, shard.shape[-1])
        return lax.dynamic_update_slice(lax.empty((N_CHIPS,) + halves.shape[1:], shard.dtype), halves, (chip[0], 0, 0, 0))

    early, late = big[:2], big[2:]
    gathered = gather_chips([own_slot(weights[k].astype(bf16)) for k in early] + [own_slot(small_w)], "gather_weights_early")
    gw = {k: t.reshape((N_CHIPS,) + weights[k].shape) for k, t in zip(early, gathered[:-1])}
    full_small = dict(zip(SMALL, [_join_last(t) for t in _unpack_rows(gathered[-1].reshape(N_CHIPS, small_rows, LANES), small_shapes)]))
    w_qkv, w_o = gw["attn_w_qkv"], gw["attn_w_o"]
    gains = full_small["norm_g"]
    gain = lambda i, k: gains[i, k][None]
    b_pw1 = full_small["conv_b_pw1"]
    conv_wdw = _pad_rows(full_small["conv_w_dw"][0], CONV_HALO)
    conv_vecs = _pad_rows(jnp.concatenate([full_small["conv_b_dw"], full_small["conv_ln_g"], full_small["conv_ln_b"]], axis=0), 8)
    b_pw2 = full_small["conv_b_pw2"]
    wbs = [_ffn_pack(full_small["ffn_w_dw"][i], ffn_b_dw[i]) for i in range(2)]
    tabs = _rope_tables(positions[0])
    pair = _pairing_matrix()
    head_ones = (jnp.arange(GROUP_WIDTH)[:, None] // HEAD_DIM == jnp.arange(GROUP_WIDTH)[None, :] // HEAD_DIM).astype(bf16)

    h0 = prenorm(x, gain(0, 0), "l0_prenorm")
    w_qkv_flat = jnp.transpose(w_qkv[:, 0], (1, 0, 2)).reshape(d, -1)
    qkvg, late_gathered = qkv_rope(h0, w_qkv_flat, pair, tabs, "qkv", gather_rider([own_slot(weights[k].astype(bf16)) for k in late]))
    gw.update({k: t.reshape((N_CHIPS,) + weights[k].shape) for k, t in zip(late, late_gathered)})
    w_pw1, w_up = gw["conv_w_pw1"], gw["ffn_w_up"]
    w_pw2 = gw["conv_w_pw2"].reshape(1, 1, -1, d)
    w_down = [gw["ffn_w_down"][:, i].reshape(1, 1, -1, d) for i in range(2)]
    att = [attn_fwd(qkvg, g, dil, f"attn_fwd{g}") for g, (_, dil) in enumerate(DILATED_GROUPS)]
    os_, lses, lses_phased = ([a[k] for a in att] for k in range(3))
    mixed = mix_fwd(os_, lses, "mix")
    y0 = mm_nn(mixed[None], w_o, 0, None, 1, f32, 1024, w_o.shape[-1], "attn_out")[0]
    x1 = postnorm_residual(x, y0, gain(0, 1), "l0_postnorm")
    y1, ffn0 = _ffn_block(x1, gain(0, 2), gain(0, 3), w_up, 0, w_down[0], wbs[0], "ffn0")
    x2 = postnorm_residual(x1, y1, gain(0, 3), "l0_ffn_postnorm")

    h2 = prenorm(x2, gain(1, 0), "l1_prenorm")
    z = mm_nn(h2[None], w_pw1, 0, b_pw1, 1, f32, 1024, w_pw1.shape[-1], "pw1")[0]
    cpre, sw = conv_module_fwd(z, conv_wdw, conv_vecs, "conv_fwd")
    y2 = mm_nn(sw[None], w_pw2, 0, b_pw2, 1, f32, 512, 512, "pw2")[0]
    x3 = postnorm_residual(x2, y2, gain(1, 1), "l1_postnorm")
    y3, ffn1 = _ffn_block(x3, gain(1, 2), gain(1, 3), w_up, 1, w_down[1], wbs[1], "ffn1")
    dx4, loss_row = final_loss(x3, y3, gain(1, 3), target, "loss")
    loss = lax.psum(loss_row[0, 0], ("x", "y", "c"))

    dgain = [[None] * 4 for _ in range(2)]
    dy3, dgain[1][3], _ = norm_bwd(y3, gain(1, 3), dx4, None, bf16, "l1_ffn_postnorm_bwd")
    dh, d_up1, d_down1, dwb1, _ = _ffn_block_bwd(dy3, ffn1, w_up, 1, w_down[1], wbs[1], "ffn1")
    dx3, dgain[1][2], _ = norm_bwd(x3, gain(1, 2), dh, dx4, f32, "l1_ffn_prenorm_bwd")

    dy2, dgain[1][1], d_b_pw2 = norm_bwd(y2, gain(1, 1), dx3, None, bf16, "l1_postnorm_bwd")
    d_pw2 = mm_tn(sw[None], dy2[None], 1, 1024, d, 512, "dw_pw2")
    dsw = mm_nt(dy2[None], w_pw2, 0, 1, f32, 1024, d, d, "d_swish")[0]
    dz, d_conv_wdw, d_conv_vecs, d_b_pw1 = conv_module_bwd(dsw, cpre, z, conv_wdw, conv_vecs, "conv_bwd")
    d_pw1 = mm_tn(h2[None], dz[None], N_CHIPS, 1024, d, w_pw1.shape[-1], "dw_pw1")
    dh = mm_nt(dz[None], w_pw1, 0, 1, f32, 1024, d, w_pw1.shape[-1], "d_h2")[0]
    dx2, dgain[1][0], _ = norm_bwd(x2, gain(1, 0), dh, dx3, f32, "l1_prenorm_bwd")

    def core_stage(named, tag):
        halves = [g.reshape(N_CHIPS, 2, -1, g.shape[-1]) for _, g in named]
        from_core = swap_core_halves(halves, f"swap_core_halves_{tag}")
        sums = [add_core_halves(a, b, where, f"add_core_{k}") for (k, _), a, b in zip(named, halves, from_core)]
        return halves, from_core, sums

    def chip_stage(named, halves, from_core, from_chips):
        return [add_chip_parts(a, b, r, where, f"add_chips_{k}") for (k, _), a, b, r in zip(named, halves, from_core, from_chips)]

    first_done = [("conv_w_pw1", d_pw1), ("conv_w_pw2", d_pw2[0].reshape(N_CHIPS, -1, d)), ("ffn_w_up1", d_up1),
                  ("ffn_w_down1", d_down1[0].reshape(N_CHIPS, -1, d))]
    halves_1, core_1, sums_1 = core_stage(first_done, "layer1")

    dy1, dgain[0][3], _ = norm_bwd(y1, gain(0, 3), dx2, None, bf16, "l0_ffn_postnorm_bwd")
    dh, d_up0, d_down0, dwb0, chips_1 = _ffn_block_bwd(dy1, ffn0, w_up, 0, w_down[0], wbs[0], "ffn0", scatter_rider(sums_1))
    mine_1 = chip_stage(first_done, halves_1, core_1, chips_1)
    mid_done = [("ffn_w_up0", d_up0), ("ffn_w_down0", d_down0[0].reshape(N_CHIPS, -1, d))]
    halves_m, core_m, sums_m = core_stage(mid_done, "ffn0")
    dx1, dgain[0][2], _ = norm_bwd(x1, gain(0, 2), dh, dx2, f32, "l0_ffn_prenorm_bwd")

    dy0, dgain[0][1], _ = norm_bwd(y0, gain(0, 1), dx1, None, bf16, "l0_postnorm_bwd")
    d_wo = mm_tn(mixed[None], dy0[None], N_CHIPS, 1024, GROUP_WIDTH, w_o.shape[-1], "dw_o")
    dmixed = mm_nt(dy0[None], w_o, 0, 1, f32, 1024, GROUP_WIDTH, w_o.shape[-1], "d_mixed")[0]
    mb = mix_bwd(dmixed, os_, lses, head_ones, "mix_bwd")
    parts = []
    for g, (_, dil) in enumerate(DILATED_GROUPS):
        *five, rode = attn_bwd(qkvg, mb[g], lses_phased[g], mb[3 + g], g, dil, f"attn_bwd{g}", scatter_rider(sums_m) if g == 0 else None)
        parts.append(five)
        if g == 0:
            mine_m = chip_stage(mid_done, halves_m, core_m, rode)
    dqkv = dqkv_assemble(parts, pair.T, tabs, "dqkv")
    d_qkv = mm_tn(h0[None], dqkv[None], N_CHIPS, 1024, d, w_qkv.shape[-1], "dw_qkv")
    dh = mm_nt(dqkv[None], w_qkv, 0, 1, f32, 1024, d, w_qkv.shape[-1], "d_h0")[0]
    grad_x, dgain[0][0], _ = norm_bwd(x, gain(0, 0), dh, dx1, f32, "l0_prenorm_bwd")

    d_ffn_dw, d_ffn_b = zip(*[_ffn_unpack(t) for t in (dwb0, dwb1)])
    gsmall = dict(
        norm_g=jnp.stack([jnp.concatenate(row, axis=0) for row in dgain], axis=0),
        conv_b_pw1=d_b_pw1, conv_w_dw=d_conv_wdw[None, :CONV_KERNEL], conv_b_dw=d_conv_vecs[0:1], conv_ln_g=d_conv_vecs[1:2],
        conv_ln_b=d_conv_vecs[2:3], conv_b_pw2=d_b_pw2, ffn_w_dw=jnp.stack(d_ffn_dw, axis=0))
    bias_rows = ffn_b_dw.size // LANES
    small_g = jnp.concatenate([jnp.concatenate([_pack_rows([_split_last(gsmall[k])[p] for k in SMALL], small_rows)
                                                for p in range(N_CHIPS)], axis=0),
                               jnp.stack(d_ffn_b, axis=0).reshape(bias_rows, LANES)], axis=0)

    small_sum = sum_devices(gather_devices(small_g, "gather_small_grads"), "sum_small_grads")
    my_small = lax.dynamic_slice_in_dim(small_sum, chip[0] * small_rows, small_rows, axis=0)
    g_small = jnp.concatenate([my_small, small_sum[N_CHIPS * small_rows:]], axis=0)

    last_done = [("attn_w_qkv", d_qkv), ("attn_w_o", d_wo)]
    halves_0, core_0, sums_0 = core_stage(last_done, "attn")
    mine_0 = chip_stage(last_done, halves_0, core_0, scatter_chips(sums_0, "scatter_chips_attn"))
    shared = share_core_halves(mine_1 + mine_m + mine_0, "share_core_halves")
    piece = {k: t.reshape(-1, t.shape[-1]) for (k, _), t in zip(first_done + mid_done + last_done, shared)}
    shard_grads = [piece["attn_w_qkv"], piece["attn_w_o"], piece["conv_w_pw1"], piece["conv_w_pw2"],
                   jnp.concatenate([piece["ffn_w_up0"], piece["ffn_w_up1"]], axis=0),
                   jnp.concatenate([piece["ffn_w_down0"], piece["ffn_w_down1"]], axis=0)]

    grads, deltas, new_m, new_v = {}, {}, {}, {}
    for k, g2 in zip(big, shard_grads):
        shp = weights[k].shape
        dl, nm, nv = adamw(weights[k].reshape(g2.shape), g2, mom1[k].reshape(g2.shape), mom2[k].reshape(g2.shape), f"adamw_{k}")
        grads[k], deltas[k], new_m[k], new_v[k] = (t.reshape(shp) for t in (g2, dl, nm, nv))
    pack_state = lambda src: jnp.concatenate([_pack_rows([src[k] for k in SMALL], small_rows), src["ffn_b_dw"].reshape(bias_rows, LANES)], axis=0)
    small_out = (g_small,) + tuple(adamw(pack_state(weights), g_small, pack_state(mom1), pack_state(mom2), "adamw_small"))
    for dst, packed in zip((grads, deltas, new_m, new_v), small_out):
        for k, t in zip(SMALL, _unpack_rows(packed[:small_rows], small_shapes)):
            dst[k] = t
        dst["ffn_b_dw"] = packed[small_rows:].reshape(ffn_b_dw.shape)

    order = ("norm_g", "attn_w_qkv", "attn_w_o", "conv_w_pw1", "conv_b_pw1", "conv_w_dw", "conv_b_dw", "conv_ln_g", "conv_ln_b",
             "conv_w_pw2", "conv_b_pw2", "ffn_w_up", "ffn_w_dw", "ffn_b_dw", "ffn_w_down")
    return (loss, grad_x[None], *[grads[k] for k in order], *[deltas[k] for k in order], *[new_m[k] for k in order],
            *[new_v[k] for k in order])
```

```python
import functools
from typing import Callable, NamedTuple

import jax
import jax.numpy as jnp
from jax import lax
from jax.experimental import pallas as pl
from jax.experimental.pallas import tpu as pltpu

f32 = jnp.float32
bf16 = jnp.bfloat16
SDS = jax.ShapeDtypeStruct

EPS = 1e-6
HEAD_DIM = 64
N_SLOTS = 8
GROUP_WIDTH = N_SLOTS * HEAD_DIM
DILATED_GROUPS = ((128, 1), (512, 4), (2048, 16))
N_GROUPS = 3
SPAN = 128
ROT_DIM = HEAD_DIM // 4
ROPE_THETA = 500000.0
CONV_KERNEL = 31
CONV_HALO = 32
FFN_CONV = 3
FFN_HALO = 16
FFN_TAIL = 8
STRIP_ROWS = 64
ADAM_LR, ADAM_B1, ADAM_B2, ADAM_EPS, ADAM_WD, ADAM_STEP = 0.001, 0.9, 0.999, 1e-08, 0.01, 10
LANES = 128
N_CHIPS = 4
VMEM_LIMIT_BYTES = 56 * 1024 * 1024
NEG_BIG = -1e30
MESH = pl.DeviceIdType.MESH
ANY = pl.BlockSpec(memory_space=pl.ANY)


def _params(*sem):
    return pltpu.CompilerParams(dimension_semantics=sem, vmem_limit_bytes=VMEM_LIMIT_BYTES)


def _sigmoid(v):
    return 1.0 / (1.0 + jnp.exp(-v))


def _dot_nt(a, b):
    return lax.dot_general(a, b, (((1,), (1,)), ((), ())), preferred_element_type=f32)


def _dot_tn(a, b):
    return lax.dot_general(a, b, (((0,), (0,)), ((), ())), preferred_element_type=f32)


def mm_nn(x, w, li, bias, out_blocks, out_dtype, tm, tn, name):
    nq, m, kq = x.shape
    p, _, k, n = w.shape
    assert k == nq * kq and n % tn == 0 and m % tm == 0
    on = p * n // out_blocks
    assert on % tn == 0
    nj, onj = n // tn, on // tn

    def body(*refs):
        if bias is None:
            x_ref, w_ref, o_ref = refs
            b_ref = None
        else:
            x_ref, w_ref, b_ref, o_ref = refs
        acc = jnp.dot(x_ref[0], w_ref[0:kq, :], preferred_element_type=f32)
        for q in range(1, nq):
            acc = acc + jnp.dot(x_ref[q], w_ref[q * kq:(q + 1) * kq, :], preferred_element_type=f32)
        if b_ref is not None:
            acc = acc + b_ref[...]
        o_ref[...] = acc.astype(o_ref.dtype)

    in_specs = [pl.BlockSpec((nq, tm, kq), lambda j, i: (0, i, 0)),
                pl.BlockSpec((None, None, k, tn), lambda j, i: (j // nj, li, 0, j % nj))]
    args = [x, w]
    if bias is not None:
        in_specs.append(pl.BlockSpec((1, tn), lambda j, i: (0, j)))
        args.append(bias)
    return pl.pallas_call(
        body, grid=(p * nj, m // tm), in_specs=in_specs,
        out_specs=pl.BlockSpec((None, tm, tn), lambda j, i: (j // onj, i, j % onj)),
        out_shape=SDS((out_blocks, m, on), out_dtype),
        compiler_params=_params("parallel", "parallel"), name=name)(*args)


def mm_nt(dy, w, li, out_blocks, out_dtype, tm, tk, tn, name):
    ob, m, on = dy.shape
    p, _, k, n = w.shape
    assert ob * on == p * n and n % tn == 0 and on % tn == 0 and k % tk == 0 and m % tm == 0
    kq = k // out_blocks
    assert kq % tk == 0
    nj, onj, kqj = n // tn, on // tn, kq // tk
    nr = p * nj

    def body(dy_ref, w_ref, o_ref, *scr):
        part = _dot_nt(dy_ref[...], w_ref[...])
        if nr == 1:
            o_ref[...] = part.astype(o_ref.dtype)
        else:
            acc_ref, = scr
            r = pl.program_id(2)

            @pl.when(r == 0)
            def _():
                acc_ref[...] = part

            @pl.when(r > 0)
            def _():
                acc_ref[...] += part

            @pl.when(r == nr - 1)
            def _():
                o_ref[...] = acc_ref[...].astype(o_ref.dtype)

    return pl.pallas_call(
        body, grid=(k // tk, m // tm, nr),
        in_specs=[pl.BlockSpec((None, tm, tn), lambda kt, i, r: (r // onj, i, r % onj)),
                  pl.BlockSpec((None, None, tk, tn), lambda kt, i, r: (r // nj, li, kt, r % nj))],
        out_specs=pl.BlockSpec((None, tm, tk), lambda kt, i, r: (kt // kqj, i, kt % kqj)),
        out_shape=SDS((out_blocks, m, kq), out_dtype),
        scratch_shapes=[] if nr == 1 else [pltpu.VMEM((tm, tk), f32)],
        compiler_params=_params("parallel", "parallel", "arbitrary"), name=name)(dy, w)


def mm_nt_whole(dy, w, li, out_dtype, tm, name):
    ob, m, on = dy.shape
    p, _, k, n = w.shape
    assert ob * on == p * n and ob in (1, p) and m % tm == 0

    def body(dy_ref, w_ref, o_ref):
        acc = None
        for pb in range(p):
            lhs = dy_ref[pb] if ob == p else dy_ref[0, :, pb * n:(pb + 1) * n]
            part = _dot_nt(lhs, w_ref[pb])
            acc = part if acc is None else acc + part
        o_ref[...] = acc.astype(o_ref.dtype)

    return pl.pallas_call(
        body, grid=(m // tm,),
        in_specs=[pl.BlockSpec((ob, tm, on), lambda i: (0, i, 0)), pl.BlockSpec((p, None, k, n), lambda i: (0, li, 0, 0))],
        out_specs=pl.BlockSpec((tm, k), lambda i: (i, 0)), out_shape=SDS((m, k), out_dtype),
        compiler_params=_params("parallel"), name=name)(dy, w)


def mm_tn(x, dy, p, tm, tk, tn, name):
    nq, m, kq = x.shape
    ob, _, on = dy.shape
    k = nq * kq
    n = ob * on // p
    assert n % tn == 0 and on % tn == 0 and kq % tk == 0 and m % tm == 0
    nj, onj, kqj = n // tn, on // tn, kq // tk

    def body(x_ref, dy_ref, o_ref):
        part = _dot_tn(x_ref[...], dy_ref[...])
        i = pl.program_id(2)

        @pl.when(i == 0)
        def _():
            o_ref[...] = part

        @pl.when(i > 0)
        def _():
            o_ref[...] += part

    return pl.pallas_call(
        body, grid=(k // tk, p * nj, m // tm),
        in_specs=[pl.BlockSpec((None, tm, tk), lambda kt, j, i: (kt // kqj, i, kt % kqj)),
                  pl.BlockSpec((None, tm, tn), lambda kt, j, i: (j // onj, i, j % onj))],
        out_specs=pl.BlockSpec((None, tk, tn), lambda kt, j, i: (j // nj, kt, j % nj)),
        out_shape=SDS((p, k, n), f32),
        compiler_params=_params("parallel", "parallel", "arbitrary"), name=name)(x, dy)


def _row_tile(s):
    return min(s, 512)


def _rows(tr, d):
    return pl.BlockSpec((tr, d), lambda i: (i, 0))


def _fixed(r, d):
    return pl.BlockSpec((r, d), lambda i: (0, 0))


def _rms(xv):
    return lax.rsqrt(jnp.mean(xv * xv, axis=-1, keepdims=True) + EPS)


def prenorm(x, g, name):
    s, d = x.shape
    tr = _row_tile(s)

    def body(x_ref, g_ref, o_ref):
        xv = x_ref[...]
        o_ref[...] = (xv * _rms(xv) * g_ref[...]).astype(o_ref.dtype)

    return pl.pallas_call(body, grid=(s // tr,), in_specs=[_rows(tr, d), _fixed(1, d)], out_specs=_rows(tr, d),
                          out_shape=SDS((s, d), bf16), compiler_params=_params("parallel"), name=name)(x, g)


def postnorm_residual(x, y, g, name):
    s, d = x.shape
    tr = _row_tile(s)

    def body(x_ref, y_ref, g_ref, o_ref):
        yv = y_ref[...]
        o_ref[...] = x_ref[...] + yv * _rms(yv) * g_ref[...]

    return pl.pallas_call(body, grid=(s // tr,), in_specs=[_rows(tr, d), _rows(tr, d), _fixed(1, d)],
                          out_specs=_rows(tr, d), out_shape=SDS((s, d), f32), compiler_params=_params("parallel"),
                          name=name)(x, y, g)


def norm_bwd(xin, g, dout, res, out_dtype, name):
    s, d = xin.shape
    tr = _row_tile(s)

    def body(*refs):
        if res is None:
            x_ref, g_ref, do_ref, dx_ref, dg_ref, cs_ref = refs
            r_ref = None
        else:
            x_ref, g_ref, do_ref, r_ref, dx_ref, dg_ref, cs_ref = refs
        xv = x_ref[...]
        r = _rms(xv)
        xh = xv * r
        dov = do_ref[...].astype(f32)
        gy = dov * g_ref[...]
        dx = r * (gy - xh * jnp.mean(gy * xh, axis=-1, keepdims=True))
        if r_ref is not None:
            dx = dx + r_ref[...]
        dx_ref[...] = dx.astype(dx_ref.dtype)
        dg = jnp.sum(dov * xh, axis=0, keepdims=True)
        cs = jnp.sum(dx, axis=0, keepdims=True)
        i = pl.program_id(0)

        @pl.when(i == 0)
        def _():
            dg_ref[...] = dg
            cs_ref[...] = cs

        @pl.when(i > 0)
        def _():
            dg_ref[...] += dg
            cs_ref[...] += cs

    in_specs = [_rows(tr, d), _fixed(1, d), _rows(tr, d)]
    args = [xin, g, dout]
    if res is not None:
        in_specs.append(_rows(tr, d))
        args.append(res)
    return pl.pallas_call(body, grid=(s // tr,), in_specs=in_specs,
                          out_specs=[_rows(tr, d), _fixed(1, d), _fixed(1, d)],
                          out_shape=[SDS((s, d), out_dtype), SDS((1, d), f32), SDS((1, d), f32)],
                          compiler_params=_params("arbitrary"), name=name)(*args)


def final_loss(x, y, g, target, name):
    s, d = x.shape
    tr = _row_tile(s)
    nt = s // tr

    def body(x_ref, y_ref, g_ref, t_ref, dx_ref, loss_ref, acc_ref):
        yv = y_ref[...]
        diff = x_ref[...] + yv * _rms(yv) * g_ref[...] - t_ref[...]
        dx_ref[...] = diff * (1.0 / d)
        sq = jnp.sum(diff * diff, axis=0, keepdims=True)
        i = pl.program_id(0)

        @pl.when(i == 0)
        def _():
            acc_ref[...] = sq

        @pl.when(i > 0)
        def _():
            acc_ref[...] += sq

        @pl.when(i == nt - 1)
        def _():
            total = jnp.sum(acc_ref[...], axis=1, keepdims=True) * (0.5 / d)
            loss_ref[...] = jnp.broadcast_to(total, (1, LANES))

    return pl.pallas_call(body, grid=(nt,), in_specs=[_rows(tr, d), _rows(tr, d), _fixed(1, d), _rows(tr, d)],
                          out_specs=[_rows(tr, d), _fixed(1, LANES)],
                          out_shape=[SDS((s, d), f32), SDS((1, LANES), f32)],
                          scratch_shapes=[pltpu.VMEM((1, d), f32)],
                          compiler_params=_params("arbitrary"), name=name)(x, y, g, target)


def _rotate(v, pair, tc, ts):
    partner = jnp.dot(v.astype(bf16), pair, preferred_element_type=f32)
    return jnp.concatenate([v[:, ch * LANES:(ch + 1) * LANES] * tc + partner[:, ch * LANES:(ch + 1) * LANES] * ts
                            for ch in range(GROUP_WIDTH // LANES)], axis=1)


def qkv_rope(h, w, pair, tabs, name, rider=None):
    s, d = h.shape
    tm = min(s, 512)
    n_kinds = 3
    grid = (n_kinds * N_GROUPS, s // tm)

    def body(*refs):
        first = jnp.logical_and(pl.program_id(0) == 0, pl.program_id(1) == 0)
        last = jnp.logical_and(pl.program_id(0) == grid[0] - 1, pl.program_id(1) == grid[1] - 1)
        (x_ref, w_ref, p_ref, tc_ref, ts_ref, o_ref), ride_start, ride_finish = _rider_run(rider, refs, 5, 1, first, last)
        ride_start()
        kind = pl.program_id(0) // N_GROUPS
        acc = jnp.dot(x_ref[...], w_ref[...], preferred_element_type=f32)

        @pl.when(kind < 2)
        def _():
            o_ref[...] = _rotate(acc, p_ref[...], tc_ref[...], ts_ref[...]).astype(o_ref.dtype)

        @pl.when(kind == 2)
        def _():
            o_ref[...] = acc.astype(o_ref.dtype)

        ride_finish()

    tab = pl.BlockSpec((tm, LANES), lambda j, i: (i, 0))
    r_in, r_out, r_shapes, r_scratch, r_alias, r_args = _rider_specs(rider, 5, 1)
    outs = pl.pallas_call(
        body, grid=grid,
        in_specs=[pl.BlockSpec((tm, d), lambda j, i: (i, 0)), pl.BlockSpec((d, GROUP_WIDTH), lambda j, i: (0, j)),
                  pl.BlockSpec((GROUP_WIDTH, GROUP_WIDTH), lambda j, i: (0, 0)), tab, tab] + r_in,
        out_specs=[pl.BlockSpec((None, tm, GROUP_WIDTH), lambda j, i: (j % N_GROUPS, i, j // N_GROUPS))] + r_out,
        out_shape=[SDS((N_GROUPS, s, n_kinds * GROUP_WIDTH), bf16)] + r_shapes,
        scratch_shapes=r_scratch, input_output_aliases=r_alias,
        compiler_params=_params("arbitrary", "arbitrary"), name=name)(h, w, pair, *tabs, *r_args)
    return outs[0], outs[1:]


def _attn_mask(j):
    row = lax.broadcasted_iota(jnp.int32, (SPAN, 2 * SPAN), 0)
    col = lax.broadcasted_iota(jnp.int32, (SPAN, 2 * SPAN), 1)
    prev = jnp.logical_and(jnp.logical_and(col < SPAN, col >= row), j > 0)
    return jnp.logical_or(prev, jnp.logical_and(col >= SPAN, col - SPAN <= row))


def _attn_in_specs(gi):
    def at(kind, prev):
        def index(r, j):
            return (gi, jnp.maximum(j - 1, 0) if prev else j, r * 3 + kind)
        return pl.BlockSpec((None, SPAN, GROUP_WIDTH), index)
    return [at(0, False), at(1, False), at(1, True), at(2, False), at(2, True)]


def _phases(qkvg, g, dil):
    if dil == 1:
        return qkvg, g
    _, s, w = qkvg.shape
    return qkvg[g].reshape(1, s // dil, dil * w), 0


def attn_fwd(qkvg, g, dil, name, rider=None):
    a, gi = _phases(qkvg, g, dil)
    s = qkvg.shape[1]
    l = s // dil
    nb = l // SPAN

    def body(*refs):
        first = jnp.logical_and(pl.program_id(0) == 0, pl.program_id(1) == 0)
        last = jnp.logical_and(pl.program_id(0) == dil - 1, pl.program_id(1) == nb - 1)
        own, ride_start, ride_finish = _rider_run(rider, refs, 5, 2, first, last)
        q_ref, ko_ref, kp_ref, vo_ref, vp_ref, o_ref, lse_ref, k_scr, v_scr = own
        ride_start()
        mask = _attn_mask(pl.program_id(1))
        k_scr[0:SPAN, :] = kp_ref[...]
        k_scr[SPAN:2 * SPAN, :] = ko_ref[...]
        v_scr[0:SPAN, :] = vp_ref[...]
        v_scr[SPAN:2 * SPAN, :] = vo_ref[...]
        for h in range(N_SLOTS):
            hs = slice(h * HEAD_DIM, (h + 1) * HEAD_DIM)
            sc = jnp.where(mask, _dot_nt(q_ref[:, hs], k_scr[:, hs]) * (HEAD_DIM ** -0.5), NEG_BIG)
            mx = jnp.max(sc, axis=-1, keepdims=True)
            p = jnp.exp(sc - mx)
            den = jnp.sum(p, axis=-1, keepdims=True)
            o_ref[:, hs] = jnp.dot(p.astype(bf16), v_scr[:, hs], preferred_element_type=f32) / den
            lse_ref[:, hs] = jnp.broadcast_to(mx + jnp.log(den), (SPAN, HEAD_DIM))
        ride_finish()

    out = pl.BlockSpec((SPAN, GROUP_WIDTH), lambda r, j: (j, r))
    r_in, r_out, r_shapes, r_scratch, r_alias, r_args = _rider_specs(rider, 5, 2)
    o, lse, *rode = pl.pallas_call(
        body, grid=(dil, nb), in_specs=_attn_in_specs(gi) + r_in, out_specs=[out, out] + r_out,
        out_shape=[SDS((l, dil * GROUP_WIDTH), f32)] * 2 + r_shapes,
        scratch_shapes=[pltpu.VMEM((2 * SPAN, GROUP_WIDTH), bf16)] * 2 + r_scratch, input_output_aliases=r_alias,
        compiler_params=_params("arbitrary", "arbitrary"), name=name)(a, a, a, a, a, *r_args)
    return o.reshape(s, GROUP_WIDTH), lse.reshape(s, GROUP_WIDTH), lse, rode


def _group_weights(lses):
    mx = jnp.maximum(jnp.maximum(lses[0], lses[1]), lses[2])
    es = [jnp.exp(v - mx) for v in lses]
    inv = 1.0 / (es[0] + es[1] + es[2])
    return [e * inv for e in es]


def mix_fwd(os_, lses, name):
    s, w = os_[0].shape
    tr = _row_tile(s)

    def body(o0, o1, o2, l0, l1, l2, out_ref):
        wg = _group_weights([l0[...], l1[...], l2[...]])
        out_ref[...] = (wg[0] * o0[...] + wg[1] * o1[...] + wg[2] * o2[...]).astype(out_ref.dtype)

    return pl.pallas_call(body, grid=(s // tr,), in_specs=[_rows(tr, w)] * 6, out_specs=_rows(tr, w),
                          out_shape=SDS((s, w), bf16), compiler_params=_params("parallel"), name=name)(*os_, *lses)


def mix_bwd(dmixed, os_, lses, head_ones, name):
    s, w = dmixed.shape
    tr = _row_tile(s)

    def head_sum(t, ones):
        hi = t.astype(bf16)
        lo = (t - hi.astype(f32)).astype(bf16)
        return jnp.dot(hi, ones, preferred_element_type=f32) + jnp.dot(lo, ones, preferred_element_type=f32)

    def body(dm_ref, o0, o1, o2, l0, l1, l2, ones_ref, d0, d1, d2, p0, p1, p2):
        dm = dm_ref[...]
        ones = ones_ref[...]
        wg = _group_weights([l0[...], l1[...], l2[...]])
        mean = sum(wg[k] * head_sum(dm * o[...], ones) for k, o in enumerate((o0, o1, o2)))
        for k, (d_ref, p_ref) in enumerate(((d0, p0), (d1, p1), (d2, p2))):
            d_ref[...] = (wg[k] * dm).astype(d_ref.dtype)
            p_ref[...] = wg[k] * mean

    return pl.pallas_call(body, grid=(s // tr,), in_specs=[_rows(tr, w)] * 7 + [_fixed(w, w)],
                          out_specs=[_rows(tr, w)] * 6,
                          out_shape=[SDS((s, w), bf16)] * 3 + [SDS((s, w), f32)] * 3,
                          compiler_params=_params("parallel"), name=name)(dmixed, *os_, *lses, head_ones)


def attn_bwd(qkvg, do, lse_phased, dterm, g, dil, name, rider=None):
    a, gi = _phases(qkvg, g, dil)
    s = qkvg.shape[1]
    l = s // dil
    nb = l // SPAN
    phased = lambda t: t.reshape(l, dil * GROUP_WIDTH)

    def body(*refs):
        first = jnp.logical_and(pl.program_id(0) == 0, pl.program_id(1) == 0)
        last = jnp.logical_and(pl.program_id(0) == dil - 1, pl.program_id(1) == nb - 1)
        own, ride_start, ride_finish = _rider_run(rider, refs, 8, 5, first, last)
        (q_ref, ko_ref, kp_ref, vo_ref, vp_ref, do_ref, lse_ref, dt_ref, dq_ref, dko_ref, dkp_ref, dvo_ref, dvp_ref,
         k_scr, v_scr) = own
        ride_start()
        mask = _attn_mask(pl.program_id(1))
        scale = HEAD_DIM ** -0.5
        k_scr[0:SPAN, :] = kp_ref[...]
        k_scr[SPAN:2 * SPAN, :] = ko_ref[...]
        v_scr[0:SPAN, :] = vp_ref[...]
        v_scr[SPAN:2 * SPAN, :] = vo_ref[...]
        for h in range(N_SLOTS):
            hs = slice(h * HEAD_DIM, (h + 1) * HEAD_DIM)
            one = slice(h * HEAD_DIM, h * HEAD_DIM + 1)
            q, kk, dov = q_ref[:, hs], k_scr[:, hs], do_ref[:, hs]
            p = jnp.exp(jnp.where(mask, _dot_nt(q, kk) * scale - lse_ref[:, one], NEG_BIG))
            ds = (p * (_dot_nt(dov, v_scr[:, hs]) - dt_ref[:, one]) * scale).astype(bf16)
            dq_ref[:, hs] = jnp.dot(ds, kk, preferred_element_type=f32).astype(dq_ref.dtype)
            dk = _dot_tn(ds, q).astype(dko_ref.dtype)
            dv = _dot_tn(p.astype(bf16), dov).astype(dvo_ref.dtype)
            dkp_ref[:, hs] = dk[:SPAN]
            dko_ref[:, hs] = dk[SPAN:]
            dvp_ref[:, hs] = dv[:SPAN]
            dvo_ref[:, hs] = dv[SPAN:]
        ride_finish()

    blk = pl.BlockSpec((SPAN, GROUP_WIDTH), lambda r, j: (j, r))
    r_in, r_out, r_shapes, r_scratch, r_alias, r_args = _rider_specs(rider, 8, 5)
    outs = pl.pallas_call(
        body, grid=(dil, nb), in_specs=_attn_in_specs(gi) + [blk, blk, blk] + r_in, out_specs=[blk] * 5 + r_out,
        out_shape=[SDS((l, dil * GROUP_WIDTH), bf16)] * 5 + r_shapes,
        scratch_shapes=[pltpu.VMEM((2 * SPAN, GROUP_WIDTH), bf16)] * 2 + r_scratch, input_output_aliases=r_alias,
        compiler_params=_params("arbitrary", "arbitrary"), name=name)(a, a, a, a, a, phased(do), lse_phased, phased(dterm), *r_args)
    return [t.reshape(s, GROUP_WIDTH) for t in outs[:5]] + [outs[5:]]


def dqkv_assemble(parts, pair_t, tabs, name):
    s = parts[0][0].shape[0]
    nblk = s // SPAN
    width = 3 * N_GROUPS * GROUP_WIDTH

    def body(*refs):
        ins, (p_ref, tc_ref, ts_ref, o_ref) = refs[:5 * N_GROUPS], refs[5 * N_GROUPS:]
        pair, tc, ts = p_ref[...], tc_ref[...], ts_ref[...]
        i = pl.program_id(0)
        for g, (_, dil) in enumerate(DILATED_GROUPS):
            dq, dko, dkp, dvo, dvp = ins[5 * g:5 * g + 5]
            has_next = i + dil < nblk
            dk = dko[...].astype(f32) + jnp.where(has_next, dkp[...].astype(f32), 0.0)
            dv = dvo[...].astype(f32) + jnp.where(has_next, dvp[...].astype(f32), 0.0)
            for kind, val in enumerate((_rotate(dq[...].astype(f32), pair, tc, ts), _rotate(dk, pair, tc, ts), dv)):
                base = (kind * N_GROUPS + g) * GROUP_WIDTH
                o_ref[:, base:base + GROUP_WIDTH] = val.astype(o_ref.dtype)

    here = _rows(SPAN, GROUP_WIDTH)
    in_specs, args = [], []
    for g, (_, dil) in enumerate(DILATED_GROUPS):
        ahead = pl.BlockSpec((SPAN, GROUP_WIDTH), functools.partial(lambda i, dil: (jnp.minimum(i + dil, nblk - 1), 0), dil=dil))
        in_specs += [here, here, ahead, here, ahead]
        args += list(parts[g])
    tab = _rows(SPAN, LANES)
    return pl.pallas_call(body, grid=(nblk,), in_specs=in_specs + [_fixed(GROUP_WIDTH, GROUP_WIDTH), tab, tab],
                          out_specs=_rows(SPAN, width), out_shape=SDS((s, width), bf16),
                          compiler_params=_params("parallel"), name=name)(*args, pair_t, *tabs)


def ffn_act_fwd(u, wb, name):
    _, nbk, s, c = u.shape
    tr = min(s, 256)

    def body(u_ref, h_ref, wb_ref, a_ref, ug_ref, su, sg):
        first = pl.program_id(1) == 0
        for lc in range(c // LANES):
            ln = slice(lc * LANES, (lc + 1) * LANES)
            for half, scr in enumerate((su, sg)):
                scr[0:FFN_HALO, ln] = jnp.where(first, 0.0, h_ref[half, :, ln].astype(f32))
            for r0 in range(0, tr, STRIP_ROWS):
                rows = slice(r0, r0 + STRIP_ROWS)
                conv = []
                for half, scr in enumerate((su, sg)):
                    xv = u_ref[half, rows, ln].astype(f32)
                    scr[FFN_HALO + r0:FFN_HALO + r0 + STRIP_ROWS, ln] = xv
                    acc = wb_ref[half, FFN_CONV:FFN_CONV + 1, ln] + wb_ref[half, FFN_CONV - 1:FFN_CONV, ln] * xv
                    for k in range(FFN_CONV - 1):
                        acc = acc + wb_ref[half, k:k + 1, ln] * scr[pl.ds(FFN_HALO + r0 - (FFN_CONV - 1) + k, STRIP_ROWS), ln]
                    ug_ref[half, rows, ln] = acc.astype(ug_ref.dtype)
                    conv.append(acc)
                up, gate = conv
                a_ref[rows, ln] = (gate * _sigmoid(gate) * up).astype(a_ref.dtype)

    both = pl.BlockSpec((2, None, tr, c), lambda p, i: (0, p, i, 0))
    return pl.pallas_call(
        body, grid=(nbk, s // tr),
        in_specs=[both,
                  pl.BlockSpec((2, None, FFN_HALO, c), lambda p, i: (0, p, jnp.maximum(i * (tr // FFN_HALO) - 1, 0), 0)),
                  pl.BlockSpec((None, 2, 8, c), lambda p, i: (p, 0, 0, 0))],
        out_specs=[pl.BlockSpec((None, tr, c), lambda p, i: (p, i, 0)), both],
        out_shape=[SDS((nbk, s, c), bf16), SDS(u.shape, bf16)],
        scratch_shapes=[pltpu.VMEM((tr + FFN_HALO, c), f32)] * 2,
        compiler_params=_params("parallel", "arbitrary"), name=name)(u, u, wb)


def ffn_act_bwd(da, ug, u, wb, name, rider=None):
    _, nbk, s, c = u.shape
    tr = min(s, 256)
    nt = s // tr

    def body(*refs):
        step = pl.program_id(1)
        first = jnp.logical_and(pl.program_id(0) == 0, step == 0)
        last = jnp.logical_and(pl.program_id(0) == nbk - 1, step == nt - 1)
        (da_ref, ug_ref, u_ref, wb_ref, du_ref, dwb_ref, eu, eg), ride_start, ride_finish = _rider_run(rider, refs, 4, 2, first, last)
        ride_start()

        @pl.when(step == 0)
        def _():
            eu[tr:tr + FFN_TAIL, :] = jnp.zeros((FFN_TAIL, c), f32)
            eg[tr:tr + FFN_TAIL, :] = jnp.zeros((FFN_TAIL, c), f32)
            dwb_ref[...] = jnp.zeros(dwb_ref.shape, f32)

        fold = lambda t: jnp.sum(t.reshape(STRIP_ROWS // 8, 8, LANES), axis=0)
        for lc in range(c // LANES):
            ln = slice(lc * LANES, (lc + 1) * LANES)
            sums = [[jnp.zeros((8, LANES), f32) for _ in range(FFN_CONV + 1)] for _ in range(2)]
            for r0 in reversed(range(0, tr, STRIP_ROWS)):
                rows = slice(r0, r0 + STRIP_ROWS)
                up, gate = ug_ref[0, rows, ln].astype(f32), ug_ref[1, rows, ln].astype(f32)
                sig = _sigmoid(gate)
                dav = da_ref[rows, ln].astype(f32)
                grads = (dav * (gate * sig), dav * up * (sig * (1.0 + gate * (1.0 - sig))))
                for half, ext in enumerate((eu, eg)):
                    dv = grads[half]
                    ext[rows, ln] = dv
                    xv = u_ref[half, rows, ln].astype(f32)
                    acc = None
                    for k in range(FFN_CONV):
                        ahead = dv if k == FFN_CONV - 1 else ext[pl.ds(r0 + FFN_CONV - 1 - k, STRIP_ROWS), ln]
                        term = wb_ref[half, k:k + 1, ln] * ahead
                        acc = term if acc is None else acc + term
                        sums[half][k] = sums[half][k] + fold(xv * ahead)
                    sums[half][FFN_CONV] = sums[half][FFN_CONV] + fold(dv)
                    du_ref[half, rows, ln] = acc.astype(du_ref.dtype)
            for half, ext in enumerate((eu, eg)):
                ext[tr:tr + FFN_TAIL, ln] = ext[0:FFN_TAIL, ln]
                for k in range(FFN_CONV + 1):
                    dwb_ref[half, k:k + 1, ln] += jnp.sum(sums[half][k], axis=0, keepdims=True)
        ride_finish()

    rev = lambda i: nt - 1 - i
    both = pl.BlockSpec((2, None, tr, c), lambda p, i: (0, p, rev(i), 0))
    r_in, r_out, r_shapes, r_scratch, r_alias, r_args = _rider_specs(rider, 4, 2)
    outs = pl.pallas_call(
        body, grid=(nbk, nt),
        in_specs=[pl.BlockSpec((None, tr, c), lambda p, i: (p, rev(i), 0)), both, both,
                  pl.BlockSpec((None, 2, 8, c), lambda p, i: (p, 0, 0, 0))] + r_in,
        out_specs=[both, pl.BlockSpec((None, 2, 8, c), lambda p, i: (p, 0, 0, 0))] + r_out,
        out_shape=[SDS((2, nbk, s, c), bf16), SDS((nbk, 2, 8, c), f32)] + r_shapes,
        scratch_shapes=[pltpu.VMEM((tr + FFN_TAIL, c), f32)] * 2 + r_scratch, input_output_aliases=r_alias,
        compiler_params=_params("arbitrary", "arbitrary"), name=name)(da, ug, u, wb, *r_args)
    return outs[0], outs[1], outs[2:]


def _glu(zv, c):
    return zv[:, :c] * _sigmoid(zv[:, c:])


def _conv_fill(z_ref, h_ref, scr, first, tr, c):
    scr[0:CONV_HALO, :] = jnp.where(first, 0.0, _glu(h_ref[...], c))
    scr[CONV_HALO:CONV_HALO + tr, :] = _glu(z_ref[...], c)


def _conv_taps(b):
    return [(a, CONV_KERNEL - 1 - 8 * a - b) for a in range(CONV_HALO // 8) if CONV_KERNEL - 1 - 8 * a - b >= 0]


def _layernorm_parts(cv):
    mu = jnp.mean(cv, axis=-1, keepdims=True)
    cen = cv - mu
    rstd = lax.rsqrt(jnp.mean(cen * cen, axis=-1, keepdims=True) + EPS)
    return cen * rstd, rstd


def conv_module_fwd(z, wdw, vecs, name):
    s, c2 = z.shape
    c = c2 // 2
    tr = min(s, 256)

    def body(z_ref, h_ref, w_ref, v_ref, c_ref, s_ref, scr, zb):
        _conv_fill(z_ref, h_ref, scr, pl.program_id(0) == 0, tr, c)
        acc = jnp.broadcast_to(v_ref[0:1, :], (tr, c))
        for b in range(8):
            part = None
            for a, j in _conv_taps(b):
                term = w_ref[j:j + 1, :] * scr[pl.ds(CONV_HALO - 8 - 8 * a, tr + 8), :]
                part = term if part is None else part + term
            if b == 0:
                acc = acc + part[8:]
            else:
                zb[...] = part
                acc = acc + zb[pl.ds(8 - b, tr), :]
        c_ref[...] = acc
        chat, _ = _layernorm_parts(acc)
        ln = chat * v_ref[1:2, :] + v_ref[2:3, :]
        s_ref[...] = (ln * _sigmoid(ln)).astype(s_ref.dtype)

    return pl.pallas_call(
        body, grid=(s // tr,),
        in_specs=[_rows(tr, c2), pl.BlockSpec((CONV_HALO, c2), lambda i: (jnp.maximum(i * (tr // CONV_HALO) - 1, 0), 0)),
                  _fixed(CONV_HALO, c), _fixed(8, c)],
        out_specs=[_rows(tr, c), _rows(tr, c)], out_shape=[SDS((s, c), f32), SDS((s, c), bf16)],
        scratch_shapes=[pltpu.VMEM((tr + CONV_HALO, c), f32), pltpu.VMEM((tr + 8, c), f32)],
        compiler_params=_params("arbitrary"), name=name)(z, z, wdw, vecs)


def conv_module_bwd(ds, cpre, z, wdw, vecs, name):
    s, c2 = z.shape
    c = c2 // 2
    tr = min(s, 256)
    nt = s // tr

    def body(ds_ref, c_ref, z_ref, w_ref, v_ref, dz_ref, dw_ref, dv_ref, db_ref, ext, dwp):
        step = pl.program_id(0)

        @pl.when(step == 0)
        def _():
            ext[tr:tr + CONV_HALO, :] = jnp.zeros((CONV_HALO, c), f32)
            dwp[...] = jnp.zeros(dwp.shape, f32)
            dv_ref[...] = jnp.zeros(dv_ref.shape, f32)
            db_ref[...] = jnp.zeros(db_ref.shape, f32)

        gain, bias = v_ref[1:2, :], v_ref[2:3, :]
        fold16 = lambda t: t[:8] + t[8:]
        sums = [jnp.zeros((8, c), f32) for _ in range(3)]
        for r0 in range(0, tr, 16):
            rows = slice(r0, r0 + 16)
            chat, rstd = _layernorm_parts(c_ref[rows, :])
            ln = chat * gain + bias
            sig = _sigmoid(ln)
            dln = ds_ref[rows, :].astype(f32) * (sig * (1.0 + ln * (1.0 - sig)))
            gy = dln * gain
            dc = rstd * (gy - jnp.mean(gy, axis=-1, keepdims=True) - chat * jnp.mean(gy * chat, axis=-1, keepdims=True))
            ext[rows, :] = dc
            for k, t in enumerate((dc, dln * chat, dln)):
                sums[k] = sums[k] + fold16(t)
        for k in range(3):
            dv_ref[k:k + 1, :] += jnp.sum(sums[k], axis=0, keepdims=True)

        fold = lambda t: jnp.sum(t.reshape(STRIP_ROWS // 8, 8, LANES), axis=0)
        for lc in range(c // LANES):
            ln_a = slice(lc * LANES, (lc + 1) * LANES)
            ln_g = slice(c + lc * LANES, c + (lc + 1) * LANES)
            dbs = [jnp.zeros((8, LANES), f32) for _ in range(2)]
            for r0 in range(0, tr, STRIP_ROWS):
                rows = slice(r0, r0 + STRIP_ROWS)
                a, sg = z_ref[rows, ln_a], _sigmoid(z_ref[rows, ln_g])
                uv = a * sg
                du = jnp.zeros((STRIP_ROWS, LANES), f32)
                for b in range(8):
                    src = ext[pl.ds(r0 + b, STRIP_ROWS + CONV_HALO - 8), ln_a]
                    for a8, j in _conv_taps(b):
                        ahead = src[8 * a8:8 * a8 + STRIP_ROWS]
                        du = du + w_ref[j:j + 1, ln_a] * ahead
                        dwp[8 * j:8 * j + 8, ln_a] += fold(uv * ahead)
                da = du * sg
                dg = du * a * (sg * (1.0 - sg))
                dz_ref[rows, ln_a] = da.astype(dz_ref.dtype)
                dz_ref[rows, ln_g] = dg.astype(dz_ref.dtype)
                dbs = [dbs[0] + fold(da), dbs[1] + fold(dg)]
            db_ref[:, ln_a] += jnp.sum(dbs[0], axis=0, keepdims=True)
            db_ref[:, ln_g] += jnp.sum(dbs[1], axis=0, keepdims=True)
        ext[tr:tr + CONV_HALO, :] = ext[0:CONV_HALO, :]

        @pl.when(step == nt - 1)
        def _():
            for j in range(CONV_HALO):
                dw_ref[j:j + 1, :] = jnp.sum(dwp[8 * j:8 * j + 8, :], axis=0, keepdims=True)

    rev = lambda i: nt - 1 - i
    back = lambda d: pl.BlockSpec((tr, d), lambda i: (rev(i), 0))
    return pl.pallas_call(
        body, grid=(nt,),
        in_specs=[back(c), back(c), back(c2), _fixed(CONV_HALO, c), _fixed(8, c)],
        out_specs=[back(c2), _fixed(CONV_HALO, c), _fixed(8, c), _fixed(1, c2)],
        out_shape=[SDS((s, c2), bf16), SDS((CONV_HALO, c), f32), SDS((8, c), f32), SDS((1, c2), f32)],
        scratch_shapes=[pltpu.VMEM((tr + CONV_HALO, c), f32), pltpu.VMEM((8 * CONV_HALO, c), f32)],
        compiler_params=_params("arbitrary"), name=name)(ds, cpre, z, wdw, vecs)


def _tile2d(r, n):
    tn = n if n <= 2048 else 1024
    tr = r
    while tr * tn * 4 > (1 << 21) and tr % 16 == 0:
        tr //= 2
    assert r % tr == 0 and n % tn == 0
    return tr, tn


def adamw(w, g, m, v, name):
    r, n = w.shape
    tr, tn = _tile2d(r, n)

    def body(w_ref, g_ref, m_ref, v_ref, d_ref, nm_ref, nv_ref):
        gv = g_ref[...]
        nm = ADAM_B1 * m_ref[...] + (1.0 - ADAM_B1) * gv
        nv = ADAM_B2 * v_ref[...] + (1.0 - ADAM_B2) * (gv * gv)
        m_hat = nm / (1.0 - ADAM_B1 ** ADAM_STEP)
        v_hat = nv / (1.0 - ADAM_B2 ** ADAM_STEP)
        d_ref[...] = -ADAM_LR * (m_hat / (jnp.sqrt(v_hat) + ADAM_EPS) + ADAM_WD * w_ref[...])
        nm_ref[...] = nm
        nv_ref[...] = nv

    blk = pl.BlockSpec((tr, tn), lambda i, j: (i, j))
    return pl.pallas_call(body, grid=(r // tr, n // tn), in_specs=[blk] * 4, out_specs=[blk] * 3,
                          out_shape=[SDS((r, n), f32)] * 3, compiler_params=_params("parallel", "parallel"),
                          name=name)(w, g, m, v)


def add_core_halves(grad, got, where, name):
    _, _, rh, n = grad.shape
    tr, tn = _tile2d(rh, n)

    def body(w_ref, a_ref, b_ref, o_ref):
        o_ref[...] = (a_ref[...] + b_ref[...]).astype(o_ref.dtype)

    return pl.pallas_call(
        body,
        grid_spec=pltpu.PrefetchScalarGridSpec(
            num_scalar_prefetch=1, grid=(N_CHIPS, rh // tr, n // tn),
            in_specs=[pl.BlockSpec((None, None, tr, tn), lambda p, i, j, w_ref: (p, w_ref[0], i, j)),
                      pl.BlockSpec((None, tr, tn), lambda p, i, j, w_ref: (p, i, j))],
            out_specs=pl.BlockSpec((None, tr, tn), lambda p, i, j, w_ref: (p, i, j))),
        out_shape=SDS((N_CHIPS, rh, n), bf16), compiler_params=_params("parallel", "parallel", "parallel"),
        name=name)(where, grad, got)


def add_chip_parts(grad, got_core, got_chips, where, name):
    _, _, rh, n = grad.shape
    tr, tn = _tile2d(rh, n)

    def body(w_ref, a_ref, b_ref, g_ref, o_ref):
        acc = a_ref[...] + b_ref[...]
        for k in range(N_CHIPS - 1):
            acc = acc + g_ref[k].astype(f32)
        o_ref[...] = acc

    return pl.pallas_call(
        body,
        grid_spec=pltpu.PrefetchScalarGridSpec(
            num_scalar_prefetch=1, grid=(rh // tr, n // tn),
            in_specs=[pl.BlockSpec((None, None, tr, tn), lambda i, j, w_ref: (w_ref[1], w_ref[0], i, j)),
                      pl.BlockSpec((None, tr, tn), lambda i, j, w_ref: (w_ref[1], i, j)),
                      pl.BlockSpec((N_CHIPS - 1, tr, tn), lambda i, j, w_ref: (0, i, j))],
            out_specs=pl.BlockSpec((None, tr, tn), lambda i, j, w_ref: (w_ref[0], i, j))),
        out_shape=SDS((2, rh, n), f32), compiler_params=_params("parallel", "parallel"),
        name=name)(where, grad, got_core, got_chips)


def sum_devices(parts, name):
    nd, r, n = parts.shape

    def body(p_ref, o_ref):
        acc = p_ref[0]
        for k in range(1, nd):
            acc = acc + p_ref[k]
        o_ref[...] = acc

    return pl.pallas_call(body, out_shape=SDS((r, n), f32), name=name)(parts)


def _position():
    return lax.axis_index("x"), lax.axis_index("y"), lax.axis_index("c")


def _other_chips(x, y):
    return [(1 - x, y), (x, 1 - y), (1 - x, 1 - y)]


def _remote(src, dst, send, recv, to):
    return pltpu.make_async_remote_copy(src_ref=src, dst_ref=dst, send_sem=send, recv_sem=recv, device_id=to,
                                        device_id_type=MESH)


def gather_chips(bufs, name):
    n = len(bufs)
    nk = N_CHIPS - 1

    def body(*refs):
        bufs_ = refs[n:2 * n]
        send, recv = refs[2 * n:]
        x, y, c = _position()
        me = 2 * x + y
        chips = _other_chips(x, y)
        sends = []
        for t in range(n):
            for k, (px, py) in enumerate(chips):
                out = _remote(bufs_[t].at[me, c], bufs_[t].at[me, c], send.at[t, k], recv.at[t, k], (px, py, c))
                out.start()
                sends.append(out)
        for t in range(n):
            for k, (px, py) in enumerate(chips):
                piece = bufs_[t].at[2 * px + py, c]
                _remote(piece, piece, send.at[t, k], recv.at[t, k], (px, py, c)).wait_recv()
                on = _remote(piece, piece, send.at[t, nk + k], recv.at[t, nk + k], (x, y, 1 - c))
                on.start()
                sends.append(on)
        for t in range(n):
            for k, (px, py) in enumerate(chips):
                piece = bufs_[t].at[2 * px + py, 1 - c]
                _remote(piece, piece, send.at[t, nk + k], recv.at[t, nk + k], (x, y, 1 - c)).wait_recv()
        for cp in sends:
            cp.wait_send()

    return pl.pallas_call(
        body, in_specs=[ANY] * n, out_specs=[ANY] * n,
        out_shape=[SDS(a.shape, a.dtype) for a in bufs],
        input_output_aliases={t: t for t in range(n)},
        scratch_shapes=[pltpu.SemaphoreType.DMA((n, 2 * nk)), pltpu.SemaphoreType.DMA((n, 2 * nk))],
        name=name)(*bufs)


class Rider(NamedTuple):
    operands: tuple
    n_aliased: int
    out_shapes: tuple
    scratch: tuple
    start: Callable
    finish: Callable


def _rider_specs(rider, n_inputs, n_outputs):
    if rider is None:
        return [], [], [], [], {}, []
    aliased = [SDS(a.shape, a.dtype) for a in rider.operands[:rider.n_aliased]]
    outs = aliased + list(rider.out_shapes)
    aliases = {n_inputs + t: n_outputs + t for t in range(rider.n_aliased)}
    return [ANY] * len(rider.operands), [ANY] * len(outs), outs, list(rider.scratch), aliases, list(rider.operands)


def _rider_run(rider, refs, n_inputs, n_outputs, first, last):
    if rider is None:
        return refs, lambda: None, lambda: None
    n_op = len(rider.operands)
    n_out = rider.n_aliased + len(rider.out_shapes)
    own_in, r_in = refs[:n_inputs], refs[n_inputs:n_inputs + n_op]
    own_out = refs[n_inputs + n_op:n_inputs + n_op + n_outputs]
    r_out = refs[n_inputs + n_op + n_outputs:n_inputs + n_op + n_outputs + n_out]
    rest = refs[n_inputs + n_op + n_outputs + n_out:]
    n_sem = len(rider.scratch)
    sems, own_scratch = rest[len(rest) - n_sem:], rest[:len(rest) - n_sem]

    def start():
        pl.when(first)(lambda: rider.start(r_in, r_out, sems))

    def finish():
        pl.when(last)(lambda: rider.finish(r_in, r_out, sems))

    return list(own_in) + list(own_out) + list(own_scratch), start, finish


def gather_rider(bufs):
    n = len(bufs)
    nk = N_CHIPS - 1

    def ici(bufs_, send, recv, x, y, c, t, k, px, py, own):
        piece = bufs_[t].at[2 * x + y if own else 2 * px + py, c]
        return _remote(piece, piece, send.at[t, k], recv.at[t, k], (px, py, c))

    def d2d(bufs_, send, recv, x, y, c, t, k, px, py, mine):
        piece = bufs_[t].at[2 * px + py, c if mine else 1 - c]
        return _remote(piece, piece, send.at[t, nk + k], recv.at[t, nk + k], (x, y, 1 - c))

    def start(r_in, r_out, sems):
        send, recv = sems
        x, y, c = _position()
        for t in range(n):
            for k, (px, py) in enumerate(_other_chips(x, y)):
                ici(r_out, send, recv, x, y, c, t, k, px, py, True).start()

    def finish(r_in, r_out, sems):
        send, recv = sems
        x, y, c = _position()
        chips = _other_chips(x, y)
        for t in range(n):
            for k, (px, py) in enumerate(chips):
                ici(r_out, send, recv, x, y, c, t, k, px, py, False).wait_recv()
                d2d(r_out, send, recv, x, y, c, t, k, px, py, True).start()
        for t in range(n):
            for k, (px, py) in enumerate(chips):
                d2d(r_out, send, recv, x, y, c, t, k, px, py, False).wait_recv()
        for t in range(n):
            for k, (px, py) in enumerate(chips):
                ici(r_out, send, recv, x, y, c, t, k, px, py, True).wait_send()
                d2d(r_out, send, recv, x, y, c, t, k, px, py, True).wait_send()

    sems = (pltpu.SemaphoreType.DMA((n, 2 * nk)), pltpu.SemaphoreType.DMA((n, 2 * nk)))
    return Rider(tuple(bufs), n, (), sems, start, finish)


def scatter_rider(parts):
    n = len(parts)

    def copies(r_in, r_out, sems):
        send, recv = sems
        x, y, c = _position()
        return [_remote(r_in[t].at[2 * px + py], r_out[t].at[k], send.at[t, k], recv.at[t, k], (px, py, c))
                for t in range(n) for k, (px, py) in enumerate(_other_chips(x, y))]

    def start(r_in, r_out, sems):
        for cp in copies(r_in, r_out, sems):
            cp.start()

    def finish(r_in, r_out, sems):
        for cp in copies(r_in, r_out, sems):
            cp.wait()

    sems = (pltpu.SemaphoreType.DMA((n, N_CHIPS - 1)), pltpu.SemaphoreType.DMA((n, N_CHIPS - 1)))
    return Rider(tuple(parts), 0, tuple(SDS((N_CHIPS - 1,) + a.shape[1:], a.dtype) for a in parts), sems, start, finish)


def swap_core_halves(grads, name):
    n = len(grads)

    def body(*refs):
        ins, outs = refs[:n], refs[n:2 * n]
        send, recv = refs[2 * n:]
        x, y, c = _position()
        pending = []
        for t in range(n):
            out = _remote(ins[t].at[:, 1 - c], outs[t], send.at[t], recv.at[t], (x, y, 1 - c))
            out.start()
            pending.append(out.wait)
        for wait in pending:
            wait()

    return pl.pallas_call(
        body, in_specs=[ANY] * n, out_specs=[ANY] * n,
        out_shape=[SDS((a.shape[0],) + a.shape[2:], a.dtype) for a in grads],
        scratch_shapes=[pltpu.SemaphoreType.DMA((n,)), pltpu.SemaphoreType.DMA((n,))],
        name=name)(*grads)


def scatter_chips(parts, name):
    n = len(parts)

    def body(*refs):
        ins, outs = refs[:n], refs[n:2 * n]
        send, recv = refs[2 * n:]
        x, y, c = _position()
        pending = []
        for t in range(n):
            for k, (px, py) in enumerate(_other_chips(x, y)):
                out = _remote(ins[t].at[2 * px + py], outs[t].at[k], send.at[t, k], recv.at[t, k], (px, py, c))
                out.start()
                pending.append(out.wait)
        for wait in pending:
            wait()

    return pl.pallas_call(
        body, in_specs=[ANY] * n, out_specs=[ANY] * n,
        out_shape=[SDS((N_CHIPS - 1,) + a.shape[1:], a.dtype) for a in parts],
        scratch_shapes=[pltpu.SemaphoreType.DMA((n, N_CHIPS - 1)), pltpu.SemaphoreType.DMA((n, N_CHIPS - 1))],
        name=name)(*parts)


def share_core_halves(bufs, name):
    n = len(bufs)

    def body(*refs):
        bufs_ = refs[n:2 * n]
        send, recv = refs[2 * n:]
        x, y, c = _position()
        pending = []
        for t in range(n):
            out = _remote(bufs_[t].at[c], bufs_[t].at[c], send.at[t], recv.at[t], (x, y, 1 - c))
            out.start()
            pending.append(out.wait_send)
            other = bufs_[t].at[1 - c]
            pending.append(_remote(other, other, send.at[t], recv.at[t], (x, y, 1 - c)).wait_recv)
        for wait in pending:
            wait()

    return pl.pallas_call(
        body, in_specs=[ANY] * n, out_specs=[ANY] * n, out_shape=[SDS(a.shape, a.dtype) for a in bufs],
        input_output_aliases={t: t for t in range(n)},
        scratch_shapes=[pltpu.SemaphoreType.DMA((n,)), pltpu.SemaphoreType.DMA((n,))],
        name=name)(*bufs)


def gather_devices(v, name):
    flips = [(dx, dy, dc) for dx in (0, 1) for dy in (0, 1) for dc in (0, 1)][1:]

    def body(v_ref, o_ref, send, recv, local):
        x, y, c = _position()
        me = 4 * x + 2 * y + c
        own = pltpu.make_async_copy(v_ref, o_ref.at[me], local)
        own.start()
        pending = [own.wait]
        for k, (dx, dy, dc) in enumerate(flips):
            px, py, pc = x ^ dx, y ^ dy, c ^ dc
            out = _remote(v_ref, o_ref.at[me], send.at[k], recv.at[k], (px, py, pc))
            out.start()
            pending.append(out.wait_send)
            pending.append(_remote(v_ref, o_ref.at[4 * px + 2 * py + pc], send.at[k], recv.at[k], (px, py, pc)).wait_recv)
        for wait in pending:
            wait()

    return pl.pallas_call(
        body, in_specs=[ANY], out_specs=ANY, out_shape=SDS((8,) + v.shape, v.dtype),
        scratch_shapes=[pltpu.SemaphoreType.DMA((7,)), pltpu.SemaphoreType.DMA((7,)), pltpu.SemaphoreType.DMA],
        name=name)(v)


SMALL = ("norm_g", "conv_b_pw1", "conv_w_dw", "conv_b_dw", "conv_ln_g", "conv_ln_b", "conv_b_pw2", "ffn_w_dw")


def _pack_rows(arrs, rows):
    flat = jnp.concatenate([a.reshape(-1, LANES) for a in arrs], axis=0)
    return jnp.pad(flat, ((0, rows - flat.shape[0]), (0, 0)))


def _unpack_rows(packed, shapes):
    out, at = [], 0
    for shp in shapes:
        size = 1
        for dim in shp:
            size *= dim
        rows = size // LANES
        out.append(packed[..., at:at + rows, :].reshape(packed.shape[:-2] + tuple(shp)))
        at += rows
    return out


def _join_last(t):
    t = jnp.moveaxis(t, 0, -2)
    return t.reshape(t.shape[:-2] + (t.shape[-2] * t.shape[-1],))


def _split_last(t):
    t = t.reshape(t.shape[:-1] + (N_CHIPS, t.shape[-1] // N_CHIPS))
    return jnp.moveaxis(t, -2, 0)


def _rope_tables(positions):
    half = ROT_DIM // 2
    inv_freq = ROPE_THETA ** (-jnp.arange(half, dtype=f32) / half)
    ang = positions.astype(f32).reshape(-1, 1) * inv_freq
    cos, sin = jnp.cos(ang), jnp.sin(ang)
    s = ang.shape[0]
    rest = HEAD_DIM - ROT_DIM
    head = lambda t, fill: jnp.concatenate([t, t, jnp.full((s, rest), fill, f32)], axis=1)
    twice = lambda t: jnp.concatenate([t] * (LANES // HEAD_DIM), axis=1)
    return twice(head(cos, 1.0)), twice(head(sin, 0.0))


def _pairing_matrix():
    half = ROT_DIM // 2
    row = jnp.arange(GROUP_WIDTH)[:, None]
    col = jnp.arange(GROUP_WIDTH)[None, :]
    d = col % HEAD_DIM
    minus = jnp.logical_and(row == col + half, d < half)
    plus = jnp.logical_and(row == col - half, jnp.logical_and(d >= half, d < ROT_DIM))
    return (plus.astype(f32) - minus.astype(f32)).astype(bf16)


def _pad_rows(t, rows):
    return jnp.pad(t, ((0, rows - t.shape[0]), (0, 0)))


def _ffn_pack(w_dw, b_dw):
    t = jnp.concatenate([w_dw, b_dw[None]], axis=0)
    t = t.reshape(FFN_CONV + 1, 2, 2, -1)
    t = jnp.transpose(t, (2, 1, 0, 3))
    return jnp.pad(t, ((0, 0), (0, 0), (0, 8 - (FFN_CONV + 1)), (0, 0)))


def _ffn_unpack(d):
    t = jnp.transpose(d[:, :, :FFN_CONV + 1], (2, 1, 0, 3)).reshape(FFN_CONV + 1, -1)
    return t[:FFN_CONV], t[FFN_CONV]


def _ffn_block(x, g_pre, g_post, w_up, li, w_down, wb, tag):
    s = x.shape[0]
    h = prenorm(x, g_pre, f"{tag}_prenorm")
    u = mm_nn(h[None], w_up, li, None, N_CHIPS, bf16, 512, w_up.shape[-1], f"{tag}_up")
    u4 = u.reshape(2, 2, s, u.shape[-1])
    a, ug = ffn_act_fwd(u4, wb, f"{tag}_act")
    y = mm_nn(a, w_down, 0, None, 1, f32, 512, 512, f"{tag}_down")[0]
    return y, (h, u4, ug, a)


def _ffn_block_bwd(dy, saved, w_up, li, w_down, wb, tag, rider=None):
    h, u4, ug, a = saved
    d_down = mm_tn(a, dy[None], 1, 1024, a.shape[-1], 512, f"{tag}_dwdown")
    da = mm_nt(dy[None], w_down, 0, 2, bf16, 1024, a.shape[-1], w_down.shape[-1], f"{tag}_da")
    du4, dwb, rode = ffn_act_bwd(da, ug, u4, wb, f"{tag}_actbwd", rider)
    du = du4.reshape((N_CHIPS,) + du4.shape[2:])
    d_up = mm_tn(h[None], du, N_CHIPS, 1024, h.shape[-1], du.shape[-1], f"{tag}_dwup")
    dh = mm_nt_whole(du, w_up, li, f32, 512, f"{tag}_dh")
    return dh, d_up, d_down, dwb, rode


def kernel(x, positions, norm_g, attn_w_qkv, attn_w_o, conv_w_pw1, conv_b_pw1, conv_w_dw, conv_b_dw, conv_ln_g, conv_ln_b, conv_w_pw2, conv_b_pw2, ffn_w_up, ffn_w_dw, ffn_b_dw, ffn_w_down, loss_target, m_norm_g, m_attn_w_qkv, m_attn_w_o, m_conv_w_pw1, m_conv_b_pw1, m_conv_w_dw, m_conv_b_dw, m_conv_ln_g, m_conv_ln_b, m_conv_w_pw2, m_conv_b_pw2, m_ffn_w_up, m_ffn_w_dw, m_ffn_b_dw, m_ffn_w_down, v_norm_g, v_attn_w_qkv, v_attn_w_o, v_conv_w_pw1, v_conv_b_pw1, v_conv_w_dw, v_conv_b_dw, v_conv_ln_g, v_conv_ln_b, v_conv_w_pw2, v_conv_b_pw2, v_ffn_w_up, v_ffn_w_dw, v_ffn_b_dw, v_ffn_w_down):
    weights = dict(norm_g=norm_g, attn_w_qkv=attn_w_qkv, attn_w_o=attn_w_o, conv_w_pw1=conv_w_pw1, conv_b_pw1=conv_b_pw1,
                   conv_w_dw=conv_w_dw, conv_b_dw=conv_b_dw, conv_ln_g=conv_ln_g, conv_ln_b=conv_ln_b, conv_w_pw2=conv_w_pw2,
                   conv_b_pw2=conv_b_pw2, ffn_w_up=ffn_w_up, ffn_w_dw=ffn_w_dw, ffn_b_dw=ffn_b_dw, ffn_w_down=ffn_w_down)
    mom1 = dict(norm_g=m_norm_g, attn_w_qkv=m_attn_w_qkv, attn_w_o=m_attn_w_o, conv_w_pw1=m_conv_w_pw1, conv_b_pw1=m_conv_b_pw1,
                conv_w_dw=m_conv_w_dw, conv_b_dw=m_conv_b_dw, conv_ln_g=m_conv_ln_g, conv_ln_b=m_conv_ln_b, conv_w_pw2=m_conv_w_pw2,
                conv_b_pw2=m_conv_b_pw2, ffn_w_up=m_ffn_w_up, ffn_w_dw=m_ffn_w_dw, ffn_b_dw=m_ffn_b_dw, ffn_w_down=m_ffn_w_down)
    mom2 = dict(norm_g=v_norm_g, attn_w_qkv=v_attn_w_qkv, attn_w_o=v_attn_w_o, conv_w_pw1=v_conv_w_pw1, conv_b_pw1=v_conv_b_pw1,
                conv_w_dw=v_conv_w_dw, conv_b_dw=v_conv_b_dw, conv_ln_g=v_conv_ln_g, conv_ln_b=v_conv_ln_b, conv_w_pw2=v_conv_w_pw2,
                conv_b_pw2=v_conv_b_pw2, ffn_w_up=v_ffn_w_up, ffn_w_dw=v_ffn_w_dw, ffn_b_dw=v_ffn_b_dw, ffn_w_down=v_ffn_w_down)
    big = ("attn_w_qkv", "attn_w_o", "conv_w_pw1", "conv_w_pw2", "ffn_w_up", "ffn_w_down")
    xi, yi, ci = _position()
    chip = (2 * xi + yi).astype(jnp.int32).reshape(1)
    where = jnp.stack([ci, 2 * xi + yi]).astype(jnp.int32)

    x = x[0]
    target = loss_target[0]
    s, d = x.shape

    small_shapes = [weights[k].shape for k in SMALL]
    small_rows = -(-sum(weights[k].size for k in SMALL) // LANES // 8) * 8
    small_w = _pack_rows([weights[k] for k in SMALL], small_rows)
    def own_slot(shard):
        halves = shard.reshape(1, 2, -1, shard.shape[-1])
        return lax.dynamic_update_slice(lax.empty((N_CHIPS,) + halves.shape[1:], shard.dtype), halves, (chip[0], 0, 0, 0))

    early, late = big[:2], big[2:]
    gathered = gather_chips([own_slot(weights[k].astype(bf16)) for k in early] + [own_slot(small_w)], "gather_weights_early")
    gw = {k: t.reshape((N_CHIPS,) + weights[k].shape) for k, t in zip(early, gathered[:-1])}
    full_small = dict(zip(SMALL, [_join_last(t) for t in _unpack_rows(gathered[-1].reshape(N_CHIPS, small_rows, LANES), small_shapes)]))
    w_qkv, w_o = gw["attn_w_qkv"], gw["attn_w_o"]
    gains = full_small["norm_g"]
    gain = lambda i, k: gains[i, k][None]
    b_pw1 = full_small["conv_b_pw1"]
    conv_wdw = _pad_rows(full_small["conv_w_dw"][0], CONV_HALO)
    conv_vecs = _pad_rows(jnp.concatenate([full_small["conv_b_dw"], full_small["conv_ln_g"], full_small["conv_ln_b"]], axis=0), 8)
    b_pw2 = full_small["conv_b_pw2"]
    wbs = [_ffn_pack(full_small["ffn_w_dw"][i], ffn_b_dw[i]) for i in range(2)]
    tabs = _rope_tables(positions[0])
    pair = _pairing_matrix()
    head_ones = (jnp.arange(GROUP_WIDTH)[:, None] // HEAD_DIM == jnp.arange(GROUP_WIDTH)[None, :] // HEAD_DIM).astype(bf16)

    h0 = prenorm(x, gain(0, 0), "l0_prenorm")
    w_qkv_flat = jnp.transpose(w_qkv[:, 0], (1, 0, 2)).reshape(d, -1)
    under_qkv, under_attn = ("ffn_w_up",), ("conv_w_pw1", "conv_w_pw2", "ffn_w_down")
    assert set(under_qkv + under_attn) == set(late)
    slots = lambda names: [own_slot(weights[k].astype(bf16)) for k in names]
    qkvg, got = qkv_rope(h0, w_qkv_flat, pair, tabs, "qkv", gather_rider(slots(under_qkv)))
    gw.update({k: t.reshape((N_CHIPS,) + weights[k].shape) for k, t in zip(under_qkv, got)})
    att = [attn_fwd(qkvg, g, dil, f"attn_fwd{g}", gather_rider(slots(under_attn)) if g == 0 else None)
           for g, (_, dil) in enumerate(DILATED_GROUPS)]
    gw.update({k: t.reshape((N_CHIPS,) + weights[k].shape) for k, t in zip(under_attn, att[0][3])})
    w_pw1, w_up = gw["conv_w_pw1"], gw["ffn_w_up"]
    w_pw2 = gw["conv_w_pw2"].reshape(1, 1, -1, d)
    w_down = [gw["ffn_w_down"][:, i].reshape(1, 1, -1, d) for i in range(2)]
    os_, lses, lses_phased = ([a[k] for a in att] for k in range(3))
    mixed = mix_fwd(os_, lses, "mix")
    w_o_flat = jnp.transpose(w_o[:, 0], (1, 0, 2)).reshape(1, 1, GROUP_WIDTH, d)
    y0 = mm_nn(mixed[None], w_o_flat, 0, None, 1, f32, 1024, d, "attn_out")[0]
    x1 = postnorm_residual(x, y0, gain(0, 1), "l0_postnorm")
    y1, ffn0 = _ffn_block(x1, gain(0, 2), gain(0, 3), w_up, 0, w_down[0], wbs[0], "ffn0")
    x2 = postnorm_residual(x1, y1, gain(0, 3), "l0_ffn_postnorm")

    h2 = prenorm(x2, gain(1, 0), "l1_prenorm")
    z = mm_nn(h2[None], w_pw1, 0, b_pw1, 1, f32, 1024, w_pw1.shape[-1], "pw1")[0]
    cpre, sw = conv_module_fwd(z, conv_wdw, conv_vecs, "conv_fwd")
    y2 = mm_nn(sw[None], w_pw2, 0, b_pw2, 1, f32, 512, 512, "pw2")[0]
    x3 = postnorm_residual(x2, y2, gain(1, 1), "l1_postnorm")
    y3, ffn1 = _ffn_block(x3, gain(1, 2), gain(1, 3), w_up, 1, w_down[1], wbs[1], "ffn1")
    dx4, loss_row = final_loss(x3, y3, gain(1, 3), target, "loss")
    loss = lax.psum(loss_row[0, 0], ("x", "y", "c"))

    dgain = [[None] * 4 for _ in range(2)]
    dy3, dgain[1][3], _ = norm_bwd(y3, gain(1, 3), dx4, None, bf16, "l1_ffn_postnorm_bwd")
    dh, d_up1, d_down1, dwb1, _ = _ffn_block_bwd(dy3, ffn1, w_up, 1, w_down[1], wbs[1], "ffn1")
    dx3, dgain[1][2], _ = norm_bwd(x3, gain(1, 2), dh, dx4, f32, "l1_ffn_prenorm_bwd")

    dy2, dgain[1][1], d_b_pw2 = norm_bwd(y2, gain(1, 1), dx3, None, bf16, "l1_postnorm_bwd")
    d_pw2 = mm_tn(sw[None], dy2[None], 1, 1024, d, 512, "dw_pw2")
    dsw = mm_nt(dy2[None], w_pw2, 0, 1, f32, 1024, d, d, "d_swish")[0]
    dz, d_conv_wdw, d_conv_vecs, d_b_pw1 = conv_module_bwd(dsw, cpre, z, conv_wdw, conv_vecs, "conv_bwd")
    d_pw1 = mm_tn(h2[None], dz[None], N_CHIPS, 1024, d, w_pw1.shape[-1], "dw_pw1")
    dh = mm_nt_whole(dz[None], w_pw1, 0, f32, 1024, "d_h2")
    dx2, dgain[1][0], _ = norm_bwd(x2, gain(1, 0), dh, dx3, f32, "l1_prenorm_bwd")

    def core_stage(named, tag):
        halves = [g.reshape(N_CHIPS, 2, -1, g.shape[-1]) for _, g in named]
        from_core = swap_core_halves(halves, f"swap_core_halves_{tag}")
        sums = [add_core_halves(a, b, where, f"add_core_{k}") for (k, _), a, b in zip(named, halves, from_core)]
        return halves, from_core, sums

    def chip_stage(named, halves, from_core, from_chips):
        return [add_chip_parts(a, b, r, where, f"add_chips_{k}") for (k, _), a, b, r in zip(named, halves, from_core, from_chips)]

    first_done = [("conv_w_pw1", d_pw1), ("conv_w_pw2", d_pw2[0].reshape(N_CHIPS, -1, d)), ("ffn_w_up1", d_up1),
                  ("ffn_w_down1", d_down1[0].reshape(N_CHIPS, -1, d))]
    halves_1, core_1, sums_1 = core_stage(first_done, "layer1")

    dy1, dgain[0][3], _ = norm_bwd(y1, gain(0, 3), dx2, None, bf16, "l0_ffn_postnorm_bwd")
    dh, d_up0, d_down0, dwb0, chips_1 = _ffn_block_bwd(dy1, ffn0, w_up, 0, w_down[0], wbs[0], "ffn0", scatter_rider(sums_1))
    mine_1 = chip_stage(first_done, halves_1, core_1, chips_1)
    mid_done = [("ffn_w_up0", d_up0), ("ffn_w_down0", d_down0[0].reshape(N_CHIPS, -1, d))]
    halves_m, core_m, sums_m = core_stage(mid_done, "ffn0")
    dx1, dgain[0][2], _ = norm_bwd(x1, gain(0, 2), dh, dx2, f32, "l0_ffn_prenorm_bwd")

    dy0, dgain[0][1], _ = norm_bwd(y0, gain(0, 1), dx1, None, bf16, "l0_postnorm_bwd")
    d_wo = mm_tn(mixed[None], dy0[None], N_CHIPS, 1024, GROUP_WIDTH, w_o.shape[-1], "dw_o")
    dmixed = mm_nt_whole(dy0[None], w_o, 0, f32, 1024, "d_mixed")
    mb = mix_bwd(dmixed, os_, lses, head_ones, "mix_bwd")
    parts = []
    for g, (_, dil) in enumerate(DILATED_GROUPS):
        *five, rode = attn_bwd(qkvg, mb[g], lses_phased[g], mb[3 + g], g, dil, f"attn_bwd{g}", scatter_rider(sums_m) if g == 0 else None)
        parts.append(five)
        if g == 0:
            mine_m = chip_stage(mid_done, halves_m, core_m, rode)
    dqkv = dqkv_assemble(parts, pair.T, tabs, "dqkv")
    d_qkv = mm_tn(h0[None], dqkv[None], N_CHIPS, 1024, d, w_qkv.shape[-1], "dw_qkv")
    dh = mm_nt_whole(dqkv[None], w_qkv, 0, f32, 512, "d_h0")
    grad_x, dgain[0][0], _ = norm_bwd(x, gain(0, 0), dh, dx1, f32, "l0_prenorm_bwd")

    d_ffn_dw, d_ffn_b = zip(*[_ffn_unpack(t) for t in (dwb0, dwb1)])
    gsmall = dict(
        norm_g=jnp.stack([jnp.concatenate(row, axis=0) for row in dgain], axis=0),
        conv_b_pw1=d_b_pw1, conv_w_dw=d_conv_wdw[None, :CONV_KERNEL], conv_b_dw=d_conv_vecs[0:1], conv_ln_g=d_conv_vecs[1:2],
        conv_ln_b=d_conv_vecs[2:3], conv_b_pw2=d_b_pw2, ffn_w_dw=jnp.stack(d_ffn_dw, axis=0))
    bias_rows = ffn_b_dw.size // LANES
    small_g = jnp.concatenate([jnp.concatenate([_pack_rows([_split_last(gsmall[k])[p] for k in SMALL], small_rows)
                                                for p in range(N_CHIPS)], axis=0),
                               jnp.stack(d_ffn_b, axis=0).reshape(bias_rows, LANES)], axis=0)

    small_sum = sum_devices(gather_devices(small_g, "gather_small_grads"), "sum_small_grads")
    my_small = lax.dynamic_slice_in_dim(small_sum, chip[0] * small_rows, small_rows, axis=0)
    g_small = jnp.concatenate([my_small, small_sum[N_CHIPS * small_rows:]], axis=0)

    last_done = [("attn_w_qkv", d_qkv), ("attn_w_o", d_wo)]
    halves_0, core_0, sums_0 = core_stage(last_done, "attn")
    mine_0 = chip_stage(last_done, halves_0, core_0, scatter_chips(sums_0, "scatter_chips_attn"))
    shared = share_core_halves(mine_1 + mine_m + mine_0, "share_core_halves")
    piece = {k: t.reshape(-1, t.shape[-1]) for (k, _), t in zip(first_done + mid_done + last_done, shared)}
    shard_grads = [piece["attn_w_qkv"], piece["attn_w_o"], piece["conv_w_pw1"], piece["conv_w_pw2"],
                   jnp.concatenate([piece["ffn_w_up0"], piece["ffn_w_up1"]], axis=0),
                   jnp.concatenate([piece["ffn_w_down0"], piece["ffn_w_down1"]], axis=0)]

    grads, deltas, new_m, new_v = {}, {}, {}, {}
    for k, g2 in zip(big, shard_grads):
        shp = weights[k].shape
        dl, nm, nv = adamw(weights[k].reshape(g2.shape), g2, mom1[k].reshape(g2.shape), mom2[k].reshape(g2.shape), f"adamw_{k}")
        grads[k], deltas[k], new_m[k], new_v[k] = (t.reshape(shp) for t in (g2, dl, nm, nv))
    pack_state = lambda src: jnp.concatenate([_pack_rows([src[k] for k in SMALL], small_rows), src["ffn_b_dw"].reshape(bias_rows, LANES)], axis=0)
    small_out = (g_small,) + tuple(adamw(pack_state(weights), g_small, pack_state(mom1), pack_state(mom2), "adamw_small"))
    for dst, packed in zip((grads, deltas, new_m, new_v), small_out):
        for k, t in zip(SMALL, _unpack_rows(packed[:small_rows], small_shapes)):
            dst[k] = t
        dst["ffn_b_dw"] = packed[small_rows:].reshape(ffn_b_dw.shape)

    order = ("norm_g", "attn_w_qkv", "attn_w_o", "conv_w_pw1", "conv_b_pw1", "conv_w_dw", "conv_b_dw", "conv_ln_g", "conv_ln_b",
             "conv_w_pw2", "conv_b_pw2", "ffn_w_up", "ffn_w_dw", "ffn_b_dw", "ffn_w_down")
    return (loss, grad_x[None], *[grads[k] for k in order], *[deltas[k] for k in order], *[new_m[k] for k in order],
            *[new_v[k] for k in order])
```

```python
import functools
from typing import Callable, NamedTuple

import jax
import jax.numpy as jnp
from jax import lax
from jax.experimental import pallas as pl
from jax.experimental.pallas import tpu as pltpu

f32 = jnp.float32
bf16 = jnp.bfloat16
SDS = jax.ShapeDtypeStruct

EPS = 1e-6
HEAD_DIM = 64
N_SLOTS = 8
GROUP_WIDTH = N_SLOTS * HEAD_DIM
DILATED_GROUPS = ((128, 1), (512, 4), (2048, 16))
N_GROUPS = 3
SPAN = 128
ROT_DIM = HEAD_DIM // 4
ROPE_THETA = 500000.0
CONV_KERNEL = 31
CONV_HALO = 32
FFN_CONV = 3
FFN_HALO = 16
FFN_TAIL = 8
STRIP_ROWS = 64
ADAM_LR, ADAM_B1, ADAM_B2, ADAM_EPS, ADAM_WD, ADAM_STEP = 0.001, 0.9, 0.999, 1e-08, 0.01, 10
LANES = 128
N_CHIPS = 4
VMEM_LIMIT_BYTES = 56 * 1024 * 1024
NEG_BIG = -1e30
MESH = pl.DeviceIdType.MESH
ANY = pl.BlockSpec(memory_space=pl.ANY)


def _params(*sem):
    return pltpu.CompilerParams(dimension_semantics=sem, vmem_limit_bytes=VMEM_LIMIT_BYTES)


def _sigmoid(v):
    return 1.0 / (1.0 + jnp.exp(-v))


def _dot_nt(a, b):
    return lax.dot_general(a, b, (((1,), (1,)), ((), ())), preferred_element_type=f32)


def _dot_tn(a, b):
    return lax.dot_general(a, b, (((0,), (0,)), ((), ())), preferred_element_type=f32)


def mm_nn(x, w, li, bias, out_blocks, out_dtype, tm, tn, name):
    nq, m, kq = x.shape
    p, _, k, n = w.shape
    assert k == nq * kq and n % tn == 0 and m % tm == 0
    on = p * n // out_blocks
    assert on % tn == 0
    nj, onj = n // tn, on // tn

    def body(*refs):
        if bias is None:
            x_ref, w_ref, o_ref = refs
            b_ref = None
        else:
            x_ref, w_ref, b_ref, o_ref = refs
        acc = jnp.dot(x_ref[0], w_ref[0:kq, :], preferred_element_type=f32)
        for q in range(1, nq):
            acc = acc + jnp.dot(x_ref[q], w_ref[q * kq:(q + 1) * kq, :], preferred_element_type=f32)
        if b_ref is not None:
            acc = acc + b_ref[...]
        o_ref[...] = acc.astype(o_ref.dtype)

    in_specs = [pl.BlockSpec((nq, tm, kq), lambda j, i: (0, i, 0)),
                pl.BlockSpec((None, None, k, tn), lambda j, i: (j // nj, li, 0, j % nj))]
    args = [x, w]
    if bias is not None:
        in_specs.append(pl.BlockSpec((1, tn), lambda j, i: (0, j)))
        args.append(bias)
    return pl.pallas_call(
        body, grid=(p * nj, m // tm), in_specs=in_specs,
        out_specs=pl.BlockSpec((None, tm, tn), lambda j, i: (j // onj, i, j % onj)),
        out_shape=SDS((out_blocks, m, on), out_dtype),
        compiler_params=_params("parallel", "parallel"), name=name)(*args)


def mm_nt(dy, w, li, out_blocks, out_dtype, tm, tk, tn, name):
    ob, m, on = dy.shape
    p, _, k, n = w.shape
    assert ob * on == p * n and n % tn == 0 and on % tn == 0 and k % tk == 0 and m % tm == 0
    kq = k // out_blocks
    assert kq % tk == 0
    nj, onj, kqj = n // tn, on // tn, kq // tk
    nr = p * nj

    def body(dy_ref, w_ref, o_ref, *scr):
        part = _dot_nt(dy_ref[...], w_ref[...])
        if nr == 1:
            o_ref[...] = part.astype(o_ref.dtype)
        else:
            acc_ref, = scr
            r = pl.program_id(2)

            @pl.when(r == 0)
            def _():
                acc_ref[...] = part

            @pl.when(r > 0)
            def _():
                acc_ref[...] += part

            @pl.when(r == nr - 1)
            def _():
                o_ref[...] = acc_ref[...].astype(o_ref.dtype)

    return pl.pallas_call(
        body, grid=(k // tk, m // tm, nr),
        in_specs=[pl.BlockSpec((None, tm, tn), lambda kt, i, r: (r // onj, i, r % onj)),
                  pl.BlockSpec((None, None, tk, tn), lambda kt, i, r: (r // nj, li, kt, r % nj))],
        out_specs=pl.BlockSpec((None, tm, tk), lambda kt, i, r: (kt // kqj, i, kt % kqj)),
        out_shape=SDS((out_blocks, m, kq), out_dtype),
        scratch_shapes=[] if nr == 1 else [pltpu.VMEM((tm, tk), f32)],
        compiler_params=_params("parallel", "parallel", "arbitrary"), name=name)(dy, w)


def mm_nt_whole(dy, w, li, out_dtype, tm, name):
    ob, m, on = dy.shape
    p, _, k, n = w.shape
    assert ob * on == p * n and ob in (1, p) and m % tm == 0

    def body(dy_ref, w_ref, o_ref):
        acc = None
        for pb in range(p):
            lhs = dy_ref[pb] if ob == p else dy_ref[0, :, pb * n:(pb + 1) * n]
            part = _dot_nt(lhs, w_ref[pb])
            acc = part if acc is None else acc + part
        o_ref[...] = acc.astype(o_ref.dtype)

    return pl.pallas_call(
        body, grid=(m // tm,),
        in_specs=[pl.BlockSpec((ob, tm, on), lambda i: (0, i, 0)), pl.BlockSpec((p, None, k, n), lambda i: (0, li, 0, 0))],
        out_specs=pl.BlockSpec((tm, k), lambda i: (i, 0)), out_shape=SDS((m, k), out_dtype),
        compiler_params=_params("parallel"), name=name)(dy, w)


def mm_tn(x, dy, p, tm, tk, tn, name):
    nq, m, kq = x.shape
    ob, _, on = dy.shape
    k = nq * kq
    n = ob * on // p
    assert n % tn == 0 and on % tn == 0 and kq % tk == 0 and m % tm == 0
    nj, onj, kqj = n // tn, on // tn, kq // tk

    def body(x_ref, dy_ref, o_ref):
        part = _dot_tn(x_ref[...], dy_ref[...])
        i = pl.program_id(2)

        @pl.when(i == 0)
        def _():
            o_ref[...] = part

        @pl.when(i > 0)
        def _():
            o_ref[...] += part

    return pl.pallas_call(
        body, grid=(k // tk, p * nj, m // tm),
        in_specs=[pl.BlockSpec((None, tm, tk), lambda kt, j, i: (kt // kqj, i, kt % kqj)),
                  pl.BlockSpec((None, tm, tn), lambda kt, j, i: (j // onj, i, j % onj))],
        out_specs=pl.BlockSpec((None, tk, tn), lambda kt, j, i: (j // nj, kt, j % nj)),
        out_shape=SDS((p, k, n), f32),
        compiler_params=_params("parallel", "parallel", "arbitrary"), name=name)(x, dy)


def _row_tile(s):
    return min(s, 512)


def _rows(tr, d):
    return pl.BlockSpec((tr, d), lambda i: (i, 0))


def _fixed(r, d):
    return pl.BlockSpec((r, d), lambda i: (0, 0))


def _rms(xv):
    return lax.rsqrt(jnp.mean(xv * xv, axis=-1, keepdims=True) + EPS)


def prenorm(x, g, name):
    s, d = x.shape
    tr = _row_tile(s)

    def body(x_ref, g_ref, o_ref):
        xv = x_ref[...]
        o_ref[...] = (xv * _rms(xv) * g_ref[...]).astype(o_ref.dtype)

    return pl.pallas_call(body, grid=(s // tr,), in_specs=[_rows(tr, d), _fixed(1, d)], out_specs=_rows(tr, d),
                          out_shape=SDS((s, d), bf16), compiler_params=_params("parallel"), name=name)(x, g)


def postnorm_residual(x, y, g, g_next, name):
    s, d = x.shape
    tr = _row_tile(s)

    def body(x_ref, y_ref, g_ref, gn_ref, o_ref, h_ref):
        yv = y_ref[...]
        out = x_ref[...] + yv * _rms(yv) * g_ref[...]
        o_ref[...] = out
        h_ref[...] = (out * _rms(out) * gn_ref[...]).astype(h_ref.dtype)

    return pl.pallas_call(body, grid=(s // tr,), in_specs=[_rows(tr, d), _rows(tr, d), _fixed(1, d), _fixed(1, d)],
                          out_specs=[_rows(tr, d), _rows(tr, d)], out_shape=[SDS((s, d), f32), SDS((s, d), bf16)],
                          compiler_params=_params("parallel"), name=name)(x, y, g, g_next)


def norm_bwd(xin, g, dout, res, out_dtype, name):
    s, d = xin.shape
    tr = _row_tile(s)

    def body(*refs):
        if res is None:
            x_ref, g_ref, do_ref, dx_ref, dg_ref, cs_ref = refs
            r_ref = None
        else:
            x_ref, g_ref, do_ref, r_ref, dx_ref, dg_ref, cs_ref = refs
        xv = x_ref[...]
        r = _rms(xv)
        xh = xv * r
        dov = do_ref[...].astype(f32)
        gy = dov * g_ref[...]
        dx = r * (gy - xh * jnp.mean(gy * xh, axis=-1, keepdims=True))
        if r_ref is not None:
            dx = dx + r_ref[...]
        dx_ref[...] = dx.astype(dx_ref.dtype)
        dg = jnp.sum(dov * xh, axis=0, keepdims=True)
        cs = jnp.sum(dx, axis=0, keepdims=True)
        i = pl.program_id(0)

        @pl.when(i == 0)
        def _():
            dg_ref[...] = dg
            cs_ref[...] = cs

        @pl.when(i > 0)
        def _():
            dg_ref[...] += dg
            cs_ref[...] += cs

    in_specs = [_rows(tr, d), _fixed(1, d), _rows(tr, d)]
    args = [xin, g, dout]
    if res is not None:
        in_specs.append(_rows(tr, d))
        args.append(res)
    return pl.pallas_call(body, grid=(s // tr,), in_specs=in_specs,
                          out_specs=[_rows(tr, d), _fixed(1, d), _fixed(1, d)],
                          out_shape=[SDS((s, d), out_dtype), SDS((1, d), f32), SDS((1, d), f32)],
                          compiler_params=_params("arbitrary"), name=name)(*args)


def final_loss(x, y, g, target, name):
    s, d = x.shape
    tr = _row_tile(s)
    nt = s // tr

    def body(x_ref, y_ref, g_ref, t_ref, dx_ref, loss_ref, acc_ref):
        yv = y_ref[...]
        diff = x_ref[...] + yv * _rms(yv) * g_ref[...] - t_ref[...]
        dx_ref[...] = diff * (1.0 / d)
        sq = jnp.sum(diff * diff, axis=0, keepdims=True)
        i = pl.program_id(0)

        @pl.when(i == 0)
        def _():
            acc_ref[...] = sq

        @pl.when(i > 0)
        def _():
            acc_ref[...] += sq

        @pl.when(i == nt - 1)
        def _():
            total = jnp.sum(acc_ref[...], axis=1, keepdims=True) * (0.5 / d)
            loss_ref[...] = jnp.broadcast_to(total, (1, LANES))

    return pl.pallas_call(body, grid=(nt,), in_specs=[_rows(tr, d), _rows(tr, d), _fixed(1, d), _rows(tr, d)],
                          out_specs=[_rows(tr, d), _fixed(1, LANES)],
                          out_shape=[SDS((s, d), f32), SDS((1, LANES), f32)],
                          scratch_shapes=[pltpu.VMEM((1, d), f32)],
                          compiler_params=_params("arbitrary"), name=name)(x, y, g, target)


def _rotate(v, pair, tc, ts):
    partner = jnp.dot(v.astype(bf16), pair, preferred_element_type=f32)
    return jnp.concatenate([v[:, ch * LANES:(ch + 1) * LANES] * tc + partner[:, ch * LANES:(ch + 1) * LANES] * ts
                            for ch in range(GROUP_WIDTH // LANES)], axis=1)


def qkv_rope(h, w, pair, tabs, name, rider=None):
    s, d = h.shape
    tm = min(s, 1024)
    n_kinds = 3
    grid = (s // tm, n_kinds * N_GROUPS)

    def body(*refs):
        first = jnp.logical_and(pl.program_id(0) == 0, pl.program_id(1) == 0)
        last = jnp.logical_and(pl.program_id(0) == grid[0] - 1, pl.program_id(1) == grid[1] - 1)
        (x_ref, w_ref, p_ref, tc_ref, ts_ref, o_ref), ride_start, ride_finish = _rider_run(rider, refs, 5, 1, first, last)
        ride_start()
        kind = pl.program_id(1) // N_GROUPS
        acc = jnp.dot(x_ref[...], w_ref[...], preferred_element_type=f32)

        @pl.when(kind < 2)
        def _():
            o_ref[...] = _rotate(acc, p_ref[...], tc_ref[...], ts_ref[...]).astype(o_ref.dtype)

        @pl.when(kind == 2)
        def _():
            o_ref[...] = acc.astype(o_ref.dtype)

        ride_finish()

    tab = pl.BlockSpec((tm, LANES), lambda i, j: (i, 0))
    r_in, r_out, r_shapes, r_scratch, r_alias, r_args = _rider_specs(rider, 5, 1)
    outs = pl.pallas_call(
        body, grid=grid,
        in_specs=[pl.BlockSpec((tm, d), lambda i, j: (i, 0)), pl.BlockSpec((d, GROUP_WIDTH), lambda i, j: (0, j)),
                  pl.BlockSpec((GROUP_WIDTH, GROUP_WIDTH), lambda i, j: (0, 0)), tab, tab] + r_in,
        out_specs=[pl.BlockSpec((None, tm, GROUP_WIDTH), lambda i, j: (j % N_GROUPS, i, j // N_GROUPS))] + r_out,
        out_shape=[SDS((N_GROUPS, s, n_kinds * GROUP_WIDTH), bf16)] + r_shapes,
        scratch_shapes=r_scratch, input_output_aliases=r_alias,
        compiler_params=_params("arbitrary", "arbitrary"), name=name)(h, w, pair, *tabs, *r_args)
    return outs[0], outs[1:]


def _attn_mask(j):
    row = lax.broadcasted_iota(jnp.int32, (SPAN, 2 * SPAN), 0)
    col = lax.broadcasted_iota(jnp.int32, (SPAN, 2 * SPAN), 1)
    prev = jnp.logical_and(jnp.logical_and(col < SPAN, col >= row), j > 0)
    return jnp.logical_or(prev, jnp.logical_and(col >= SPAN, col - SPAN <= row))


def _attn_in_specs(gi):
    def at(kind, prev):
        def index(r, j):
            return (gi, jnp.maximum(j - 1, 0) if prev else j, r * 3 + kind)
        return pl.BlockSpec((None, SPAN, GROUP_WIDTH), index)
    return [at(0, False), at(1, False), at(1, True), at(2, False), at(2, True)]


def _phases(qkvg, g, dil):
    if dil == 1:
        return qkvg, g
    _, s, w = qkvg.shape
    return qkvg[g].reshape(1, s // dil, dil * w), 0


def attn_fwd(qkvg, g, dil, name, rider=None):
    a, gi = _phases(qkvg, g, dil)
    s = qkvg.shape[1]
    l = s // dil
    nb = l // SPAN

    def body(*refs):
        first = jnp.logical_and(pl.program_id(0) == 0, pl.program_id(1) == 0)
        last = jnp.logical_and(pl.program_id(0) == dil - 1, pl.program_id(1) == nb - 1)
        own, ride_start, ride_finish = _rider_run(rider, refs, 5, 2, first, last)
        q_ref, ko_ref, kp_ref, vo_ref, vp_ref, o_ref, lse_ref, k_scr, v_scr = own
        ride_start()
        mask = _attn_mask(pl.program_id(1))
        k_scr[0:SPAN, :] = kp_ref[...]
        k_scr[SPAN:2 * SPAN, :] = ko_ref[...]
        v_scr[0:SPAN, :] = vp_ref[...]
        v_scr[SPAN:2 * SPAN, :] = vo_ref[...]
        for h in range(N_SLOTS):
            hs = slice(h * HEAD_DIM, (h + 1) * HEAD_DIM)
            sc = jnp.where(mask, _dot_nt(q_ref[:, hs], k_scr[:, hs]) * (HEAD_DIM ** -0.5), NEG_BIG)
            mx = jnp.max(sc, axis=-1, keepdims=True)
            p = jnp.exp(sc - mx)
            den = jnp.sum(p, axis=-1, keepdims=True)
            o_ref[:, hs] = jnp.dot(p.astype(bf16), v_scr[:, hs], preferred_element_type=f32) / den
            lse_ref[:, hs] = jnp.broadcast_to(mx + jnp.log(den), (SPAN, HEAD_DIM))
        ride_finish()

    out = pl.BlockSpec((SPAN, GROUP_WIDTH), lambda r, j: (j, r))
    r_in, r_out, r_shapes, r_scratch, r_alias, r_args = _rider_specs(rider, 5, 2)
    o, lse, *rode = pl.pallas_call(
        body, grid=(dil, nb), in_specs=_attn_in_specs(gi) + r_in, out_specs=[out, out] + r_out,
        out_shape=[SDS((l, dil * GROUP_WIDTH), f32)] * 2 + r_shapes,
        scratch_shapes=[pltpu.VMEM((2 * SPAN, GROUP_WIDTH), bf16)] * 2 + r_scratch, input_output_aliases=r_alias,
        compiler_params=_params("arbitrary", "arbitrary"), name=name)(a, a, a, a, a, *r_args)
    return o.reshape(s, GROUP_WIDTH), lse.reshape(s, GROUP_WIDTH), lse, rode


def _group_weights(lses):
    mx = jnp.maximum(jnp.maximum(lses[0], lses[1]), lses[2])
    es = [jnp.exp(v - mx) for v in lses]
    inv = 1.0 / (es[0] + es[1] + es[2])
    return [e * inv for e in es]


def mix_fwd(os_, lses, name):
    s, w = os_[0].shape
    tr = _row_tile(s)

    def body(o0, o1, o2, l0, l1, l2, out_ref):
        wg = _group_weights([l0[...], l1[...], l2[...]])
        out_ref[...] = (wg[0] * o0[...] + wg[1] * o1[...] + wg[2] * o2[...]).astype(out_ref.dtype)

    return pl.pallas_call(body, grid=(s // tr,), in_specs=[_rows(tr, w)] * 6, out_specs=_rows(tr, w),
                          out_shape=SDS((s, w), bf16), compiler_params=_params("parallel"), name=name)(*os_, *lses)


def mix_bwd(dmixed, os_, lses, head_ones, name):
    s, w = dmixed.shape
    tr = _row_tile(s)

    def head_sum(t, ones):
        hi = t.astype(bf16)
        lo = (t - hi.astype(f32)).astype(bf16)
        return jnp.dot(hi, ones, preferred_element_type=f32) + jnp.dot(lo, ones, preferred_element_type=f32)

    def body(dm_ref, o0, o1, o2, l0, l1, l2, ones_ref, d0, d1, d2, p0, p1, p2):
        dm = dm_ref[...]
        ones = ones_ref[...]
        wg = _group_weights([l0[...], l1[...], l2[...]])
        mean = sum(wg[k] * head_sum(dm * o[...], ones) for k, o in enumerate((o0, o1, o2)))
        for k, (d_ref, p_ref) in enumerate(((d0, p0), (d1, p1), (d2, p2))):
            d_ref[...] = (wg[k] * dm).astype(d_ref.dtype)
            p_ref[...] = wg[k] * mean

    return pl.pallas_call(body, grid=(s // tr,), in_specs=[_rows(tr, w)] * 7 + [_fixed(w, w)],
                          out_specs=[_rows(tr, w)] * 6,
                          out_shape=[SDS((s, w), bf16)] * 3 + [SDS((s, w), f32)] * 3,
                          compiler_params=_params("parallel"), name=name)(dmixed, *os_, *lses, head_ones)


def attn_bwd(qkvg, do, lse_phased, dterm, g, dil, name, rider=None):
    a, gi = _phases(qkvg, g, dil)
    s = qkvg.shape[1]
    l = s // dil
    nb = l // SPAN
    phased = lambda t: t.reshape(l, dil * GROUP_WIDTH)

    def body(*refs):
        first = jnp.logical_and(pl.program_id(0) == 0, pl.program_id(1) == 0)
        last = jnp.logical_and(pl.program_id(0) == dil - 1, pl.program_id(1) == nb - 1)
        own, ride_start, ride_finish = _rider_run(rider, refs, 8, 5, first, last)
        (q_ref, ko_ref, kp_ref, vo_ref, vp_ref, do_ref, lse_ref, dt_ref, dq_ref, dko_ref, dkp_ref, dvo_ref, dvp_ref,
         k_scr, v_scr) = own
        ride_start()
        mask = _attn_mask(pl.program_id(1))
        scale = HEAD_DIM ** -0.5
        k_scr[0:SPAN, :] = kp_ref[...]
        k_scr[SPAN:2 * SPAN, :] = ko_ref[...]
        v_scr[0:SPAN, :] = vp_ref[...]
        v_scr[SPAN:2 * SPAN, :] = vo_ref[...]
        for h in range(N_SLOTS):
            hs = slice(h * HEAD_DIM, (h + 1) * HEAD_DIM)
            one = slice(h * HEAD_DIM, h * HEAD_DIM + 1)
            q, kk, dov = q_ref[:, hs], k_scr[:, hs], do_ref[:, hs]
            p = jnp.exp(jnp.where(mask, _dot_nt(q, kk) * scale - lse_ref[:, one], NEG_BIG))
            ds = (p * (_dot_nt(dov, v_scr[:, hs]) - dt_ref[:, one]) * scale).astype(bf16)
            dq_ref[:, hs] = jnp.dot(ds, kk, preferred_element_type=f32).astype(dq_ref.dtype)
            dk = _dot_tn(ds, q).astype(dko_ref.dtype)
            dv = _dot_tn(p.astype(bf16), dov).astype(dvo_ref.dtype)
            dkp_ref[:, hs] = dk[:SPAN]
            dko_ref[:, hs] = dk[SPAN:]
            dvp_ref[:, hs] = dv[:SPAN]
            dvo_ref[:, hs] = dv[SPAN:]
        ride_finish()

    blk = pl.BlockSpec((SPAN, GROUP_WIDTH), lambda r, j: (j, r))
    r_in, r_out, r_shapes, r_scratch, r_alias, r_args = _rider_specs(rider, 8, 5)
    outs = pl.pallas_call(
        body, grid=(dil, nb), in_specs=_attn_in_specs(gi) + [blk, blk, blk] + r_in, out_specs=[blk] * 5 + r_out,
        out_shape=[SDS((l, dil * GROUP_WIDTH), bf16)] * 5 + r_shapes,
        scratch_shapes=[pltpu.VMEM((2 * SPAN, GROUP_WIDTH), bf16)] * 2 + r_scratch, input_output_aliases=r_alias,
        compiler_params=_params("arbitrary", "arbitrary"), name=name)(a, a, a, a, a, phased(do), lse_phased, phased(dterm), *r_args)
    return [t.reshape(s, GROUP_WIDTH) for t in outs[:5]] + [outs[5:]]


def dqkv_assemble(parts, pair_t, tabs, name):
    s = parts[0][0].shape[0]
    nblk = s // SPAN
    width = 3 * N_GROUPS * GROUP_WIDTH

    def body(*refs):
        ins, (p_ref, tc_ref, ts_ref, o_ref) = refs[:5 * N_GROUPS], refs[5 * N_GROUPS:]
        pair, tc, ts = p_ref[...], tc_ref[...], ts_ref[...]
        i = pl.program_id(0)
        for g, (_, dil) in enumerate(DILATED_GROUPS):
            dq, dko, dkp, dvo, dvp = ins[5 * g:5 * g + 5]
            has_next = i + dil < nblk
            dk = dko[...].astype(f32) + jnp.where(has_next, dkp[...].astype(f32), 0.0)
            dv = dvo[...].astype(f32) + jnp.where(has_next, dvp[...].astype(f32), 0.0)
            for kind, val in enumerate((_rotate(dq[...].astype(f32), pair, tc, ts), _rotate(dk, pair, tc, ts), dv)):
                base = (kind * N_GROUPS + g) * GROUP_WIDTH
                o_ref[:, base:base + GROUP_WIDTH] = val.astype(o_ref.dtype)

    here = _rows(SPAN, GROUP_WIDTH)
    in_specs, args = [], []
    for g, (_, dil) in enumerate(DILATED_GROUPS):
        ahead = pl.BlockSpec((SPAN, GROUP_WIDTH), functools.partial(lambda i, dil: (jnp.minimum(i + dil, nblk - 1), 0), dil=dil))
        in_specs += [here, here, ahead, here, ahead]
        args += list(parts[g])
    tab = _rows(SPAN, LANES)
    return pl.pallas_call(body, grid=(nblk,), in_specs=in_specs + [_fixed(GROUP_WIDTH, GROUP_WIDTH), tab, tab],
                          out_specs=_rows(SPAN, width), out_shape=SDS((s, width), bf16),
                          compiler_params=_params("parallel"), name=name)(*args, pair_t, *tabs)


def ffn_act_fwd(u, wb, name):
    _, nbk, s, c = u.shape
    tr = min(s, 256)

    def body(u_ref, h_ref, wb_ref, a_ref, ug_ref, su, sg):
        first = pl.program_id(1) == 0
        for lc in range(c // LANES):
            ln = slice(lc * LANES, (lc + 1) * LANES)
            for half, scr in enumerate((su, sg)):
                scr[0:FFN_HALO, ln] = jnp.where(first, 0.0, h_ref[half, :, ln].astype(f32))
            for r0 in range(0, tr, STRIP_ROWS):
                rows = slice(r0, r0 + STRIP_ROWS)
                conv = []
                for half, scr in enumerate((su, sg)):
                    xv = u_ref[half, rows, ln].astype(f32)
                    scr[FFN_HALO + r0:FFN_HALO + r0 + STRIP_ROWS, ln] = xv
                    acc = wb_ref[half, FFN_CONV:FFN_CONV + 1, ln] + wb_ref[half, FFN_CONV - 1:FFN_CONV, ln] * xv
                    for k in range(FFN_CONV - 1):
                        acc = acc + wb_ref[half, k:k + 1, ln] * scr[pl.ds(FFN_HALO + r0 - (FFN_CONV - 1) + k, STRIP_ROWS), ln]
                    ug_ref[half, rows, ln] = acc.astype(ug_ref.dtype)
                    conv.append(acc)
                up, gate = conv
                a_ref[rows, ln] = (gate * _sigmoid(gate) * up).astype(a_ref.dtype)

    both = pl.BlockSpec((2, None, tr, c), lambda p, i: (0, p, i, 0))
    return pl.pallas_call(
        body, grid=(nbk, s // tr),
        in_specs=[both,
                  pl.BlockSpec((2, None, FFN_HALO, c), lambda p, i: (0, p, jnp.maximum(i * (tr // FFN_HALO) - 1, 0), 0)),
                  pl.BlockSpec((None, 2, 8, c), lambda p, i: (p, 0, 0, 0))],
        out_specs=[pl.BlockSpec((None, tr, c), lambda p, i: (p, i, 0)), both],
        out_shape=[SDS((nbk, s, c), bf16), SDS(u.shape, bf16)],
        scratch_shapes=[pltpu.VMEM((tr + FFN_HALO, c), f32)] * 2,
        compiler_params=_params("parallel", "arbitrary"), name=name)(u, u, wb)


def ffn_act_bwd(da, ug, u, wb, name, rider=None):
    _, nbk, s, c = u.shape
    tr = min(s, 256)
    nt = s // tr

    def body(*refs):
        step = pl.program_id(1)
        first = jnp.logical_and(pl.program_id(0) == 0, step == 0)
        last = jnp.logical_and(pl.program_id(0) == nbk - 1, step == nt - 1)
        (da_ref, ug_ref, u_ref, wb_ref, du_ref, dwb_ref, eu, eg), ride_start, ride_finish = _rider_run(rider, refs, 4, 2, first, last)
        ride_start()

        @pl.when(step == 0)
        def _():
            eu[tr:tr + FFN_TAIL, :] = jnp.zeros((FFN_TAIL, c), f32)
            eg[tr:tr + FFN_TAIL, :] = jnp.zeros((FFN_TAIL, c), f32)
            dwb_ref[...] = jnp.zeros(dwb_ref.shape, f32)

        fold = lambda t: jnp.sum(t.reshape(STRIP_ROWS // 8, 8, LANES), axis=0)
        for lc in range(c // LANES):
            ln = slice(lc * LANES, (lc + 1) * LANES)
            sums = [[jnp.zeros((8, LANES), f32) for _ in range(FFN_CONV + 1)] for _ in range(2)]
            for r0 in reversed(range(0, tr, STRIP_ROWS)):
                rows = slice(r0, r0 + STRIP_ROWS)
                up, gate = ug_ref[0, rows, ln].astype(f32), ug_ref[1, rows, ln].astype(f32)
                sig = _sigmoid(gate)
                dav = da_ref[rows, ln].astype(f32)
                grads = (dav * (gate * sig), dav * up * (sig * (1.0 + gate * (1.0 - sig))))
                for half, ext in enumerate((eu, eg)):
                    dv = grads[half]
                    ext[rows, ln] = dv
                    xv = u_ref[half, rows, ln].astype(f32)
                    acc = None
                    for k in range(FFN_CONV):
                        ahead = dv if k == FFN_CONV - 1 else ext[pl.ds(r0 + FFN_CONV - 1 - k, STRIP_ROWS), ln]
                        term = wb_ref[half, k:k + 1, ln] * ahead
                        acc = term if acc is None else acc + term
                        sums[half][k] = sums[half][k] + fold(xv * ahead)
                    sums[half][FFN_CONV] = sums[half][FFN_CONV] + fold(dv)
                    du_ref[half, rows, ln] = acc.astype(du_ref.dtype)
            for half, ext in enumerate((eu, eg)):
                ext[tr:tr + FFN_TAIL, ln] = ext[0:FFN_TAIL, ln]
                for k in range(FFN_CONV + 1):
                    dwb_ref[half, k:k + 1, ln] += jnp.sum(sums[half][k], axis=0, keepdims=True)
        ride_finish()

    rev = lambda i: nt - 1 - i
    both = pl.BlockSpec((2, None, tr, c), lambda p, i: (0, p, rev(i), 0))
    r_in, r_out, r_shapes, r_scratch, r_alias, r_args = _rider_specs(rider, 4, 2)
    outs = pl.pallas_call(
        body, grid=(nbk, nt),
        in_specs=[pl.BlockSpec((None, tr, c), lambda p, i: (p, rev(i), 0)), both, both,
                  pl.BlockSpec((None, 2, 8, c), lambda p, i: (p, 0, 0, 0))] + r_in,
        out_specs=[both, pl.BlockSpec((None, 2, 8, c), lambda p, i: (p, 0, 0, 0))] + r_out,
        out_shape=[SDS((2, nbk, s, c), bf16), SDS((nbk, 2, 8, c), f32)] + r_shapes,
        scratch_shapes=[pltpu.VMEM((tr + FFN_TAIL, c), f32)] * 2 + r_scratch, input_output_aliases=r_alias,
        compiler_params=_params("arbitrary", "arbitrary"), name=name)(da, ug, u, wb, *r_args)
    return outs[0], outs[1], outs[2:]


def _glu(zv, c):
    return zv[:, :c] * _sigmoid(zv[:, c:])


def _conv_fill(z_ref, h_ref, scr, first, tr, c):
    scr[0:CONV_HALO, :] = jnp.where(first, 0.0, _glu(h_ref[...], c))
    scr[CONV_HALO:CONV_HALO + tr, :] = _glu(z_ref[...], c)


def _conv_taps(b):
    return [(a, CONV_KERNEL - 1 - 8 * a - b) for a in range(CONV_HALO // 8) if CONV_KERNEL - 1 - 8 * a - b >= 0]


def _layernorm_parts(cv):
    mu = jnp.mean(cv, axis=-1, keepdims=True)
    cen = cv - mu
    rstd = lax.rsqrt(jnp.mean(cen * cen, axis=-1, keepdims=True) + EPS)
    return cen * rstd, rstd


def conv_module_fwd(z, wdw, vecs, name):
    s, c2 = z.shape
    c = c2 // 2
    tr = min(s, 256)

    def body(z_ref, h_ref, w_ref, v_ref, c_ref, s_ref, scr, zb):
        _conv_fill(z_ref, h_ref, scr, pl.program_id(0) == 0, tr, c)
        acc = jnp.broadcast_to(v_ref[0:1, :], (tr, c))
        for b in range(8):
            part = None
            for a, j in _conv_taps(b):
                term = w_ref[j:j + 1, :] * scr[pl.ds(CONV_HALO - 8 - 8 * a, tr + 8), :]
                part = term if part is None else part + term
            if b == 0:
                acc = acc + part[8:]
            else:
                zb[...] = part
                acc = acc + zb[pl.ds(8 - b, tr), :]
        c_ref[...] = acc
        chat, _ = _layernorm_parts(acc)
        ln = chat * v_ref[1:2, :] + v_ref[2:3, :]
        s_ref[...] = (ln * _sigmoid(ln)).astype(s_ref.dtype)

    return pl.pallas_call(
        body, grid=(s // tr,),
        in_specs=[_rows(tr, c2), pl.BlockSpec((CONV_HALO, c2), lambda i: (jnp.maximum(i * (tr // CONV_HALO) - 1, 0), 0)),
                  _fixed(CONV_HALO, c), _fixed(8, c)],
        out_specs=[_rows(tr, c), _rows(tr, c)], out_shape=[SDS((s, c), f32), SDS((s, c), bf16)],
        scratch_shapes=[pltpu.VMEM((tr + CONV_HALO, c), f32), pltpu.VMEM((tr + 8, c), f32)],
        compiler_params=_params("arbitrary"), name=name)(z, z, wdw, vecs)


def conv_module_bwd(ds, cpre, z, wdw, vecs, name):
    s, c2 = z.shape
    c = c2 // 2
    tr = min(s, 256)
    nt = s // tr

    def body(ds_ref, c_ref, z_ref, w_ref, v_ref, dz_ref, dw_ref, dv_ref, db_ref, ext, dwp):
        step = pl.program_id(0)

        @pl.when(step == 0)
        def _():
            ext[tr:tr + CONV_HALO, :] = jnp.zeros((CONV_HALO, c), f32)
            dwp[...] = jnp.zeros(dwp.shape, f32)
            dv_ref[...] = jnp.zeros(dv_ref.shape, f32)
            db_ref[...] = jnp.zeros(db_ref.shape, f32)

        gain, bias = v_ref[1:2, :], v_ref[2:3, :]
        fold16 = lambda t: t[:8] + t[8:]
        sums = [jnp.zeros((8, c), f32) for _ in range(3)]
        for r0 in range(0, tr, 16):
            rows = slice(r0, r0 + 16)
            chat, rstd = _layernorm_parts(c_ref[rows, :])
            ln = chat * gain + bias
            sig = _sigmoid(ln)
            dln = ds_ref[rows, :].astype(f32) * (sig * (1.0 + ln * (1.0 - sig)))
            gy = dln * gain
            dc = rstd * (gy - jnp.mean(gy, axis=-1, keepdims=True) - chat * jnp.mean(gy * chat, axis=-1, keepdims=True))
            ext[rows, :] = dc
            for k, t in enumerate((dc, dln * chat, dln)):
                sums[k] = sums[k] + fold16(t)
        for k in range(3):
            dv_ref[k:k + 1, :] += jnp.sum(sums[k], axis=0, keepdims=True)

        fold = lambda t: jnp.sum(t.reshape(STRIP_ROWS // 8, 8, LANES), axis=0)
        for lc in range(c // LANES):
            ln_a = slice(lc * LANES, (lc + 1) * LANES)
            ln_g = slice(c + lc * LANES, c + (lc + 1) * LANES)
            dbs = [jnp.zeros((8, LANES), f32) for _ in range(2)]
            for r0 in range(0, tr, STRIP_ROWS):
                rows = slice(r0, r0 + STRIP_ROWS)
                a, sg = z_ref[rows, ln_a], _sigmoid(z_ref[rows, ln_g])
                uv = a * sg
                du = jnp.zeros((STRIP_ROWS, LANES), f32)
                for b in range(8):
                    src = ext[pl.ds(r0 + b, STRIP_ROWS + CONV_HALO - 8), ln_a]
                    for a8, j in _conv_taps(b):
                        ahead = src[8 * a8:8 * a8 + STRIP_ROWS]
                        du = du + w_ref[j:j + 1, ln_a] * ahead
                        dwp[8 * j:8 * j + 8, ln_a] += fold(uv * ahead)
                da = du * sg
                dg = du * a * (sg * (1.0 - sg))
                dz_ref[rows, ln_a] = da.astype(dz_ref.dtype)
                dz_ref[rows, ln_g] = dg.astype(dz_ref.dtype)
                dbs = [dbs[0] + fold(da), dbs[1] + fold(dg)]
            db_ref[:, ln_a] += jnp.sum(dbs[0], axis=0, keepdims=True)
            db_ref[:, ln_g] += jnp.sum(dbs[1], axis=0, keepdims=True)
        ext[tr:tr + CONV_HALO, :] = ext[0:CONV_HALO, :]

        @pl.when(step == nt - 1)
        def _():
            for j in range(CONV_HALO):
                dw_ref[j:j + 1, :] = jnp.sum(dwp[8 * j:8 * j + 8, :], axis=0, keepdims=True)

    rev = lambda i: nt - 1 - i
    back = lambda d: pl.BlockSpec((tr, d), lambda i: (rev(i), 0))
    return pl.pallas_call(
        body, grid=(nt,),
        in_specs=[back(c), back(c), back(c2), _fixed(CONV_HALO, c), _fixed(8, c)],
        out_specs=[back(c2), _fixed(CONV_HALO, c), _fixed(8, c), _fixed(1, c2)],
        out_shape=[SDS((s, c2), bf16), SDS((CONV_HALO, c), f32), SDS((8, c), f32), SDS((1, c2), f32)],
        scratch_shapes=[pltpu.VMEM((tr + CONV_HALO, c), f32), pltpu.VMEM((8 * CONV_HALO, c), f32)],
        compiler_params=_params("arbitrary"), name=name)(ds, cpre, z, wdw, vecs)


def _tile2d(r, n):
    tn = n if n <= 2048 else 1024
    tr = r
    while tr * tn * 4 > (1 << 21) and tr % 16 == 0:
        tr //= 2
    assert r % tr == 0 and n % tn == 0
    return tr, tn


def adamw(w, g, m, v, name):
    r, n = w.shape
    tr, tn = _tile2d(r, n)

    def body(w_ref, g_ref, m_ref, v_ref, d_ref, nm_ref, nv_ref):
        gv = g_ref[...]
        nm = ADAM_B1 * m_ref[...] + (1.0 - ADAM_B1) * gv
        nv = ADAM_B2 * v_ref[...] + (1.0 - ADAM_B2) * (gv * gv)
        m_hat = nm / (1.0 - ADAM_B1 ** ADAM_STEP)
        v_hat = nv / (1.0 - ADAM_B2 ** ADAM_STEP)
        d_ref[...] = -ADAM_LR * (m_hat / (jnp.sqrt(v_hat) + ADAM_EPS) + ADAM_WD * w_ref[...])
        nm_ref[...] = nm
        nv_ref[...] = nv

    blk = pl.BlockSpec((tr, tn), lambda i, j: (i, j))
    return pl.pallas_call(body, grid=(r // tr, n // tn), in_specs=[blk] * 4, out_specs=[blk] * 3,
                          out_shape=[SDS((r, n), f32)] * 3, compiler_params=_params("parallel", "parallel"),
                          name=name)(w, g, m, v)


def add_core_halves(grad, got, where, name):
    _, _, rh, n = grad.shape
    tr, tn = _tile2d(rh, n)

    def body(w_ref, a_ref, b_ref, o_ref):
        o_ref[...] = (a_ref[...] + b_ref[...]).astype(o_ref.dtype)

    return pl.pallas_call(
        body,
        grid_spec=pltpu.PrefetchScalarGridSpec(
            num_scalar_prefetch=1, grid=(N_CHIPS, rh // tr, n // tn),
            in_specs=[pl.BlockSpec((None, None, tr, tn), lambda p, i, j, w_ref: (p, w_ref[0], i, j)),
                      pl.BlockSpec((None, tr, tn), lambda p, i, j, w_ref: (p, i, j))],
            out_specs=pl.BlockSpec((None, tr, tn), lambda p, i, j, w_ref: (p, i, j))),
        out_shape=SDS((N_CHIPS, rh, n), bf16), compiler_params=_params("parallel", "parallel", "parallel"),
        name=name)(where, grad, got)


def add_chip_parts(grad, got_core, got_chips, where, name):
    _, _, rh, n = grad.shape
    tr, tn = _tile2d(rh, n)

    def body(w_ref, a_ref, b_ref, g_ref, o_ref):
        acc = a_ref[...] + b_ref[...]
        for k in range(N_CHIPS - 1):
            acc = acc + g_ref[k].astype(f32)
        o_ref[...] = acc

    return pl.pallas_call(
        body,
        grid_spec=pltpu.PrefetchScalarGridSpec(
            num_scalar_prefetch=1, grid=(rh // tr, n // tn),
            in_specs=[pl.BlockSpec((None, None, tr, tn), lambda i, j, w_ref: (w_ref[1], w_ref[0], i, j)),
                      pl.BlockSpec((None, tr, tn), lambda i, j, w_ref: (w_ref[1], i, j)),
                      pl.BlockSpec((N_CHIPS - 1, tr, tn), lambda i, j, w_ref: (0, i, j))],
            out_specs=pl.BlockSpec((None, tr, tn), lambda i, j, w_ref: (w_ref[0], i, j))),
        out_shape=SDS((2, rh, n), f32), compiler_params=_params("parallel", "parallel"),
        name=name)(where, grad, got_core, got_chips)


def sum_devices(parts, name):
    nd, r, n = parts.shape

    def body(p_ref, o_ref):
        acc = p_ref[0]
        for k in range(1, nd):
            acc = acc + p_ref[k]
        o_ref[...] = acc

    return pl.pallas_call(body, out_shape=SDS((r, n), f32), name=name)(parts)


def _position():
    return lax.axis_index("x"), lax.axis_index("y"), lax.axis_index("c")


def _other_chips(x, y):
    return [(1 - x, y), (x, 1 - y), (1 - x, 1 - y)]


def _remote(src, dst, send, recv, to):
    return pltpu.make_async_remote_copy(src_ref=src, dst_ref=dst, send_sem=send, recv_sem=recv, device_id=to,
                                        device_id_type=MESH)


def gather_chips(bufs, name):
    n = len(bufs)
    nk = N_CHIPS - 1

    def body(*refs):
        bufs_ = refs[n:2 * n]
        send, recv = refs[2 * n:]
        x, y, c = _position()
        me = 2 * x + y
        chips = _other_chips(x, y)
        sends = []
        for t in range(n):
            for k, (px, py) in enumerate(chips):
                out = _remote(bufs_[t].at[me, c], bufs_[t].at[me, c], send.at[t, k], recv.at[t, k], (px, py, c))
                out.start()
                sends.append(out)
        for t in range(n):
            for k, (px, py) in enumerate(chips):
                piece = bufs_[t].at[2 * px + py, c]
                _remote(piece, piece, send.at[t, k], recv.at[t, k], (px, py, c)).wait_recv()
                on = _remote(piece, piece, send.at[t, nk + k], recv.at[t, nk + k], (x, y, 1 - c))
                on.start()
                sends.append(on)
        for t in range(n):
            for k, (px, py) in enumerate(chips):
                piece = bufs_[t].at[2 * px + py, 1 - c]
                _remote(piece, piece, send.at[t, nk + k], recv.at[t, nk + k], (x, y, 1 - c)).wait_recv()
        for cp in sends:
            cp.wait_send()

    return pl.pallas_call(
        body, in_specs=[ANY] * n, out_specs=[ANY] * n,
        out_shape=[SDS(a.shape, a.dtype) for a in bufs],
        input_output_aliases={t: t for t in range(n)},
        scratch_shapes=[pltpu.SemaphoreType.DMA((n, 2 * nk)), pltpu.SemaphoreType.DMA((n, 2 * nk))],
        name=name)(*bufs)


class Rider(NamedTuple):
    operands: tuple
    n_aliased: int
    out_shapes: tuple
    scratch: tuple
    start: Callable
    finish: Callable


def _rider_specs(rider, n_inputs, n_outputs):
    if rider is None:
        return [], [], [], [], {}, []
    aliased = [SDS(a.shape, a.dtype) for a in rider.operands[:rider.n_aliased]]
    outs = aliased + list(rider.out_shapes)
    aliases = {n_inputs + t: n_outputs + t for t in range(rider.n_aliased)}
    return [ANY] * len(rider.operands), [ANY] * len(outs), outs, list(rider.scratch), aliases, list(rider.operands)


def _rider_run(rider, refs, n_inputs, n_outputs, first, last):
    if rider is None:
        return refs, lambda: None, lambda: None
    n_op = len(rider.operands)
    n_out = rider.n_aliased + len(rider.out_shapes)
    own_in, r_in = refs[:n_inputs], refs[n_inputs:n_inputs + n_op]
    own_out = refs[n_inputs + n_op:n_inputs + n_op + n_outputs]
    r_out = refs[n_inputs + n_op + n_outputs:n_inputs + n_op + n_outputs + n_out]
    rest = refs[n_inputs + n_op + n_outputs + n_out:]
    n_sem = len(rider.scratch)
    sems, own_scratch = rest[len(rest) - n_sem:], rest[:len(rest) - n_sem]

    def start():
        pl.when(first)(lambda: rider.start(r_in, r_out, sems))

    def finish():
        pl.when(last)(lambda: rider.finish(r_in, r_out, sems))

    return list(own_in) + list(own_out) + list(own_scratch), start, finish


def gather_rider(bufs):
    n = len(bufs)
    nk = N_CHIPS - 1

    def ici(bufs_, send, recv, x, y, c, t, k, px, py, own):
        piece = bufs_[t].at[2 * x + y if own else 2 * px + py, c]
        return _remote(piece, piece, send.at[t, k], recv.at[t, k], (px, py, c))

    def d2d(bufs_, send, recv, x, y, c, t, k, px, py, mine):
        piece = bufs_[t].at[2 * px + py, c if mine else 1 - c]
        return _remote(piece, piece, send.at[t, nk + k], recv.at[t, nk + k], (x, y, 1 - c))

    def start(r_in, r_out, sems):
        send, recv = sems
        x, y, c = _position()
        for t in range(n):
            for k, (px, py) in enumerate(_other_chips(x, y)):
                ici(r_out, send, recv, x, y, c, t, k, px, py, True).start()

    def finish(r_in, r_out, sems):
        send, recv = sems
        x, y, c = _position()
        chips = _other_chips(x, y)
        for t in range(n):
            for k, (px, py) in enumerate(chips):
                ici(r_out, send, recv, x, y, c, t, k, px, py, False).wait_recv()
                d2d(r_out, send, recv, x, y, c, t, k, px, py, True).start()
        for t in range(n):
            for k, (px, py) in enumerate(chips):
                d2d(r_out, send, recv, x, y, c, t, k, px, py, False).wait_recv()
        for t in range(n):
            for k, (px, py) in enumerate(chips):
                ici(r_out, send, recv, x, y, c, t, k, px, py, True).wait_send()
                d2d(r_out, send, recv, x, y, c, t, k, px, py, True).wait_send()

    sems = (pltpu.SemaphoreType.DMA((n, 2 * nk)), pltpu.SemaphoreType.DMA((n, 2 * nk)))
    return Rider(tuple(bufs), n, (), sems, start, finish)


def scatter_rider(parts):
    n = len(parts)

    def copies(r_in, r_out, sems):
        send, recv = sems
        x, y, c = _position()
        return [_remote(r_in[t].at[2 * px + py], r_out[t].at[k], send.at[t, k], recv.at[t, k], (px, py, c))
                for t in range(n) for k, (px, py) in enumerate(_other_chips(x, y))]

    def start(r_in, r_out, sems):
        for cp in copies(r_in, r_out, sems):
            cp.start()

    def finish(r_in, r_out, sems):
        for cp in copies(r_in, r_out, sems):
            cp.wait()

    sems = (pltpu.SemaphoreType.DMA((n, N_CHIPS - 1)), pltpu.SemaphoreType.DMA((n, N_CHIPS - 1)))
    return Rider(tuple(parts), 0, tuple(SDS((N_CHIPS - 1,) + a.shape[1:], a.dtype) for a in parts), sems, start, finish)


def swap_core_halves(grads, name):
    n = len(grads)

    def body(*refs):
        ins, outs = refs[:n], refs[n:2 * n]
        send, recv = refs[2 * n:]
        x, y, c = _position()
        pending = []
        for t in range(n):
            out = _remote(ins[t].at[:, 1 - c], outs[t], send.at[t], recv.at[t], (x, y, 1 - c))
            out.start()
            pending.append(out.wait)
        for wait in pending:
            wait()

    return pl.pallas_call(
        body, in_specs=[ANY] * n, out_specs=[ANY] * n,
        out_shape=[SDS((a.shape[0],) + a.shape[2:], a.dtype) for a in grads],
        scratch_shapes=[pltpu.SemaphoreType.DMA((n,)), pltpu.SemaphoreType.DMA((n,))],
        name=name)(*grads)


def scatter_chips(parts, name):
    n = len(parts)

    def body(*refs):
        ins, outs = refs[:n], refs[n:2 * n]
        send, recv = refs[2 * n:]
        x, y, c = _position()
        pending = []
        for t in range(n):
            for k, (px, py) in enumerate(_other_chips(x, y)):
                out = _remote(ins[t].at[2 * px + py], outs[t].at[k], send.at[t, k], recv.at[t, k], (px, py, c))
                out.start()
                pending.append(out.wait)
        for wait in pending:
            wait()

    return pl.pallas_call(
        body, in_specs=[ANY] * n, out_specs=[ANY] * n,
        out_shape=[SDS((N_CHIPS - 1,) + a.shape[1:], a.dtype) for a in parts],
        scratch_shapes=[pltpu.SemaphoreType.DMA((n, N_CHIPS - 1)), pltpu.SemaphoreType.DMA((n, N_CHIPS - 1))],
        name=name)(*parts)


def share_core_halves(bufs, name):
    n = len(bufs)

    def body(*refs):
        bufs_ = refs[n:2 * n]
        send, recv = refs[2 * n:]
        x, y, c = _position()
        pending = []
        for t in range(n):
            out = _remote(bufs_[t].at[c], bufs_[t].at[c], send.at[t], recv.at[t], (x, y, 1 - c))
            out.start()
            pending.append(out.wait_send)
            other = bufs_[t].at[1 - c]
            pending.append(_remote(other, other, send.at[t], recv.at[t], (x, y, 1 - c)).wait_recv)
        for wait in pending:
            wait()

    return pl.pallas_call(
        body, in_specs=[ANY] * n, out_specs=[ANY] * n, out_shape=[SDS(a.shape, a.dtype) for a in bufs],
        input_output_aliases={t: t for t in range(n)},
        scratch_shapes=[pltpu.SemaphoreType.DMA((n,)), pltpu.SemaphoreType.DMA((n,))],
        name=name)(*bufs)


def gather_devices(v, name):
    flips = [(dx, dy, dc) for dx in (0, 1) for dy in (0, 1) for dc in (0, 1)][1:]

    def body(v_ref, o_ref, send, recv, local):
        x, y, c = _position()
        me = 4 * x + 2 * y + c
        own = pltpu.make_async_copy(v_ref, o_ref.at[me], local)
        own.start()
        pending = [own.wait]
        for k, (dx, dy, dc) in enumerate(flips):
            px, py, pc = x ^ dx, y ^ dy, c ^ dc
            out = _remote(v_ref, o_ref.at[me], send.at[k], recv.at[k], (px, py, pc))
            out.start()
            pending.append(out.wait_send)
            pending.append(_remote(v_ref, o_ref.at[4 * px + 2 * py + pc], send.at[k], recv.at[k], (px, py, pc)).wait_recv)
        for wait in pending:
            wait()

    return pl.pallas_call(
        body, in_specs=[ANY], out_specs=ANY, out_shape=SDS((8,) + v.shape, v.dtype),
        scratch_shapes=[pltpu.SemaphoreType.DMA((7,)), pltpu.SemaphoreType.DMA((7,)), pltpu.SemaphoreType.DMA],
        name=name)(v)


SMALL = ("norm_g", "conv_b_pw1", "conv_w_dw", "conv_b_dw", "conv_ln_g", "conv_ln_b", "conv_b_pw2", "ffn_w_dw")


def _pack_rows(arrs, rows):
    flat = jnp.concatenate([a.reshape(-1, LANES) for a in arrs], axis=0)
    return jnp.pad(flat, ((0, rows - flat.shape[0]), (0, 0)))


def _unpack_rows(packed, shapes):
    out, at = [], 0
    for shp in shapes:
        size = 1
        for dim in shp:
            size *= dim
        rows = size // LANES
        out.append(packed[..., at:at + rows, :].reshape(packed.shape[:-2] + tuple(shp)))
        at += rows
    return out


def _join_last(t):
    t = jnp.moveaxis(t, 0, -2)
    return t.reshape(t.shape[:-2] + (t.shape[-2] * t.shape[-1],))


def _split_last(t):
    t = t.reshape(t.shape[:-1] + (N_CHIPS, t.shape[-1] // N_CHIPS))
    return jnp.moveaxis(t, -2, 0)


def _rope_tables(positions):
    half = ROT_DIM // 2
    inv_freq = ROPE_THETA ** (-jnp.arange(half, dtype=f32) / half)
    ang = positions.astype(f32).reshape(-1, 1) * inv_freq
    cos, sin = jnp.cos(ang), jnp.sin(ang)
    s = ang.shape[0]
    rest = HEAD_DIM - ROT_DIM
    head = lambda t, fill: jnp.concatenate([t, t, jnp.full((s, rest), fill, f32)], axis=1)
    twice = lambda t: jnp.concatenate([t] * (LANES // HEAD_DIM), axis=1)
    return twice(head(cos, 1.0)), twice(head(sin, 0.0))


def _pairing_matrix():
    half = ROT_DIM // 2
    row = jnp.arange(GROUP_WIDTH)[:, None]
    col = jnp.arange(GROUP_WIDTH)[None, :]
    d = col % HEAD_DIM
    minus = jnp.logical_and(row == col + half, d < half)
    plus = jnp.logical_and(row == col - half, jnp.logical_and(d >= half, d < ROT_DIM))
    return (plus.astype(f32) - minus.astype(f32)).astype(bf16)


def _pad_rows(t, rows):
    return jnp.pad(t, ((0, rows - t.shape[0]), (0, 0)))


def _ffn_pack(w_dw, b_dw):
    t = jnp.concatenate([w_dw, b_dw[None]], axis=0)
    t = t.reshape(FFN_CONV + 1, 2, 2, -1)
    t = jnp.transpose(t, (2, 1, 0, 3))
    return jnp.pad(t, ((0, 0), (0, 0), (0, 8 - (FFN_CONV + 1)), (0, 0)))


def _ffn_unpack(d):
    t = jnp.transpose(d[:, :, :FFN_CONV + 1], (2, 1, 0, 3)).reshape(FFN_CONV + 1, -1)
    return t[:FFN_CONV], t[FFN_CONV]


def _ffn_block(h, w_up, li, w_down, wb, tag):
    s = h.shape[0]
    u = mm_nn(h[None], w_up, li, None, N_CHIPS, bf16, 1024, w_up.shape[-1], f"{tag}_up")
    u4 = u.reshape(2, 2, s, u.shape[-1])
    a, ug = ffn_act_fwd(u4, wb, f"{tag}_act")
    y = mm_nn(a, w_down, 0, None, 1, f32, 512, 512, f"{tag}_down")[0]
    return y, (h, u4, ug, a)


def _ffn_block_bwd(dy, saved, w_up, li, w_down, wb, tag, rider=None):
    h, u4, ug, a = saved
    d_down = mm_tn(a, dy[None], 1, 1024, a.shape[-1], 512, f"{tag}_dwdown")
    da = mm_nt(dy[None], w_down, 0, 2, bf16, 1024, a.shape[-1], w_down.shape[-1], f"{tag}_da")
    du4, dwb, rode = ffn_act_bwd(da, ug, u4, wb, f"{tag}_actbwd", rider)
    du = du4.reshape((N_CHIPS,) + du4.shape[2:])
    d_up = mm_tn(h[None], du, N_CHIPS, 1024, h.shape[-1], du.shape[-1], f"{tag}_dwup")
    dh = mm_nt_whole(du, w_up, li, f32, 512, f"{tag}_dh")
    return dh, d_up, d_down, dwb, rode


def kernel(x, positions, norm_g, attn_w_qkv, attn_w_o, conv_w_pw1, conv_b_pw1, conv_w_dw, conv_b_dw, conv_ln_g, conv_ln_b, conv_w_pw2, conv_b_pw2, ffn_w_up, ffn_w_dw, ffn_b_dw, ffn_w_down, loss_target, m_norm_g, m_attn_w_qkv, m_attn_w_o, m_conv_w_pw1, m_conv_b_pw1, m_conv_w_dw, m_conv_b_dw, m_conv_ln_g, m_conv_ln_b, m_conv_w_pw2, m_conv_b_pw2, m_ffn_w_up, m_ffn_w_dw, m_ffn_b_dw, m_ffn_w_down, v_norm_g, v_attn_w_qkv, v_attn_w_o, v_conv_w_pw1, v_conv_b_pw1, v_conv_w_dw, v_conv_b_dw, v_conv_ln_g, v_conv_ln_b, v_conv_w_pw2, v_conv_b_pw2, v_ffn_w_up, v_ffn_w_dw, v_ffn_b_dw, v_ffn_w_down):
    weights = dict(norm_g=norm_g, attn_w_qkv=attn_w_qkv, attn_w_o=attn_w_o, conv_w_pw1=conv_w_pw1, conv_b_pw1=conv_b_pw1,
                   conv_w_dw=conv_w_dw, conv_b_dw=conv_b_dw, conv_ln_g=conv_ln_g, conv_ln_b=conv_ln_b, conv_w_pw2=conv_w_pw2,
                   conv_b_pw2=conv_b_pw2, ffn_w_up=ffn_w_up, ffn_w_dw=ffn_w_dw, ffn_b_dw=ffn_b_dw, ffn_w_down=ffn_w_down)
    mom1 = dict(norm_g=m_norm_g, attn_w_qkv=m_attn_w_qkv, attn_w_o=m_attn_w_o, conv_w_pw1=m_conv_w_pw1, conv_b_pw1=m_conv_b_pw1,
                conv_w_dw=m_conv_w_dw, conv_b_dw=m_conv_b_dw, conv_ln_g=m_conv_ln_g, conv_ln_b=m_conv_ln_b, conv_w_pw2=m_conv_w_pw2,
                conv_b_pw2=m_conv_b_pw2, ffn_w_up=m_ffn_w_up, ffn_w_dw=m_ffn_w_dw, ffn_b_dw=m_ffn_b_dw, ffn_w_down=m_ffn_w_down)
    mom2 = dict(norm_g=v_norm_g, attn_w_qkv=v_attn_w_qkv, attn_w_o=v_attn_w_o, conv_w_pw1=v_conv_w_pw1, conv_b_pw1=v_conv_b_pw1,
                conv_w_dw=v_conv_w_dw, conv_b_dw=v_conv_b_dw, conv_ln_g=v_conv_ln_g, conv_ln_b=v_conv_ln_b, conv_w_pw2=v_conv_w_pw2,
                conv_b_pw2=v_conv_b_pw2, ffn_w_up=v_ffn_w_up, ffn_w_dw=v_ffn_w_dw, ffn_b_dw=v_ffn_b_dw, ffn_w_down=v_ffn_w_down)
    big = ("attn_w_qkv", "attn_w_o", "conv_w_pw1", "conv_w_pw2", "ffn_w_up", "ffn_w_down")
    xi, yi, ci = _position()
    chip = (2 * xi + yi).astype(jnp.int32).reshape(1)
    where = jnp.stack([ci, 2 * xi + yi]).astype(jnp.int32)

    x = x[0]
    target = loss_target[0]
    s, d = x.shape

    small_shapes = [weights[k].shape for k in SMALL]
    small_rows = -(-sum(weights[k].size for k in SMALL) // LANES // 8) * 8
    small_w = _pack_rows([weights[k] for k in SMALL], small_rows)
    def own_slot(shard):
        halves = shard.reshape(1, 2, -1, shard.shape[-1])
        return lax.dynamic_update_slice(lax.empty((N_CHIPS,) + halves.shape[1:], shard.dtype), halves, (chip[0], 0, 0, 0))

    early, late = big[:2], big[2:]
    gathered = gather_chips([own_slot(weights[k].astype(bf16)) for k in early] + [own_slot(small_w)], "gather_weights_early")
    gw = {k: t.reshape((N_CHIPS,) + weights[k].shape) for k, t in zip(early, gathered[:-1])}
    full_small = dict(zip(SMALL, [_join_last(t) for t in _unpack_rows(gathered[-1].reshape(N_CHIPS, small_rows, LANES), small_shapes)]))
    w_qkv, w_o = gw["attn_w_qkv"], gw["attn_w_o"]
    gains = full_small["norm_g"]
    gain = lambda i, k: gains[i, k][None]
    b_pw1 = full_small["conv_b_pw1"]
    conv_wdw = _pad_rows(full_small["conv_w_dw"][0], CONV_HALO)
    conv_vecs = _pad_rows(jnp.concatenate([full_small["conv_b_dw"], full_small["conv_ln_g"], full_small["conv_ln_b"]], axis=0), 8)
    b_pw2 = full_small["conv_b_pw2"]
    wbs = [_ffn_pack(full_small["ffn_w_dw"][i], ffn_b_dw[i]) for i in range(2)]
    tabs = _rope_tables(positions[0])
    pair = _pairing_matrix()
    head_ones = (jnp.arange(GROUP_WIDTH)[:, None] // HEAD_DIM == jnp.arange(GROUP_WIDTH)[None, :] // HEAD_DIM).astype(bf16)

    h0 = prenorm(x, gain(0, 0), "l0_prenorm")
    w_qkv_flat = jnp.transpose(w_qkv[:, 0], (1, 0, 2)).reshape(d, -1)
    under_qkv, under_attn = ("ffn_w_up",), ("conv_w_pw1", "conv_w_pw2", "ffn_w_down")
    assert set(under_qkv + under_attn) == set(late)
    slots = lambda names: [own_slot(weights[k].astype(bf16)) for k in names]
    qkvg, got = qkv_rope(h0, w_qkv_flat, pair, tabs, "qkv", gather_rider(slots(under_qkv)))
    gw.update({k: t.reshape((N_CHIPS,) + weights[k].shape) for k, t in zip(under_qkv, got)})
    att = [attn_fwd(qkvg, g, dil, f"attn_fwd{g}", gather_rider(slots(under_attn)) if g == 0 else None)
           for g, (_, dil) in enumerate(DILATED_GROUPS)]
    gw.update({k: t.reshape((N_CHIPS,) + weights[k].shape) for k, t in zip(under_attn, att[0][3])})
    w_pw1, w_up = gw["conv_w_pw1"], gw["ffn_w_up"]
    w_pw2 = gw["conv_w_pw2"].reshape(1, 1, -1, d)
    w_down = [gw["ffn_w_down"][:, i].reshape(1, 1, -1, d) for i in range(2)]
    os_, lses, lses_phased = ([a[k] for a in att] for k in range(3))
    mixed = mix_fwd(os_, lses, "mix")
    w_o_flat = jnp.transpose(w_o[:, 0], (1, 0, 2)).reshape(1, 1, GROUP_WIDTH, d)
    y0 = mm_nn(mixed[None], w_o_flat, 0, None, 1, f32, 1024, d, "attn_out")[0]
    x1, h1 = postnorm_residual(x, y0, gain(0, 1), gain(0, 2), "l0_postnorm")
    y1, ffn0 = _ffn_block(h1, w_up, 0, w_down[0], wbs[0], "ffn0")
    x2, h2 = postnorm_residual(x1, y1, gain(0, 3), gain(1, 0), "l0_ffn_postnorm")

    z = mm_nn(h2[None], w_pw1, 0, b_pw1, 1, f32, 1024, w_pw1.shape[-1], "pw1")[0]
    cpre, sw = conv_module_fwd(z, conv_wdw, conv_vecs, "conv_fwd")
    y2 = mm_nn(sw[None], w_pw2, 0, b_pw2, 1, f32, 512, 512, "pw2")[0]
    x3, h3 = postnorm_residual(x2, y2, gain(1, 1), gain(1, 2), "l1_postnorm")
    y3, ffn1 = _ffn_block(h3, w_up, 1, w_down[1], wbs[1], "ffn1")
    dx4, loss_row = final_loss(x3, y3, gain(1, 3), target, "loss")
    loss = lax.psum(loss_row[0, 0], ("x", "y", "c"))

    dgain = [[None] * 4 for _ in range(2)]
    dy3, dgain[1][3], _ = norm_bwd(y3, gain(1, 3), dx4, None, bf16, "l1_ffn_postnorm_bwd")
    dh, d_up1, d_down1, dwb1, _ = _ffn_block_bwd(dy3, ffn1, w_up, 1, w_down[1], wbs[1], "ffn1")
    dx3, dgain[1][2], _ = norm_bwd(x3, gain(1, 2), dh, dx4, f32, "l1_ffn_prenorm_bwd")

    dy2, dgain[1][1], d_b_pw2 = norm_bwd(y2, gain(1, 1), dx3, None, bf16, "l1_postnorm_bwd")
    d_pw2 = mm_tn(sw[None], dy2[None], 1, 1024, d, 512, "dw_pw2")
    dsw = mm_nt(dy2[None], w_pw2, 0, 1, f32, 1024, d, d, "d_swish")[0]
    dz, d_conv_wdw, d_conv_vecs, d_b_pw1 = conv_module_bwd(dsw, cpre, z, conv_wdw, conv_vecs, "conv_bwd")
    d_pw1 = mm_tn(h2[None], dz[None], N_CHIPS, 1024, d, w_pw1.shape[-1], "dw_pw1")
    dh = mm_nt_whole(dz[None], w_pw1, 0, f32, 1024, "d_h2")
    dx2, dgain[1][0], _ = norm_bwd(x2, gain(1, 0), dh, dx3, f32, "l1_prenorm_bwd")

    def core_stage(named, tag):
        halves = [g.reshape(N_CHIPS, 2, -1, g.shape[-1]) for _, g in named]
        from_core = swap_core_halves(halves, f"swap_core_halves_{tag}")
        sums = [add_core_halves(a, b, where, f"add_core_{k}") for (k, _), a, b in zip(named, halves, from_core)]
        return halves, from_core, sums

    def chip_stage(named, halves, from_core, from_chips):
        return [add_chip_parts(a, b, r, where, f"add_chips_{k}") for (k, _), a, b, r in zip(named, halves, from_core, from_chips)]

    first_done = [("conv_w_pw1", d_pw1), ("conv_w_pw2", d_pw2[0].reshape(N_CHIPS, -1, d)), ("ffn_w_up1", d_up1),
                  ("ffn_w_down1", d_down1[0].reshape(N_CHIPS, -1, d))]
    halves_1, core_1, sums_1 = core_stage(first_done, "layer1")

    dy1, dgain[0][3], _ = norm_bwd(y1, gain(0, 3), dx2, None, bf16, "l0_ffn_postnorm_bwd")
    dh, d_up0, d_down0, dwb0, chips_1 = _ffn_block_bwd(dy1, ffn0, w_up, 0, w_down[0], wbs[0], "ffn0", scatter_rider(sums_1))
    mine_1 = chip_stage(first_done, halves_1, core_1, chips_1)
    mid_done = [("ffn_w_up0", d_up0), ("ffn_w_down0", d_down0[0].reshape(N_CHIPS, -1, d))]
    halves_m, core_m, sums_m = core_stage(mid_done, "ffn0")
    dx1, dgain[0][2], _ = norm_bwd(x1, gain(0, 2), dh, dx2, f32, "l0_ffn_prenorm_bwd")

    dy0, dgain[0][1], _ = norm_bwd(y0, gain(0, 1), dx1, None, bf16, "l0_postnorm_bwd")
    d_wo = mm_tn(mixed[None], dy0[None], N_CHIPS, 1024, GROUP_WIDTH, w_o.shape[-1], "dw_o")
    dmixed = mm_nt_whole(dy0[None], w_o, 0, f32, 1024, "d_mixed")
    mb = mix_bwd(dmixed, os_, lses, head_ones, "mix_bwd")
    parts = []
    for g, (_, dil) in enumerate(DILATED_GROUPS):
        *five, rode = attn_bwd(qkvg, mb[g], lses_phased[g], mb[3 + g], g, dil, f"attn_bwd{g}", scatter_rider(sums_m) if g == 0 else None)
        parts.append(five)
        if g == 0:
            mine_m = chip_stage(mid_done, halves_m, core_m, rode)
    dqkv = dqkv_assemble(parts, pair.T, tabs, "dqkv")
    d_qkv = mm_tn(h0[None], dqkv[None], N_CHIPS, 1024, d, w_qkv.shape[-1], "dw_qkv")
    dh = mm_nt_whole(dqkv[None], w_qkv, 0, f32, 512, "d_h0")
    grad_x, dgain[0][0], _ = norm_bwd(x, gain(0, 0), dh, dx1, f32, "l0_prenorm_bwd")

    d_ffn_dw, d_ffn_b = zip(*[_ffn_unpack(t) for t in (dwb0, dwb1)])
    gsmall = dict(
        norm_g=jnp.stack([jnp.concatenate(row, axis=0) for row in dgain], axis=0),
        conv_b_pw1=d_b_pw1, conv_w_dw=d_conv_wdw[None, :CONV_KERNEL], conv_b_dw=d_conv_vecs[0:1], conv_ln_g=d_conv_vecs[1:2],
        conv_ln_b=d_conv_vecs[2:3], conv_b_pw2=d_b_pw2, ffn_w_dw=jnp.stack(d_ffn_dw, axis=0))
    bias_rows = ffn_b_dw.size // LANES
    small_g = jnp.concatenate([jnp.concatenate([_pack_rows([_split_last(gsmall[k])[p] for k in SMALL], small_rows)
                                                for p in range(N_CHIPS)], axis=0),
                               jnp.stack(d_ffn_b, axis=0).reshape(bias_rows, LANES)], axis=0)

    small_sum = sum_devices(gather_devices(small_g, "gather_small_grads"), "sum_small_grads")
    my_small = lax.dynamic_slice_in_dim(small_sum, chip[0] * small_rows, small_rows, axis=0)
    g_small = jnp.concatenate([my_small, small_sum[N_CHIPS * small_rows:]], axis=0)

    last_done = [("attn_w_qkv", d_qkv), ("attn_w_o", d_wo)]
    halves_0, core_0, sums_0 = core_stage(last_done, "attn")
    mine_0 = chip_stage(last_done, halves_0, core_0, scatter_chips(sums_0, "scatter_chips_attn"))
    shared = share_core_halves(mine_1 + mine_m + mine_0, "share_core_halves")
    piece = {k: t.reshape(-1, t.shape[-1]) for (k, _), t in zip(first_done + mid_done + last_done, shared)}
    shard_grads = [piece["attn_w_qkv"], piece["attn_w_o"], piece["conv_w_pw1"], piece["conv_w_pw2"],
                   jnp.concatenate([piece["ffn_w_up0"], piece["ffn_w_up1"]], axis=0),
                   jnp.concatenate([piece["ffn_w_down0"], piece["ffn_w_down1"]], axis=0)]

    grads, deltas, new_m, new_v = {}, {}, {}, {}
    for k, g2 in zip(big, shard_grads):
        shp = weights[k].shape
        dl, nm, nv = adamw(weights[k].reshape(g2.shape), g2, mom1[k].reshape(g2.shape), mom2[k].reshape(g2.shape), f"adamw_{k}")
        grads[k], deltas[k], new_m[k], new_v[k] = (t.reshape(shp) for t in (g2, dl, nm, nv))
    pack_state = lambda src: jnp.concatenate([_pack_rows([src[k] for k in SMALL], small_rows), src["ffn_b_dw"].reshape(bias_rows, LANES)], axis=0)
    small_out = (g_small,) + tuple(adamw(pack_state(weights), g_small, pack_state(mom1), pack_state(mom2), "adamw_small"))
    for dst, packed in zip((grads, deltas, new_m, new_v), small_out):
        for k, t in zip(SMALL, _unpack_rows(packed[:small_rows], small_shapes)):
            dst[k] = t
        dst["ffn_b_dw"] = packed[small_rows:].reshape(ffn_b_dw.shape)

    order = ("norm_g", "attn_w_qkv", "attn_w_o", "conv_w_pw1", "conv_b_pw1", "conv_w_dw", "conv_b_dw", "conv_ln_g", "conv_ln_b",
             "conv_w_pw2", "conv_b_pw2", "ffn_w_up", "ffn_w_dw", "ffn_b_dw", "ffn_w_down")
    return (loss, grad_x[None], *[grads[k] for k in order], *[deltas[k] for k in order], *[new_m[k] for k in order],
            *[new_v[k] for k in order])
```

```python
import functools
from typing import Callable, NamedTuple

import jax
import jax.numpy as jnp
from jax import lax
from jax.experimental import pallas as pl
from jax.experimental.pallas import tpu as pltpu

f32 = jnp.float32
bf16 = jnp.bfloat16
SDS = jax.ShapeDtypeStruct

EPS = 1e-6
HEAD_DIM = 64
N_SLOTS = 8
GROUP_WIDTH = N_SLOTS * HEAD_DIM
DILATED_GROUPS = ((128, 1), (512, 4), (2048, 16))
N_GROUPS = 3
SPAN = 128
ROT_DIM = HEAD_DIM // 4
ROPE_THETA = 500000.0
CONV_KERNEL = 31
CONV_HALO = 32
FFN_CONV = 3
FFN_HALO = 16
FFN_TAIL = 8
STRIP_ROWS = 64
ADAM_LR, ADAM_B1, ADAM_B2, ADAM_EPS, ADAM_WD, ADAM_STEP = 0.001, 0.9, 0.999, 1e-08, 0.01, 10
LANES = 128
N_CHIPS = 4
VMEM_LIMIT_BYTES = 56 * 1024 * 1024
NEG_BIG = -1e30
MESH = pl.DeviceIdType.MESH
ANY = pl.BlockSpec(memory_space=pl.ANY)


def _params(*sem):
    return pltpu.CompilerParams(dimension_semantics=sem, vmem_limit_bytes=VMEM_LIMIT_BYTES)


def _sigmoid(v):
    return 1.0 / (1.0 + jnp.exp(-v))


def _dot_nt(a, b):
    return lax.dot_general(a, b, (((1,), (1,)), ((), ())), preferred_element_type=f32)


def _dot_tn(a, b):
    return lax.dot_general(a, b, (((0,), (0,)), ((), ())), preferred_element_type=f32)


def mm_nn(x, w, li, bias, out_blocks, out_dtype, tm, tn, name):
    nq, m, kq = x.shape
    p, _, k, n = w.shape
    assert k == nq * kq and n % tn == 0 and m % tm == 0
    on = p * n // out_blocks
    assert on % tn == 0
    nj, onj = n // tn, on // tn

    def body(*refs):
        if bias is None:
            x_ref, w_ref, o_ref = refs
            b_ref = None
        else:
            x_ref, w_ref, b_ref, o_ref = refs
        acc = jnp.dot(x_ref[0], w_ref[0:kq, :], preferred_element_type=f32)
        for q in range(1, nq):
            acc = acc + jnp.dot(x_ref[q], w_ref[q * kq:(q + 1) * kq, :], preferred_element_type=f32)
        if b_ref is not None:
            acc = acc + b_ref[...]
        o_ref[...] = acc.astype(o_ref.dtype)

    in_specs = [pl.BlockSpec((nq, tm, kq), lambda j, i: (0, i, 0)),
                pl.BlockSpec((None, None, k, tn), lambda j, i: (j // nj, li, 0, j % nj))]
    args = [x, w]
    if bias is not None:
        in_specs.append(pl.BlockSpec((1, tn), lambda j, i: (0, j)))
        args.append(bias)
    return pl.pallas_call(
        body, grid=(p * nj, m // tm), in_specs=in_specs,
        out_specs=pl.BlockSpec((None, tm, tn), lambda j, i: (j // onj, i, j % onj)),
        out_shape=SDS((out_blocks, m, on), out_dtype),
        compiler_params=_params("parallel", "parallel"), name=name)(*args)


def mm_nt(dy, w, li, out_blocks, out_dtype, tm, tk, tn, name):
    ob, m, on = dy.shape
    p, _, k, n = w.shape
    assert ob * on == p * n and n % tn == 0 and on % tn == 0 and k % tk == 0 and m % tm == 0
    kq = k // out_blocks
    assert kq % tk == 0
    nj, onj, kqj = n // tn, on // tn, kq // tk
    nr = p * nj

    def body(dy_ref, w_ref, o_ref, *scr):
        part = _dot_nt(dy_ref[...], w_ref[...])
        if nr == 1:
            o_ref[...] = part.astype(o_ref.dtype)
        else:
            acc_ref, = scr
            r = pl.program_id(2)

            @pl.when(r == 0)
            def _():
                acc_ref[...] = part

            @pl.when(r > 0)
            def _():
                acc_ref[...] += part

            @pl.when(r == nr - 1)
            def _():
                o_ref[...] = acc_ref[...].astype(o_ref.dtype)

    return pl.pallas_call(
        body, grid=(k // tk, m // tm, nr),
        in_specs=[pl.BlockSpec((None, tm, tn), lambda kt, i, r: (r // onj, i, r % onj)),
                  pl.BlockSpec((None, None, tk, tn), lambda kt, i, r: (r // nj, li, kt, r % nj))],
        out_specs=pl.BlockSpec((None, tm, tk), lambda kt, i, r: (kt // kqj, i, kt % kqj)),
        out_shape=SDS((out_blocks, m, kq), out_dtype),
        scratch_shapes=[] if nr == 1 else [pltpu.VMEM((tm, tk), f32)],
        compiler_params=_params("parallel", "parallel", "arbitrary"), name=name)(dy, w)


def mm_nt_whole(dy, w, li, out_dtype, tm, name):
    ob, m, on = dy.shape
    p, _, k, n = w.shape
    assert ob * on == p * n and ob in (1, p) and m % tm == 0

    def body(dy_ref, w_ref, o_ref):
        acc = None
        for pb in range(p):
            lhs = dy_ref[pb] if ob == p else dy_ref[0, :, pb * n:(pb + 1) * n]
            part = _dot_nt(lhs, w_ref[pb])
            acc = part if acc is None else acc + part
        o_ref[...] = acc.astype(o_ref.dtype)

    return pl.pallas_call(
        body, grid=(m // tm,),
        in_specs=[pl.BlockSpec((ob, tm, on), lambda i: (0, i, 0)), pl.BlockSpec((p, None, k, n), lambda i: (0, li, 0, 0))],
        out_specs=pl.BlockSpec((tm, k), lambda i: (i, 0)), out_shape=SDS((m, k), out_dtype),
        compiler_params=_params("parallel"), name=name)(dy, w)


def mm_tn(x, dy, p, tm, tk, tn, name):
    nq, m, kq = x.shape
    ob, _, on = dy.shape
    k = nq * kq
    n = ob * on // p
    assert n % tn == 0 and on % tn == 0 and kq % tk == 0 and m % tm == 0
    nj, onj, kqj = n // tn, on // tn, kq // tk

    def body(x_ref, dy_ref, o_ref):
        part = _dot_tn(x_ref[...], dy_ref[...])
        i = pl.program_id(2)

        @pl.when(i == 0)
        def _():
            o_ref[...] = part

        @pl.when(i > 0)
        def _():
            o_ref[...] += part

    return pl.pallas_call(
        body, grid=(k // tk, p * nj, m // tm),
        in_specs=[pl.BlockSpec((None, tm, tk), lambda kt, j, i: (kt // kqj, i, kt % kqj)),
                  pl.BlockSpec((None, tm, tn), lambda kt, j, i: (j // onj, i, j % onj))],
        out_specs=pl.BlockSpec((None, tk, tn), lambda kt, j, i: (j // nj, kt, j % nj)),
        out_shape=SDS((p, k, n), f32),
        compiler_params=_params("parallel", "parallel", "arbitrary"), name=name)(x, dy)


def _row_tile(s):
    return min(s, 512)


def _rows(tr, d):
    return pl.BlockSpec((tr, d), lambda i: (i, 0))


def _fixed(r, d):
    return pl.BlockSpec((r, d), lambda i: (0, 0))


def _rms(xv):
    return lax.rsqrt(jnp.mean(xv * xv, axis=-1, keepdims=True) + EPS)


def prenorm(x, g, name):
    s, d = x.shape
    tr = _row_tile(s)

    def body(x_ref, g_ref, o_ref):
        xv = x_ref[...]
        o_ref[...] = (xv * _rms(xv) * g_ref[...]).astype(o_ref.dtype)

    return pl.pallas_call(body, grid=(s // tr,), in_specs=[_rows(tr, d), _fixed(1, d)], out_specs=_rows(tr, d),
                          out_shape=SDS((s, d), bf16), compiler_params=_params("parallel"), name=name)(x, g)


def postnorm_residual(x, y, g, g_next, name):
    s, d = x.shape
    tr = _row_tile(s)

    def body(x_ref, y_ref, g_ref, gn_ref, o_ref, h_ref):
        yv = y_ref[...]
        out = x_ref[...] + yv * _rms(yv) * g_ref[...]
        o_ref[...] = out
        h_ref[...] = (out * _rms(out) * gn_ref[...]).astype(h_ref.dtype)

    return pl.pallas_call(body, grid=(s // tr,), in_specs=[_rows(tr, d), _rows(tr, d), _fixed(1, d), _fixed(1, d)],
                          out_specs=[_rows(tr, d), _rows(tr, d)], out_shape=[SDS((s, d), f32), SDS((s, d), bf16)],
                          compiler_params=_params("parallel"), name=name)(x, y, g, g_next)


def norm_bwd(xin, g, dout, res, out_dtype, name):
    s, d = xin.shape
    tr = _row_tile(s)

    def body(*refs):
        if res is None:
            x_ref, g_ref, do_ref, dx_ref, dg_ref, cs_ref = refs
            r_ref = None
        else:
            x_ref, g_ref, do_ref, r_ref, dx_ref, dg_ref, cs_ref = refs
        xv = x_ref[...]
        r = _rms(xv)
        xh = xv * r
        dov = do_ref[...].astype(f32)
        gy = dov * g_ref[...]
        dx = r * (gy - xh * jnp.mean(gy * xh, axis=-1, keepdims=True))
        if r_ref is not None:
            dx = dx + r_ref[...]
        dx_ref[...] = dx.astype(dx_ref.dtype)
        dg = jnp.sum(dov * xh, axis=0, keepdims=True)
        cs = jnp.sum(dx, axis=0, keepdims=True)
        i = pl.program_id(0)

        @pl.when(i == 0)
        def _():
            dg_ref[...] = dg
            cs_ref[...] = cs

        @pl.when(i > 0)
        def _():
            dg_ref[...] += dg
            cs_ref[...] += cs

    in_specs = [_rows(tr, d), _fixed(1, d), _rows(tr, d)]
    args = [xin, g, dout]
    if res is not None:
        in_specs.append(_rows(tr, d))
        args.append(res)
    return pl.pallas_call(body, grid=(s // tr,), in_specs=in_specs,
                          out_specs=[_rows(tr, d), _fixed(1, d), _fixed(1, d)],
                          out_shape=[SDS((s, d), out_dtype), SDS((1, d), f32), SDS((1, d), f32)],
                          compiler_params=_params("arbitrary"), name=name)(*args)


def final_loss(x, y, g, target, name):
    s, d = x.shape
    tr = _row_tile(s)
    nt = s // tr

    def body(x_ref, y_ref, g_ref, t_ref, dx_ref, loss_ref, acc_ref):
        yv = y_ref[...]
        diff = x_ref[...] + yv * _rms(yv) * g_ref[...] - t_ref[...]
        dx_ref[...] = diff * (1.0 / d)
        sq = jnp.sum(diff * diff, axis=0, keepdims=True)
        i = pl.program_id(0)

        @pl.when(i == 0)
        def _():
            acc_ref[...] = sq

        @pl.when(i > 0)
        def _():
            acc_ref[...] += sq

        @pl.when(i == nt - 1)
        def _():
            total = jnp.sum(acc_ref[...], axis=1, keepdims=True) * (0.5 / d)
            loss_ref[...] = jnp.broadcast_to(total, (1, LANES))

    return pl.pallas_call(body, grid=(nt,), in_specs=[_rows(tr, d), _rows(tr, d), _fixed(1, d), _rows(tr, d)],
                          out_specs=[_rows(tr, d), _fixed(1, LANES)],
                          out_shape=[SDS((s, d), f32), SDS((1, LANES), f32)],
                          scratch_shapes=[pltpu.VMEM((1, d), f32)],
                          compiler_params=_params("arbitrary"), name=name)(x, y, g, target)


def _rotate(v, pair, tc, ts):
    partner = jnp.dot(v.astype(bf16), pair, preferred_element_type=f32)
    return jnp.concatenate([v[:, ch * LANES:(ch + 1) * LANES] * tc + partner[:, ch * LANES:(ch + 1) * LANES] * ts
                            for ch in range(GROUP_WIDTH // LANES)], axis=1)


def qkv_rope(h, w, pair, tabs, name, rider=None):
    s, d = h.shape
    tm = min(s, 1024)
    n_kinds = 3
    grid = (s // tm, n_kinds * N_GROUPS)

    def body(*refs):
        first = jnp.logical_and(pl.program_id(0) == 0, pl.program_id(1) == 0)
        last = jnp.logical_and(pl.program_id(0) == grid[0] - 1, pl.program_id(1) == grid[1] - 1)
        (x_ref, w_ref, p_ref, tc_ref, ts_ref, o_ref), ride_start, ride_finish = _rider_run(rider, refs, 5, 1, first, last)
        ride_start()
        kind = pl.program_id(1) // N_GROUPS
        acc = jnp.dot(x_ref[...], w_ref[...], preferred_element_type=f32)

        @pl.when(kind < 2)
        def _():
            o_ref[...] = _rotate(acc, p_ref[...], tc_ref[...], ts_ref[...]).astype(o_ref.dtype)

        @pl.when(kind == 2)
        def _():
            o_ref[...] = acc.astype(o_ref.dtype)

        ride_finish()

    tab = pl.BlockSpec((tm, LANES), lambda i, j: (i, 0))
    r_in, r_out, r_shapes, r_scratch, r_alias, r_args = _rider_specs(rider, 5, 1)
    outs = pl.pallas_call(
        body, grid=grid,
        in_specs=[pl.BlockSpec((tm, d), lambda i, j: (i, 0)), pl.BlockSpec((d, GROUP_WIDTH), lambda i, j: (0, j)),
                  pl.BlockSpec((GROUP_WIDTH, GROUP_WIDTH), lambda i, j: (0, 0)), tab, tab] + r_in,
        out_specs=[pl.BlockSpec((None, tm, GROUP_WIDTH), lambda i, j: (j % N_GROUPS, i, j // N_GROUPS))] + r_out,
        out_shape=[SDS((N_GROUPS, s, n_kinds * GROUP_WIDTH), bf16)] + r_shapes,
        scratch_shapes=r_scratch, input_output_aliases=r_alias,
        compiler_params=_params("arbitrary", "arbitrary"), name=name)(h, w, pair, *tabs, *r_args)
    return outs[0], outs[1:]


def _attn_mask(j):
    row = lax.broadcasted_iota(jnp.int32, (SPAN, 2 * SPAN), 0)
    col = lax.broadcasted_iota(jnp.int32, (SPAN, 2 * SPAN), 1)
    prev = jnp.logical_and(jnp.logical_and(col < SPAN, col >= row), j > 0)
    return jnp.logical_or(prev, jnp.logical_and(col >= SPAN, col - SPAN <= row))


def _attn_in_specs(gi):
    def at(kind, prev):
        def index(r, j):
            return (gi, jnp.maximum(j - 1, 0) if prev else j, r * 3 + kind)
        return pl.BlockSpec((None, SPAN, GROUP_WIDTH), index)
    return [at(0, False), at(1, False), at(1, True), at(2, False), at(2, True)]


def _phases(qkvg, g, dil):
    if dil == 1:
        return qkvg, g
    _, s, w = qkvg.shape
    return qkvg[g].reshape(1, s // dil, dil * w), 0


def attn_fwd(qkvg, g, dil, name, rider=None):
    a, gi = _phases(qkvg, g, dil)
    s = qkvg.shape[1]
    l = s // dil
    nb = l // SPAN

    def body(*refs):
        first = jnp.logical_and(pl.program_id(0) == 0, pl.program_id(1) == 0)
        last = jnp.logical_and(pl.program_id(0) == dil - 1, pl.program_id(1) == nb - 1)
        own, ride_start, ride_finish = _rider_run(rider, refs, 5, 2, first, last)
        q_ref, ko_ref, kp_ref, vo_ref, vp_ref, o_ref, lse_ref, k_scr, v_scr = own
        ride_start()
        mask = _attn_mask(pl.program_id(1))
        k_scr[0:SPAN, :] = kp_ref[...]
        k_scr[SPAN:2 * SPAN, :] = ko_ref[...]
        v_scr[0:SPAN, :] = vp_ref[...]
        v_scr[SPAN:2 * SPAN, :] = vo_ref[...]
        for h in range(N_SLOTS):
            hs = slice(h * HEAD_DIM, (h + 1) * HEAD_DIM)
            sc = jnp.where(mask, _dot_nt(q_ref[:, hs], k_scr[:, hs]) * (HEAD_DIM ** -0.5), NEG_BIG)
            mx = jnp.max(sc, axis=-1, keepdims=True)
            p = jnp.exp(sc - mx)
            den = jnp.sum(p, axis=-1, keepdims=True)
            o_ref[:, hs] = jnp.dot(p.astype(bf16), v_scr[:, hs], preferred_element_type=f32) / den
            lse_ref[:, hs] = jnp.broadcast_to(mx + jnp.log(den), (SPAN, HEAD_DIM))
        ride_finish()

    out = pl.BlockSpec((SPAN, GROUP_WIDTH), lambda r, j: (j, r))
    r_in, r_out, r_shapes, r_scratch, r_alias, r_args = _rider_specs(rider, 5, 2)
    o, lse, *rode = pl.pallas_call(
        body, grid=(dil, nb), in_specs=_attn_in_specs(gi) + r_in, out_specs=[out, out] + r_out,
        out_shape=[SDS((l, dil * GROUP_WIDTH), f32)] * 2 + r_shapes,
        scratch_shapes=[pltpu.VMEM((2 * SPAN, GROUP_WIDTH), bf16)] * 2 + r_scratch, input_output_aliases=r_alias,
        compiler_params=_params("arbitrary", "arbitrary"), name=name)(a, a, a, a, a, *r_args)
    return o.reshape(s, GROUP_WIDTH), lse.reshape(s, GROUP_WIDTH), lse, rode


def _group_weights(lses):
    mx = jnp.maximum(jnp.maximum(lses[0], lses[1]), lses[2])
    es = [jnp.exp(v - mx) for v in lses]
    inv = 1.0 / (es[0] + es[1] + es[2])
    return [e * inv for e in es]


def mix_fwd(os_, lses, name):
    s, w = os_[0].shape
    tr = _row_tile(s)

    def body(o0, o1, o2, l0, l1, l2, out_ref):
        wg = _group_weights([l0[...], l1[...], l2[...]])
        out_ref[...] = (wg[0] * o0[...] + wg[1] * o1[...] + wg[2] * o2[...]).astype(out_ref.dtype)

    return pl.pallas_call(body, grid=(s // tr,), in_specs=[_rows(tr, w)] * 6, out_specs=_rows(tr, w),
                          out_shape=SDS((s, w), bf16), compiler_params=_params("parallel"), name=name)(*os_, *lses)


def mix_bwd(dmixed, os_, lses, head_ones, name):
    s, w = dmixed.shape
    tr = _row_tile(s)

    def head_sum(t, ones):
        hi = t.astype(bf16)
        lo = (t - hi.astype(f32)).astype(bf16)
        return jnp.dot(hi, ones, preferred_element_type=f32) + jnp.dot(lo, ones, preferred_element_type=f32)

    def body(dm_ref, o0, o1, o2, l0, l1, l2, ones_ref, d0, d1, d2, p0, p1, p2):
        dm = dm_ref[...]
        ones = ones_ref[...]
        wg = _group_weights([l0[...], l1[...], l2[...]])
        mean = sum(wg[k] * head_sum(dm * o[...], ones) for k, o in enumerate((o0, o1, o2)))
        for k, (d_ref, p_ref) in enumerate(((d0, p0), (d1, p1), (d2, p2))):
            d_ref[...] = (wg[k] * dm).astype(d_ref.dtype)
            p_ref[...] = wg[k] * mean

    return pl.pallas_call(body, grid=(s // tr,), in_specs=[_rows(tr, w)] * 7 + [_fixed(w, w)],
                          out_specs=[_rows(tr, w)] * 6,
                          out_shape=[SDS((s, w), bf16)] * 3 + [SDS((s, w), f32)] * 3,
                          compiler_params=_params("parallel"), name=name)(dmixed, *os_, *lses, head_ones)


def attn_bwd(qkvg, do, lse_phased, dterm, g, dil, name, rider=None):
    a, gi = _phases(qkvg, g, dil)
    s = qkvg.shape[1]
    l = s // dil
    nb = l // SPAN
    phased = lambda t: t.reshape(l, dil * GROUP_WIDTH)

    def body(*refs):
        first = jnp.logical_and(pl.program_id(0) == 0, pl.program_id(1) == 0)
        last = jnp.logical_and(pl.program_id(0) == dil - 1, pl.program_id(1) == nb - 1)
        own, ride_start, ride_finish = _rider_run(rider, refs, 8, 5, first, last)
        (q_ref, ko_ref, kp_ref, vo_ref, vp_ref, do_ref, lse_ref, dt_ref, dq_ref, dko_ref, dkp_ref, dvo_ref, dvp_ref,
         k_scr, v_scr) = own
        ride_start()
        mask = _attn_mask(pl.program_id(1))
        scale = HEAD_DIM ** -0.5
        k_scr[0:SPAN, :] = kp_ref[...]
        k_scr[SPAN:2 * SPAN, :] = ko_ref[...]
        v_scr[0:SPAN, :] = vp_ref[...]
        v_scr[SPAN:2 * SPAN, :] = vo_ref[...]
        for h in range(N_SLOTS):
            hs = slice(h * HEAD_DIM, (h + 1) * HEAD_DIM)
            one = slice(h * HEAD_DIM, h * HEAD_DIM + 1)
            q, kk, dov = q_ref[:, hs], k_scr[:, hs], do_ref[:, hs]
            p = jnp.exp(jnp.where(mask, _dot_nt(q, kk) * scale - lse_ref[:, one], NEG_BIG))
            ds = (p * (_dot_nt(dov, v_scr[:, hs]) - dt_ref[:, one]) * scale).astype(bf16)
            dq_ref[:, hs] = jnp.dot(ds, kk, preferred_element_type=f32).astype(dq_ref.dtype)
            dk = _dot_tn(ds, q).astype(dko_ref.dtype)
            dv = _dot_tn(p.astype(bf16), dov).astype(dvo_ref.dtype)
            dkp_ref[:, hs] = dk[:SPAN]
            dko_ref[:, hs] = dk[SPAN:]
            dvp_ref[:, hs] = dv[:SPAN]
            dvo_ref[:, hs] = dv[SPAN:]
        ride_finish()

    blk = pl.BlockSpec((SPAN, GROUP_WIDTH), lambda r, j: (j, r))
    r_in, r_out, r_shapes, r_scratch, r_alias, r_args = _rider_specs(rider, 8, 5)
    outs = pl.pallas_call(
        body, grid=(dil, nb), in_specs=_attn_in_specs(gi) + [blk, blk, blk] + r_in, out_specs=[blk] * 5 + r_out,
        out_shape=[SDS((l, dil * GROUP_WIDTH), bf16)] * 5 + r_shapes,
        scratch_shapes=[pltpu.VMEM((2 * SPAN, GROUP_WIDTH), bf16)] * 2 + r_scratch, input_output_aliases=r_alias,
        compiler_params=_params("arbitrary", "arbitrary"), name=name)(a, a, a, a, a, phased(do), lse_phased, phased(dterm), *r_args)
    return [t.reshape(s, GROUP_WIDTH) for t in outs[:5]] + [outs[5:]]


def dqkv_assemble(parts, pair_t, tabs, name):
    s = parts[0][0].shape[0]
    nblk = s // SPAN
    width = 3 * N_GROUPS * GROUP_WIDTH

    def body(*refs):
        ins, (p_ref, tc_ref, ts_ref, o_ref) = refs[:5 * N_GROUPS], refs[5 * N_GROUPS:]
        pair, tc, ts = p_ref[...], tc_ref[...], ts_ref[...]
        i = pl.program_id(0)
        for g, (_, dil) in enumerate(DILATED_GROUPS):
            dq, dko, dkp, dvo, dvp = ins[5 * g:5 * g + 5]
            has_next = i + dil < nblk
            dk = dko[...].astype(f32) + jnp.where(has_next, dkp[...].astype(f32), 0.0)
            dv = dvo[...].astype(f32) + jnp.where(has_next, dvp[...].astype(f32), 0.0)
            for kind, val in enumerate((_rotate(dq[...].astype(f32), pair, tc, ts), _rotate(dk, pair, tc, ts), dv)):
                base = (kind * N_GROUPS + g) * GROUP_WIDTH
                o_ref[:, base:base + GROUP_WIDTH] = val.astype(o_ref.dtype)

    here = _rows(SPAN, GROUP_WIDTH)
    in_specs, args = [], []
    for g, (_, dil) in enumerate(DILATED_GROUPS):
        ahead = pl.BlockSpec((SPAN, GROUP_WIDTH), functools.partial(lambda i, dil: (jnp.minimum(i + dil, nblk - 1), 0), dil=dil))
        in_specs += [here, here, ahead, here, ahead]
        args += list(parts[g])
    tab = _rows(SPAN, LANES)
    return pl.pallas_call(body, grid=(nblk,), in_specs=in_specs + [_fixed(GROUP_WIDTH, GROUP_WIDTH), tab, tab],
                          out_specs=_rows(SPAN, width), out_shape=SDS((s, width), bf16),
                          compiler_params=_params("parallel"), name=name)(*args, pair_t, *tabs)


def ffn_act_fwd(u, wb, name):
    _, nbk, s, c = u.shape
    tr = min(s, 256)

    def body(u_ref, h_ref, wb_ref, a_ref, ug_ref, su, sg):
        first = pl.program_id(1) == 0
        for lc in range(c // LANES):
            ln = slice(lc * LANES, (lc + 1) * LANES)
            for half, scr in enumerate((su, sg)):
                scr[0:FFN_HALO, ln] = jnp.where(first, 0.0, h_ref[half, :, ln].astype(f32))
            for r0 in range(0, tr, STRIP_ROWS):
                rows = slice(r0, r0 + STRIP_ROWS)
                conv = []
                for half, scr in enumerate((su, sg)):
                    xv = u_ref[half, rows, ln].astype(f32)
                    scr[FFN_HALO + r0:FFN_HALO + r0 + STRIP_ROWS, ln] = xv
                    acc = wb_ref[half, FFN_CONV:FFN_CONV + 1, ln] + wb_ref[half, FFN_CONV - 1:FFN_CONV, ln] * xv
                    for k in range(FFN_CONV - 1):
                        acc = acc + wb_ref[half, k:k + 1, ln] * scr[pl.ds(FFN_HALO + r0 - (FFN_CONV - 1) + k, STRIP_ROWS), ln]
                    ug_ref[half, rows, ln] = acc.astype(ug_ref.dtype)
                    conv.append(acc)
                up, gate = conv
                a_ref[rows, ln] = (gate * _sigmoid(gate) * up).astype(a_ref.dtype)

    both = pl.BlockSpec((2, None, tr, c), lambda p, i: (0, p, i, 0))
    return pl.pallas_call(
        body, grid=(nbk, s // tr),
        in_specs=[both,
                  pl.BlockSpec((2, None, FFN_HALO, c), lambda p, i: (0, p, jnp.maximum(i * (tr // FFN_HALO) - 1, 0), 0)),
                  pl.BlockSpec((None, 2, 8, c), lambda p, i: (p, 0, 0, 0))],
        out_specs=[pl.BlockSpec((None, tr, c), lambda p, i: (p, i, 0)), both],
        out_shape=[SDS((nbk, s, c), bf16), SDS(u.shape, bf16)],
        scratch_shapes=[pltpu.VMEM((tr + FFN_HALO, c), f32)] * 2,
        compiler_params=_params("parallel", "arbitrary"), name=name)(u, u, wb)


def ffn_act_bwd(da, ug, u, wb, name, rider=None):
    _, nbk, s, c = u.shape
    tr = min(s, 256)
    nt = s // tr

    def body(*refs):
        step = pl.program_id(1)
        first = jnp.logical_and(pl.program_id(0) == 0, step == 0)
        last = jnp.logical_and(pl.program_id(0) == nbk - 1, step == nt - 1)
        (da_ref, ug_ref, u_ref, wb_ref, du_ref, dwb_ref, eu, eg), ride_start, ride_finish = _rider_run(rider, refs, 4, 2, first, last)
        ride_start()

        @pl.when(step == 0)
        def _():
            eu[tr:tr + FFN_TAIL, :] = jnp.zeros((FFN_TAIL, c), f32)
            eg[tr:tr + FFN_TAIL, :] = jnp.zeros((FFN_TAIL, c), f32)
            dwb_ref[...] = jnp.zeros(dwb_ref.shape, f32)

        fold = lambda t: jnp.sum(t.reshape(STRIP_ROWS // 8, 8, LANES), axis=0)
        for lc in range(c // LANES):
            ln = slice(lc * LANES, (lc + 1) * LANES)
            sums = [[jnp.zeros((8, LANES), f32) for _ in range(FFN_CONV + 1)] for _ in range(2)]
            for r0 in reversed(range(0, tr, STRIP_ROWS)):
                rows = slice(r0, r0 + STRIP_ROWS)
                up, gate = ug_ref[0, rows, ln].astype(f32), ug_ref[1, rows, ln].astype(f32)
                sig = _sigmoid(gate)
                dav = da_ref[rows, ln].astype(f32)
                grads = (dav * (gate * sig), dav * up * (sig * (1.0 + gate * (1.0 - sig))))
                for half, ext in enumerate((eu, eg)):
                    dv = grads[half]
                    ext[rows, ln] = dv
                    xv = u_ref[half, rows, ln].astype(f32)
                    acc = None
                    for k in range(FFN_CONV):
                        ahead = dv if k == FFN_CONV - 1 else ext[pl.ds(r0 + FFN_CONV - 1 - k, STRIP_ROWS), ln]
                        term = wb_ref[half, k:k + 1, ln] * ahead
                        acc = term if acc is None else acc + term
                        sums[half][k] = sums[half][k] + fold(xv * ahead)
                    sums[half][FFN_CONV] = sums[half][FFN_CONV] + fold(dv)
                    du_ref[half, rows, ln] = acc.astype(du_ref.dtype)
            for half, ext in enumerate((eu, eg)):
                ext[tr:tr + FFN_TAIL, ln] = ext[0:FFN_TAIL, ln]
                for k in range(FFN_CONV + 1):
                    dwb_ref[half, k:k + 1, ln] += jnp.sum(sums[half][k], axis=0, keepdims=True)
        ride_finish()

    rev = lambda i: nt - 1 - i
    both = pl.BlockSpec((2, None, tr, c), lambda p, i: (0, p, rev(i), 0))
    r_in, r_out, r_shapes, r_scratch, r_alias, r_args = _rider_specs(rider, 4, 2)
    outs = pl.pallas_call(
        body, grid=(nbk, nt),
        in_specs=[pl.BlockSpec((None, tr, c), lambda p, i: (p, rev(i), 0)), both, both,
                  pl.BlockSpec((None, 2, 8, c), lambda p, i: (p, 0, 0, 0))] + r_in,
        out_specs=[both, pl.BlockSpec((None, 2, 8, c), lambda p, i: (p, 0, 0, 0))] + r_out,
        out_shape=[SDS((2, nbk, s, c), bf16), SDS((nbk, 2, 8, c), f32)] + r_shapes,
        scratch_shapes=[pltpu.VMEM((tr + FFN_TAIL, c), f32)] * 2 + r_scratch, input_output_aliases=r_alias,
        compiler_params=_params("arbitrary", "arbitrary"), name=name)(da, ug, u, wb, *r_args)
    return outs[0], outs[1], outs[2:]


def _glu(zv, c):
    return zv[:, :c] * _sigmoid(zv[:, c:])


def _conv_fill(z_ref, h_ref, scr, first, tr, c):
    scr[0:CONV_HALO, :] = jnp.where(first, 0.0, _glu(h_ref[...], c))
    scr[CONV_HALO:CONV_HALO + tr, :] = _glu(z_ref[...], c)


def _conv_taps(b):
    return [(a, CONV_KERNEL - 1 - 8 * a - b) for a in range(CONV_HALO // 8) if CONV_KERNEL - 1 - 8 * a - b >= 0]


def _layernorm_parts(cv):
    mu = jnp.mean(cv, axis=-1, keepdims=True)
    cen = cv - mu
    rstd = lax.rsqrt(jnp.mean(cen * cen, axis=-1, keepdims=True) + EPS)
    return cen * rstd, rstd


def conv_module_fwd(z, wdw, vecs, name):
    s, c2 = z.shape
    c = c2 // 2
    tr = min(s, 256)

    def body(z_ref, h_ref, w_ref, v_ref, c_ref, s_ref, scr, zb):
        _conv_fill(z_ref, h_ref, scr, pl.program_id(0) == 0, tr, c)
        acc = jnp.broadcast_to(v_ref[0:1, :], (tr, c))
        for b in range(8):
            part = None
            for a, j in _conv_taps(b):
                term = w_ref[j:j + 1, :] * scr[pl.ds(CONV_HALO - 8 - 8 * a, tr + 8), :]
                part = term if part is None else part + term
            if b == 0:
                acc = acc + part[8:]
            else:
                zb[...] = part
                acc = acc + zb[pl.ds(8 - b, tr), :]
        c_ref[...] = acc
        chat, _ = _layernorm_parts(acc)
        ln = chat * v_ref[1:2, :] + v_ref[2:3, :]
        s_ref[...] = (ln * _sigmoid(ln)).astype(s_ref.dtype)

    return pl.pallas_call(
        body, grid=(s // tr,),
        in_specs=[_rows(tr, c2), pl.BlockSpec((CONV_HALO, c2), lambda i: (jnp.maximum(i * (tr // CONV_HALO) - 1, 0), 0)),
                  _fixed(CONV_HALO, c), _fixed(8, c)],
        out_specs=[_rows(tr, c), _rows(tr, c)], out_shape=[SDS((s, c), f32), SDS((s, c), bf16)],
        scratch_shapes=[pltpu.VMEM((tr + CONV_HALO, c), f32), pltpu.VMEM((tr + 8, c), f32)],
        compiler_params=_params("arbitrary"), name=name)(z, z, wdw, vecs)


def conv_module_bwd(ds, cpre, z, wdw, vecs, name):
    s, c2 = z.shape
    c = c2 // 2
    tr = min(s, 256)
    nt = s // tr

    def body(ds_ref, c_ref, z_ref, w_ref, v_ref, dz_ref, dw_ref, dv_ref, db_ref, ext, dwp):
        step = pl.program_id(0)

        @pl.when(step == 0)
        def _():
            ext[tr:tr + CONV_HALO, :] = jnp.zeros((CONV_HALO, c), f32)
            dwp[...] = jnp.zeros(dwp.shape, f32)
            dv_ref[...] = jnp.zeros(dv_ref.shape, f32)
            db_ref[...] = jnp.zeros(db_ref.shape, f32)

        gain, bias = v_ref[1:2, :], v_ref[2:3, :]
        fold16 = lambda t: t[:8] + t[8:]
        sums = [jnp.zeros((8, c), f32) for _ in range(3)]
        for r0 in range(0, tr, 16):
            rows = slice(r0, r0 + 16)
            chat, rstd = _layernorm_parts(c_ref[rows, :])
            ln = chat * gain + bias
            sig = _sigmoid(ln)
            dln = ds_ref[rows, :].astype(f32) * (sig * (1.0 + ln * (1.0 - sig)))
            gy = dln * gain
            dc = rstd * (gy - jnp.mean(gy, axis=-1, keepdims=True) - chat * jnp.mean(gy * chat, axis=-1, keepdims=True))
            ext[rows, :] = dc
            for k, t in enumerate((dc, dln * chat, dln)):
                sums[k] = sums[k] + fold16(t)
        for k in range(3):
            dv_ref[k:k + 1, :] += jnp.sum(sums[k], axis=0, keepdims=True)

        fold = lambda t: jnp.sum(t.reshape(STRIP_ROWS // 8, 8, LANES), axis=0)
        for lc in range(c // LANES):
            ln_a = slice(lc * LANES, (lc + 1) * LANES)
            ln_g = slice(c + lc * LANES, c + (lc + 1) * LANES)
            dbs = [jnp.zeros((8, LANES), f32) for _ in range(2)]
            for r0 in range(0, tr, STRIP_ROWS):
                rows = slice(r0, r0 + STRIP_ROWS)
                a, sg = z_ref[rows, ln_a], _sigmoid(z_ref[rows, ln_g])
                uv = a * sg
                du = jnp.zeros((STRIP_ROWS, LANES), f32)
                for b in range(8):
                    src = ext[pl.ds(r0 + b, STRIP_ROWS + CONV_HALO - 8), ln_a]
                    for a8, j in _conv_taps(b):
                        ahead = src[8 * a8:8 * a8 + STRIP_ROWS]
                        du = du + w_ref[j:j + 1, ln_a] * ahead
                        dwp[8 * j:8 * j + 8, ln_a] += fold(uv * ahead)
                da = du * sg
                dg = du * a * (sg * (1.0 - sg))
                dz_ref[rows, ln_a] = da.astype(dz_ref.dtype)
                dz_ref[rows, ln_g] = dg.astype(dz_ref.dtype)
                dbs = [dbs[0] + fold(da), dbs[1] + fold(dg)]
            db_ref[:, ln_a] += jnp.sum(dbs[0], axis=0, keepdims=True)
            db_ref[:, ln_g] += jnp.sum(dbs[1], axis=0, keepdims=True)
        ext[tr:tr + CONV_HALO, :] = ext[0:CONV_HALO, :]

        @pl.when(step == nt - 1)
        def _():
            for j in range(CONV_HALO):
                dw_ref[j:j + 1, :] = jnp.sum(dwp[8 * j:8 * j + 8, :], axis=0, keepdims=True)

    rev = lambda i: nt - 1 - i
    back = lambda d: pl.BlockSpec((tr, d), lambda i: (rev(i), 0))
    return pl.pallas_call(
        body, grid=(nt,),
        in_specs=[back(c), back(c), back(c2), _fixed(CONV_HALO, c), _fixed(8, c)],
        out_specs=[back(c2), _fixed(CONV_HALO, c), _fixed(8, c), _fixed(1, c2)],
        out_shape=[SDS((s, c2), bf16), SDS((CONV_HALO, c), f32), SDS((8, c), f32), SDS((1, c2), f32)],
        scratch_shapes=[pltpu.VMEM((tr + CONV_HALO, c), f32), pltpu.VMEM((8 * CONV_HALO, c), f32)],
        compiler_params=_params("arbitrary"), name=name)(ds, cpre, z, wdw, vecs)


def _tile2d(r, n):
    tn = n if n <= 2048 else 1024
    tr = r
    while tr * tn * 4 > (1 << 21) and tr % 16 == 0:
        tr //= 2
    assert r % tr == 0 and n % tn == 0
    return tr, tn


def adamw(w, g, m, v, name):
    r, n = w.shape
    tr, tn = _tile2d(r, n)

    def body(w_ref, g_ref, m_ref, v_ref, d_ref, nm_ref, nv_ref):
        gv = g_ref[...]
        nm = ADAM_B1 * m_ref[...] + (1.0 - ADAM_B1) * gv
        nv = ADAM_B2 * v_ref[...] + (1.0 - ADAM_B2) * (gv * gv)
        m_hat = nm / (1.0 - ADAM_B1 ** ADAM_STEP)
        v_hat = nv / (1.0 - ADAM_B2 ** ADAM_STEP)
        d_ref[...] = -ADAM_LR * (m_hat / (jnp.sqrt(v_hat) + ADAM_EPS) + ADAM_WD * w_ref[...])
        nm_ref[...] = nm
        nv_ref[...] = nv

    blk = pl.BlockSpec((tr, tn), lambda i, j: (i, j))
    return pl.pallas_call(body, grid=(r // tr, n // tn), in_specs=[blk] * 4, out_specs=[blk] * 3,
                          out_shape=[SDS((r, n), f32)] * 3, compiler_params=_params("parallel", "parallel"),
                          name=name)(w, g, m, v)


def add_core_halves(grad, got, where, name):
    _, _, rh, n = grad.shape
    tr, tn = _tile2d(rh, n)

    def body(w_ref, a_ref, b_ref, o_ref):
        o_ref[...] = (a_ref[...] + b_ref[...]).astype(o_ref.dtype)

    return pl.pallas_call(
        body,
        grid_spec=pltpu.PrefetchScalarGridSpec(
            num_scalar_prefetch=1, grid=(N_CHIPS, rh // tr, n // tn),
            in_specs=[pl.BlockSpec((None, None, tr, tn), lambda p, i, j, w_ref: (p, w_ref[0], i, j)),
                      pl.BlockSpec((None, tr, tn), lambda p, i, j, w_ref: (p, i, j))],
            out_specs=pl.BlockSpec((None, tr, tn), lambda p, i, j, w_ref: (p, i, j))),
        out_shape=SDS((N_CHIPS, rh, n), bf16), compiler_params=_params("parallel", "parallel", "parallel"),
        name=name)(where, grad, got)


def add_chip_parts(grad, got_core, got_chips, where, name):
    _, _, rh, n = grad.shape
    tr, tn = _tile2d(rh, n)

    def body(w_ref, a_ref, b_ref, g_ref, o_ref):
        acc = a_ref[...] + b_ref[...]
        for k in range(N_CHIPS - 1):
            acc = acc + g_ref[k].astype(f32)
        o_ref[...] = acc

    return pl.pallas_call(
        body,
        grid_spec=pltpu.PrefetchScalarGridSpec(
            num_scalar_prefetch=1, grid=(rh // tr, n // tn),
            in_specs=[pl.BlockSpec((None, None, tr, tn), lambda i, j, w_ref: (w_ref[1], w_ref[0], i, j)),
                      pl.BlockSpec((None, tr, tn), lambda i, j, w_ref: (w_ref[1], i, j)),
                      pl.BlockSpec((N_CHIPS - 1, tr, tn), lambda i, j, w_ref: (0, i, j))],
            out_specs=pl.BlockSpec((None, tr, tn), lambda i, j, w_ref: (w_ref[0], i, j))),
        out_shape=SDS((2, rh, n), f32), compiler_params=_params("parallel", "parallel"),
        name=name)(where, grad, got_core, got_chips)


def sum_devices(parts, name):
    nd, r, n = parts.shape

    def body(p_ref, o_ref):
        acc = p_ref[0]
        for k in range(1, nd):
            acc = acc + p_ref[k]
        o_ref[...] = acc

    return pl.pallas_call(body, out_shape=SDS((r, n), f32), name=name)(parts)


def _position():
    return lax.axis_index("x"), lax.axis_index("y"), lax.axis_index("c")


def _other_chips(x, y):
    return [(1 - x, y), (x, 1 - y), (1 - x, 1 - y)]


def _remote(src, dst, send, recv, to):
    return pltpu.make_async_remote_copy(src_ref=src, dst_ref=dst, send_sem=send, recv_sem=recv, device_id=to,
                                        device_id_type=MESH)


def gather_chips(bufs, name):
    n = len(bufs)
    nk = N_CHIPS - 1

    def body(*refs):
        bufs_ = refs[n:2 * n]
        send, recv = refs[2 * n:]
        x, y, c = _position()
        me = 2 * x + y
        chips = _other_chips(x, y)
        sends = []
        for t in range(n):
            for k, (px, py) in enumerate(chips):
                out = _remote(bufs_[t].at[me, c], bufs_[t].at[me, c], send.at[t, k], recv.at[t, k], (px, py, c))
                out.start()
                sends.append(out)
        for t in range(n):
            for k, (px, py) in enumerate(chips):
                piece = bufs_[t].at[2 * px + py, c]
                _remote(piece, piece, send.at[t, k], recv.at[t, k], (px, py, c)).wait_recv()
                on = _remote(piece, piece, send.at[t, nk + k], recv.at[t, nk + k], (x, y, 1 - c))
                on.start()
                sends.append(on)
        for t in range(n):
            for k, (px, py) in enumerate(chips):
                piece = bufs_[t].at[2 * px + py, 1 - c]
                _remote(piece, piece, send.at[t, nk + k], recv.at[t, nk + k], (x, y, 1 - c)).wait_recv()
        for cp in sends:
            cp.wait_send()

    return pl.pallas_call(
        body, in_specs=[ANY] * n, out_specs=[ANY] * n,
        out_shape=[SDS(a.shape, a.dtype) for a in bufs],
        input_output_aliases={t: t for t in range(n)},
        scratch_shapes=[pltpu.SemaphoreType.DMA((n, 2 * nk)), pltpu.SemaphoreType.DMA((n, 2 * nk))],
        name=name)(*bufs)


class Rider(NamedTuple):
    operands: tuple
    n_aliased: int
    out_shapes: tuple
    scratch: tuple
    start: Callable
    finish: Callable


def _rider_specs(rider, n_inputs, n_outputs):
    if rider is None:
        return [], [], [], [], {}, []
    aliased = [SDS(a.shape, a.dtype) for a in rider.operands[:rider.n_aliased]]
    outs = aliased + list(rider.out_shapes)
    aliases = {n_inputs + t: n_outputs + t for t in range(rider.n_aliased)}
    return [ANY] * len(rider.operands), [ANY] * len(outs), outs, list(rider.scratch), aliases, list(rider.operands)


def _rider_run(rider, refs, n_inputs, n_outputs, first, last):
    if rider is None:
        return refs, lambda: None, lambda: None
    n_op = len(rider.operands)
    n_out = rider.n_aliased + len(rider.out_shapes)
    own_in, r_in = refs[:n_inputs], refs[n_inputs:n_inputs + n_op]
    own_out = refs[n_inputs + n_op:n_inputs + n_op + n_outputs]
    r_out = refs[n_inputs + n_op + n_outputs:n_inputs + n_op + n_outputs + n_out]
    rest = refs[n_inputs + n_op + n_outputs + n_out:]
    n_sem = len(rider.scratch)
    sems, own_scratch = rest[len(rest) - n_sem:], rest[:len(rest) - n_sem]

    def start():
        pl.when(first)(lambda: rider.start(r_in, r_out, sems))

    def finish():
        pl.when(last)(lambda: rider.finish(r_in, r_out, sems))

    return list(own_in) + list(own_out) + list(own_scratch), start, finish


def gather_rider(bufs):
    n = len(bufs)
    nk = N_CHIPS - 1

    def ici(bufs_, send, recv, x, y, c, t, k, px, py, own):
        piece = bufs_[t].at[2 * x + y if own else 2 * px + py, c]
        return _remote(piece, piece, send.at[t, k], recv.at[t, k], (px, py, c))

    def d2d(bufs_, send, recv, x, y, c, t, k, px, py, mine):
        piece = bufs_[t].at[2 * px + py, c if mine else 1 - c]
        return _remote(piece, piece, send.at[t, nk + k], recv.at[t, nk + k], (x, y, 1 - c))

    def start(r_in, r_out, sems):
        send, recv = sems
        x, y, c = _position()
        for t in range(n):
            for k, (px, py) in enumerate(_other_chips(x, y)):
                ici(r_out, send, recv, x, y, c, t, k, px, py, True).start()

    def finish(r_in, r_out, sems):
        send, recv = sems
        x, y, c = _position()
        chips = _other_chips(x, y)
        for t in range(n):
            for k, (px, py) in enumerate(chips):
                ici(r_out, send, recv, x, y, c, t, k, px, py, False).wait_recv()
                d2d(r_out, send, recv, x, y, c, t, k, px, py, True).start()
        for t in range(n):
            for k, (px, py) in enumerate(chips):
                d2d(r_out, send, recv, x, y, c, t, k, px, py, False).wait_recv()
        for t in range(n):
            for k, (px, py) in enumerate(chips):
                ici(r_out, send, recv, x, y, c, t, k, px, py, True).wait_send()
                d2d(r_out, send, recv, x, y, c, t, k, px, py, True).wait_send()

    sems = (pltpu.SemaphoreType.DMA((n, 2 * nk)), pltpu.SemaphoreType.DMA((n, 2 * nk)))
    return Rider(tuple(bufs), n, (), sems, start, finish)


def scatter_rider(parts):
    n = len(parts)

    def copies(r_in, r_out, sems):
        send, recv = sems
        x, y, c = _position()
        return [_remote(r_in[t].at[2 * px + py], r_out[t].at[k], send.at[t, k], recv.at[t, k], (px, py, c))
                for t in range(n) for k, (px, py) in enumerate(_other_chips(x, y))]

    def start(r_in, r_out, sems):
        for cp in copies(r_in, r_out, sems):
            cp.start()

    def finish(r_in, r_out, sems):
        for cp in copies(r_in, r_out, sems):
            cp.wait()

    sems = (pltpu.SemaphoreType.DMA((n, N_CHIPS - 1)), pltpu.SemaphoreType.DMA((n, N_CHIPS - 1)))
    return Rider(tuple(parts), 0, tuple(SDS((N_CHIPS - 1,) + a.shape[1:], a.dtype) for a in parts), sems, start, finish)


def swap_core_halves(grads, name):
    n = len(grads)

    def body(*refs):
        ins, outs = refs[:n], refs[n:2 * n]
        send, recv = refs[2 * n:]
        x, y, c = _position()
        pending = []
        for t in range(n):
            out = _remote(ins[t].at[:, 1 - c], outs[t], send.at[t], recv.at[t], (x, y, 1 - c))
            out.start()
            pending.append(out.wait)
        for wait in pending:
            wait()

    return pl.pallas_call(
        body, in_specs=[ANY] * n, out_specs=[ANY] * n,
        out_shape=[SDS((a.shape[0],) + a.shape[2:], a.dtype) for a in grads],
        scratch_shapes=[pltpu.SemaphoreType.DMA((n,)), pltpu.SemaphoreType.DMA((n,))],
        name=name)(*grads)


def scatter_chips(parts, name):
    n = len(parts)

    def body(*refs):
        ins, outs = refs[:n], refs[n:2 * n]
        send, recv = refs[2 * n:]
        x, y, c = _position()
        pending = []
        for t in range(n):
            for k, (px, py) in enumerate(_other_chips(x, y)):
                out = _remote(ins[t].at[2 * px + py], outs[t].at[k], send.at[t, k], recv.at[t, k], (px, py, c))
                out.start()
                pending.append(out.wait)
        for wait in pending:
            wait()

    return pl.pallas_call(
        body, in_specs=[ANY] * n, out_specs=[ANY] * n,
        out_shape=[SDS((N_CHIPS - 1,) + a.shape[1:], a.dtype) for a in parts],
        scratch_shapes=[pltpu.SemaphoreType.DMA((n, N_CHIPS - 1)), pltpu.SemaphoreType.DMA((n, N_CHIPS - 1))],
        name=name)(*parts)


def share_core_halves(bufs, name):
    n = len(bufs)

    def body(*refs):
        bufs_ = refs[n:2 * n]
        send, recv = refs[2 * n:]
        x, y, c = _position()
        pending = []
        for t in range(n):
            out = _remote(bufs_[t].at[c], bufs_[t].at[c], send.at[t], recv.at[t], (x, y, 1 - c))
            out.start()
            pending.append(out.wait_send)
            other = bufs_[t].at[1 - c]
            pending.append(_remote(other, other, send.at[t], recv.at[t], (x, y, 1 - c)).wait_recv)
        for wait in pending:
            wait()

    return pl.pallas_call(
        body, in_specs=[ANY] * n, out_specs=[ANY] * n, out_shape=[SDS(a.shape, a.dtype) for a in bufs],
        input_output_aliases={t: t for t in range(n)},
        scratch_shapes=[pltpu.SemaphoreType.DMA((n,)), pltpu.SemaphoreType.DMA((n,))],
        name=name)(*bufs)


def gather_devices(v, name):
    flips = [(dx, dy, dc) for dx in (0, 1) for dy in (0, 1) for dc in (0, 1)][1:]

    def body(v_ref, o_ref, send, recv, local):
        x, y, c = _position()
        me = 4 * x + 2 * y + c
        own = pltpu.make_async_copy(v_ref, o_ref.at[me], local)
        own.start()
        pending = [own.wait]
        for k, (dx, dy, dc) in enumerate(flips):
            px, py, pc = x ^ dx, y ^ dy, c ^ dc
            out = _remote(v_ref, o_ref.at[me], send.at[k], recv.at[k], (px, py, pc))
            out.start()
            pending.append(out.wait_send)
            pending.append(_remote(v_ref, o_ref.at[4 * px + 2 * py + pc], send.at[k], recv.at[k], (px, py, pc)).wait_recv)
        for wait in pending:
            wait()

    return pl.pallas_call(
        body, in_specs=[ANY], out_specs=ANY, out_shape=SDS((8,) + v.shape, v.dtype),
        scratch_shapes=[pltpu.SemaphoreType.DMA((7,)), pltpu.SemaphoreType.DMA((7,)), pltpu.SemaphoreType.DMA],
        name=name)(v)


SMALL = ("norm_g", "conv_b_pw1", "conv_w_dw", "conv_b_dw", "conv_ln_g", "conv_ln_b", "conv_b_pw2", "ffn_w_dw")


def _pack_rows(arrs, rows):
    flat = jnp.concatenate([a.reshape(-1, LANES) for a in arrs], axis=0)
    return jnp.pad(flat, ((0, rows - flat.shape[0]), (0, 0)))


def _unpack_rows(packed, shapes):
    out, at = [], 0
    for shp in shapes:
        size = 1
        for dim in shp:
            size *= dim
        rows = size // LANES
        out.append(packed[..., at:at + rows, :].reshape(packed.shape[:-2] + tuple(shp)))
        at += rows
    return out


def _join_last(t):
    t = jnp.moveaxis(t, 0, -2)
    return t.reshape(t.shape[:-2] + (t.shape[-2] * t.shape[-1],))


def _split_last(t):
    t = t.reshape(t.shape[:-1] + (N_CHIPS, t.shape[-1] // N_CHIPS))
    return jnp.moveaxis(t, -2, 0)


def _rope_tables(positions):
    half = ROT_DIM // 2
    inv_freq = ROPE_THETA ** (-jnp.arange(half, dtype=f32) / half)
    ang = positions.astype(f32).reshape(-1, 1) * inv_freq
    cos, sin = jnp.cos(ang), jnp.sin(ang)
    s = ang.shape[0]
    rest = HEAD_DIM - ROT_DIM
    head = lambda t, fill: jnp.concatenate([t, t, jnp.full((s, rest), fill, f32)], axis=1)
    twice = lambda t: jnp.concatenate([t] * (LANES // HEAD_DIM), axis=1)
    return twice(head(cos, 1.0)), twice(head(sin, 0.0))


def _pairing_matrix():
    half = ROT_DIM // 2
    row = jnp.arange(GROUP_WIDTH)[:, None]
    col = jnp.arange(GROUP_WIDTH)[None, :]
    d = col % HEAD_DIM
    minus = jnp.logical_and(row == col + half, d < half)
    plus = jnp.logical_and(row == col - half, jnp.logical_and(d >= half, d < ROT_DIM))
    return (plus.astype(f32) - minus.astype(f32)).astype(bf16)


def _pad_rows(t, rows):
    return jnp.pad(t, ((0, rows - t.shape[0]), (0, 0)))


def _ffn_pack(w_dw, b_dw):
    t = jnp.concatenate([w_dw, b_dw[None]], axis=0)
    t = t.reshape(FFN_CONV + 1, 2, 2, -1)
    t = jnp.transpose(t, (2, 1, 0, 3))
    return jnp.pad(t, ((0, 0), (0, 0), (0, 8 - (FFN_CONV + 1)), (0, 0)))


def _ffn_unpack(d):
    t = jnp.transpose(d[:, :, :FFN_CONV + 1], (2, 1, 0, 3)).reshape(FFN_CONV + 1, -1)
    return t[:FFN_CONV], t[FFN_CONV]


def _ffn_block(h, w_up, li, w_down, wb, tag):
    s = h.shape[0]
    u = mm_nn(h[None], w_up, li, None, N_CHIPS, bf16, 1024, w_up.shape[-1], f"{tag}_up")
    u4 = u.reshape(2, 2, s, u.shape[-1])
    a, ug = ffn_act_fwd(u4, wb, f"{tag}_act")
    y = mm_nn(a, w_down, 0, None, 1, f32, 1024, w_down.shape[-1], f"{tag}_down")[0]
    return y, (h, u4, ug, a)


def _ffn_block_bwd(dy, saved, w_up, li, w_down, wb, tag, rider=None):
    h, u4, ug, a = saved
    d_down = mm_tn(a, dy[None], 1, 1024, a.shape[-1], dy.shape[-1], f"{tag}_dwdown")
    da = mm_nt(dy[None], w_down, 0, 2, bf16, 1024, a.shape[-1], w_down.shape[-1], f"{tag}_da")
    du4, dwb, rode = ffn_act_bwd(da, ug, u4, wb, f"{tag}_actbwd", rider)
    du = du4.reshape((N_CHIPS,) + du4.shape[2:])
    d_up = mm_tn(h[None], du, N_CHIPS, 1024, h.shape[-1], du.shape[-1], f"{tag}_dwup")
    dh = mm_nt_whole(du, w_up, li, f32, 512, f"{tag}_dh")
    return dh, d_up, d_down, dwb, rode


def kernel(x, positions, norm_g, attn_w_qkv, attn_w_o, conv_w_pw1, conv_b_pw1, conv_w_dw, conv_b_dw, conv_ln_g, conv_ln_b, conv_w_pw2, conv_b_pw2, ffn_w_up, ffn_w_dw, ffn_b_dw, ffn_w_down, loss_target, m_norm_g, m_attn_w_qkv, m_attn_w_o, m_conv_w_pw1, m_conv_b_pw1, m_conv_w_dw, m_conv_b_dw, m_conv_ln_g, m_conv_ln_b, m_conv_w_pw2, m_conv_b_pw2, m_ffn_w_up, m_ffn_w_dw, m_ffn_b_dw, m_ffn_w_down, v_norm_g, v_attn_w_qkv, v_attn_w_o, v_conv_w_pw1, v_conv_b_pw1, v_conv_w_dw, v_conv_b_dw, v_conv_ln_g, v_conv_ln_b, v_conv_w_pw2, v_conv_b_pw2, v_ffn_w_up, v_ffn_w_dw, v_ffn_b_dw, v_ffn_w_down):
    weights = dict(norm_g=norm_g, attn_w_qkv=attn_w_qkv, attn_w_o=attn_w_o, conv_w_pw1=conv_w_pw1, conv_b_pw1=conv_b_pw1,
                   conv_w_dw=conv_w_dw, conv_b_dw=conv_b_dw, conv_ln_g=conv_ln_g, conv_ln_b=conv_ln_b, conv_w_pw2=conv_w_pw2,
                   conv_b_pw2=conv_b_pw2, ffn_w_up=ffn_w_up, ffn_w_dw=ffn_w_dw, ffn_b_dw=ffn_b_dw, ffn_w_down=ffn_w_down)
    mom1 = dict(norm_g=m_norm_g, attn_w_qkv=m_attn_w_qkv, attn_w_o=m_attn_w_o, conv_w_pw1=m_conv_w_pw1, conv_b_pw1=m_conv_b_pw1,
                conv_w_dw=m_conv_w_dw, conv_b_dw=m_conv_b_dw, conv_ln_g=m_conv_ln_g, conv_ln_b=m_conv_ln_b, conv_w_pw2=m_conv_w_pw2,
                conv_b_pw2=m_conv_b_pw2, ffn_w_up=m_ffn_w_up, ffn_w_dw=m_ffn_w_dw, ffn_b_dw=m_ffn_b_dw, ffn_w_down=m_ffn_w_down)
    mom2 = dict(norm_g=v_norm_g, attn_w_qkv=v_attn_w_qkv, attn_w_o=v_attn_w_o, conv_w_pw1=v_conv_w_pw1, conv_b_pw1=v_conv_b_pw1,
                conv_w_dw=v_conv_w_dw, conv_b_dw=v_conv_b_dw, conv_ln_g=v_conv_ln_g, conv_ln_b=v_conv_ln_b, conv_w_pw2=v_conv_w_pw2,
                conv_b_pw2=v_conv_b_pw2, ffn_w_up=v_ffn_w_up, ffn_w_dw=v_ffn_w_dw, ffn_b_dw=v_ffn_b_dw, ffn_w_down=v_ffn_w_down)
    big = ("attn_w_qkv", "attn_w_o", "conv_w_pw1", "conv_w_pw2", "ffn_w_up", "ffn_w_down")
    xi, yi, ci = _position()
    chip = (2 * xi + yi).astype(jnp.int32).reshape(1)
    where = jnp.stack([ci, 2 * xi + yi]).astype(jnp.int32)

    x = x[0]
    target = loss_target[0]
    s, d = x.shape

    small_shapes = [weights[k].shape for k in SMALL]
    small_rows = -(-sum(weights[k].size for k in SMALL) // LANES // 8) * 8
    small_w = _pack_rows([weights[k] for k in SMALL], small_rows)
    def own_slot(shard):
        halves = shard.reshape(1, 2, -1, shard.shape[-1])
        return lax.dynamic_update_slice(lax.empty((N_CHIPS,) + halves.shape[1:], shard.dtype), halves, (chip[0], 0, 0, 0))

    early, late = big[:2], big[2:]
    gathered = gather_chips([own_slot(weights[k].astype(bf16)) for k in early] + [own_slot(small_w)], "gather_weights_early")
    gw = {k: t.reshape((N_CHIPS,) + weights[k].shape) for k, t in zip(early, gathered[:-1])}
    full_small = dict(zip(SMALL, [_join_last(t) for t in _unpack_rows(gathered[-1].reshape(N_CHIPS, small_rows, LANES), small_shapes)]))
    w_qkv, w_o = gw["attn_w_qkv"], gw["attn_w_o"]
    gains = full_small["norm_g"]
    gain = lambda i, k: gains[i, k][None]
    b_pw1 = full_small["conv_b_pw1"]
    conv_wdw = _pad_rows(full_small["conv_w_dw"][0], CONV_HALO)
    conv_vecs = _pad_rows(jnp.concatenate([full_small["conv_b_dw"], full_small["conv_ln_g"], full_small["conv_ln_b"]], axis=0), 8)
    b_pw2 = full_small["conv_b_pw2"]
    wbs = [_ffn_pack(full_small["ffn_w_dw"][i], ffn_b_dw[i]) for i in range(2)]
    tabs = _rope_tables(positions[0])
    pair = _pairing_matrix()
    head_ones = (jnp.arange(GROUP_WIDTH)[:, None] // HEAD_DIM == jnp.arange(GROUP_WIDTH)[None, :] // HEAD_DIM).astype(bf16)

    h0 = prenorm(x, gain(0, 0), "l0_prenorm")
    w_qkv_flat = jnp.transpose(w_qkv[:, 0], (1, 0, 2)).reshape(d, -1)
    under_qkv, under_attn = ("ffn_w_up",), ("conv_w_pw1", "conv_w_pw2", "ffn_w_down")
    assert set(under_qkv + under_attn) == set(late)
    slots = lambda names: [own_slot(weights[k].astype(bf16)) for k in names]
    qkvg, got = qkv_rope(h0, w_qkv_flat, pair, tabs, "qkv", gather_rider(slots(under_qkv)))
    gw.update({k: t.reshape((N_CHIPS,) + weights[k].shape) for k, t in zip(under_qkv, got)})
    att = [attn_fwd(qkvg, g, dil, f"attn_fwd{g}", gather_rider(slots(under_attn)) if g == 0 else None)
           for g, (_, dil) in enumerate(DILATED_GROUPS)]
    gw.update({k: t.reshape((N_CHIPS,) + weights[k].shape) for k, t in zip(under_attn, att[0][3])})
    w_pw1, w_up = gw["conv_w_pw1"], gw["ffn_w_up"]
    w_pw2 = gw["conv_w_pw2"].reshape(1, 1, -1, d)
    w_down = [gw["ffn_w_down"][:, i].reshape(1, 1, -1, d) for i in range(2)]
    os_, lses, lses_phased = ([a[k] for a in att] for k in range(3))
    mixed = mix_fwd(os_, lses, "mix")
    w_o_flat = jnp.transpose(w_o[:, 0], (1, 0, 2)).reshape(1, 1, GROUP_WIDTH, d)
    y0 = mm_nn(mixed[None], w_o_flat, 0, None, 1, f32, 1024, d, "attn_out")[0]
    x1, h1 = postnorm_residual(x, y0, gain(0, 1), gain(0, 2), "l0_postnorm")
    y1, ffn0 = _ffn_block(h1, w_up, 0, w_down[0], wbs[0], "ffn0")
    x2, h2 = postnorm_residual(x1, y1, gain(0, 3), gain(1, 0), "l0_ffn_postnorm")

    z = mm_nn(h2[None], w_pw1, 0, b_pw1, 1, f32, 1024, w_pw1.shape[-1], "pw1")[0]
    cpre, sw = conv_module_fwd(z, conv_wdw, conv_vecs, "conv_fwd")
    y2 = mm_nn(sw[None], w_pw2, 0, b_pw2, 1, f32, 1024, d, "pw2")[0]
    x3, h3 = postnorm_residual(x2, y2, gain(1, 1), gain(1, 2), "l1_postnorm")
    y3, ffn1 = _ffn_block(h3, w_up, 1, w_down[1], wbs[1], "ffn1")
    dx4, loss_row = final_loss(x3, y3, gain(1, 3), target, "loss")
    loss = lax.psum(loss_row[0, 0], ("x", "y", "c"))

    dgain = [[None] * 4 for _ in range(2)]
    dy3, dgain[1][3], _ = norm_bwd(y3, gain(1, 3), dx4, None, bf16, "l1_ffn_postnorm_bwd")
    dh, d_up1, d_down1, dwb1, _ = _ffn_block_bwd(dy3, ffn1, w_up, 1, w_down[1], wbs[1], "ffn1")
    dx3, dgain[1][2], _ = norm_bwd(x3, gain(1, 2), dh, dx4, f32, "l1_ffn_prenorm_bwd")

    dy2, dgain[1][1], d_b_pw2 = norm_bwd(y2, gain(1, 1), dx3, None, bf16, "l1_postnorm_bwd")
    d_pw2 = mm_tn(sw[None], dy2[None], 1, 1024, d, d, "dw_pw2")
    dsw = mm_nt(dy2[None], w_pw2, 0, 1, f32, 1024, d, d, "d_swish")[0]
    dz, d_conv_wdw, d_conv_vecs, d_b_pw1 = conv_module_bwd(dsw, cpre, z, conv_wdw, conv_vecs, "conv_bwd")
    d_pw1 = mm_tn(h2[None], dz[None], N_CHIPS, 1024, d, w_pw1.shape[-1], "dw_pw1")
    dh = mm_nt_whole(dz[None], w_pw1, 0, f32, 1024, "d_h2")
    dx2, dgain[1][0], _ = norm_bwd(x2, gain(1, 0), dh, dx3, f32, "l1_prenorm_bwd")

    def core_stage(named, tag):
        halves = [g.reshape(N_CHIPS, 2, -1, g.shape[-1]) for _, g in named]
        from_core = swap_core_halves(halves, f"swap_core_halves_{tag}")
        sums = [add_core_halves(a, b, where, f"add_core_{k}") for (k, _), a, b in zip(named, halves, from_core)]
        return halves, from_core, sums

    def chip_stage(named, halves, from_core, from_chips):
        return [add_chip_parts(a, b, r, where, f"add_chips_{k}") for (k, _), a, b, r in zip(named, halves, from_core, from_chips)]

    first_done = [("conv_w_pw1", d_pw1), ("conv_w_pw2", d_pw2[0].reshape(N_CHIPS, -1, d)), ("ffn_w_up1", d_up1),
                  ("ffn_w_down1", d_down1[0].reshape(N_CHIPS, -1, d))]
    halves_1, core_1, sums_1 = core_stage(first_done, "layer1")

    dy1, dgain[0][3], _ = norm_bwd(y1, gain(0, 3), dx2, None, bf16, "l0_ffn_postnorm_bwd")
    dh, d_up0, d_down0, dwb0, chips_1 = _ffn_block_bwd(dy1, ffn0, w_up, 0, w_down[0], wbs[0], "ffn0", scatter_rider(sums_1))
    mine_1 = chip_stage(first_done, halves_1, core_1, chips_1)
    mid_done = [("ffn_w_up0", d_up0), ("ffn_w_down0", d_down0[0].reshape(N_CHIPS, -1, d))]
    halves_m, core_m, sums_m = core_stage(mid_done, "ffn0")
    dx1, dgain[0][2], _ = norm_bwd(x1, gain(0, 2), dh, dx2, f32, "l0_ffn_prenorm_bwd")

    dy0, dgain[0][1], _ = norm_bwd(y0, gain(0, 1), dx1, None, bf16, "l0_postnorm_bwd")
    d_wo = mm_tn(mixed[None], dy0[None], N_CHIPS, 1024, GROUP_WIDTH, w_o.shape[-1], "dw_o")
    dmixed = mm_nt_whole(dy0[None], w_o, 0, f32, 1024, "d_mixed")
    mb = mix_bwd(dmixed, os_, lses, head_ones, "mix_bwd")
    parts = []
    for g, (_, dil) in enumerate(DILATED_GROUPS):
        *five, rode = attn_bwd(qkvg, mb[g], lses_phased[g], mb[3 + g], g, dil, f"attn_bwd{g}", scatter_rider(sums_m) if g == 0 else None)
        parts.append(five)
        if g == 0:
            mine_m = chip_stage(mid_done, halves_m, core_m, rode)
    dqkv = dqkv_assemble(parts, pair.T, tabs, "dqkv")
    d_qkv = mm_tn(h0[None], dqkv[None], N_CHIPS, 1024, d, w_qkv.shape[-1], "dw_qkv")
    dh = mm_nt_whole(dqkv[None], w_qkv, 0, f32, 512, "d_h0")
    grad_x, dgain[0][0], _ = norm_bwd(x, gain(0, 0), dh, dx1, f32, "l0_prenorm_bwd")

    d_ffn_dw, d_ffn_b = zip(*[_ffn_unpack(t) for t in (dwb0, dwb1)])
    gsmall = dict(
        norm_g=jnp.stack([jnp.concatenate(row, axis=0) for row in dgain], axis=0),
        conv_b_pw1=d_b_pw1, conv_w_dw=d_conv_wdw[None, :CONV_KERNEL], conv_b_dw=d_conv_vecs[0:1], conv_ln_g=d_conv_vecs[1:2],
        conv_ln_b=d_conv_vecs[2:3], conv_b_pw2=d_b_pw2, ffn_w_dw=jnp.stack(d_ffn_dw, axis=0))
    bias_rows = ffn_b_dw.size // LANES
    small_g = jnp.concatenate([jnp.concatenate([_pack_rows([_split_last(gsmall[k])[p] for k in SMALL], small_rows)
                                                for p in range(N_CHIPS)], axis=0),
                               jnp.stack(d_ffn_b, axis=0).reshape(bias_rows, LANES)], axis=0)

    small_sum = sum_devices(gather_devices(small_g, "gather_small_grads"), "sum_small_grads")
    my_small = lax.dynamic_slice_in_dim(small_sum, chip[0] * small_rows, small_rows, axis=0)
    g_small = jnp.concatenate([my_small, small_sum[N_CHIPS * small_rows:]], axis=0)

    last_done = [("attn_w_qkv", d_qkv), ("attn_w_o", d_wo)]
    halves_0, core_0, sums_0 = core_stage(last_done, "attn")
    mine_0 = chip_stage(last_done, halves_0, core_0, scatter_chips(sums_0, "scatter_chips_attn"))
    shared = share_core_halves(mine_1 + mine_m + mine_0, "share_core_halves")
    piece = {k: t.reshape(-1, t.shape[-1]) for (k, _), t in zip(first_done + mid_done + last_done, shared)}
    shard_grads = [piece["attn_w_qkv"], piece["attn_w_o"], piece["conv_w_pw1"], piece["conv_w_pw2"],
                   jnp.concatenate([piece["ffn_w_up0"], piece["ffn_w_up1"]], axis=0),
                   jnp.concatenate([piece["ffn_w_down0"], piece["ffn_w_down1"]], axis=0)]

    grads, deltas, new_m, new_v = {}, {}, {}, {}
    for k, g2 in zip(big, shard_grads):
        shp = weights[k].shape
        dl, nm, nv = adamw(weights[k].reshape(g2.shape), g2, mom1[k].reshape(g2.shape), mom2[k].reshape(g2.shape), f"adamw_{k}")
        grads[k], deltas[k], new_m[k], new_v[k] = (t.reshape(shp) for t in (g2, dl, nm, nv))
    pack_state = lambda src: jnp.concatenate([_pack_rows([src[k] for k in SMALL], small_rows), src["ffn_b_dw"].reshape(bias_rows, LANES)], axis=0)
    small_out = (g_small,) + tuple(adamw(pack_state(weights), g_small, pack_state(mom1), pack_state(mom2), "adamw_small"))
    for dst, packed in zip((grads, deltas, new_m, new_v), small_out):
        for k, t in zip(SMALL, _unpack_rows(packed[:small_rows], small_shapes)):
            dst[k] = t
        dst["ffn_b_dw"] = packed[small_rows:].reshape(ffn_b_dw.shape)

    order = ("norm_g", "attn_w_qkv", "attn_w_o", "conv_w_pw1", "conv_b_pw1", "conv_w_dw", "conv_b_dw", "conv_ln_g", "conv_ln_b",
             "conv_w_pw2", "conv_b_pw2", "ffn_w_up", "ffn_w_dw", "ffn_b_dw", "ffn_w_down")
    return (loss, grad_x[None], *[grads[k] for k in order], *[deltas[k] for k in order], *[new_m[k] for k in order],
            *[new_v[k] for k in order])
```

```python
import functools
from typing import Callable, NamedTuple

import jax
import jax.numpy as jnp
from jax import lax
from jax.experimental import pallas as pl
from jax.experimental.pallas import tpu as pltpu

f32 = jnp.float32
bf16 = jnp.bfloat16
SDS = jax.ShapeDtypeStruct

EPS = 1e-6
HEAD_DIM = 64
N_SLOTS = 8
GROUP_WIDTH = N_SLOTS * HEAD_DIM
DILATED_GROUPS = ((128, 1), (512, 4), (2048, 16))
N_GROUPS = 3
SPAN = 128
ROT_DIM = HEAD_DIM // 4
ROPE_THETA = 500000.0
CONV_KERNEL = 31
CONV_HALO = 32
FFN_CONV = 3
FFN_HALO = 16
FFN_TAIL = 8
STRIP_ROWS = 64
ADAM_LR, ADAM_B1, ADAM_B2, ADAM_EPS, ADAM_WD, ADAM_STEP = 0.001, 0.9, 0.999, 1e-08, 0.01, 10
LANES = 128
N_CHIPS = 4
VMEM_LIMIT_BYTES = 56 * 1024 * 1024
NEG_BIG = -1e30
MESH = pl.DeviceIdType.MESH
ANY = pl.BlockSpec(memory_space=pl.ANY)


def _params(*sem):
    return pltpu.CompilerParams(dimension_semantics=sem, vmem_limit_bytes=VMEM_LIMIT_BYTES)


def _sigmoid(v):
    return 1.0 / (1.0 + jnp.exp(-v))


def _dot_nt(a, b):
    return lax.dot_general(a, b, (((1,), (1,)), ((), ())), preferred_element_type=f32)


def _dot_tn(a, b):
    return lax.dot_general(a, b, (((0,), (0,)), ((), ())), preferred_element_type=f32)


def mm_nn(x, w, li, bias, out_blocks, out_dtype, tm, tn, name):
    nq, m, kq = x.shape
    p, _, k, n = w.shape
    assert k == nq * kq and n % tn == 0 and m % tm == 0
    on = p * n // out_blocks
    assert on % tn == 0
    nj, onj = n // tn, on // tn

    def body(*refs):
        if bias is None:
            x_ref, w_ref, o_ref = refs
            b_ref = None
        else:
            x_ref, w_ref, b_ref, o_ref = refs
        acc = jnp.dot(x_ref[0], w_ref[0:kq, :], preferred_element_type=f32)
        for q in range(1, nq):
            acc = acc + jnp.dot(x_ref[q], w_ref[q * kq:(q + 1) * kq, :], preferred_element_type=f32)
        if b_ref is not None:
            acc = acc + b_ref[...]
        o_ref[...] = acc.astype(o_ref.dtype)

    in_specs = [pl.BlockSpec((nq, tm, kq), lambda j, i: (0, i, 0)),
                pl.BlockSpec((None, None, k, tn), lambda j, i: (j // nj, li, 0, j % nj))]
    args = [x, w]
    if bias is not None:
        in_specs.append(pl.BlockSpec((1, tn), lambda j, i: (0, j)))
        args.append(bias)
    return pl.pallas_call(
        body, grid=(p * nj, m // tm), in_specs=in_specs,
        out_specs=pl.BlockSpec((None, tm, tn), lambda j, i: (j // onj, i, j % onj)),
        out_shape=SDS((out_blocks, m, on), out_dtype),
        compiler_params=_params("parallel", "parallel"), name=name)(*args)


def mm_nt(dy, w, li, out_blocks, out_dtype, tm, tk, tn, name):
    ob, m, on = dy.shape
    p, _, k, n = w.shape
    assert ob * on == p * n and n % tn == 0 and on % tn == 0 and k % tk == 0 and m % tm == 0
    kq = k // out_blocks
    assert kq % tk == 0
    nj, onj, kqj = n // tn, on // tn, kq // tk
    nr = p * nj

    def body(dy_ref, w_ref, o_ref, *scr):
        part = _dot_nt(dy_ref[...], w_ref[...])
        if nr == 1:
            o_ref[...] = part.astype(o_ref.dtype)
        else:
            acc_ref, = scr
            r = pl.program_id(2)

            @pl.when(r == 0)
            def _():
                acc_ref[...] = part

            @pl.when(r > 0)
            def _():
                acc_ref[...] += part

            @pl.when(r == nr - 1)
            def _():
                o_ref[...] = acc_ref[...].astype(o_ref.dtype)

    return pl.pallas_call(
        body, grid=(k // tk, m // tm, nr),
        in_specs=[pl.BlockSpec((None, tm, tn), lambda kt, i, r: (r // onj, i, r % onj)),
                  pl.BlockSpec((None, None, tk, tn), lambda kt, i, r: (r // nj, li, kt, r % nj))],
        out_specs=pl.BlockSpec((None, tm, tk), lambda kt, i, r: (kt // kqj, i, kt % kqj)),
        out_shape=SDS((out_blocks, m, kq), out_dtype),
        scratch_shapes=[] if nr == 1 else [pltpu.VMEM((tm, tk), f32)],
        compiler_params=_params("parallel", "parallel", "arbitrary"), name=name)(dy, w)


def mm_nt_whole(dy, w, li, out_dtype, tm, name):
    ob, m, on = dy.shape
    p, _, k, n = w.shape
    assert ob * on == p * n and ob in (1, p) and m % tm == 0

    def body(dy_ref, w_ref, o_ref):
        acc = None
        for pb in range(p):
            lhs = dy_ref[pb] if ob == p else dy_ref[0, :, pb * n:(pb + 1) * n]
            part = _dot_nt(lhs, w_ref[pb])
            acc = part if acc is None else acc + part
        o_ref[...] = acc.astype(o_ref.dtype)

    return pl.pallas_call(
        body, grid=(m // tm,),
        in_specs=[pl.BlockSpec((ob, tm, on), lambda i: (0, i, 0)), pl.BlockSpec((p, None, k, n), lambda i: (0, li, 0, 0))],
        out_specs=pl.BlockSpec((tm, k), lambda i: (i, 0)), out_shape=SDS((m, k), out_dtype),
        compiler_params=_params("parallel"), name=name)(dy, w)


def mm_tn(x, dy, p, tm, tk, tn, name):
    nq, m, kq = x.shape
    ob, _, on = dy.shape
    k = nq * kq
    n = ob * on // p
    assert n % tn == 0 and on % tn == 0 and kq % tk == 0 and m % tm == 0
    nj, onj, kqj = n // tn, on // tn, kq // tk

    def body(x_ref, dy_ref, o_ref):
        part = _dot_tn(x_ref[...], dy_ref[...])
        i = pl.program_id(2)

        @pl.when(i == 0)
        def _():
            o_ref[...] = part

        @pl.when(i > 0)
        def _():
            o_ref[...] += part

    return pl.pallas_call(
        body, grid=(k // tk, p * nj, m // tm),
        in_specs=[pl.BlockSpec((None, tm, tk), lambda kt, j, i: (kt // kqj, i, kt % kqj)),
                  pl.BlockSpec((None, tm, tn), lambda kt, j, i: (j // onj, i, j % onj))],
        out_specs=pl.BlockSpec((None, tk, tn), lambda kt, j, i: (j // nj, kt, j % nj)),
        out_shape=SDS((p, k, n), f32),
        compiler_params=_params("parallel", "parallel", "arbitrary"), name=name)(x, dy)


def _row_tile(s):
    return min(s, 512)


def _rows(tr, d):
    return pl.BlockSpec((tr, d), lambda i: (i, 0))


def _fixed(r, d):
    return pl.BlockSpec((r, d), lambda i: (0, 0))


def _rms(xv):
    return lax.rsqrt(jnp.mean(xv * xv, axis=-1, keepdims=True) + EPS)


def prenorm(x, g, name):
    s, d = x.shape
    tr = _row_tile(s)

    def body(x_ref, g_ref, o_ref):
        xv = x_ref[...]
        o_ref[...] = (xv * _rms(xv) * g_ref[...]).astype(o_ref.dtype)

    return pl.pallas_call(body, grid=(s // tr,), in_specs=[_rows(tr, d), _fixed(1, d)], out_specs=_rows(tr, d),
                          out_shape=SDS((s, d), bf16), compiler_params=_params("parallel"), name=name)(x, g)


def postnorm_residual(x, y, g, g_next, name):
    s, d = x.shape
    tr = _row_tile(s)

    def body(x_ref, y_ref, g_ref, gn_ref, o_ref, h_ref):
        yv = y_ref[...]
        out = x_ref[...] + yv * _rms(yv) * g_ref[...]
        o_ref[...] = out
        h_ref[...] = (out * _rms(out) * gn_ref[...]).astype(h_ref.dtype)

    return pl.pallas_call(body, grid=(s // tr,), in_specs=[_rows(tr, d), _rows(tr, d), _fixed(1, d), _fixed(1, d)],
                          out_specs=[_rows(tr, d), _rows(tr, d)], out_shape=[SDS((s, d), f32), SDS((s, d), bf16)],
                          compiler_params=_params("parallel"), name=name)(x, y, g, g_next)


def norm_bwd(xin, g, dout, res, out_dtype, name):
    s, d = xin.shape
    tr = _row_tile(s)

    def body(*refs):
        if res is None:
            x_ref, g_ref, do_ref, dx_ref, dg_ref, cs_ref = refs
            r_ref = None
        else:
            x_ref, g_ref, do_ref, r_ref, dx_ref, dg_ref, cs_ref = refs
        xv = x_ref[...]
        r = _rms(xv)
        xh = xv * r
        dov = do_ref[...].astype(f32)
        gy = dov * g_ref[...]
        dx = r * (gy - xh * jnp.mean(gy * xh, axis=-1, keepdims=True))
        if r_ref is not None:
            dx = dx + r_ref[...]
        dx_ref[...] = dx.astype(dx_ref.dtype)
        dg = jnp.sum(dov * xh, axis=0, keepdims=True)
        cs = jnp.sum(dx, axis=0, keepdims=True)
        i = pl.program_id(0)

        @pl.when(i == 0)
        def _():
            dg_ref[...] = dg
            cs_ref[...] = cs

        @pl.when(i > 0)
        def _():
            dg_ref[...] += dg
            cs_ref[...] += cs

    in_specs = [_rows(tr, d), _fixed(1, d), _rows(tr, d)]
    args = [xin, g, dout]
    if res is not None:
        in_specs.append(_rows(tr, d))
        args.append(res)
    return pl.pallas_call(body, grid=(s // tr,), in_specs=in_specs,
                          out_specs=[_rows(tr, d), _fixed(1, d), _fixed(1, d)],
                          out_shape=[SDS((s, d), out_dtype), SDS((1, d), f32), SDS((1, d), f32)],
                          compiler_params=_params("arbitrary"), name=name)(*args)


def final_loss(x, y, g, target, name):
    s, d = x.shape
    tr = _row_tile(s)
    nt = s // tr

    def body(x_ref, y_ref, g_ref, t_ref, dx_ref, loss_ref, acc_ref):
        yv = y_ref[...]
        diff = x_ref[...] + yv * _rms(yv) * g_ref[...] - t_ref[...]
        dx_ref[...] = diff * (1.0 / d)
        sq = jnp.sum(diff * diff, axis=0, keepdims=True)
        i = pl.program_id(0)

        @pl.when(i == 0)
        def _():
            acc_ref[...] = sq

        @pl.when(i > 0)
        def _():
            acc_ref[...] += sq

        @pl.when(i == nt - 1)
        def _():
            total = jnp.sum(acc_ref[...], axis=1, keepdims=True) * (0.5 / d)
            loss_ref[...] = jnp.broadcast_to(total, (1, LANES))

    return pl.pallas_call(body, grid=(nt,), in_specs=[_rows(tr, d), _rows(tr, d), _fixed(1, d), _rows(tr, d)],
                          out_specs=[_rows(tr, d), _fixed(1, LANES)],
                          out_shape=[SDS((s, d), f32), SDS((1, LANES), f32)],
                          scratch_shapes=[pltpu.VMEM((1, d), f32)],
                          compiler_params=_params("arbitrary"), name=name)(x, y, g, target)


def _rotate(v, pair, tc, ts):
    partner = jnp.dot(v.astype(bf16), pair, preferred_element_type=f32)
    return jnp.concatenate([v[:, ch * LANES:(ch + 1) * LANES] * tc + partner[:, ch * LANES:(ch + 1) * LANES] * ts
                            for ch in range(GROUP_WIDTH // LANES)], axis=1)


def qkv_rope(h, w, pair, tabs, name, rider=None):
    s, d = h.shape
    tm = min(s, 1024)
    n_kinds = 3
    grid = (s // tm, n_kinds * N_GROUPS)

    def body(*refs):
        first = jnp.logical_and(pl.program_id(0) == 0, pl.program_id(1) == 0)
        last = jnp.logical_and(pl.program_id(0) == grid[0] - 1, pl.program_id(1) == grid[1] - 1)
        (x_ref, w_ref, p_ref, tc_ref, ts_ref, o_ref), ride_start, ride_finish = _rider_run(rider, refs, 5, 1, first, last)
        ride_start()
        kind = pl.program_id(1) // N_GROUPS
        acc = jnp.dot(x_ref[...], w_ref[...], preferred_element_type=f32)

        @pl.when(kind < 2)
        def _():
            o_ref[...] = _rotate(acc, p_ref[...], tc_ref[...], ts_ref[...]).astype(o_ref.dtype)

        @pl.when(kind == 2)
        def _():
            o_ref[...] = acc.astype(o_ref.dtype)

        ride_finish()

    tab = pl.BlockSpec((tm, LANES), lambda i, j: (i, 0))
    r_in, r_out, r_shapes, r_scratch, r_alias, r_args = _rider_specs(rider, 5, 1)
    outs = pl.pallas_call(
        body, grid=grid,
        in_specs=[pl.BlockSpec((tm, d), lambda i, j: (i, 0)), pl.BlockSpec((d, GROUP_WIDTH), lambda i, j: (0, j)),
                  pl.BlockSpec((GROUP_WIDTH, GROUP_WIDTH), lambda i, j: (0, 0)), tab, tab] + r_in,
        out_specs=[pl.BlockSpec((None, tm, GROUP_WIDTH), lambda i, j: (j % N_GROUPS, i, j // N_GROUPS))] + r_out,
        out_shape=[SDS((N_GROUPS, s, n_kinds * GROUP_WIDTH), bf16)] + r_shapes,
        scratch_shapes=r_scratch, input_output_aliases=r_alias,
        compiler_params=_params("arbitrary", "arbitrary"), name=name)(h, w, pair, *tabs, *r_args)
    return outs[0], outs[1:]


def _attn_mask(j):
    row = lax.broadcasted_iota(jnp.int32, (SPAN, 2 * SPAN), 0)
    col = lax.broadcasted_iota(jnp.int32, (SPAN, 2 * SPAN), 1)
    prev = jnp.logical_and(jnp.logical_and(col < SPAN, col >= row), j > 0)
    return jnp.logical_or(prev, jnp.logical_and(col >= SPAN, col - SPAN <= row))


def _attn_in_specs(gi):
    def at(kind, prev):
        def index(r, j):
            return (gi, jnp.maximum(j - 1, 0) if prev else j, r * 3 + kind)
        return pl.BlockSpec((None, SPAN, GROUP_WIDTH), index)
    return [at(0, False), at(1, False), at(1, True), at(2, False), at(2, True)]


def _phases(qkvg, g, dil):
    if dil == 1:
        return qkvg, g
    _, s, w = qkvg.shape
    return qkvg[g].reshape(1, s // dil, dil * w), 0


def attn_fwd(qkvg, g, dil, name, rider=None):
    a, gi = _phases(qkvg, g, dil)
    s = qkvg.shape[1]
    l = s // dil
    nb = l // SPAN

    def body(*refs):
        first = jnp.logical_and(pl.program_id(0) == 0, pl.program_id(1) == 0)
        last = jnp.logical_and(pl.program_id(0) == dil - 1, pl.program_id(1) == nb - 1)
        own, ride_start, ride_finish = _rider_run(rider, refs, 5, 2, first, last)
        q_ref, ko_ref, kp_ref, vo_ref, vp_ref, o_ref, lse_ref, k_scr, v_scr = own
        ride_start()
        mask = _attn_mask(pl.program_id(1))
        k_scr[0:SPAN, :] = kp_ref[...]
        k_scr[SPAN:2 * SPAN, :] = ko_ref[...]
        v_scr[0:SPAN, :] = vp_ref[...]
        v_scr[SPAN:2 * SPAN, :] = vo_ref[...]
        for h in range(N_SLOTS):
            hs = slice(h * HEAD_DIM, (h + 1) * HEAD_DIM)
            sc = jnp.where(mask, _dot_nt(q_ref[:, hs], k_scr[:, hs]) * (HEAD_DIM ** -0.5), NEG_BIG)
            mx = jnp.max(sc, axis=-1, keepdims=True)
            p = jnp.exp(sc - mx)
            den = jnp.sum(p, axis=-1, keepdims=True)
            o_ref[:, hs] = jnp.dot(p.astype(bf16), v_scr[:, hs], preferred_element_type=f32) / den
            lse_ref[:, hs] = jnp.broadcast_to(mx + jnp.log(den), (SPAN, HEAD_DIM))
        ride_finish()

    out = pl.BlockSpec((SPAN, GROUP_WIDTH), lambda r, j: (j, r))
    r_in, r_out, r_shapes, r_scratch, r_alias, r_args = _rider_specs(rider, 5, 2)
    o, lse, *rode = pl.pallas_call(
        body, grid=(dil, nb), in_specs=_attn_in_specs(gi) + r_in, out_specs=[out, out] + r_out,
        out_shape=[SDS((l, dil * GROUP_WIDTH), f32)] * 2 + r_shapes,
        scratch_shapes=[pltpu.VMEM((2 * SPAN, GROUP_WIDTH), bf16)] * 2 + r_scratch, input_output_aliases=r_alias,
        compiler_params=_params("arbitrary", "arbitrary"), name=name)(a, a, a, a, a, *r_args)
    return o.reshape(s, GROUP_WIDTH), lse.reshape(s, GROUP_WIDTH), lse, rode


def _group_weights(lses):
    mx = jnp.maximum(jnp.maximum(lses[0], lses[1]), lses[2])
    es = [jnp.exp(v - mx) for v in lses]
    inv = 1.0 / (es[0] + es[1] + es[2])
    return [e * inv for e in es]


def mix_fwd(os_, lses, name):
    s, w = os_[0].shape
    tr = _row_tile(s)

    def body(o0, o1, o2, l0, l1, l2, out_ref):
        wg = _group_weights([l0[...], l1[...], l2[...]])
        out_ref[...] = (wg[0] * o0[...] + wg[1] * o1[...] + wg[2] * o2[...]).astype(out_ref.dtype)

    return pl.pallas_call(body, grid=(s // tr,), in_specs=[_rows(tr, w)] * 6, out_specs=_rows(tr, w),
                          out_shape=SDS((s, w), bf16), compiler_params=_params("parallel"), name=name)(*os_, *lses)


def mix_bwd(dmixed, os_, lses, head_ones, name):
    s, w = dmixed.shape
    tr = _row_tile(s)

    def head_sum(t, ones):
        hi = t.astype(bf16)
        lo = (t - hi.astype(f32)).astype(bf16)
        return jnp.dot(hi, ones, preferred_element_type=f32) + jnp.dot(lo, ones, preferred_element_type=f32)

    def body(dm_ref, o0, o1, o2, l0, l1, l2, ones_ref, d0, d1, d2, p0, p1, p2):
        dm = dm_ref[...]
        ones = ones_ref[...]
        wg = _group_weights([l0[...], l1[...], l2[...]])
        mean = sum(wg[k] * head_sum(dm * o[...], ones) for k, o in enumerate((o0, o1, o2)))
        for k, (d_ref, p_ref) in enumerate(((d0, p0), (d1, p1), (d2, p2))):
            d_ref[...] = (wg[k] * dm).astype(d_ref.dtype)
            p_ref[...] = wg[k] * mean

    return pl.pallas_call(body, grid=(s // tr,), in_specs=[_rows(tr, w)] * 7 + [_fixed(w, w)],
                          out_specs=[_rows(tr, w)] * 6,
                          out_shape=[SDS((s, w), bf16)] * 3 + [SDS((s, w), f32)] * 3,
                          compiler_params=_params("parallel"), name=name)(dmixed, *os_, *lses, head_ones)


def attn_bwd(qkvg, do, lse_phased, dterm, g, dil, name, rider=None):
    a, gi = _phases(qkvg, g, dil)
    s = qkvg.shape[1]
    l = s // dil
    nb = l // SPAN
    phased = lambda t: t.reshape(l, dil * GROUP_WIDTH)

    def body(*refs):
        first = jnp.logical_and(pl.program_id(0) == 0, pl.program_id(1) == 0)
        last = jnp.logical_and(pl.program_id(0) == dil - 1, pl.program_id(1) == nb - 1)
        own, ride_start, ride_finish = _rider_run(rider, refs, 8, 5, first, last)
        (q_ref, ko_ref, kp_ref, vo_ref, vp_ref, do_ref, lse_ref, dt_ref, dq_ref, dko_ref, dkp_ref, dvo_ref, dvp_ref,
         k_scr, v_scr) = own
        ride_start()
        mask = _attn_mask(pl.program_id(1))
        scale = HEAD_DIM ** -0.5
        k_scr[0:SPAN, :] = kp_ref[...]
        k_scr[SPAN:2 * SPAN, :] = ko_ref[...]
        v_scr[0:SPAN, :] = vp_ref[...]
        v_scr[SPAN:2 * SPAN, :] = vo_ref[...]
        for h in range(N_SLOTS):
            hs = slice(h * HEAD_DIM, (h + 1) * HEAD_DIM)
            one = slice(h * HEAD_DIM, h * HEAD_DIM + 1)
            q, kk, dov = q_ref[:, hs], k_scr[:, hs], do_ref[:, hs]
            p = jnp.exp(jnp.where(mask, _dot_nt(q, kk) * scale - lse_ref[:, one], NEG_BIG))
            ds = (p * (_dot_nt(dov, v_scr[:, hs]) - dt_ref[:, one]) * scale).astype(bf16)
            dq_ref[:, hs] = jnp.dot(ds, kk, preferred_element_type=f32).astype(dq_ref.dtype)
            dk = _dot_tn(ds, q).astype(dko_ref.dtype)
            dv = _dot_tn(p.astype(bf16), dov).astype(dvo_ref.dtype)
            dkp_ref[:, hs] = dk[:SPAN]
            dko_ref[:, hs] = dk[SPAN:]
            dvp_ref[:, hs] = dv[:SPAN]
            dvo_ref[:, hs] = dv[SPAN:]
        ride_finish()

    blk = pl.BlockSpec((SPAN, GROUP_WIDTH), lambda r, j: (j, r))
    r_in, r_out, r_shapes, r_scratch, r_alias, r_args = _rider_specs(rider, 8, 5)
    outs = pl.pallas_call(
        body, grid=(dil, nb), in_specs=_attn_in_specs(gi) + [blk, blk, blk] + r_in, out_specs=[blk] * 5 + r_out,
        out_shape=[SDS((l, dil * GROUP_WIDTH), bf16)] * 5 + r_shapes,
        scratch_shapes=[pltpu.VMEM((2 * SPAN, GROUP_WIDTH), bf16)] * 2 + r_scratch, input_output_aliases=r_alias,
        compiler_params=_params("arbitrary", "arbitrary"), name=name)(a, a, a, a, a, phased(do), lse_phased, phased(dterm), *r_args)
    return [t.reshape(s, GROUP_WIDTH) for t in outs[:5]] + [outs[5:]]


def dqkv_assemble(parts, pair_t, tabs, name):
    s = parts[0][0].shape[0]
    nblk = s // SPAN
    width = 3 * N_GROUPS * GROUP_WIDTH

    def body(*refs):
        ins, (p_ref, tc_ref, ts_ref, o_ref) = refs[:5 * N_GROUPS], refs[5 * N_GROUPS:]
        pair, tc, ts = p_ref[...], tc_ref[...], ts_ref[...]
        i = pl.program_id(0)
        for g, (_, dil) in enumerate(DILATED_GROUPS):
            dq, dko, dkp, dvo, dvp = ins[5 * g:5 * g + 5]
            has_next = i + dil < nblk
            dk = dko[...].astype(f32) + jnp.where(has_next, dkp[...].astype(f32), 0.0)
            dv = dvo[...].astype(f32) + jnp.where(has_next, dvp[...].astype(f32), 0.0)
            for kind, val in enumerate((_rotate(dq[...].astype(f32), pair, tc, ts), _rotate(dk, pair, tc, ts), dv)):
                base = (kind * N_GROUPS + g) * GROUP_WIDTH
                o_ref[:, base:base + GROUP_WIDTH] = val.astype(o_ref.dtype)

    here = _rows(SPAN, GROUP_WIDTH)
    in_specs, args = [], []
    for g, (_, dil) in enumerate(DILATED_GROUPS):
        ahead = pl.BlockSpec((SPAN, GROUP_WIDTH), functools.partial(lambda i, dil: (jnp.minimum(i + dil, nblk - 1), 0), dil=dil))
        in_specs += [here, here, ahead, here, ahead]
        args += list(parts[g])
    tab = _rows(SPAN, LANES)
    return pl.pallas_call(body, grid=(nblk,), in_specs=in_specs + [_fixed(GROUP_WIDTH, GROUP_WIDTH), tab, tab],
                          out_specs=_rows(SPAN, width), out_shape=SDS((s, width), bf16),
                          compiler_params=_params("parallel"), name=name)(*args, pair_t, *tabs)


def ffn_act_fwd(u, wb, name):
    _, nbk, s, c = u.shape
    tr = min(s, 256)

    def body(u_ref, h_ref, wb_ref, a_ref, ug_ref, su, sg):
        first = pl.program_id(1) == 0
        for lc in range(c // LANES):
            ln = slice(lc * LANES, (lc + 1) * LANES)
            for half, scr in enumerate((su, sg)):
                scr[0:FFN_HALO, ln] = jnp.where(first, 0.0, h_ref[half, :, ln].astype(f32))
            for r0 in range(0, tr, STRIP_ROWS):
                rows = slice(r0, r0 + STRIP_ROWS)
                conv = []
                for half, scr in enumerate((su, sg)):
                    xv = u_ref[half, rows, ln].astype(f32)
                    scr[FFN_HALO + r0:FFN_HALO + r0 + STRIP_ROWS, ln] = xv
                    acc = wb_ref[half, FFN_CONV:FFN_CONV + 1, ln] + wb_ref[half, FFN_CONV - 1:FFN_CONV, ln] * xv
                    for k in range(FFN_CONV - 1):
                        acc = acc + wb_ref[half, k:k + 1, ln] * scr[pl.ds(FFN_HALO + r0 - (FFN_CONV - 1) + k, STRIP_ROWS), ln]
                    ug_ref[half, rows, ln] = acc.astype(ug_ref.dtype)
                    conv.append(acc)
                up, gate = conv
                a_ref[rows, ln] = (gate * _sigmoid(gate) * up).astype(a_ref.dtype)

    both = pl.BlockSpec((2, None, tr, c), lambda p, i: (0, p, i, 0))
    return pl.pallas_call(
        body, grid=(nbk, s // tr),
        in_specs=[both,
                  pl.BlockSpec((2, None, FFN_HALO, c), lambda p, i: (0, p, jnp.maximum(i * (tr // FFN_HALO) - 1, 0), 0)),
                  pl.BlockSpec((None, 2, 8, c), lambda p, i: (p, 0, 0, 0))],
        out_specs=[pl.BlockSpec((None, tr, c), lambda p, i: (p, i, 0)), both],
        out_shape=[SDS((nbk, s, c), bf16), SDS(u.shape, bf16)],
        scratch_shapes=[pltpu.VMEM((tr + FFN_HALO, c), f32)] * 2,
        compiler_params=_params("parallel", "arbitrary"), name=name)(u, u, wb)


def ffn_act_bwd(da, ug, u, wb, name, rider=None):
    _, nbk, s, c = u.shape
    tr = min(s, 256)
    nt = s // tr

    def body(*refs):
        step = pl.program_id(1)
        first = jnp.logical_and(pl.program_id(0) == 0, step == 0)
        last = jnp.logical_and(pl.program_id(0) == nbk - 1, step == nt - 1)
        (da_ref, ug_ref, u_ref, wb_ref, du_ref, dwb_ref, eu, eg), ride_start, ride_finish = _rider_run(rider, refs, 4, 2, first, last)
        ride_start()

        @pl.when(step == 0)
        def _():
            eu[tr:tr + FFN_TAIL, :] = jnp.zeros((FFN_TAIL, c), f32)
            eg[tr:tr + FFN_TAIL, :] = jnp.zeros((FFN_TAIL, c), f32)
            dwb_ref[...] = jnp.zeros(dwb_ref.shape, f32)

        fold = lambda t: jnp.sum(t.reshape(STRIP_ROWS // 8, 8, LANES), axis=0)
        for lc in range(c // LANES):
            ln = slice(lc * LANES, (lc + 1) * LANES)
            sums = [[jnp.zeros((8, LANES), f32) for _ in range(FFN_CONV + 1)] for _ in range(2)]
            for r0 in reversed(range(0, tr, STRIP_ROWS)):
                rows = slice(r0, r0 + STRIP_ROWS)
                up, gate = ug_ref[0, rows, ln].astype(f32), ug_ref[1, rows, ln].astype(f32)
                sig = _sigmoid(gate)
                dav = da_ref[rows, ln].astype(f32)
                grads = (dav * (gate * sig), dav * up * (sig * (1.0 + gate * (1.0 - sig))))
                for half, ext in enumerate((eu, eg)):
                    dv = grads[half]
                    ext[rows, ln] = dv
                    xv = u_ref[half, rows, ln].astype(f32)
                    acc = None
                    for k in range(FFN_CONV):
                        ahead = dv if k == FFN_CONV - 1 else ext[pl.ds(r0 + FFN_CONV - 1 - k, STRIP_ROWS), ln]
                        term = wb_ref[half, k:k + 1, ln] * ahead
                        acc = term if acc is None else acc + term
                        sums[half][k] = sums[half][k] + fold(xv * ahead)
                    sums[half][FFN_CONV] = sums[half][FFN_CONV] + fold(dv)
                    du_ref[half, rows, ln] = acc.astype(du_ref.dtype)
            for half, ext in enumerate((eu, eg)):
                ext[tr:tr + FFN_TAIL, ln] = ext[0:FFN_TAIL, ln]
                for k in range(FFN_CONV + 1):
                    dwb_ref[half, k:k + 1, ln] += jnp.sum(sums[half][k], axis=0, keepdims=True)
        ride_finish()

    rev = lambda i: nt - 1 - i
    both = pl.BlockSpec((2, None, tr, c), lambda p, i: (0, p, rev(i), 0))
    r_in, r_out, r_shapes, r_scratch, r_alias, r_args = _rider_specs(rider, 4, 2)
    outs = pl.pallas_call(
        body, grid=(nbk, nt),
        in_specs=[pl.BlockSpec((None, tr, c), lambda p, i: (p, rev(i), 0)), both, both,
                  pl.BlockSpec((None, 2, 8, c), lambda p, i: (p, 0, 0, 0))] + r_in,
        out_specs=[both, pl.BlockSpec((None, 2, 8, c), lambda p, i: (p, 0, 0, 0))] + r_out,
        out_shape=[SDS((2, nbk, s, c), bf16), SDS((nbk, 2, 8, c), f32)] + r_shapes,
        scratch_shapes=[pltpu.VMEM((tr + FFN_TAIL, c), f32)] * 2 + r_scratch, input_output_aliases=r_alias,
        compiler_params=_params("arbitrary", "arbitrary"), name=name)(da, ug, u, wb, *r_args)
    return outs[0], outs[1], outs[2:]


def _glu(zv, c):
    return zv[:, :c] * _sigmoid(zv[:, c:])


def _conv_fill(z_ref, h_ref, scr, first, tr, c):
    scr[0:CONV_HALO, :] = jnp.where(first, 0.0, _glu(h_ref[...], c))
    scr[CONV_HALO:CONV_HALO + tr, :] = _glu(z_ref[...], c)


def _conv_taps(b):
    return [(a, CONV_KERNEL - 1 - 8 * a - b) for a in range(CONV_HALO // 8) if CONV_KERNEL - 1 - 8 * a - b >= 0]


def _layernorm_parts(cv):
    mu = jnp.mean(cv, axis=-1, keepdims=True)
    cen = cv - mu
    rstd = lax.rsqrt(jnp.mean(cen * cen, axis=-1, keepdims=True) + EPS)
    return cen * rstd, rstd


def conv_module_fwd(z, wdw, vecs, name):
    s, c2 = z.shape
    c = c2 // 2
    tr = min(s, 256)

    def body(z_ref, h_ref, w_ref, v_ref, c_ref, s_ref, scr, zb):
        _conv_fill(z_ref, h_ref, scr, pl.program_id(0) == 0, tr, c)
        acc = jnp.broadcast_to(v_ref[0:1, :], (tr, c))
        for b in range(8):
            part = None
            for a, j in _conv_taps(b):
                term = w_ref[j:j + 1, :] * scr[pl.ds(CONV_HALO - 8 - 8 * a, tr + 8), :]
                part = term if part is None else part + term
            if b == 0:
                acc = acc + part[8:]
            else:
                zb[...] = part
                acc = acc + zb[pl.ds(8 - b, tr), :]
        c_ref[...] = acc
        chat, _ = _layernorm_parts(acc)
        ln = chat * v_ref[1:2, :] + v_ref[2:3, :]
        s_ref[...] = (ln * _sigmoid(ln)).astype(s_ref.dtype)

    return pl.pallas_call(
        body, grid=(s // tr,),
        in_specs=[_rows(tr, c2), pl.BlockSpec((CONV_HALO, c2), lambda i: (jnp.maximum(i * (tr // CONV_HALO) - 1, 0), 0)),
                  _fixed(CONV_HALO, c), _fixed(8, c)],
        out_specs=[_rows(tr, c), _rows(tr, c)], out_shape=[SDS((s, c), f32), SDS((s, c), bf16)],
        scratch_shapes=[pltpu.VMEM((tr + CONV_HALO, c), f32), pltpu.VMEM((tr + 8, c), f32)],
        compiler_params=_params("arbitrary"), name=name)(z, z, wdw, vecs)


def conv_module_bwd(ds, cpre, z, wdw, vecs, name):
    s, c2 = z.shape
    c = c2 // 2
    tr = min(s, 256)
    nt = s // tr

    def body(ds_ref, c_ref, z_ref, w_ref, v_ref, dz_ref, dw_ref, dv_ref, db_ref, ext, dwp):
        step = pl.program_id(0)

        @pl.when(step == 0)
        def _():
            ext[tr:tr + CONV_HALO, :] = jnp.zeros((CONV_HALO, c), f32)
            dwp[...] = jnp.zeros(dwp.shape, f32)
            dv_ref[...] = jnp.zeros(dv_ref.shape, f32)
            db_ref[...] = jnp.zeros(db_ref.shape, f32)

        gain, bias = v_ref[1:2, :], v_ref[2:3, :]
        fold16 = lambda t: t[:8] + t[8:]
        sums = [jnp.zeros((8, c), f32) for _ in range(3)]
        for r0 in range(0, tr, 16):
            rows = slice(r0, r0 + 16)
            chat, rstd = _layernorm_parts(c_ref[rows, :])
            ln = chat * gain + bias
            sig = _sigmoid(ln)
            dln = ds_ref[rows, :].astype(f32) * (sig * (1.0 + ln * (1.0 - sig)))
            gy = dln * gain
            dc = rstd * (gy - jnp.mean(gy, axis=-1, keepdims=True) - chat * jnp.mean(gy * chat, axis=-1, keepdims=True))
            ext[rows, :] = dc
            for k, t in enumerate((dc, dln * chat, dln)):
                sums[k] = sums[k] + fold16(t)
        for k in range(3):
            dv_ref[k:k + 1, :] += jnp.sum(sums[k], axis=0, keepdims=True)

        fold = lambda t: jnp.sum(t.reshape(STRIP_ROWS // 8, 8, LANES), axis=0)
        for lc in range(c // LANES):
            ln_a = slice(lc * LANES, (lc + 1) * LANES)
            ln_g = slice(c + lc * LANES, c + (lc + 1) * LANES)
            dbs = [jnp.zeros((8, LANES), f32) for _ in range(2)]
            for r0 in range(0, tr, STRIP_ROWS):
                rows = slice(r0, r0 + STRIP_ROWS)
                a, sg = z_ref[rows, ln_a], _sigmoid(z_ref[rows, ln_g])
                uv = a * sg
                du = jnp.zeros((STRIP_ROWS, LANES), f32)
                for b in range(8):
                    src = ext[pl.ds(r0 + b, STRIP_ROWS + CONV_HALO - 8), ln_a]
                    for a8, j in _conv_taps(b):
                        ahead = src[8 * a8:8 * a8 + STRIP_ROWS]
                        du = du + w_ref[j:j + 1, ln_a] * ahead
                        dwp[8 * j:8 * j + 8, ln_a] += fold(uv * ahead)
                da = du * sg
                dg = du * a * (sg * (1.0 - sg))
                dz_ref[rows, ln_a] = da.astype(dz_ref.dtype)
                dz_ref[rows, ln_g] = dg.astype(dz_ref.dtype)
                dbs = [dbs[0] + fold(da), dbs[1] + fold(dg)]
            db_ref[:, ln_a] += jnp.sum(dbs[0], axis=0, keepdims=True)
            db_ref[:, ln_g] += jnp.sum(dbs[1], axis=0, keepdims=True)
        ext[tr:tr + CONV_HALO, :] = ext[0:CONV_HALO, :]

        @pl.when(step == nt - 1)
        def _():
            for j in range(CONV_HALO):
                dw_ref[j:j + 1, :] = jnp.sum(dwp[8 * j:8 * j + 8, :], axis=0, keepdims=True)

    rev = lambda i: nt - 1 - i
    back = lambda d: pl.BlockSpec((tr, d), lambda i: (rev(i), 0))
    return pl.pallas_call(
        body, grid=(nt,),
        in_specs=[back(c), back(c), back(c2), _fixed(CONV_HALO, c), _fixed(8, c)],
        out_specs=[back(c2), _fixed(CONV_HALO, c), _fixed(8, c), _fixed(1, c2)],
        out_shape=[SDS((s, c2), bf16), SDS((CONV_HALO, c), f32), SDS((8, c), f32), SDS((1, c2), f32)],
        scratch_shapes=[pltpu.VMEM((tr + CONV_HALO, c), f32), pltpu.VMEM((8 * CONV_HALO, c), f32)],
        compiler_params=_params("arbitrary"), name=name)(ds, cpre, z, wdw, vecs)


def _tile2d(r, n):
    tn = n if n <= 2048 else 1024
    tr = r
    while tr * tn * 4 > (1 << 21) and tr % 16 == 0:
        tr //= 2
    assert r % tr == 0 and n % tn == 0
    return tr, tn


def adamw(w, g, m, v, name):
    r, n = w.shape
    tr, tn = _tile2d(r, n)

    def body(w_ref, g_ref, m_ref, v_ref, d_ref, nm_ref, nv_ref):
        gv = g_ref[...]
        nm = ADAM_B1 * m_ref[...] + (1.0 - ADAM_B1) * gv
        nv = ADAM_B2 * v_ref[...] + (1.0 - ADAM_B2) * (gv * gv)
        m_hat = nm / (1.0 - ADAM_B1 ** ADAM_STEP)
        v_hat = nv / (1.0 - ADAM_B2 ** ADAM_STEP)
        d_ref[...] = -ADAM_LR * (m_hat / (jnp.sqrt(v_hat) + ADAM_EPS) + ADAM_WD * w_ref[...])
        nm_ref[...] = nm
        nv_ref[...] = nv

    blk = pl.BlockSpec((tr, tn), lambda i, j: (i, j))
    return pl.pallas_call(body, grid=(r // tr, n // tn), in_specs=[blk] * 4, out_specs=[blk] * 3,
                          out_shape=[SDS((r, n), f32)] * 3, compiler_params=_params("parallel", "parallel"),
                          name=name)(w, g, m, v)


def add_core_halves(grad, got, where, name):
    _, _, rh, n = grad.shape
    tr, tn = _tile2d(rh, n)

    def body(w_ref, a_ref, b_ref, o_ref):
        o_ref[...] = (a_ref[...] + b_ref[...]).astype(o_ref.dtype)

    return pl.pallas_call(
        body,
        grid_spec=pltpu.PrefetchScalarGridSpec(
            num_scalar_prefetch=1, grid=(N_CHIPS, rh // tr, n // tn),
            in_specs=[pl.BlockSpec((None, None, tr, tn), lambda p, i, j, w_ref: (p, w_ref[0], i, j)),
                      pl.BlockSpec((None, tr, tn), lambda p, i, j, w_ref: (p, i, j))],
            out_specs=pl.BlockSpec((None, tr, tn), lambda p, i, j, w_ref: (p, i, j))),
        out_shape=SDS((N_CHIPS, rh, n), bf16), compiler_params=_params("parallel", "parallel", "parallel"),
        name=name)(where, grad, got)


def add_chip_parts(grad, got_core, got_chips, where, name):
    _, _, rh, n = grad.shape
    tr, tn = _tile2d(rh, n)

    def body(w_ref, a_ref, b_ref, g_ref, o_ref):
        acc = a_ref[...] + b_ref[...]
        for k in range(N_CHIPS - 1):
            acc = acc + g_ref[k].astype(f32)
        o_ref[...] = acc

    return pl.pallas_call(
        body,
        grid_spec=pltpu.PrefetchScalarGridSpec(
            num_scalar_prefetch=1, grid=(rh // tr, n // tn),
            in_specs=[pl.BlockSpec((None, None, tr, tn), lambda i, j, w_ref: (w_ref[1], w_ref[0], i, j)),
                      pl.BlockSpec((None, tr, tn), lambda i, j, w_ref: (w_ref[1], i, j)),
                      pl.BlockSpec((N_CHIPS - 1, tr, tn), lambda i, j, w_ref: (0, i, j))],
            out_specs=pl.BlockSpec((None, tr, tn), lambda i, j, w_ref: (w_ref[0], i, j))),
        out_shape=SDS((2, rh, n), f32), compiler_params=_params("parallel", "parallel"),
        name=name)(where, grad, got_core, got_chips)


def sum_devices(parts, name):
    nd, r, n = parts.shape

    def body(p_ref, o_ref):
        acc = p_ref[0]
        for k in range(1, nd):
            acc = acc + p_ref[k]
        o_ref[...] = acc

    return pl.pallas_call(body, out_shape=SDS((r, n), f32), name=name)(parts)


def _position():
    return lax.axis_index("x"), lax.axis_index("y"), lax.axis_index("c")


def _other_chips(x, y):
    return [(1 - x, y), (x, 1 - y), (1 - x, 1 - y)]


def _remote(src, dst, send, recv, to):
    return pltpu.make_async_remote_copy(src_ref=src, dst_ref=dst, send_sem=send, recv_sem=recv, device_id=to,
                                        device_id_type=MESH)


def gather_chips(bufs, name):
    n = len(bufs)
    nk = N_CHIPS - 1

    def body(*refs):
        bufs_ = refs[n:2 * n]
        send, recv = refs[2 * n:]
        x, y, c = _position()
        me = 2 * x + y
        chips = _other_chips(x, y)
        sends = []
        for t in range(n):
            for k, (px, py) in enumerate(chips):
                out = _remote(bufs_[t].at[me, c], bufs_[t].at[me, c], send.at[t, k], recv.at[t, k], (px, py, c))
                out.start()
                sends.append(out)
        for t in range(n):
            for k, (px, py) in enumerate(chips):
                piece = bufs_[t].at[2 * px + py, c]
                _remote(piece, piece, send.at[t, k], recv.at[t, k], (px, py, c)).wait_recv()
                on = _remote(piece, piece, send.at[t, nk + k], recv.at[t, nk + k], (x, y, 1 - c))
                on.start()
                sends.append(on)
        for t in range(n):
            for k, (px, py) in enumerate(chips):
                piece = bufs_[t].at[2 * px + py, 1 - c]
                _remote(piece, piece, send.at[t, nk + k], recv.at[t, nk + k], (x, y, 1 - c)).wait_recv()
        for cp in sends:
            cp.wait_send()

    return pl.pallas_call(
        body, in_specs=[ANY] * n, out_specs=[ANY] * n,
        out_shape=[SDS(a.shape, a.dtype) for a in bufs],
        input_output_aliases={t: t for t in range(n)},
        scratch_shapes=[pltpu.SemaphoreType.DMA((n, 2 * nk)), pltpu.SemaphoreType.DMA((n, 2 * nk))],
        name=name)(*bufs)


class Rider(NamedTuple):
    operands: tuple
    n_aliased: int
    out_shapes: tuple
    scratch: tuple
    start: Callable
    finish: Callable


def _rider_specs(rider, n_inputs, n_outputs):
    if rider is None:
        return [], [], [], [], {}, []
    aliased = [SDS(a.shape, a.dtype) for a in rider.operands[:rider.n_aliased]]
    outs = aliased + list(rider.out_shapes)
    aliases = {n_inputs + t: n_outputs + t for t in range(rider.n_aliased)}
    return [ANY] * len(rider.operands), [ANY] * len(outs), outs, list(rider.scratch), aliases, list(rider.operands)


def _rider_run(rider, refs, n_inputs, n_outputs, first, last):
    if rider is None:
        return refs, lambda: None, lambda: None
    n_op = len(rider.operands)
    n_out = rider.n_aliased + len(rider.out_shapes)
    own_in, r_in = refs[:n_inputs], refs[n_inputs:n_inputs + n_op]
    own_out = refs[n_inputs + n_op:n_inputs + n_op + n_outputs]
    r_out = refs[n_inputs + n_op + n_outputs:n_inputs + n_op + n_outputs + n_out]
    rest = refs[n_inputs + n_op + n_outputs + n_out:]
    n_sem = len(rider.scratch)
    sems, own_scratch = rest[len(rest) - n_sem:], rest[:len(rest) - n_sem]

    def start():
        pl.when(first)(lambda: rider.start(r_in, r_out, sems))

    def finish():
        pl.when(last)(lambda: rider.finish(r_in, r_out, sems))

    return list(own_in) + list(own_out) + list(own_scratch), start, finish


def gather_rider(bufs):
    n = len(bufs)
    nk = N_CHIPS - 1

    def ici(bufs_, send, recv, x, y, c, t, k, px, py, own):
        piece = bufs_[t].at[2 * x + y if own else 2 * px + py, c]
        return _remote(piece, piece, send.at[t, k], recv.at[t, k], (px, py, c))

    def d2d(bufs_, send, recv, x, y, c, t, k, px, py, mine):
        piece = bufs_[t].at[2 * px + py, c if mine else 1 - c]
        return _remote(piece, piece, send.at[t, nk + k], recv.at[t, nk + k], (x, y, 1 - c))

    def start(r_in, r_out, sems):
        send, recv = sems
        x, y, c = _position()
        for t in range(n):
            for k, (px, py) in enumerate(_other_chips(x, y)):
                ici(r_out, send, recv, x, y, c, t, k, px, py, True).start()

    def finish(r_in, r_out, sems):
        send, recv = sems
        x, y, c = _position()
        chips = _other_chips(x, y)
        for t in range(n):
            for k, (px, py) in enumerate(chips):
                ici(r_out, send, recv, x, y, c, t, k, px, py, False).wait_recv()
                d2d(r_out, send, recv, x, y, c, t, k, px, py, True).start()
        for t in range(n):
            for k, (px, py) in enumerate(chips):
                d2d(r_out, send, recv, x, y, c, t, k, px, py, False).wait_recv()
        for t in range(n):
            for k, (px, py) in enumerate(chips):
                ici(r_out, send, recv, x, y, c, t, k, px, py, True).wait_send()
                d2d(r_out, send, recv, x, y, c, t, k, px, py, True).wait_send()

    sems = (pltpu.SemaphoreType.DMA((n, 2 * nk)), pltpu.SemaphoreType.DMA((n, 2 * nk)))
    return Rider(tuple(bufs), n, (), sems, start, finish)


def scatter_rider(parts):
    n = len(parts)

    def copies(r_in, r_out, sems):
        send, recv = sems
        x, y, c = _position()
        return [_remote(r_in[t].at[2 * px + py], r_out[t].at[k], send.at[t, k], recv.at[t, k], (px, py, c))
                for t in range(n) for k, (px, py) in enumerate(_other_chips(x, y))]

    def start(r_in, r_out, sems):
        for cp in copies(r_in, r_out, sems):
            cp.start()

    def finish(r_in, r_out, sems):
        for cp in copies(r_in, r_out, sems):
            cp.wait()

    sems = (pltpu.SemaphoreType.DMA((n, N_CHIPS - 1)), pltpu.SemaphoreType.DMA((n, N_CHIPS - 1)))
    return Rider(tuple(parts), 0, tuple(SDS((N_CHIPS - 1,) + a.shape[1:], a.dtype) for a in parts), sems, start, finish)


def swap_rider(halves):
    n = len(halves)

    def copies(r_in, r_out, sems):
        send, recv = sems
        x, y, c = _position()
        return [_remote(r_in[t].at[:, 1 - c], r_out[t], send.at[t], recv.at[t], (x, y, 1 - c)) for t in range(n)]

    def start(r_in, r_out, sems):
        for cp in copies(r_in, r_out, sems):
            cp.start()

    def finish(r_in, r_out, sems):
        for cp in copies(r_in, r_out, sems):
            cp.wait()

    sems = (pltpu.SemaphoreType.DMA((n,)), pltpu.SemaphoreType.DMA((n,)))
    return Rider(tuple(halves), 0, tuple(SDS((a.shape[0],) + a.shape[2:], a.dtype) for a in halves), sems, start, finish)


def join_riders(a, b):
    assert a.n_aliased == 0 and b.n_aliased == 0
    na, nao, nas = len(a.operands), len(a.out_shapes), len(a.scratch)

    def start(r_in, r_out, sems):
        a.start(r_in[:na], r_out[:nao], sems[:nas])
        b.start(r_in[na:], r_out[nao:], sems[nas:])

    def finish(r_in, r_out, sems):
        a.finish(r_in[:na], r_out[:nao], sems[:nas])
        b.finish(r_in[na:], r_out[nao:], sems[nas:])

    return Rider(a.operands + b.operands, 0, a.out_shapes + b.out_shapes, a.scratch + b.scratch, start, finish)


def swap_core_halves(grads, name):
    n = len(grads)

    def body(*refs):
        ins, outs = refs[:n], refs[n:2 * n]
        send, recv = refs[2 * n:]
        x, y, c = _position()
        pending = []
        for t in range(n):
            out = _remote(ins[t].at[:, 1 - c], outs[t], send.at[t], recv.at[t], (x, y, 1 - c))
            out.start()
            pending.append(out.wait)
        for wait in pending:
            wait()

    return pl.pallas_call(
        body, in_specs=[ANY] * n, out_specs=[ANY] * n,
        out_shape=[SDS((a.shape[0],) + a.shape[2:], a.dtype) for a in grads],
        scratch_shapes=[pltpu.SemaphoreType.DMA((n,)), pltpu.SemaphoreType.DMA((n,))],
        name=name)(*grads)


def scatter_chips(parts, name):
    n = len(parts)

    def body(*refs):
        ins, outs = refs[:n], refs[n:2 * n]
        send, recv = refs[2 * n:]
        x, y, c = _position()
        pending = []
        for t in range(n):
            for k, (px, py) in enumerate(_other_chips(x, y)):
                out = _remote(ins[t].at[2 * px + py], outs[t].at[k], send.at[t, k], recv.at[t, k], (px, py, c))
                out.start()
                pending.append(out.wait)
        for wait in pending:
            wait()

    return pl.pallas_call(
        body, in_specs=[ANY] * n, out_specs=[ANY] * n,
        out_shape=[SDS((N_CHIPS - 1,) + a.shape[1:], a.dtype) for a in parts],
        scratch_shapes=[pltpu.SemaphoreType.DMA((n, N_CHIPS - 1)), pltpu.SemaphoreType.DMA((n, N_CHIPS - 1))],
        name=name)(*parts)


def share_core_halves(bufs, name):
    n = len(bufs)

    def body(*refs):
        bufs_ = refs[n:2 * n]
        send, recv = refs[2 * n:]
        x, y, c = _position()
        pending = []
        for t in range(n):
            out = _remote(bufs_[t].at[c], bufs_[t].at[c], send.at[t], recv.at[t], (x, y, 1 - c))
            out.start()
            pending.append(out.wait_send)
            other = bufs_[t].at[1 - c]
            pending.append(_remote(other, other, send.at[t], recv.at[t], (x, y, 1 - c)).wait_recv)
        for wait in pending:
            wait()

    return pl.pallas_call(
        body, in_specs=[ANY] * n, out_specs=[ANY] * n, out_shape=[SDS(a.shape, a.dtype) for a in bufs],
        input_output_aliases={t: t for t in range(n)},
        scratch_shapes=[pltpu.SemaphoreType.DMA((n,)), pltpu.SemaphoreType.DMA((n,))],
        name=name)(*bufs)


def gather_devices(v, name):
    flips = [(dx, dy, dc) for dx in (0, 1) for dy in (0, 1) for dc in (0, 1)][1:]

    def body(v_ref, o_ref, send, recv, local):
        x, y, c = _position()
        me = 4 * x + 2 * y + c
        own = pltpu.make_async_copy(v_ref, o_ref.at[me], local)
        own.start()
        pending = [own.wait]
        for k, (dx, dy, dc) in enumerate(flips):
            px, py, pc = x ^ dx, y ^ dy, c ^ dc
            out = _remote(v_ref, o_ref.at[me], send.at[k], recv.at[k], (px, py, pc))
            out.start()
            pending.append(out.wait_send)
            pending.append(_remote(v_ref, o_ref.at[4 * px + 2 * py + pc], send.at[k], recv.at[k], (px, py, pc)).wait_recv)
        for wait in pending:
            wait()

    return pl.pallas_call(
        body, in_specs=[ANY], out_specs=ANY, out_shape=SDS((8,) + v.shape, v.dtype),
        scratch_shapes=[pltpu.SemaphoreType.DMA((7,)), pltpu.SemaphoreType.DMA((7,)), pltpu.SemaphoreType.DMA],
        name=name)(v)


SMALL = ("norm_g", "conv_b_pw1", "conv_w_dw", "conv_b_dw", "conv_ln_g", "conv_ln_b", "conv_b_pw2", "ffn_w_dw")


def _pack_rows(arrs, rows):
    flat = jnp.concatenate([a.reshape(-1, LANES) for a in arrs], axis=0)
    return jnp.pad(flat, ((0, rows - flat.shape[0]), (0, 0)))


def _unpack_rows(packed, shapes):
    out, at = [], 0
    for shp in shapes:
        size = 1
        for dim in shp:
            size *= dim
        rows = size // LANES
        out.append(packed[..., at:at + rows, :].reshape(packed.shape[:-2] + tuple(shp)))
        at += rows
    return out


def _join_last(t):
    t = jnp.moveaxis(t, 0, -2)
    return t.reshape(t.shape[:-2] + (t.shape[-2] * t.shape[-1],))


def _split_last(t):
    t = t.reshape(t.shape[:-1] + (N_CHIPS, t.shape[-1] // N_CHIPS))
    return jnp.moveaxis(t, -2, 0)


def _rope_tables(positions):
    half = ROT_DIM // 2
    inv_freq = ROPE_THETA ** (-jnp.arange(half, dtype=f32) / half)
    ang = positions.astype(f32).reshape(-1, 1) * inv_freq
    cos, sin = jnp.cos(ang), jnp.sin(ang)
    s = ang.shape[0]
    rest = HEAD_DIM - ROT_DIM
    head = lambda t, fill: jnp.concatenate([t, t, jnp.full((s, rest), fill, f32)], axis=1)
    twice = lambda t: jnp.concatenate([t] * (LANES // HEAD_DIM), axis=1)
    return twice(head(cos, 1.0)), twice(head(sin, 0.0))


def _pairing_matrix():
    half = ROT_DIM // 2
    row = jnp.arange(GROUP_WIDTH)[:, None]
    col = jnp.arange(GROUP_WIDTH)[None, :]
    d = col % HEAD_DIM
    minus = jnp.logical_and(row == col + half, d < half)
    plus = jnp.logical_and(row == col - half, jnp.logical_and(d >= half, d < ROT_DIM))
    return (plus.astype(f32) - minus.astype(f32)).astype(bf16)


def _pad_rows(t, rows):
    return jnp.pad(t, ((0, rows - t.shape[0]), (0, 0)))


def _ffn_pack(w_dw, b_dw):
    t = jnp.concatenate([w_dw, b_dw[None]], axis=0)
    t = t.reshape(FFN_CONV + 1, 2, 2, -1)
    t = jnp.transpose(t, (2, 1, 0, 3))
    return jnp.pad(t, ((0, 0), (0, 0), (0, 8 - (FFN_CONV + 1)), (0, 0)))


def _ffn_unpack(d):
    t = jnp.transpose(d[:, :, :FFN_CONV + 1], (2, 1, 0, 3)).reshape(FFN_CONV + 1, -1)
    return t[:FFN_CONV], t[FFN_CONV]


def _ffn_block(h, w_up, li, w_down, wb, tag):
    s = h.shape[0]
    u = mm_nn(h[None], w_up, li, None, N_CHIPS, bf16, 1024, w_up.shape[-1], f"{tag}_up")
    u4 = u.reshape(2, 2, s, u.shape[-1])
    a, ug = ffn_act_fwd(u4, wb, f"{tag}_act")
    y = mm_nn(a, w_down, 0, None, 1, f32, 1024, w_down.shape[-1], f"{tag}_down")[0]
    return y, (h, u4, ug, a)


def _ffn_block_bwd(dy, saved, w_up, li, w_down, wb, tag, rider=None):
    h, u4, ug, a = saved
    d_down = mm_tn(a, dy[None], 1, 1024, a.shape[-1], dy.shape[-1], f"{tag}_dwdown")
    da = mm_nt(dy[None], w_down, 0, 2, bf16, 1024, a.shape[-1], w_down.shape[-1], f"{tag}_da")
    du4, dwb, rode = ffn_act_bwd(da, ug, u4, wb, f"{tag}_actbwd", rider)
    du = du4.reshape((N_CHIPS,) + du4.shape[2:])
    d_up = mm_tn(h[None], du, N_CHIPS, 1024, h.shape[-1], du.shape[-1], f"{tag}_dwup")
    dh = mm_nt_whole(du, w_up, li, f32, 512, f"{tag}_dh")
    return dh, d_up, d_down, dwb, rode


def kernel(x, positions, norm_g, attn_w_qkv, attn_w_o, conv_w_pw1, conv_b_pw1, conv_w_dw, conv_b_dw, conv_ln_g, conv_ln_b, conv_w_pw2, conv_b_pw2, ffn_w_up, ffn_w_dw, ffn_b_dw, ffn_w_down, loss_target, m_norm_g, m_attn_w_qkv, m_attn_w_o, m_conv_w_pw1, m_conv_b_pw1, m_conv_w_dw, m_conv_b_dw, m_conv_ln_g, m_conv_ln_b, m_conv_w_pw2, m_conv_b_pw2, m_ffn_w_up, m_ffn_w_dw, m_ffn_b_dw, m_ffn_w_down, v_norm_g, v_attn_w_qkv, v_attn_w_o, v_conv_w_pw1, v_conv_b_pw1, v_conv_w_dw, v_conv_b_dw, v_conv_ln_g, v_conv_ln_b, v_conv_w_pw2, v_conv_b_pw2, v_ffn_w_up, v_ffn_w_dw, v_ffn_b_dw, v_ffn_w_down):
    weights = dict(norm_g=norm_g, attn_w_qkv=attn_w_qkv, attn_w_o=attn_w_o, conv_w_pw1=conv_w_pw1, conv_b_pw1=conv_b_pw1,
                   conv_w_dw=conv_w_dw, conv_b_dw=conv_b_dw, conv_ln_g=conv_ln_g, conv_ln_b=conv_ln_b, conv_w_pw2=conv_w_pw2,
                   conv_b_pw2=conv_b_pw2, ffn_w_up=ffn_w_up, ffn_w_dw=ffn_w_dw, ffn_b_dw=ffn_b_dw, ffn_w_down=ffn_w_down)
    mom1 = dict(norm_g=m_norm_g, attn_w_qkv=m_attn_w_qkv, attn_w_o=m_attn_w_o, conv_w_pw1=m_conv_w_pw1, conv_b_pw1=m_conv_b_pw1,
                conv_w_dw=m_conv_w_dw, conv_b_dw=m_conv_b_dw, conv_ln_g=m_conv_ln_g, conv_ln_b=m_conv_ln_b, conv_w_pw2=m_conv_w_pw2,
                conv_b_pw2=m_conv_b_pw2, ffn_w_up=m_ffn_w_up, ffn_w_dw=m_ffn_w_dw, ffn_b_dw=m_ffn_b_dw, ffn_w_down=m_ffn_w_down)
    mom2 = dict(norm_g=v_norm_g, attn_w_qkv=v_attn_w_qkv, attn_w_o=v_attn_w_o, conv_w_pw1=v_conv_w_pw1, conv_b_pw1=v_conv_b_pw1,
                conv_w_dw=v_conv_w_dw, conv_b_dw=v_conv_b_dw, conv_ln_g=v_conv_ln_g, conv_ln_b=v_conv_ln_b, conv_w_pw2=v_conv_w_pw2,
                conv_b_pw2=v_conv_b_pw2, ffn_w_up=v_ffn_w_up, ffn_w_dw=v_ffn_w_dw, ffn_b_dw=v_ffn_b_dw, ffn_w_down=v_ffn_w_down)
    big = ("attn_w_qkv", "attn_w_o", "conv_w_pw1", "conv_w_pw2", "ffn_w_up", "ffn_w_down")
    xi, yi, ci = _position()
    chip = (2 * xi + yi).astype(jnp.int32).reshape(1)
    where = jnp.stack([ci, 2 * xi + yi]).astype(jnp.int32)

    x = x[0]
    target = loss_target[0]
    s, d = x.shape

    small_shapes = [weights[k].shape for k in SMALL]
    small_rows = -(-sum(weights[k].size for k in SMALL) // LANES // 8) * 8
    small_w = _pack_rows([weights[k] for k in SMALL], small_rows)
    def own_slot(shard):
        halves = shard.reshape(1, 2, -1, shard.shape[-1])
        return lax.dynamic_update_slice(lax.empty((N_CHIPS,) + halves.shape[1:], shard.dtype), halves, (chip[0], 0, 0, 0))

    early, late = big[:2], big[2:]
    gathered = gather_chips([own_slot(weights[k].astype(bf16)) for k in early] + [own_slot(small_w)], "gather_weights_early")
    gw = {k: t.reshape((N_CHIPS,) + weights[k].shape) for k, t in zip(early, gathered[:-1])}
    full_small = dict(zip(SMALL, [_join_last(t) for t in _unpack_rows(gathered[-1].reshape(N_CHIPS, small_rows, LANES), small_shapes)]))
    w_qkv, w_o = gw["attn_w_qkv"], gw["attn_w_o"]
    gains = full_small["norm_g"]
    gain = lambda i, k: gains[i, k][None]
    b_pw1 = full_small["conv_b_pw1"]
    conv_wdw = _pad_rows(full_small["conv_w_dw"][0], CONV_HALO)
    conv_vecs = _pad_rows(jnp.concatenate([full_small["conv_b_dw"], full_small["conv_ln_g"], full_small["conv_ln_b"]], axis=0), 8)
    b_pw2 = full_small["conv_b_pw2"]
    wbs = [_ffn_pack(full_small["ffn_w_dw"][i], ffn_b_dw[i]) for i in range(2)]
    tabs = _rope_tables(positions[0])
    pair = _pairing_matrix()
    head_ones = (jnp.arange(GROUP_WIDTH)[:, None] // HEAD_DIM == jnp.arange(GROUP_WIDTH)[None, :] // HEAD_DIM).astype(bf16)

    h0 = prenorm(x, gain(0, 0), "l0_prenorm")
    w_qkv_flat = jnp.transpose(w_qkv[:, 0], (1, 0, 2)).reshape(d, -1)
    under_qkv, under_attn = ("ffn_w_up",), ("conv_w_pw1", "conv_w_pw2", "ffn_w_down")
    assert set(under_qkv + under_attn) == set(late)
    slots = lambda names: [own_slot(weights[k].astype(bf16)) for k in names]
    qkvg, got = qkv_rope(h0, w_qkv_flat, pair, tabs, "qkv", gather_rider(slots(under_qkv)))
    gw.update({k: t.reshape((N_CHIPS,) + weights[k].shape) for k, t in zip(under_qkv, got)})
    att = [attn_fwd(qkvg, g, dil, f"attn_fwd{g}", gather_rider(slots(under_attn)) if g == 0 else None)
           for g, (_, dil) in enumerate(DILATED_GROUPS)]
    gw.update({k: t.reshape((N_CHIPS,) + weights[k].shape) for k, t in zip(under_attn, att[0][3])})
    w_pw1, w_up = gw["conv_w_pw1"], gw["ffn_w_up"]
    w_pw2 = gw["conv_w_pw2"].reshape(1, 1, -1, d)
    w_down = [gw["ffn_w_down"][:, i].reshape(1, 1, -1, d) for i in range(2)]
    os_, lses, lses_phased = ([a[k] for a in att] for k in range(3))
    mixed = mix_fwd(os_, lses, "mix")
    w_o_flat = jnp.transpose(w_o[:, 0], (1, 0, 2)).reshape(1, 1, GROUP_WIDTH, d)
    y0 = mm_nn(mixed[None], w_o_flat, 0, None, 1, f32, 1024, d, "attn_out")[0]
    x1, h1 = postnorm_residual(x, y0, gain(0, 1), gain(0, 2), "l0_postnorm")
    y1, ffn0 = _ffn_block(h1, w_up, 0, w_down[0], wbs[0], "ffn0")
    x2, h2 = postnorm_residual(x1, y1, gain(0, 3), gain(1, 0), "l0_ffn_postnorm")

    z = mm_nn(h2[None], w_pw1, 0, b_pw1, 1, f32, 1024, w_pw1.shape[-1], "pw1")[0]
    cpre, sw = conv_module_fwd(z, conv_wdw, conv_vecs, "conv_fwd")
    y2 = mm_nn(sw[None], w_pw2, 0, b_pw2, 1, f32, 1024, d, "pw2")[0]
    x3, h3 = postnorm_residual(x2, y2, gain(1, 1), gain(1, 2), "l1_postnorm")
    y3, ffn1 = _ffn_block(h3, w_up, 1, w_down[1], wbs[1], "ffn1")
    dx4, loss_row = final_loss(x3, y3, gain(1, 3), target, "loss")
    loss = lax.psum(loss_row[0, 0], ("x", "y", "c"))

    dgain = [[None] * 4 for _ in range(2)]
    dy3, dgain[1][3], _ = norm_bwd(y3, gain(1, 3), dx4, None, bf16, "l1_ffn_postnorm_bwd")
    dh, d_up1, d_down1, dwb1, _ = _ffn_block_bwd(dy3, ffn1, w_up, 1, w_down[1], wbs[1], "ffn1")
    dx3, dgain[1][2], _ = norm_bwd(x3, gain(1, 2), dh, dx4, f32, "l1_ffn_prenorm_bwd")

    dy2, dgain[1][1], d_b_pw2 = norm_bwd(y2, gain(1, 1), dx3, None, bf16, "l1_postnorm_bwd")
    d_pw2 = mm_tn(sw[None], dy2[None], 1, 1024, d, d, "dw_pw2")
    dsw = mm_nt(dy2[None], w_pw2, 0, 1, f32, 1024, d, d, "d_swish")[0]
    dz, d_conv_wdw, d_conv_vecs, d_b_pw1 = conv_module_bwd(dsw, cpre, z, conv_wdw, conv_vecs, "conv_bwd")
    d_pw1 = mm_tn(h2[None], dz[None], N_CHIPS, 1024, d, w_pw1.shape[-1], "dw_pw1")
    dh = mm_nt_whole(dz[None], w_pw1, 0, f32, 1024, "d_h2")
    dx2, dgain[1][0], _ = norm_bwd(x2, gain(1, 0), dh, dx3, f32, "l1_prenorm_bwd")

    def core_stage(named, tag):
        halves = [g.reshape(N_CHIPS, 2, -1, g.shape[-1]) for _, g in named]
        from_core = swap_core_halves(halves, f"swap_core_halves_{tag}")
        sums = [add_core_halves(a, b, where, f"add_core_{k}") for (k, _), a, b in zip(named, halves, from_core)]
        return halves, from_core, sums

    def chip_stage(named, halves, from_core, from_chips):
        return [add_chip_parts(a, b, r, where, f"add_chips_{k}") for (k, _), a, b, r in zip(named, halves, from_core, from_chips)]

    first_done = [("conv_w_pw1", d_pw1), ("conv_w_pw2", d_pw2[0].reshape(N_CHIPS, -1, d)), ("ffn_w_up1", d_up1),
                  ("ffn_w_down1", d_down1[0].reshape(N_CHIPS, -1, d))]
    as_halves = lambda named: [g.reshape(N_CHIPS, 2, -1, g.shape[-1]) for _, g in named]
    add_cores = lambda named, halves, from_core: [add_core_halves(a, b, where, f"add_core_{k}")
                                                  for (k, _), a, b in zip(named, halves, from_core)]
    halves_1 = as_halves(first_done)

    dy1, dgain[0][3], _ = norm_bwd(y1, gain(0, 3), dx2, None, bf16, "l0_ffn_postnorm_bwd")
    dh, d_up0, d_down0, dwb0, core_1 = _ffn_block_bwd(dy1, ffn0, w_up, 0, w_down[0], wbs[0], "ffn0", swap_rider(halves_1))
    sums_1 = add_cores(first_done, halves_1, core_1)
    mid_done = [("ffn_w_up0", d_up0), ("ffn_w_down0", d_down0[0].reshape(N_CHIPS, -1, d))]
    halves_m = as_halves(mid_done)
    dx1, dgain[0][2], _ = norm_bwd(x1, gain(0, 2), dh, dx2, f32, "l0_ffn_prenorm_bwd")

    dy0, dgain[0][1], _ = norm_bwd(y0, gain(0, 1), dx1, None, bf16, "l0_postnorm_bwd")
    d_wo = mm_tn(mixed[None], dy0[None], N_CHIPS, 1024, GROUP_WIDTH, w_o.shape[-1], "dw_o")
    dmixed = mm_nt_whole(dy0[None], w_o, 0, f32, 1024, "d_mixed")
    mb = mix_bwd(dmixed, os_, lses, head_ones, "mix_bwd")
    parts = []
    for g, (_, dil) in enumerate(DILATED_GROUPS):
        rider = None
        if g == 0:
            rider = join_riders(scatter_rider(sums_1), swap_rider(halves_m))
        elif g == 1:
            rider = scatter_rider(sums_m)
        *five, rode = attn_bwd(qkvg, mb[g], lses_phased[g], mb[3 + g], g, dil, f"attn_bwd{g}", rider)
        parts.append(five)
        if g == 0:
            mine_1 = chip_stage(first_done, halves_1, core_1, rode[:len(first_done)])
            core_m = rode[len(first_done):]
            sums_m = add_cores(mid_done, halves_m, core_m)
        if g == 1:
            mine_m = chip_stage(mid_done, halves_m, core_m, rode)
    dqkv = dqkv_assemble(parts, pair.T, tabs, "dqkv")
    d_qkv = mm_tn(h0[None], dqkv[None], N_CHIPS, 1024, d, w_qkv.shape[-1], "dw_qkv")
    dh = mm_nt_whole(dqkv[None], w_qkv, 0, f32, 512, "d_h0")
    grad_x, dgain[0][0], _ = norm_bwd(x, gain(0, 0), dh, dx1, f32, "l0_prenorm_bwd")

    d_ffn_dw, d_ffn_b = zip(*[_ffn_unpack(t) for t in (dwb0, dwb1)])
    gsmall = dict(
        norm_g=jnp.stack([jnp.concatenate(row, axis=0) for row in dgain], axis=0),
        conv_b_pw1=d_b_pw1, conv_w_dw=d_conv_wdw[None, :CONV_KERNEL], conv_b_dw=d_conv_vecs[0:1], conv_ln_g=d_conv_vecs[1:2],
        conv_ln_b=d_conv_vecs[2:3], conv_b_pw2=d_b_pw2, ffn_w_dw=jnp.stack(d_ffn_dw, axis=0))
    bias_rows = ffn_b_dw.size // LANES
    small_g = jnp.concatenate([jnp.concatenate([_pack_rows([_split_last(gsmall[k])[p] for k in SMALL], small_rows)
                                                for p in range(N_CHIPS)], axis=0),
                               jnp.stack(d_ffn_b, axis=0).reshape(bias_rows, LANES)], axis=0)

    small_sum = sum_devices(gather_devices(small_g, "gather_small_grads"), "sum_small_grads")
    my_small = lax.dynamic_slice_in_dim(small_sum, chip[0] * small_rows, small_rows, axis=0)
    g_small = jnp.concatenate([my_small, small_sum[N_CHIPS * small_rows:]], axis=0)

    last_done = [("attn_w_qkv", d_qkv), ("attn_w_o", d_wo)]
    halves_0, core_0, sums_0 = core_stage(last_done, "attn")
    mine_0 = chip_stage(last_done, halves_0, core_0, scatter_chips(sums_0, "scatter_chips_attn"))
    shared = share_core_halves(mine_1 + mine_m + mine_0, "share_core_halves")
    piece = {k: t.reshape(-1, t.shape[-1]) for (k, _), t in zip(first_done + mid_done + last_done, shared)}
    shard_grads = [piece["attn_w_qkv"], piece["attn_w_o"], piece["conv_w_pw1"], piece["conv_w_pw2"],
                   jnp.concatenate([piece["ffn_w_up0"], piece["ffn_w_up1"]], axis=0),
                   jnp.concatenate([piece["ffn_w_down0"], piece["ffn_w_down1"]], axis=0)]

    grads, deltas, new_m, new_v = {}, {}, {}, {}
    for k, g2 in zip(big, shard_grads):
        shp = weights[k].shape
        dl, nm, nv = adamw(weights[k].reshape(g2.shape), g2, mom1[k].reshape(g2.shape), mom2[k].reshape(g2.shape), f"adamw_{k}")
        grads[k], deltas[k], new_m[k], new_v[k] = (t.reshape(shp) for t in (g2, dl, nm, nv))
    pack_state = lambda src: jnp.concatenate([_pack_rows([src[k] for k in SMALL], small_rows), src["ffn_b_dw"].reshape(bias_rows, LANES)], axis=0)
    small_out = (g_small,) + tuple(adamw(pack_state(weights), g_small, pack_state(mom1), pack_state(mom2), "adamw_small"))
    for dst, packed in zip((grads, deltas, new_m, new_v), small_out):
        for k, t in zip(SMALL, _unpack_rows(packed[:small_rows], small_shapes)):
            dst[k] = t
        dst["ffn_b_dw"] = packed[small_rows:].reshape(ffn_b_dw.shape)

    order = ("norm_g", "attn_w_qkv", "attn_w_o", "conv_w_pw1", "conv_b_pw1", "conv_w_dw", "conv_b_dw", "conv_ln_g", "conv_ln_b",
             "conv_w_pw2", "conv_b_pw2", "ffn_w_up", "ffn_w_dw", "ffn_b_dw", "ffn_w_down")
    return (loss, grad_x[None], *[grads[k] for k in order], *[deltas[k] for k in order], *[new_m[k] for k in order],
            *[new_v[k] for k in order])
```
